```python
import math
import jax, jax.numpy as jnp
from jax import lax
import numpy as np

D_MODEL = 1024
BATCH = 16
SEQ = 2048
DEPTH = 2

ATT_HEADS = 8
HEAD_DIM = 64
ATT_WIDTH = ATT_HEADS * HEAD_DIM
MOBA_BLOCK = 256
MOBA_TOPK = 3
MOBA_QCHUNK = 32
REL_BUCKETS = 32
REL_MAX_DIST = 128
SSM_WIDTH = D_MODEL // 2
SSM_GROUP = 16
SSM_GROUPS = SSM_WIDTH // SSM_GROUP
SSM_STATE = 64
DT_MIN = 1e-3
DT_MAX = 1e-1
N_BRANCH = 2
IN_WIDTH = 3 * ATT_WIDTH + SSM_WIDTH + N_BRANCH * D_MODEL
N_EXPERTS = 32
TOP_K = 4
D_FF = D_MODEL
SWIGLU_ALPHA = 1.702
SWIGLU_LIMIT = 7.0
MOE_BLOCK = 256
RMS_EPS = 1e-6
N_MOD = 6

kernel_name = "hybrid_s5_moba_moe_adaln_block"


def rms_norm(x, g):
    xf = x.astype(jnp.float32)
    y = xf * lax.rsqrt(jnp.mean(xf * xf, axis=-1, keepdims=True) + RMS_EPS)
    return (y * g.astype(jnp.float32)).astype(x.dtype)


def rel_bucket(dist):
    n = jnp.maximum(dist, 0)
    max_exact = REL_BUCKETS // 2
    nf = jnp.maximum(n, 1).astype(jnp.float32)
    large = max_exact + (jnp.log(nf / max_exact) / math.log(REL_MAX_DIST / max_exact)
                         * (REL_BUCKETS - max_exact)).astype(jnp.int32)
    large = jnp.minimum(large, REL_BUCKETS - 1)
    return jnp.where(n < max_exact, n, large)


def s5_ssm(u, a_re, a_im, log_dt, b_re, b_im, c_re, c_im, d_skip):
    bsz, s, _ = u.shape
    f32 = jnp.float32
    uf = u.astype(f32).reshape(bsz, s, SSM_GROUPS, SSM_GROUP)
    lam_re = jnp.minimum(a_re.astype(f32), -1e-4)
    lam_im = a_im.astype(f32)
    dt = jnp.exp(log_dt.astype(f32))[:, None]
    mag = jnp.exp(lam_re * dt)
    lb_re = mag * jnp.cos(lam_im * dt)
    lb_im = mag * jnp.sin(lam_im * dt)
    n_re = lb_re - 1.0
    n_im = lb_im
    den = lam_re * lam_re + lam_im * lam_im
    z_re = ((n_re * lam_re + n_im * lam_im) / den)[..., None]
    z_im = ((n_im * lam_re - n_re * lam_im) / den)[..., None]
    br = b_re.astype(f32)
    bi = b_im.astype(f32)
    bb_re = z_re * br - z_im * bi
    bb_im = z_re * bi + z_im * br
    bu_re = jnp.einsum('bsgc,gpc->bsgp', uf, bb_re)
    bu_im = jnp.einsum('bsgc,gpc->bsgp', uf, bb_im)
    a_re_s = jnp.broadcast_to(lb_re, (1, s, SSM_GROUPS, SSM_STATE))
    a_im_s = jnp.broadcast_to(lb_im, (1, s, SSM_GROUPS, SSM_STATE))

    def combine(e1, e2):
        ar1, ai1, br1, bi1 = e1
        ar2, ai2, br2, bi2 = e2
        return (ar2 * ar1 - ai2 * ai1,
                ar2 * ai1 + ai2 * ar1,
                ar2 * br1 - ai2 * bi1 + br2,
                ar2 * bi1 + ai2 * br1 + bi2)

    _, _, x_re, x_im = lax.associative_scan(combine, (a_re_s, a_im_s, bu_re, bu_im), axis=1)
    y = (jnp.einsum('bsgp,gcp->bsgc', x_re, c_re.astype(f32))
         - jnp.einsum('bsgp,gcp->bsgc', x_im, c_im.astype(f32)))
    y = y.reshape(bsz, s, SSM_WIDTH) + d_skip.astype(f32) * uf.reshape(bsz, s, SSM_WIDTH)
    return y.astype(u.dtype)


def moba_attention(q, k, v, rel_bias):
    bsz, s, h, hd = q.shape
    f32 = jnp.float32
    nb = -(-s // MOBA_BLOCK)
    sp = nb * MOBA_BLOCK
    pad = ((0, 0), (0, sp - s), (0, 0), (0, 0))
    qh = jnp.pad(q, pad).transpose(0, 2, 1, 3) * (hd ** -0.5)
    kb = jnp.pad(k, pad).transpose(0, 2, 1, 3).reshape(bsz, h, nb, MOBA_BLOCK, hd)
    vb = jnp.pad(v, pad).transpose(0, 2, 1, 3).reshape(bsz, h, nb, MOBA_BLOCK, hd)
    k_mean = jnp.mean(kb.astype(f32), axis=3)
    gate = jnp.einsum('bhsd,bhnd->bhsn', qh.astype(f32), k_mean)
    q_blk = jnp.arange(sp) // MOBA_BLOCK
    gate = jnp.where(jnp.arange(nb)[None, :] < q_blk[:, None], gate, -jnp.inf)
    n_sel = min(MOBA_TOPK, nb)
    _, sel = lax.top_k(gate, n_sel)
    n_chunks = sp // MOBA_QCHUNK
    q_c = qh.reshape(bsz, h, n_chunks, MOBA_QCHUNK, hd).transpose(2, 0, 1, 3, 4)
    sel_c = sel.reshape(bsz, h, n_chunks, MOBA_QCHUNK, n_sel).transpose(2, 0, 1, 3, 4)
    b_ix = jnp.arange(bsz)[:, None, None, None]
    h_ix = jnp.arange(h)[None, :, None, None]
    h_ix5 = jnp.arange(h)[None, :, None, None, None]
    offs = jnp.arange(MOBA_BLOCK)

    def chunk(args):
        ci, qc, sc = args
        q_pos = ci * MOBA_QCHUNK + jnp.arange(MOBA_QCHUNK)
        own = (ci * MOBA_QCHUNK) // MOBA_BLOCK
        k_sel = kb[b_ix, h_ix, sc]
        v_sel = vb[b_ix, h_ix, sc]
        s_sel = jnp.einsum('bhqd,bhqnjd->bhqnj', qc, k_sel).astype(f32)
        kpos = sc[..., None] * MOBA_BLOCK + offs
        bias_sel = rel_bias[rel_bucket(q_pos[:, None, None] - kpos), h_ix5]
        s_sel = jnp.where((sc < own)[..., None], s_sel + bias_sel, -jnp.inf)
        k_own = lax.dynamic_index_in_dim(kb, own, axis=2, keepdims=False)
        v_own = lax.dynamic_index_in_dim(vb, own, axis=2, keepdims=False)
        kpos_own = own * MOBA_BLOCK + offs
        bias_own = rel_bias[rel_bucket(q_pos[:, None] - kpos_own[None, :])].transpose(2, 0, 1)
        s_own = jnp.einsum('bhqd,bhjd->bhqj', qc, k_own).astype(f32) + bias_own
        s_own = jnp.where(kpos_own[None, :] <= q_pos[:, None], s_own, -jnp.inf)
        logits = jnp.concatenate(
            [s_sel.reshape(bsz, h, MOBA_QCHUNK, n_sel * MOBA_BLOCK), s_own], axis=-1)
        p = jax.nn.softmax(logits, axis=-1)
        p_sel = p[..., :n_sel * MOBA_BLOCK].reshape(bsz, h, MOBA_QCHUNK, n_sel, MOBA_BLOCK)
        p_own = p[..., n_sel * MOBA_BLOCK:]
        out = (jnp.einsum('bhqnj,bhqnjd->bhqd', p_sel.astype(v_sel.dtype), v_sel)
               + jnp.einsum('bhqj,bhjd->bhqd', p_own.astype(v_own.dtype), v_own))
        return out

    out = lax.map(chunk, (jnp.arange(n_chunks), q_c, sel_c))
    out = out.transpose(1, 0, 3, 2, 4).reshape(bsz, sp, h, hd)
    return out[:, :s]


def hybrid_mixer(h, rel_bias, w_in, a_re, a_im, log_dt, b_re, b_im, c_re, c_im, d_skip,
                 w_glu, w_ssm_out, w_att_out, w_out):
    bsz, s, d = h.shape
    proj = h @ w_in
    q, k, v, u, g_att, g_ssm = jnp.split(
        proj, [ATT_WIDTH, 2 * ATT_WIDTH, 3 * ATT_WIDTH, 3 * ATT_WIDTH + SSM_WIDTH,
               3 * ATT_WIDTH + SSM_WIDTH + D_MODEL], axis=-1)
    heads = (bsz, s, ATT_HEADS, HEAD_DIM)
    att = moba_attention(q.reshape(heads), k.reshape(heads), v.reshape(heads), rel_bias)
    att = att.reshape(bsz, s, ATT_WIDTH) @ w_att_out
    y = jax.nn.gelu(s5_ssm(u, a_re, a_im, log_dt, b_re, b_im, c_re, c_im, d_skip))
    glu_v, glu_g = jnp.split(y @ w_glu, 2, axis=-1)
    ssm = (glu_v * jax.nn.sigmoid(glu_g)) @ w_ssm_out
    merged = jax.nn.sigmoid(g_att) * att + jax.nn.sigmoid(g_ssm) * ssm
    return merged @ w_out


def moe_ffn(h, w_router, b_router, w1, b1, w2, b2):
    bsz, s, d = h.shape
    t = bsz * s
    xt = h.reshape(t, d)
    logits = (xt @ w_router + b_router).astype(jnp.float32)
    top_val, top_idx = lax.top_k(logits, TOP_K)
    gates = jax.nn.softmax(top_val, axis=-1)
    tk = t * TOP_K
    e_flat = top_idx.reshape(tk)
    tok_flat = jnp.arange(tk, dtype=jnp.int32) // TOP_K
    w_flat = gates.reshape(tk)
    order = jnp.argsort(e_flat)
    e_sorted = e_flat[order]
    tok_sorted = tok_flat[order]
    w_sorted = w_flat[order]
    counts = jnp.bincount(e_flat, length=N_EXPERTS)
    starts = jnp.cumsum(counts) - counts
    padded = (counts + MOE_BLOCK - 1) // MOE_BLOCK * MOE_BLOCK
    pad_ends = jnp.cumsum(padded)
    pad_starts = pad_ends - padded
    dest = pad_starts[e_sorted] + (jnp.arange(tk) - starts[e_sorted])
    n_blocks = tk // MOE_BLOCK + N_EXPERTS
    p_rows = n_blocks * MOE_BLOCK
    buf_tok = jnp.zeros((p_rows,), jnp.int32).at[dest].set(tok_sorted)
    buf_w = jnp.zeros((p_rows,), jnp.float32).at[dest].set(w_sorted)
    block_e = jnp.minimum(
        jnp.searchsorted(pad_ends, jnp.arange(n_blocks) * MOE_BLOCK, side='right'), N_EXPERTS - 1)
    xb = xt[buf_tok].reshape(n_blocks, MOE_BLOCK, d)

    def expert_block(args):
        xblk, e = args
        gu = xblk @ w1[e] + b1[e]
        g, up = gu[:, :D_FF], gu[:, D_FF:]
        g = jnp.minimum(g, SWIGLU_LIMIT)
        up = jnp.clip(up, -SWIGLU_LIMIT, SWIGLU_LIMIT)
        act = (up + 1.0) * g * jax.nn.sigmoid(SWIGLU_ALPHA * g)
        return act @ w2[e] + b2[e]

    yb = lax.map(expert_block, (xb, block_e)).reshape(p_rows, d)
    out = jnp.zeros((t, d), yb.dtype).at[buf_tok].add(yb * buf_w[:, None].astype(yb.dtype))
    return out.reshape(bsz, s, d)


def setup_inputs(seed: int = 0) -> dict:
    key = jax.random.key(seed)
    ks = iter(jax.random.split(key, 32))
    f32 = jnp.float32
    L, D, G, P, C = DEPTH, D_MODEL, SSM_GROUPS, SSM_STATE, SSM_GROUP

    def nrm(shape, scale):
        return jax.random.normal(next(ks), shape, f32) * scale

    return {
        "x": nrm((BATCH, SEQ, D), 1.0),
        "c": nrm((BATCH, D), 1.0),
        "rel_bias": nrm((REL_BUCKETS, ATT_HEADS), 0.5),
        "w_ada": nrm((L, D, N_MOD * D), 0.5 * D ** -0.5),
        "b_ada": nrm((L, N_MOD * D), 0.02),
        "g_pre_mix": 1.0 + nrm((L, D), 0.05),
        "g_post_mix": 1.0 + nrm((L, D), 0.05),
        "g_pre_ffn": 1.0 + nrm((L, D), 0.05),
        "g_post_ffn": 1.0 + nrm((L, D), 0.05),
        "w_in": nrm((L, D, IN_WIDTH), D ** -0.5),
        "ssm_a_re": -0.5 + nrm((L, G, P), 0.01),
        "ssm_a_im": math.pi * jnp.arange(P, dtype=f32)[None, None, :] + nrm((L, G, P), 0.01),
        "ssm_log_dt": jax.random.uniform(next(ks), (L, G), f32,
                                         minval=math.log(DT_MIN), maxval=math.log(DT_MAX)),
        "ssm_b_re": nrm((L, G, P, C), (2 * C) ** -0.5),
        "ssm_b_im": nrm((L, G, P, C), (2 * C) ** -0.5),
        "ssm_c_re": nrm((L, G, C, P), (2 * P) ** -0.5 * 4.0),
        "ssm_c_im": nrm((L, G, C, P), (2 * P) ** -0.5 * 4.0),
        "ssm_d": nrm((L, SSM_WIDTH), 1.0),
        "w_glu": nrm((L, SSM_WIDTH, 2 * SSM_WIDTH), SSM_WIDTH ** -0.5),
        "w_ssm_out": nrm((L, SSM_WIDTH, D), SSM_WIDTH ** -0.5),
        "w_att_out": nrm((L, ATT_WIDTH, D), ATT_WIDTH ** -0.5),
        "w_out": nrm((L, D, D), D ** -0.5),
        "w_router": nrm((L, D, N_EXPERTS), D ** -0.5),
        "b_router": nrm((L, N_EXPERTS), 0.01),
        "w_exp_in": nrm((L, N_EXPERTS, D, 2 * D_FF), D ** -0.5),
        "b_exp_in": nrm((L, N_EXPERTS, 2 * D_FF), 0.01),
        "w_exp_out": nrm((L, N_EXPERTS, D_FF, D), D_FF ** -0.5),
        "b_exp_out": nrm((L, N_EXPERTS, D), 0.01),
    }


def reference(x, c, rel_bias, w_ada, b_ada, g_pre_mix, g_post_mix, g_pre_ffn, g_post_ffn,
              w_in, ssm_a_re, ssm_a_im, ssm_log_dt, ssm_b_re, ssm_b_im, ssm_c_re, ssm_c_im,
              ssm_d, w_glu, w_ssm_out, w_att_out, w_out, w_router, b_router,
              w_exp_in, b_exp_in, w_exp_out, b_exp_out):
    cond = jax.nn.silu(c)
    for l in range(DEPTH):
        mod = cond @ w_ada[l] + b_ada[l]
        sh1, sc1, gt1, sh2, sc2, gt2 = [m[:, None, :] for m in jnp.split(mod, N_MOD, axis=-1)]
        h = rms_norm(x, g_pre_mix[l]) * (1.0 + sc1) + sh1
        y = hybrid_mixer(h, rel_bias, w_in[l], ssm_a_re[l], ssm_a_im[l], ssm_log_dt[l],
                         ssm_b_re[l], ssm_b_im[l], ssm_c_re[l], ssm_c_im[l], ssm_d[l],
                         w_glu[l], w_ssm_out[l], w_att_out[l], w_out[l])
        x = x + gt1 * rms_norm(y, g_post_mix[l])
        h = rms_norm(x, g_pre_ffn[l]) * (1.0 + sc2) + sh2
        y = moe_ffn(h, w_router[l], b_router[l], w_exp_in[l], b_exp_in[l],
                    w_exp_out[l], b_exp_out[l])
        x = x + gt2 * rms_norm(y, g_post_ffn[l])
    return x
```

```python
import functools
import math

import jax
import jax.numpy as jnp
from jax import lax
from jax.experimental import pallas as pl
from jax.experimental.pallas import tpu as pltpu

F32 = jnp.float32
BF16 = jnp.bfloat16

D_MODEL = 1024
ATT_HEADS = 8
HEAD_DIM = 64
ATT_WIDTH = ATT_HEADS * HEAD_DIM
MOBA_BLOCK = 256
MOBA_TOPK = 3
REL_BUCKETS = 32
REL_MAX_DIST = 128
SSM_WIDTH = D_MODEL // 2
SSM_GROUP = 16
SSM_GROUPS = SSM_WIDTH // SSM_GROUP
SSM_STATE = 64
N_EXPERTS = 32
TOP_K = 4
D_FF = D_MODEL
SWIGLU_ALPHA = 1.702
SWIGLU_LIMIT = 7.0
RMS_EPS = 1e-6
N_MOD = 6

SSM_CHUNK = 16
TOKEN_TILE = 512
EXPERT_ROWS = 256
MASK_NEG = -1e30
VMEM_LIMIT = 56 * 1024 * 1024

_NT = (((1,), (1,)), ((), ()))


def _cparams(*sem):
    return pltpu.CompilerParams(dimension_semantics=sem, vmem_limit_bytes=VMEM_LIMIT)


def _rms(x, g):
    return x * lax.rsqrt(jnp.mean(x * x, axis=-1, keepdims=True) + RMS_EPS) * g


def _full(shape):
    n = len(shape)
    return pl.BlockSpec(shape, lambda *_: (0,) * n)


def _ada_kernel(c_ref, w_ref, b_ref, o_ref):
    c = c_ref[...]
    cond = c * jax.nn.sigmoid(c)
    o_ref[0] = jnp.dot(cond, w_ref[0], preferred_element_type=F32,
                       precision=lax.Precision.HIGHEST) + b_ref[0]


def _ada_mod(c, w_ada, b_ada):
    depth, d, nd = w_ada.shape
    bsz = c.shape[0]
    return pl.pallas_call(
        _ada_kernel,
        grid=(depth, nd // d),
        in_specs=[
            pl.BlockSpec((bsz, d), lambda l, j: (0, 0)),
            pl.BlockSpec((1, d, d), lambda l, j: (l, 0, j)),
            pl.BlockSpec((1, 1, d), lambda l, j: (l, 0, j)),
        ],
        out_specs=pl.BlockSpec((1, bsz, d), lambda l, j: (l, 0, j)),
        out_shape=jax.ShapeDtypeStruct((depth, bsz, nd), F32),
        compiler_params=_cparams("arbitrary", "arbitrary"),
        name="ada_mod",
    )(c, w_ada, b_ada.reshape(depth, 1, nd))


def _premix_kernel(x_ref, g_ref, sc_ref, sh_ref, wq_ref, wk_ref, wvt_ref, wu_ref, wga_ref, wgs_ref,
                   q_ref, k_ref, vt_ref, u_ref, ga_ref, gs_ref):
    x = x_ref[...]
    h = _rms(x, g_ref[...]) * (1.0 + sc_ref[0]) + sh_ref[0]
    hb = h.astype(BF16)
    nblk = q_ref.shape[0]
    q = (jnp.dot(hb, wq_ref[...], preferred_element_type=F32) * (HEAD_DIM ** -0.5)).astype(BF16)
    k = jnp.dot(hb, wk_ref[...], preferred_element_type=F32).astype(BF16)
    vt = lax.dot_general(wvt_ref[...], hb, _NT, preferred_element_type=F32).astype(BF16)
    for r in range(nblk):
        q_ref[r] = q[r * MOBA_BLOCK:(r + 1) * MOBA_BLOCK]
        k_ref[r] = k[r * MOBA_BLOCK:(r + 1) * MOBA_BLOCK]
        vt_ref[r] = vt[:, r * MOBA_BLOCK:(r + 1) * MOBA_BLOCK]
    u_ref[...] = jnp.dot(hb, wu_ref[...], preferred_element_type=F32).astype(BF16)
    ga = jnp.dot(hb, wga_ref[...], preferred_element_type=F32)
    ga_ref[...] = jax.nn.sigmoid(ga).astype(BF16)
    gs = jnp.dot(hb, wgs_ref[...], preferred_element_type=F32)
    gs_ref[...] = jax.nn.sigmoid(gs).astype(BF16)


def _premix(x2, g, sc, sh, w_in, seq):
    t, d = x2.shape
    tm = TOKEN_TILE
    tiles_per_seq = seq // tm
    nblk = tm // MOBA_BLOCK
    a = ATT_WIDTH
    wb = w_in.astype(BF16)
    wq, wk, wv = wb[:, :a], wb[:, a:2 * a], wb[:, 2 * a:3 * a]
    wu = wb[:, 3 * a:3 * a + SSM_WIDTH]
    wga = wb[:, 3 * a + SSM_WIDTH:3 * a + SSM_WIDTH + d]
    wgs = wb[:, 3 * a + SSM_WIDTH + d:]
    mod_spec = pl.BlockSpec((1, 1, d), lambda i: (i // tiles_per_seq, 0, 0))
    blk3 = lambda rows, cols: pl.BlockSpec((nblk, rows, cols), lambda i: (i, 0, 0))
    row = lambda cols: pl.BlockSpec((tm, cols), lambda i: (i, 0))
    nb_total = t // MOBA_BLOCK
    return pl.pallas_call(
        _premix_kernel,
        grid=(t // tm,),
        in_specs=[row(d), _full((1, d)), mod_spec, mod_spec,
                  _full(wq.shape), _full(wk.shape), _full((a, d)), _full(wu.shape),
                  _full(wga.shape), _full(wgs.shape)],
        out_specs=[blk3(MOBA_BLOCK, a), blk3(MOBA_BLOCK, a), blk3(a, MOBA_BLOCK),
                   row(SSM_WIDTH), row(d), row(d)],
        out_shape=[jax.ShapeDtypeStruct((nb_total, MOBA_BLOCK, a), BF16),
                   jax.ShapeDtypeStruct((nb_total, MOBA_BLOCK, a), BF16),
                   jax.ShapeDtypeStruct((nb_total, a, MOBA_BLOCK), BF16),
                   jax.ShapeDtypeStruct((t, SSM_WIDTH), BF16),
                   jax.ShapeDtypeStruct((t, d), BF16),
                   jax.ShapeDtypeStruct((t, d), BF16)],
        compiler_params=_cparams("arbitrary"),
        name="premix_inproj",
    )(x2, g.reshape(1, d), sc, sh, wq, wk, wv.T, wu, wga, wgs)


def _moba_kernel(far_ref, q_ref, k_ref, vt_ref, bias_ref, o_ref, kmean_ref):
    nb = k_ref.shape[0]
    blk = MOBA_BLOCK
    i = pl.program_id(1)

    @pl.when(i == 0)
    def _():
        for n in range(nb):
            kmean_ref[n:n + 1, :] = jnp.mean(k_ref[n].astype(F32), axis=0, keepdims=True)

    lane = lax.broadcasted_iota(jnp.int32, (blk, 2 * HEAD_DIM), 1)
    nidx = lax.broadcasted_iota(jnp.int32, (nb, blk), 0)
    valid = nidx < i
    for hp in range(ATT_HEADS // 2):
        cols = slice(hp * 2 * HEAD_DIM, (hp + 1) * 2 * HEAD_DIM)
        q2 = q_ref[0, :, cols]
        km = kmean_ref[:, cols]
        km_hi = km.astype(BF16)
        km_lo = (km - km_hi.astype(F32)).astype(BF16)
        outs = []
        for hh in range(2):
            head = hp * 2 + hh
            in_head = (lane >= hh * HEAD_DIM) & (lane < (hh + 1) * HEAD_DIM)
            qh = jnp.where(in_head, q2, jnp.zeros_like(q2))
            gate = (lax.dot_general(km_hi, qh, _NT, preferred_element_type=F32)
                    + lax.dot_general(km_lo, qh, _NT, preferred_element_type=F32))
            g = jnp.where(valid, gate, -jnp.inf)
            sel = jnp.zeros((nb, blk), jnp.bool_)
            for _ in range(MOBA_TOPK):
                mx = jnp.max(g, axis=0, keepdims=True)
                first = jnp.min(jnp.where(g == mx, nidx, nb), axis=0, keepdims=True)
                pick = nidx == first
                sel = sel | pick
                g = jnp.where(pick, -jnp.inf, g)
            neg = jnp.where(sel & valid, 0.0, MASK_NEG)

            def block(j, add, carry, head=head, qh=qh, cols=cols):
                m, l, acc = carry
                kb = k_ref[j, :, cols]
                st = lax.dot_general(kb, qh, _NT, preferred_element_type=F32) + add
                m_new = jnp.maximum(m, jnp.max(st, axis=0, keepdims=True))
                alpha = jnp.exp(m - m_new)
                p = jnp.exp(st - m_new)
                l = alpha * l + jnp.sum(p, axis=0, keepdims=True)
                vb = vt_ref[j, head * HEAD_DIM:(head + 1) * HEAD_DIM, :]
                acc = alpha * acc + jnp.dot(vb, p.astype(BF16), preferred_element_type=F32)
                return m_new, l, acc

            def neg_row(j, neg=neg):
                return jnp.sum(jnp.where(nidx == j, neg, 0.0), axis=0, keepdims=True)

            carry = (jnp.full((1, blk), MASK_NEG, F32), jnp.zeros((1, blk), F32),
                     jnp.zeros((HEAD_DIM, blk), F32))
            carry = block(i, bias_ref[head, 0], carry)
            prev_add = bias_ref[head, 1] + neg_row(i - 1) + jnp.where(i == 0, MASK_NEG, 0.0)
            carry = block(jnp.maximum(i - 1, 0), prev_add, carry)
            far = far_ref[head]
            carry = lax.fori_loop(
                0, jnp.maximum(i - 1, 0),
                lambda j, c, far=far, block=block, neg_row=neg_row: block(j, neg_row(j) + far, c),
                carry)
            _, l, acc = carry
            outs.append(acc / l)
        pair = jnp.concatenate(outs, axis=0)
        o_ref[0, :, cols] = pair.T.astype(BF16)


def _rel_bucket(dist):
    n = jnp.maximum(dist, 0)
    max_exact = REL_BUCKETS // 2
    nf = jnp.maximum(n, 1).astype(F32)
    large = max_exact + (jnp.log(nf / max_exact) / math.log(REL_MAX_DIST / max_exact)
                         * (REL_BUCKETS - max_exact)).astype(jnp.int32)
    large = jnp.minimum(large, REL_BUCKETS - 1)
    return jnp.where(n < max_exact, n, large)


def _moba_bias_tiles(rel_bias):
    blk = MOBA_BLOCK
    kj = jnp.arange(blk)[:, None]
    qi = jnp.arange(blk)[None, :]
    d_own = qi - kj
    own = rel_bias[_rel_bucket(d_own)]
    own = jnp.where((d_own >= 0)[..., None], own, MASK_NEG)
    prev = rel_bias[_rel_bucket(d_own + blk)]
    tiles = jnp.stack([own, prev], axis=0)
    far = rel_bias[_rel_bucket(jnp.array(2 * blk))]
    return tiles.transpose(3, 0, 1, 2).astype(F32), far.astype(F32)


def _moba(q4, k4, vt4, bias_tiles, far_bias, bsz):
    nb_total, blk, a = q4.shape
    nb = nb_total // bsz
    return pl.pallas_call(
        _moba_kernel,
        grid=(bsz, nb),
        in_specs=[
            pl.BlockSpec(memory_space=pltpu.SMEM),
            pl.BlockSpec((1, blk, a), lambda b, i: (b * nb + i, 0, 0)),
            pl.BlockSpec((nb, blk, a), lambda b, i: (b, 0, 0)),
            pl.BlockSpec((nb, a, blk), lambda b, i: (b, 0, 0)),
            _full(bias_tiles.shape),
        ],
        out_specs=pl.BlockSpec((1, blk, a), lambda b, i: (b * nb + i, 0, 0)),
        out_shape=jax.ShapeDtypeStruct((nb_total, blk, a), BF16),
        scratch_shapes=[pltpu.VMEM((nb, a), F32)],
        compiler_params=_cparams("arbitrary", "arbitrary"),
        name="moba_attention",
    )(far_bias, q4, k4, vt4, bias_tiles)


def _ssm_weights(a_re, a_im, log_dt, b_re, b_im, c_re, c_im, d_skip):
    g, p, cw, n = SSM_GROUPS, SSM_STATE, SSM_GROUP, SSM_CHUNK
    lam_re = jnp.minimum(a_re.astype(F32), -1e-4)
    lam_im = a_im.astype(F32)
    dt = jnp.exp(log_dt.astype(F32))[:, None]
    mag = jnp.exp(lam_re * dt)
    lb_re = mag * jnp.cos(lam_im * dt)
    lb_im = mag * jnp.sin(lam_im * dt)
    n_re = lb_re - 1.0
    n_im = lb_im
    den = lam_re * lam_re + lam_im * lam_im
    z_re = ((n_re * lam_re + n_im * lam_im) / den)[..., None]
    z_im = ((n_im * lam_re - n_re * lam_im) / den)[..., None]
    br, bi = b_re.astype(F32), b_im.astype(F32)
    bb_re = z_re * br - z_im * bi
    bb_im = z_re * bi + z_im * br
    pw_re, pw_im = [jnp.ones_like(lb_re)], [jnp.zeros_like(lb_im)]
    for _ in range(n):
        r, im = pw_re[-1], pw_im[-1]
        pw_re.append(r * lb_re - im * lb_im)
        pw_im.append(r * lb_im + im * lb_re)
    pw_re, pw_im = jnp.stack(pw_re), jnp.stack(pw_im)
    cr, ci = c_re.astype(F32), c_im.astype(F32)
    hi = lax.Precision.HIGHEST
    rev_re, rev_im = pw_re[n - 1::-1][:n], pw_im[n - 1::-1][:n]
    sb_re = jnp.einsum('sgp,gpc->gscp', rev_re, bb_re) - jnp.einsum('sgp,gpc->gscp', rev_im, bb_im)
    sb_im = jnp.einsum('sgp,gpc->gscp', rev_re, bb_im) + jnp.einsum('sgp,gpc->gscp', rev_im, bb_re)
    sb_re = sb_re.reshape(g, n * cw, p)
    sb_im = sb_im.reshape(g, n * cw, p)
    cl_re = jnp.einsum('gcp,tgp->gptc', cr, pw_re[1:]) - jnp.einsum('gcp,tgp->gptc', ci, pw_im[1:])
    cl_im = jnp.einsum('gcp,tgp->gptc', cr, pw_im[1:]) + jnp.einsum('gcp,tgp->gptc', ci, pw_re[1:])
    so_re = cl_re.reshape(g, p, n * cw)
    so_im = (-cl_im).reshape(g, p, n * cw)
    cp_re = jnp.einsum('gcp,tgp->tgcp', cr, pw_re[:n]) - jnp.einsum('gcp,tgp->tgcp', ci, pw_im[:n])
    cp_im = jnp.einsum('gcp,tgp->tgcp', cr, pw_im[:n]) + jnp.einsum('gcp,tgp->tgcp', ci, pw_re[:n])
    kern = (jnp.einsum('tgcp,gpd->gtcd', cp_re, bb_re, precision=hi)
            - jnp.einsum('tgcp,gpd->gtcd', cp_im, bb_im, precision=hi))
    s_ix = jnp.arange(n)[:, None]
    t_ix = jnp.arange(n)[None, :]
    tau = t_ix - s_ix
    toe = kern[:, jnp.clip(tau, 0, n - 1)]
    toe = jnp.where((tau >= 0)[None, :, :, None, None], toe, 0.0)
    toe = toe.transpose(0, 1, 4, 2, 3).reshape(g, n * cw, n * cw)

    gp = g // 2
    zeros_sb = jnp.zeros_like(sb_re[0])
    zeros_so = jnp.zeros_like(so_re[0])

    def pair_in(m):
        m = m.reshape(gp, 2, n * cw, p)
        top = jnp.concatenate([m[:, 0], jnp.broadcast_to(zeros_sb, m[:, 0].shape)], axis=-1)
        bot = jnp.concatenate([jnp.broadcast_to(zeros_sb, m[:, 1].shape), m[:, 1]], axis=-1)
        return jnp.concatenate([top, bot], axis=1)

    def pair_out(m):
        m = m.reshape(gp, 2, p, n * cw)
        top = jnp.concatenate([m[:, 0], jnp.broadcast_to(zeros_so, m[:, 0].shape)], axis=-1)
        bot = jnp.concatenate([jnp.broadcast_to(zeros_so, m[:, 1].shape), m[:, 1]], axis=-1)
        return jnp.concatenate([top, bot], axis=1)

    w_in_state = jnp.concatenate([pair_in(sb_re), pair_in(sb_im)], axis=-1).astype(BF16)
    w_state_out = jnp.concatenate([pair_out(so_re), pair_out(so_im)], axis=1).astype(BF16)
    toe = toe.reshape(gp, 2, n * cw, n * cw).astype(BF16)
    decay = jnp.stack([pw_re[n].reshape(gp, 2 * p), pw_im[n].reshape(gp, 2 * p)], axis=1)
    dvec = jnp.broadcast_to(d_skip.astype(F32).reshape(gp, 2, 1, cw), (gp, 2, n, cw)).reshape(gp, 1, 2 * n * cw)
    return w_in_state, toe, w_state_out, decay, dvec


def _ssm_kernel(u_ref, wis_ref, toe_ref, wso_ref, dec_ref, d_ref, y_ref, s_ref, xp_ref, *, bsz):
    rows = u_ref.shape[1]
    half = u_ref.shape[2] // 2
    lanes = dec_ref.shape[2]
    u = u_ref[0]
    s_ref[...] = jnp.dot(u, wis_ref[0], preferred_element_type=F32)
    ar = dec_ref[0, 0:1, :]
    ai = dec_ref[0, 1:2, :]

    def step(kc, carry):
        xr, xi = carry
        r0 = pl.multiple_of(kc * bsz, bsz)
        xp_ref[pl.ds(r0, bsz), 0:lanes] = xr
        xp_ref[pl.ds(r0, bsz), lanes:2 * lanes] = xi
        sr = s_ref[pl.ds(r0, bsz), 0:lanes]
        si = s_ref[pl.ds(r0, bsz), lanes:2 * lanes]
        return ar * xr - ai * xi + sr, ar * xi + ai * xr + si

    zero = jnp.zeros((bsz, lanes), F32)
    lax.fori_loop(0, rows // bsz, step, (zero, zero))
    y = jnp.dot(xp_ref[...].astype(BF16), wso_ref[0], preferred_element_type=F32)
    y0 = y[:, :half] + jnp.dot(u[:, :half], toe_ref[0, 0], preferred_element_type=F32)
    y1 = y[:, half:] + jnp.dot(u[:, half:], toe_ref[0, 1], preferred_element_type=F32)
    y = jnp.concatenate([y0, y1], axis=-1) + d_ref[0] * u.astype(F32)
    y_ref[0] = jax.nn.gelu(y).astype(BF16)


def _ssm(u2, weights, bsz, seq):
    w_in_state, toe, w_state_out, decay, dvec = weights
    n, cw = SSM_CHUNK, SSM_GROUP
    gp = SSM_GROUPS // 2
    nchunk = seq // n
    rows = nchunk * bsz
    width = 2 * n * cw
    uc = u2.reshape(bsz, nchunk, n, gp, 2, cw).transpose(3, 1, 0, 4, 2, 5).reshape(gp, rows, width)
    spec3 = lambda a: pl.BlockSpec((1,) + a.shape[1:], lambda g: (g,) + (0,) * (a.ndim - 1))
    yc = pl.pallas_call(
        functools.partial(_ssm_kernel, bsz=bsz),
        grid=(gp,),
        in_specs=[pl.BlockSpec((1, rows, width), lambda g: (g, 0, 0)),
                  spec3(w_in_state), spec3(toe), spec3(w_state_out), spec3(decay), spec3(dvec)],
        out_specs=pl.BlockSpec((1, rows, width), lambda g: (g, 0, 0)),
        out_shape=jax.ShapeDtypeStruct((gp, rows, width), BF16),
        scratch_shapes=[pltpu.VMEM((rows, 4 * SSM_STATE), F32), pltpu.VMEM((rows, 4 * SSM_STATE), F32)],
        compiler_params=_cparams("arbitrary"),
        name="s5_chunked_scan",
    )(uc, w_in_state, toe, w_state_out, decay, dvec)
    return yc.reshape(gp, nchunk, bsz, 2, n, cw).transpose(2, 1, 4, 0, 3, 5).reshape(bsz * seq, SSM_WIDTH)


def _postmix_kernel(att_ref, ys_ref, ga_ref, gs_ref, x_ref, gt_ref, sc_ref, sh_ref, gpost_ref, gpre_ref,
                    wglu_ref, wso_ref, wao_ref, wo_ref, wrh_ref, wrl_ref, br_ref,
                    x1_ref, h2_ref, idx_ref, gate_ref, rank_ref, cnt_ref, run_ref):
    tm = x_ref.shape[0]
    ne = wrh_ref.shape[0]

    @pl.when(pl.program_id(0) == 0)
    def _():
        run_ref[...] = jnp.zeros_like(run_ref)

    glu = jnp.dot(ys_ref[...], wglu_ref[...], preferred_element_type=F32)
    sg = (glu[:, :SSM_WIDTH] * jax.nn.sigmoid(glu[:, SSM_WIDTH:])).astype(BF16)
    ssm = jnp.dot(sg, wso_ref[...], preferred_element_type=F32)
    att = jnp.dot(att_ref[...], wao_ref[...], preferred_element_type=F32)
    merged = (ga_ref[...].astype(F32) * att + gs_ref[...].astype(F32) * ssm).astype(BF16)
    y = jnp.dot(merged, wo_ref[...], preferred_element_type=F32)
    x1 = x_ref[...] + gt_ref[0] * _rms(y, gpost_ref[...])
    x1_ref[...] = x1
    h2 = _rms(x1, gpre_ref[...]) * (1.0 + sc_ref[0]) + sh_ref[0]
    h2_ref[...] = h2
    h_hi = h2.astype(BF16)
    h_lo = (h2 - h_hi.astype(F32)).astype(BF16)
    logits = (lax.dot_general(wrh_ref[...], h_hi, _NT, preferred_element_type=F32)
              + lax.dot_general(wrh_ref[...], h_lo, _NT, preferred_element_type=F32)
              + lax.dot_general(wrl_ref[...], h_hi, _NT, preferred_element_type=F32)
              + br_ref[...])
    eidx = lax.broadcasted_iota(jnp.int32, (ne, tm), 0)
    lg = logits
    vals, idxs = [], []
    for _ in range(TOP_K):
        mx = jnp.max(lg, axis=0, keepdims=True)
        first = jnp.min(jnp.where(lg == mx, eidx, ne), axis=0, keepdims=True)
        vals.append(mx)
        idxs.append(first)
        lg = jnp.where(eidx == first, -jnp.inf, lg)
    exps = [jnp.exp(v - vals[0]) for v in vals]
    denom = exps[0] + exps[1] + exps[2] + exps[3]
    gate_ref[...] = jnp.concatenate([e / denom for e in exps], axis=0)
    idx_ref[...] = jnp.concatenate(idxs, axis=0)
    onehot = jnp.where(lg == -jnp.inf, 1.0, 0.0)
    t_row = lax.broadcasted_iota(jnp.int32, (tm, tm), 0)
    t_col = lax.broadcasted_iota(jnp.int32, (tm, tm), 1)
    earlier = jnp.where(t_row < t_col, 1.0, 0.0).astype(BF16)
    before = jnp.dot(onehot.astype(BF16), earlier, preferred_element_type=F32) + run_ref[:, 0:1]
    ranks = [jnp.sum(jnp.where(eidx == ix, before, 0.0), axis=0, keepdims=True) for ix in idxs]
    rank_ref[...] = jnp.concatenate(ranks, axis=0).astype(jnp.int32)
    run_ref[...] = run_ref[...] + jnp.sum(onehot, axis=1, keepdims=True)
    cnt_ref[...] = run_ref[...].astype(jnp.int32)


def _postmix(att2, ys2, ga, gs, x2, gt1, sc2, sh2, g_post, g_pre, w_glu, w_ssm_out, w_att_out, w_out,
             w_router, b_router, seq):
    t, d = x2.shape
    tm = TOKEN_TILE
    tiles_per_seq = seq // tm
    ne = w_router.shape[1]
    wr_t = w_router.T.astype(F32)
    wr_hi = wr_t.astype(BF16)
    wr_lo = (wr_t - wr_hi.astype(F32)).astype(BF16)
    mod_spec = pl.BlockSpec((1, 1, d), lambda i: (i // tiles_per_seq, 0, 0))
    row = lambda cols: pl.BlockSpec((tm, cols), lambda i: (i, 0))
    col = pl.BlockSpec((TOP_K, tm), lambda i: (0, i))
    weights = [w_glu.astype(BF16), w_ssm_out.astype(BF16), w_att_out.astype(BF16), w_out.astype(BF16),
               wr_hi, wr_lo, b_router.astype(F32).reshape(ne, 1)]
    return pl.pallas_call(
        _postmix_kernel,
        grid=(t // tm,),
        in_specs=[row(ATT_WIDTH), row(SSM_WIDTH), row(d), row(d), row(d), mod_spec, mod_spec, mod_spec,
                  _full((1, d)), _full((1, d))] + [_full(w.shape) for w in weights],
        out_specs=[row(d), row(d), col, col, col, _full((ne, 128))],
        out_shape=[jax.ShapeDtypeStruct((t, d), F32), jax.ShapeDtypeStruct((t, d), F32),
                   jax.ShapeDtypeStruct((TOP_K, t), jnp.int32), jax.ShapeDtypeStruct((TOP_K, t), F32),
                   jax.ShapeDtypeStruct((TOP_K, t), jnp.int32), jax.ShapeDtypeStruct((ne, 128), jnp.int32)],
        scratch_shapes=[pltpu.VMEM((ne, 128), F32)],
        compiler_params=_cparams("arbitrary"),
        name="postmix_router",
    )(att2, ys2, ga, gs, x2, gt1, sc2, sh2, g_post.reshape(1, d), g_pre.reshape(1, d), *weights)


def _experts_kernel(be_ref, nused_ref, x_ref, w1_ref, b1_ref, w2_ref, b2_ref, y_ref, w1b_ref, w2b_ref):
    i = pl.program_id(0)
    prev = be_ref[jnp.maximum(i - 1, 0)]
    fresh = (i == 0) | (be_ref[i] != prev)

    @pl.when(fresh)
    def _():
        w1b_ref[...] = w1_ref[0].astype(BF16)
        w2b_ref[...] = w2_ref[0].astype(BF16)

    @pl.when(i < nused_ref[0])
    def _():
        gu = jnp.dot(x_ref[...].astype(BF16), w1b_ref[...], preferred_element_type=F32) + b1_ref[0]
        g = jnp.minimum(gu[:, :D_FF], SWIGLU_LIMIT)
        up = jnp.clip(gu[:, D_FF:], -SWIGLU_LIMIT, SWIGLU_LIMIT)
        act = ((up + 1.0) * g * jax.nn.sigmoid(SWIGLU_ALPHA * g)).astype(BF16)
        y_ref[...] = jnp.dot(act, w2b_ref[...], preferred_element_type=F32) + b2_ref[0]

    @pl.when(i >= nused_ref[0])
    def _():
        y_ref[...] = jnp.zeros_like(y_ref)


def _experts(xb, block_e, n_used, w1, b1, w2, b2):
    p_rows, d = xb.shape
    ne = w1.shape[0]
    rb = EXPERT_ROWS
    grid_spec = pltpu.PrefetchScalarGridSpec(
        num_scalar_prefetch=2,
        grid=(p_rows // rb,),
        in_specs=[
            pl.BlockSpec((rb, d), lambda i, be, nu: (i, 0)),
            pl.BlockSpec((1, d, 2 * D_FF), lambda i, be, nu: (be[i], 0, 0)),
            pl.BlockSpec((1, 1, 2 * D_FF), lambda i, be, nu: (be[i], 0, 0)),
            pl.BlockSpec((1, D_FF, d), lambda i, be, nu: (be[i], 0, 0)),
            pl.BlockSpec((1, 1, d), lambda i, be, nu: (be[i], 0, 0)),
        ],
        out_specs=pl.BlockSpec((rb, d), lambda i, be, nu: (i, 0)),
        scratch_shapes=[pltpu.VMEM((d, 2 * D_FF), BF16), pltpu.VMEM((D_FF, d), BF16)],
    )
    return pl.pallas_call(
        _experts_kernel,
        grid_spec=grid_spec,
        out_shape=jax.ShapeDtypeStruct((p_rows, d), F32),
        compiler_params=_cparams("arbitrary"),
        name="expert_ffn",
    )(block_e, n_used, xb, w1, b1.reshape(ne, 1, 2 * D_FF), w2, b2.reshape(ne, 1, d))


def _postffn_kernel(y_ref, x_ref, gt_ref, g_ref, o_ref):
    o_ref[...] = x_ref[...] + gt_ref[0] * _rms(y_ref[...], g_ref[...])


def _postffn(y2, x2, gt2, g_post, seq):
    t, d = x2.shape
    tm = TOKEN_TILE
    tiles_per_seq = seq // tm
    row = pl.BlockSpec((tm, d), lambda i: (i, 0))
    return pl.pallas_call(
        _postffn_kernel,
        grid=(t // tm,),
        in_specs=[row, row, pl.BlockSpec((1, 1, d), lambda i: (i // tiles_per_seq, 0, 0)), _full((1, d))],
        out_specs=row,
        out_shape=jax.ShapeDtypeStruct((t, d), F32),
        compiler_params=_cparams("arbitrary"),
        name="postffn_residual",
    )(y2, x2, gt2, g_post.reshape(1, d))


def _route_plan(idx_t, rank_t, counts):
    rb = EXPERT_ROWS
    tk = idx_t.size
    padded = (counts + rb - 1) // rb * rb
    pad_ends = jnp.cumsum(padded)
    pad_starts = pad_ends - padded
    dest = pad_starts[idx_t] + rank_t
    n_blocks = tk // rb + N_EXPERTS
    block_e = jnp.minimum(
        jnp.searchsorted(pad_ends, jnp.arange(n_blocks) * rb, side='right'), N_EXPERTS - 1).astype(jnp.int32)
    n_used = (pad_ends[-1] // rb).astype(jnp.int32).reshape(1)
    return dest, block_e, n_used, n_blocks * rb


def kernel(x, c, rel_bias, w_ada, b_ada, g_pre_mix, g_post_mix, g_pre_ffn, g_post_ffn, w_in, ssm_a_re, ssm_a_im, ssm_log_dt, ssm_b_re, ssm_b_im, ssm_c_re, ssm_c_im, ssm_d, w_glu, w_ssm_out, w_att_out, w_out, w_router, b_router, w_exp_in, b_exp_in, w_exp_out, b_exp_out):
    bsz, seq, d = x.shape
    depth = w_ada.shape[0]
    t = bsz * seq
    assert d == D_MODEL and seq % TOKEN_TILE == 0 and TOKEN_TILE % MOBA_BLOCK == 0 and seq % SSM_CHUNK == 0

    mod = _ada_mod(c, w_ada, b_ada)
    bias_tiles, far_bias = _moba_bias_tiles(rel_bias.astype(F32))
    x2 = x.reshape(t, d)
    for l in range(depth):
        sh1, sc1, gt1, sh2, sc2, gt2 = [m.reshape(bsz, 1, d) for m in jnp.split(mod[l], N_MOD, axis=-1)]
        q4, k4, vt4, u2, ga, gs = _premix(x2, g_pre_mix[l], sc1, sh1, w_in[l], seq)
        att = _moba(q4, k4, vt4, bias_tiles, far_bias, bsz).reshape(t, ATT_WIDTH)
        ssm_w = _ssm_weights(ssm_a_re[l], ssm_a_im[l], ssm_log_dt[l], ssm_b_re[l], ssm_b_im[l],
                             ssm_c_re[l], ssm_c_im[l], ssm_d[l])
        ys = _ssm(u2, ssm_w, bsz, seq)
        x1, h2, idx_t, gate_t, rank_t, cnt = _postmix(
            att, ys, ga, gs, x2, gt1, sc2, sh2, g_post_mix[l], g_pre_ffn[l],
            w_glu[l], w_ssm_out[l], w_att_out[l], w_out[l], w_router[l], b_router[l], seq)
        dest, block_e, n_used, p_rows = _route_plan(idx_t, rank_t, cnt[:, 0])
        tok = jnp.broadcast_to(jnp.arange(t, dtype=jnp.int32)[None, :], dest.shape)
        buf_tok = jnp.zeros((p_rows,), jnp.int32).at[dest.reshape(-1)].set(tok.reshape(-1))
        xb = h2[buf_tok]
        yb = _experts(xb, block_e, n_used, w_exp_in[l], b_exp_in[l], w_exp_out[l], b_exp_out[l])
        y = jnp.sum(yb[dest] * gate_t[..., None], axis=0)
        x2 = _postffn(y, x1, gt2, g_post_ffn[l], seq)
    return x2.reshape(bsz, seq, d)
```

```python
import functools
import math

import jax
import jax.numpy as jnp
from jax import lax
from jax.experimental import pallas as pl
from jax.experimental.pallas import tpu as pltpu

F32 = jnp.float32
BF16 = jnp.bfloat16

D_MODEL = 1024
ATT_HEADS = 8
HEAD_DIM = 64
ATT_WIDTH = ATT_HEADS * HEAD_DIM
MOBA_BLOCK = 256
MOBA_TOPK = 3
REL_BUCKETS = 32
REL_MAX_DIST = 128
SSM_WIDTH = D_MODEL // 2
SSM_GROUP = 16
SSM_GROUPS = SSM_WIDTH // SSM_GROUP
SSM_STATE = 64
N_EXPERTS = 32
TOP_K = 4
D_FF = D_MODEL
SWIGLU_ALPHA = 1.702
SWIGLU_LIMIT = 7.0
RMS_EPS = 1e-6
N_MOD = 6

SSM_CHUNK = 8
SSM_CHUNKS_PER_STEP = 16
LANES = 128
MXU_DIM = 256
TOKEN_TILE = 512
ROUTE_TILE = 256
EXPERT_ROWS = 256
MASK_NEG = -1e30
VMEM_LIMIT = 56 * 1024 * 1024

_NT = (((1,), (1,)), ((), ()))


def _cparams(*sem):
    return pltpu.CompilerParams(dimension_semantics=sem, vmem_limit_bytes=VMEM_LIMIT)


def _rms(x, g):
    return x * lax.rsqrt(jnp.mean(x * x, axis=-1, keepdims=True) + RMS_EPS) * g


def _full(shape, **kw):
    n = len(shape)
    return pl.BlockSpec(shape, lambda *_: (0,) * n, **kw)


def _ada_kernel(c_ref, w_ref, b_ref, o_ref):
    c = c_ref[...]
    cond = c * jax.nn.sigmoid(c)
    o_ref[0] = jnp.dot(cond, w_ref[0], preferred_element_type=F32,
                       precision=lax.Precision.HIGHEST) + b_ref[0]


def _ada_mod(c, w_ada, b_ada):
    depth, d, nd = w_ada.shape
    bsz = c.shape[0]
    return pl.pallas_call(
        _ada_kernel,
        grid=(depth, nd // d),
        in_specs=[
            pl.BlockSpec((bsz, d), lambda l, j: (0, 0)),
            pl.BlockSpec((1, d, d), lambda l, j: (l, 0, j)),
            pl.BlockSpec((1, 1, d), lambda l, j: (l, 0, j)),
        ],
        out_specs=pl.BlockSpec((1, bsz, d), lambda l, j: (l, 0, j)),
        out_shape=jax.ShapeDtypeStruct((depth, bsz, nd), F32),
        compiler_params=_cparams("arbitrary", "arbitrary"),
        name="ada_mod",
    )(c, w_ada, b_ada.reshape(depth, 1, nd))


def _premix_kernel(x_ref, g_ref, sc_ref, sh_ref, wq_ref, wk_ref, wvt_ref, wu_ref, wga_ref, wgs_ref,
                   q_ref, k_ref, vt_ref, u_ref, ga_ref, gs_ref):
    x = x_ref[...]
    h = _rms(x, g_ref[...]) * (1.0 + sc_ref[0]) + sh_ref[0]
    hb = h.astype(BF16)
    nblk = q_ref.shape[0]
    q = (jnp.dot(hb, wq_ref[...], preferred_element_type=F32) * (HEAD_DIM ** -0.5)).astype(BF16)
    k = jnp.dot(hb, wk_ref[...], preferred_element_type=F32).astype(BF16)
    vt = lax.dot_general(wvt_ref[...], hb, _NT, preferred_element_type=F32).astype(BF16)
    for r in range(nblk):
        q_ref[r] = q[r * MOBA_BLOCK:(r + 1) * MOBA_BLOCK]
        k_ref[r] = k[r * MOBA_BLOCK:(r + 1) * MOBA_BLOCK]
        vt_ref[r] = vt[:, r * MOBA_BLOCK:(r + 1) * MOBA_BLOCK]
    u_ref[...] = jnp.dot(hb, wu_ref[...], preferred_element_type=F32).astype(BF16)
    ga = jnp.dot(hb, wga_ref[...], preferred_element_type=F32)
    ga_ref[...] = jax.nn.sigmoid(ga).astype(BF16)
    gs = jnp.dot(hb, wgs_ref[...], preferred_element_type=F32)
    gs_ref[...] = jax.nn.sigmoid(gs).astype(BF16)


def _premix(x2, g, sc, sh, w_in, seq):
    t, d = x2.shape
    tm = TOKEN_TILE
    tiles_per_seq = seq // tm
    nblk = tm // MOBA_BLOCK
    a = ATT_WIDTH
    wb = w_in.astype(BF16)
    wq, wk, wv = wb[:, :a], wb[:, a:2 * a], wb[:, 2 * a:3 * a]
    wu = wb[:, 3 * a:3 * a + SSM_WIDTH]
    wga = wb[:, 3 * a + SSM_WIDTH:3 * a + SSM_WIDTH + d]
    wgs = wb[:, 3 * a + SSM_WIDTH + d:]
    mod_spec = pl.BlockSpec((1, 1, d), lambda i: (i // tiles_per_seq, 0, 0))
    blk3 = lambda rows, cols: pl.BlockSpec((nblk, rows, cols), lambda i: (i, 0, 0))
    row = lambda cols: pl.BlockSpec((tm, cols), lambda i: (i, 0))
    time_major = pl.BlockSpec((tm, SSM_WIDTH), lambda i: (i % tiles_per_seq, i // tiles_per_seq))
    nb_total = t // MOBA_BLOCK
    return pl.pallas_call(
        _premix_kernel,
        grid=(t // tm,),
        in_specs=[row(d), _full((1, d)), mod_spec, mod_spec,
                  _full(wq.shape), _full(wk.shape), _full((a, d)), _full(wu.shape),
                  _full(wga.shape), _full(wgs.shape)],
        out_specs=[blk3(MOBA_BLOCK, a), blk3(MOBA_BLOCK, a), blk3(a, MOBA_BLOCK),
                   time_major, row(d), row(d)],
        out_shape=[jax.ShapeDtypeStruct((nb_total, MOBA_BLOCK, a), BF16),
                   jax.ShapeDtypeStruct((nb_total, MOBA_BLOCK, a), BF16),
                   jax.ShapeDtypeStruct((nb_total, a, MOBA_BLOCK), BF16),
                   jax.ShapeDtypeStruct((seq, (t // seq) * SSM_WIDTH), BF16),
                   jax.ShapeDtypeStruct((t, d), BF16),
                   jax.ShapeDtypeStruct((t, d), BF16)],
        compiler_params=_cparams("arbitrary"),
        name="premix_inproj",
    )(x2, g.reshape(1, d), sc, sh, wq, wk, wv.T, wu, wga, wgs)


def _moba_kernel(far_ref, q_ref, k_ref, vt_ref, bias_ref, o_ref, kmean_ref):
    nb = k_ref.shape[0]
    blk = MOBA_BLOCK
    i = pl.program_id(1)

    @pl.when(i == 0)
    def _():
        for n in range(nb):
            kmean_ref[n:n + 1, :] = jnp.mean(k_ref[n].astype(F32), axis=0, keepdims=True)

    lane = lax.broadcasted_iota(jnp.int32, (blk, 2 * HEAD_DIM), 1)
    nidx = lax.broadcasted_iota(jnp.int32, (nb, blk), 0)
    valid = nidx < i
    for hp in range(ATT_HEADS // 2):
        cols = slice(hp * 2 * HEAD_DIM, (hp + 1) * 2 * HEAD_DIM)
        q2 = q_ref[0, :, cols]
        km = kmean_ref[:, cols]
        km_hi = km.astype(BF16)
        km_lo = (km - km_hi.astype(F32)).astype(BF16)
        outs = []
        for hh in range(2):
            head = hp * 2 + hh
            in_head = (lane >= hh * HEAD_DIM) & (lane < (hh + 1) * HEAD_DIM)
            qh = jnp.where(in_head, q2, jnp.zeros_like(q2))
            gate = (lax.dot_general(km_hi, qh, _NT, preferred_element_type=F32)
                    + lax.dot_general(km_lo, qh, _NT, preferred_element_type=F32))
            g = jnp.where(valid, gate, -jnp.inf)
            sel = jnp.zeros((nb, blk), jnp.bool_)
            for _ in range(MOBA_TOPK):
                mx = jnp.max(g, axis=0, keepdims=True)
                first = jnp.min(jnp.where(g == mx, nidx, nb), axis=0, keepdims=True)
                pick = nidx == first
                sel = sel | pick
                g = jnp.where(pick, -jnp.inf, g)
            neg = jnp.where(sel & valid, 0.0, MASK_NEG)

            def block(j, add, carry, head=head, qh=qh, cols=cols):
                m, l, acc = carry
                kb = k_ref[j, :, cols]
                st = lax.dot_general(kb, qh, _NT, preferred_element_type=F32) + add
                m_new = jnp.maximum(m, jnp.max(st, axis=0, keepdims=True))
                alpha = jnp.exp(m - m_new)
                p = jnp.exp(st - m_new)
                l = alpha * l + jnp.sum(p, axis=0, keepdims=True)
                vb = vt_ref[j, head * HEAD_DIM:(head + 1) * HEAD_DIM, :]
                acc = alpha * acc + jnp.dot(vb, p.astype(BF16), preferred_element_type=F32)
                return m_new, l, acc

            def neg_row(j, neg=neg):
                return jnp.sum(jnp.where(nidx == j, neg, 0.0), axis=0, keepdims=True)

            carry = (jnp.full((1, blk), MASK_NEG, F32), jnp.zeros((1, blk), F32),
                     jnp.zeros((HEAD_DIM, blk), F32))
            carry = block(i, bias_ref[head, 0], carry)
            prev_add = bias_ref[head, 1] + neg_row(i - 1) + jnp.where(i == 0, MASK_NEG, 0.0)
            carry = block(jnp.maximum(i - 1, 0), prev_add, carry)
            far = far_ref[head]
            carry = lax.fori_loop(
                0, jnp.maximum(i - 1, 0),
                lambda j, c, far=far, block=block, neg_row=neg_row: block(j, neg_row(j) + far, c),
                carry)
            _, l, acc = carry
            outs.append(acc / l)
        pair = jnp.concatenate(outs, axis=0)
        o_ref[0, :, cols] = pair.T.astype(BF16)


def _rel_bucket(dist):
    n = jnp.maximum(dist, 0)
    max_exact = REL_BUCKETS // 2
    nf = jnp.maximum(n, 1).astype(F32)
    large = max_exact + (jnp.log(nf / max_exact) / math.log(REL_MAX_DIST / max_exact)
                         * (REL_BUCKETS - max_exact)).astype(jnp.int32)
    large = jnp.minimum(large, REL_BUCKETS - 1)
    return jnp.where(n < max_exact, n, large)


def _moba_bias_tiles(rel_bias):
    blk = MOBA_BLOCK
    kj = jnp.arange(blk)[:, None]
    qi = jnp.arange(blk)[None, :]
    d_own = qi - kj
    own = rel_bias[_rel_bucket(d_own)]
    own = jnp.where((d_own >= 0)[..., None], own, MASK_NEG)
    prev = rel_bias[_rel_bucket(d_own + blk)]
    tiles = jnp.stack([own, prev], axis=0)
    far = rel_bias[_rel_bucket(jnp.array(2 * blk))]
    return tiles.transpose(3, 0, 1, 2).astype(F32), far.astype(F32)


def _moba(q4, k4, vt4, bias_tiles, far_bias, bsz):
    nb_total, blk, a = q4.shape
    nb = nb_total // bsz
    return pl.pallas_call(
        _moba_kernel,
        grid=(bsz, nb),
        in_specs=[
            pl.BlockSpec(memory_space=pltpu.SMEM),
            pl.BlockSpec((1, blk, a), lambda b, i: (b * nb + i, 0, 0)),
            pl.BlockSpec((nb, blk, a), lambda b, i: (b, 0, 0)),
            pl.BlockSpec((nb, a, blk), lambda b, i: (b, 0, 0)),
            _full(bias_tiles.shape),
        ],
        out_specs=pl.BlockSpec((1, blk, a), lambda b, i: (b * nb + i, 0, 0)),
        out_shape=jax.ShapeDtypeStruct((nb_total, blk, a), BF16),
        scratch_shapes=[pltpu.VMEM((nb, a), F32)],
        compiler_params=_cparams("arbitrary", "arbitrary"),
        name="moba_attention",
    )(far_bias, q4, k4, vt4, bias_tiles)


def _ssm_weights(a_re, a_im, log_dt, b_re, b_im, c_re, c_im, d_skip):
    g, p, cw, n = SSM_GROUPS, SSM_STATE, SSM_GROUP, SSM_CHUNK
    gs = LANES // cw
    nq = g // gs
    npair = n // 2
    pairs_per_set = gs // 2
    lam_re = jnp.minimum(a_re.astype(F32), -1e-4)
    lam_im = a_im.astype(F32)
    dt = jnp.exp(log_dt.astype(F32))[:, None]
    mag = jnp.exp(lam_re * dt)
    lb_re = mag * jnp.cos(lam_im * dt)
    lb_im = mag * jnp.sin(lam_im * dt)
    n_re = lb_re - 1.0
    n_im = lb_im
    den = lam_re * lam_re + lam_im * lam_im
    z_re = ((n_re * lam_re + n_im * lam_im) / den)[..., None]
    z_im = ((n_im * lam_re - n_re * lam_im) / den)[..., None]
    br, bi = b_re.astype(F32), b_im.astype(F32)
    bb_re = z_re * br - z_im * bi
    bb_im = z_re * bi + z_im * br
    pw_re, pw_im = [jnp.ones_like(lb_re)], [jnp.zeros_like(lb_im)]
    for _ in range(n):
        r, im = pw_re[-1], pw_im[-1]
        pw_re.append(r * lb_re - im * lb_im)
        pw_im.append(r * lb_im + im * lb_re)
    pw_re, pw_im = jnp.stack(pw_re), jnp.stack(pw_im)
    cr, ci = c_re.astype(F32), c_im.astype(F32)
    hi = lax.Precision.HIGHEST
    rev_re, rev_im = pw_re[n - 1::-1], pw_im[n - 1::-1]
    sb_re = jnp.einsum('sgp,gpc->sgcp', rev_re, bb_re) - jnp.einsum('sgp,gpc->sgcp', rev_im, bb_im)
    sb_im = jnp.einsum('sgp,gpc->sgcp', rev_re, bb_im) + jnp.einsum('sgp,gpc->sgcp', rev_im, bb_re)
    cl_re = jnp.einsum('gcp,tgp->tgpc', cr, pw_re[1:]) - jnp.einsum('gcp,tgp->tgpc', ci, pw_im[1:])
    cl_im = jnp.einsum('gcp,tgp->tgpc', cr, pw_im[1:]) + jnp.einsum('gcp,tgp->tgpc', ci, pw_re[1:])
    cp_re = jnp.einsum('gcp,tgp->tgcp', cr, pw_re[:n]) - jnp.einsum('gcp,tgp->tgcp', ci, pw_im[:n])
    cp_im = jnp.einsum('gcp,tgp->tgcp', cr, pw_im[:n]) + jnp.einsum('gcp,tgp->tgcp', ci, pw_re[:n])
    kern = (jnp.einsum('tgcp,gpd->gtcd', cp_re, bb_re, precision=hi)
            - jnp.einsum('tgcp,gpd->gtcd', cp_im, bb_im, precision=hi))

    kp = jnp.concatenate([jnp.zeros_like(kern[:, :1]), kern], axis=1)
    dl = jnp.arange(npair)[:, None, None]
    s0 = jnp.arange(2)[None, :, None]
    t0 = jnp.arange(2)[None, None, :]
    lag = 2 * dl + t0 - s0
    kl = kp[:, lag + 1].reshape(nq, gs, npair, 2, 2, cw, cw)
    base = kl.transpose(0, 2, 3, 1, 6, 4, 5)
    eye = jnp.eye(gs, dtype=F32)
    w_toe = (base[:, :, :, :, :, :, None, :] * eye[None, None, None, :, None, None, :, None])
    w_toe = w_toe.reshape(nq, npair, MXU_DIM, MXU_DIM).astype(BF16)

    member = (jnp.arange(gs)[None, :, None]
              == 2 * jnp.arange(pairs_per_set)[:, None, None] + jnp.arange(2)[None, None, :]).astype(F32)
    sbs = jnp.stack([sb_re, sb_im]).reshape(2, npair, 2, nq, gs, cw, p)
    sbs = sbs.transpose(3, 1, 2, 4, 5, 0, 6)
    w_in_state = (sbs[:, None, :, :, :, :, :, None, :]
                  * member[None, :, None, None, :, None, None, :, None])
    w_in_state = w_in_state.reshape(g // 2, npair, MXU_DIM, MXU_DIM).astype(BF16)
    sos = jnp.stack([cl_re, -cl_im]).reshape(2, npair, 2, nq, gs, p, cw)
    sos = sos.transpose(3, 1, 0, 5, 2, 4, 6)
    member_t = member.transpose(0, 2, 1)
    w_state_out = (sos[:, None, :, :, None, :, :, :, :]
                   * member_t[None, :, None, None, :, None, None, :, None])
    w_state_out = w_state_out.reshape(g // 2, npair, MXU_DIM, MXU_DIM).astype(BF16)
    decay = jnp.stack([pw_re[n].reshape(g // 2, 2 * p), pw_im[n].reshape(g // 2, 2 * p)])
    dvec = d_skip.astype(F32).reshape(1, g * cw)
    return w_toe, w_in_state, w_state_out, decay, dvec


def _ssm_kernel(u_ref, wt_ref, wi_ref, wo_ref, dec_ref, d_ref, y_ref, s_ref, xp_ref, x_ref, *, bsz):
    kt = u_ref.shape[0]
    n = u_ref.shape[1] // bsz
    rows = kt * bsz
    npair = n // 2
    nq = wt_ref.shape[0]
    npairs_g = wi_ref.shape[0]
    per_set = npairs_g // nq
    half = MXU_DIM // 2

    @pl.when(pl.program_id(0) == 0)
    def _():
        x_ref[...] = jnp.zeros_like(x_ref)

    def piece(s, q):
        return u_ref[:, s * bsz:(s + 1) * bsz, q * LANES:(q + 1) * LANES].reshape(rows, LANES)

    lhs = {(sp, q): jnp.concatenate([piece(2 * sp, q), piece(2 * sp + 1, q)], axis=-1)
           for sp in range(npair) for q in range(nq)}

    for gp in range(npairs_g):
        q = gp // per_set
        acc = jnp.dot(lhs[0, q], wi_ref[gp, 0], preferred_element_type=F32)
        for sp in range(1, npair):
            acc = acc + jnp.dot(lhs[sp, q], wi_ref[gp, sp], preferred_element_type=F32)
        s_ref[:, gp * MXU_DIM:(gp + 1) * MXU_DIM] = acc

    for gp in range(npairs_g):
        re_cols = slice(gp * MXU_DIM, gp * MXU_DIM + half)
        im_cols = slice(gp * MXU_DIM + half, (gp + 1) * MXU_DIM)
        ar = dec_ref[0, gp:gp + 1, :]
        ai = dec_ref[1, gp:gp + 1, :]
        xr = x_ref[:, re_cols]
        xi = x_ref[:, im_cols]
        for kc in range(kt):
            rs = slice(kc * bsz, (kc + 1) * bsz)
            xp_ref[rs, re_cols] = xr.astype(BF16)
            xp_ref[rs, im_cols] = xi.astype(BF16)
            sr = s_ref[rs, re_cols]
            si = s_ref[rs, im_cols]
            xr, xi = ar * xr - ai * xi + sr, ar * xi + ai * xr + si
        x_ref[:, re_cols] = xr
        x_ref[:, im_cols] = xi

    for q in range(nq):
        for tp in range(npair):
            acc = jnp.dot(lhs[0, q], wt_ref[q, tp], preferred_element_type=F32)
            for sp in range(1, tp + 1):
                acc = acc + jnp.dot(lhs[sp, q], wt_ref[q, tp - sp], preferred_element_type=F32)
            for gp in range(q * per_set, (q + 1) * per_set):
                acc = acc + jnp.dot(xp_ref[:, gp * MXU_DIM:(gp + 1) * MXU_DIM], wo_ref[gp, tp],
                                    preferred_element_type=F32)
            for t0 in range(2):
                s = 2 * tp + t0
                y = acc[:, t0 * half:(t0 + 1) * half] + d_ref[:, q * LANES:(q + 1) * LANES] * piece(s, q).astype(F32)
                y_ref[:, s * bsz:(s + 1) * bsz, q * LANES:(q + 1) * LANES] = (
                    jax.nn.gelu(y).astype(BF16).reshape(kt, bsz, LANES))


def _ssm(u_tm, weights, bsz, seq):
    w_toe, w_in_state, w_state_out, decay, dvec = weights
    n = SSM_CHUNK
    kt = SSM_CHUNKS_PER_STEP
    nchunk = seq // n
    rows = kt * bsz
    state_w = SSM_GROUPS * 2 * SSM_STATE
    u3 = u_tm.reshape(nchunk, n * bsz, SSM_WIDTH)
    once = pl.Buffered(1)
    blk = pl.BlockSpec((kt, n * bsz, SSM_WIDTH), lambda i: (i, 0, 0))
    y3 = pl.pallas_call(
        functools.partial(_ssm_kernel, bsz=bsz),
        grid=(nchunk // kt,),
        in_specs=[blk, _full(w_toe.shape, pipeline_mode=once), _full(w_in_state.shape, pipeline_mode=once),
                  _full(w_state_out.shape, pipeline_mode=once), _full(decay.shape), _full(dvec.shape)],
        out_specs=blk,
        out_shape=jax.ShapeDtypeStruct(u3.shape, BF16),
        scratch_shapes=[pltpu.VMEM((rows, state_w), F32), pltpu.VMEM((rows, state_w), BF16),
                        pltpu.VMEM((bsz, state_w), F32)],
        compiler_params=_cparams("arbitrary"),
        name="s5_chunked_scan",
    )(u3, w_toe, w_in_state, w_state_out, decay, dvec)
    return y3.reshape(seq, bsz * SSM_WIDTH)


def _postmix_kernel(att_ref, ys_ref, ga_ref, gs_ref, x_ref, gt_ref, sc_ref, sh_ref, gpost_ref, gpre_ref,
                    wglu_ref, wso_ref, wao_ref, wo_ref, wrh_ref, wrl_ref, br_ref,
                    x1_ref, h2_ref, idx_ref, gate_ref, rank_ref, cnt_ref, run_ref):
    tm = x_ref.shape[0]
    ne = wrh_ref.shape[0]

    @pl.when(pl.program_id(0) == 0)
    def _():
        run_ref[...] = jnp.zeros_like(run_ref)

    glu = jnp.dot(ys_ref[...], wglu_ref[...], preferred_element_type=F32)
    sg = (glu[:, :SSM_WIDTH] * jax.nn.sigmoid(glu[:, SSM_WIDTH:])).astype(BF16)
    ssm = jnp.dot(sg, wso_ref[...], preferred_element_type=F32)
    att = jnp.dot(att_ref[...], wao_ref[...], preferred_element_type=F32)
    merged = (ga_ref[...].astype(F32) * att + gs_ref[...].astype(F32) * ssm).astype(BF16)
    y = jnp.dot(merged, wo_ref[...], preferred_element_type=F32)
    x1 = x_ref[...] + gt_ref[0] * _rms(y, gpost_ref[...])
    x1_ref[...] = x1
    h2 = _rms(x1, gpre_ref[...]) * (1.0 + sc_ref[0]) + sh_ref[0]
    h2_ref[...] = h2
    h_hi = h2.astype(BF16)
    h_lo = (h2 - h_hi.astype(F32)).astype(BF16)
    logits = (lax.dot_general(wrh_ref[...], h_hi, _NT, preferred_element_type=F32)
              + lax.dot_general(wrh_ref[...], h_lo, _NT, preferred_element_type=F32)
              + lax.dot_general(wrl_ref[...], h_hi, _NT, preferred_element_type=F32)
              + br_ref[...])
    eidx = lax.broadcasted_iota(jnp.int32, (ne, tm), 0)
    lg = logits
    vals, idxs = [], []
    for _ in range(TOP_K):
        mx = jnp.max(lg, axis=0, keepdims=True)
        first = jnp.min(jnp.where(lg == mx, eidx, ne), axis=0, keepdims=True)
        vals.append(mx)
        idxs.append(first)
        lg = jnp.where(eidx == first, -jnp.inf, lg)
    exps = [jnp.exp(v - vals[0]) for v in vals]
    denom = exps[0] + exps[1] + exps[2] + exps[3]
    gate_ref[...] = jnp.concatenate([e / denom for e in exps], axis=0)
    idx_ref[...] = jnp.concatenate(idxs, axis=0)
    onehot = jnp.where(lg == -jnp.inf, 1.0, 0.0)
    t_row = lax.broadcasted_iota(jnp.int32, (tm, tm), 0)
    t_col = lax.broadcasted_iota(jnp.int32, (tm, tm), 1)
    earlier = jnp.where(t_row < t_col, 1.0, 0.0).astype(BF16)
    before = jnp.dot(onehot.astype(BF16), earlier, preferred_element_type=F32) + run_ref[:, 0:1]
    ranks = [jnp.sum(jnp.where(eidx == ix, before, 0.0), axis=0, keepdims=True) for ix in idxs]
    rank_ref[...] = jnp.concatenate(ranks, axis=0).astype(jnp.int32)
    run_ref[...] = run_ref[...] + jnp.sum(onehot, axis=1, keepdims=True)
    cnt_ref[...] = run_ref[...].astype(jnp.int32)


def _postmix(att2, ys_tm, ga, gs, x2, gt1, sc2, sh2, g_post, g_pre, w_glu, w_ssm_out, w_att_out, w_out,
             w_router, b_router, seq):
    t, d = x2.shape
    tm = TOKEN_TILE
    tiles_per_seq = seq // tm
    ne = w_router.shape[1]
    wr_t = w_router.T.astype(F32)
    wr_hi = wr_t.astype(BF16)
    wr_lo = (wr_t - wr_hi.astype(F32)).astype(BF16)
    mod_spec = pl.BlockSpec((1, 1, d), lambda i: (i // tiles_per_seq, 0, 0))
    row = lambda cols: pl.BlockSpec((tm, cols), lambda i: (i, 0))
    time_major = pl.BlockSpec((tm, SSM_WIDTH), lambda i: (i % tiles_per_seq, i // tiles_per_seq))
    col = pl.BlockSpec((TOP_K, tm), lambda i: (0, i))
    weights = [w_glu.astype(BF16), w_ssm_out.astype(BF16), w_att_out.astype(BF16), w_out.astype(BF16),
               wr_hi, wr_lo, b_router.astype(F32).reshape(ne, 1)]
    return pl.pallas_call(
        _postmix_kernel,
        grid=(t // tm,),
        in_specs=[row(ATT_WIDTH), time_major, row(d), row(d), row(d), mod_spec, mod_spec, mod_spec,
                  _full((1, d)), _full((1, d))] + [_full(w.shape) for w in weights],
        out_specs=[row(d), row(d), col, col, col, _full((ne, 128))],
        out_shape=[jax.ShapeDtypeStruct((t, d), F32), jax.ShapeDtypeStruct((t, d), F32),
                   jax.ShapeDtypeStruct((TOP_K, t), jnp.int32), jax.ShapeDtypeStruct((TOP_K, t), F32),
                   jax.ShapeDtypeStruct((TOP_K, t), jnp.int32), jax.ShapeDtypeStruct((ne, 128), jnp.int32)],
        scratch_shapes=[pltpu.VMEM((ne, 128), F32)],
        compiler_params=_cparams("arbitrary"),
        name="postmix_router",
    )(att2, ys_tm, ga, gs, x2, gt1, sc2, sh2, g_post.reshape(1, d), g_pre.reshape(1, d), *weights)


def _row_copies(src_row, dst_row, sem, dest_ref, tm):
    def body(t, carry):
        for k in range(TOP_K):
            r = dest_ref[k, t]
            pltpu.make_async_copy(src_row(t, k, r), dst_row(t, k, r), sem).start()
        return carry
    lax.fori_loop(0, tm, body, 0)


def _dispatch_kernel(ends_ref, dest_ref, h_ref, xb_ref, zero_ref, sem, zsem):
    tm = h_ref.shape[0]
    rb = zero_ref.shape[0]

    @pl.when(pl.program_id(0) == 0)
    def _():
        zero_ref[...] = jnp.zeros_like(zero_ref)

        def tail(e):
            start = pl.multiple_of(ends_ref[e] - rb, rb)
            return pltpu.make_async_copy(zero_ref, xb_ref.at[pl.ds(start, rb), :], zsem)

        for e in range(ends_ref.shape[0]):
            pl.when(ends_ref[e] > 0)(lambda e=e: tail(e).start())
        for e in range(ends_ref.shape[0]):
            pl.when(ends_ref[e] > 0)(lambda e=e: tail(e).wait())

        def unused(b):
            return pltpu.make_async_copy(zero_ref, xb_ref.at[pl.ds(pl.multiple_of(b * rb, rb), rb), :], zsem)

        first_unused = ends_ref[ends_ref.shape[0] - 1] // rb
        n_blocks = xb_ref.shape[0] // rb
        lax.fori_loop(first_unused, n_blocks, lambda b, c: (unused(b).start(), c)[1], 0)
        lax.fori_loop(first_unused, n_blocks, lambda b, c: (unused(b).wait(), c)[1], 0)

    _row_copies(lambda t, k, r: h_ref.at[pl.ds(t, 1), :],
                lambda t, k, r: xb_ref.at[pl.ds(r, 1), :], sem, dest_ref, tm)
    for k in range(TOP_K):
        pltpu.make_async_copy(h_ref, xb_ref.at[pl.ds(0, tm), :], sem).wait()


def _dispatch(h2, dest_tiles, pad_ends, p_rows):
    t, d = h2.shape
    tm = ROUTE_TILE
    return pl.pallas_call(
        _dispatch_kernel,
        grid=(t // tm,),
        in_specs=[pl.BlockSpec(memory_space=pltpu.SMEM),
                  pl.BlockSpec((None, TOP_K, tm), lambda i: (i, 0, 0), memory_space=pltpu.SMEM),
                  pl.BlockSpec((tm, d), lambda i: (i, 0))],
        out_specs=pl.BlockSpec(memory_space=pl.ANY),
        out_shape=jax.ShapeDtypeStruct((p_rows, d), h2.dtype),
        scratch_shapes=[pltpu.VMEM((EXPERT_ROWS, d), h2.dtype), pltpu.SemaphoreType.DMA(()),
                        pltpu.SemaphoreType.DMA(())],
        compiler_params=_cparams("arbitrary"),
        name="expert_dispatch",
    )(pad_ends, dest_tiles, h2)


def _experts_kernel(be_ref, nused_ref, x_ref, w1_ref, b1_ref, w2_ref, b2_ref, y_ref, w1b_ref, w2b_ref):
    i = pl.program_id(0)
    prev = be_ref[jnp.maximum(i - 1, 0)]
    fresh = (i == 0) | (be_ref[i] != prev)

    @pl.when(fresh)
    def _():
        w1b_ref[...] = w1_ref[...].astype(BF16)
        w2b_ref[...] = w2_ref[...].astype(BF16)

    @pl.when(i < nused_ref[0])
    def _():
        gu = jnp.dot(x_ref[...].astype(BF16), w1b_ref[...], preferred_element_type=F32) + b1_ref[...]
        g = jnp.minimum(gu[:, :D_FF], SWIGLU_LIMIT)
        up = jnp.clip(gu[:, D_FF:], -SWIGLU_LIMIT, SWIGLU_LIMIT)
        act = ((up + 1.0) * g * jax.nn.sigmoid(SWIGLU_ALPHA * g)).astype(BF16)
        y_ref[...] = jnp.dot(act, w2b_ref[...], preferred_element_type=F32) + b2_ref[...]

    @pl.when(i >= nused_ref[0])
    def _():
        y_ref[...] = jnp.zeros_like(y_ref)


def _experts(xb, block_e, n_used, w1, b1, w2, b2, layer):
    p_rows, d = xb.shape
    depth, ne = w1.shape[:2]
    rb = EXPERT_ROWS
    wmap = lambda i, be, nu: (layer, be[i], 0, 0)
    grid_spec = pltpu.PrefetchScalarGridSpec(
        num_scalar_prefetch=2,
        grid=(p_rows // rb,),
        in_specs=[
            pl.BlockSpec((rb, d), lambda i, be, nu: (i, 0)),
            pl.BlockSpec((None, None, d, 2 * D_FF), wmap),
            pl.BlockSpec((None, None, 1, 2 * D_FF), wmap),
            pl.BlockSpec((None, None, D_FF, d), wmap),
            pl.BlockSpec((None, None, 1, d), wmap),
        ],
        out_specs=pl.BlockSpec((rb, d), lambda i, be, nu: (i, 0)),
        scratch_shapes=[pltpu.VMEM((d, 2 * D_FF), BF16), pltpu.VMEM((D_FF, d), BF16)],
    )
    return pl.pallas_call(
        _experts_kernel,
        grid_spec=grid_spec,
        out_shape=jax.ShapeDtypeStruct((p_rows, d), F32),
        compiler_params=_cparams("arbitrary"),
        name="expert_ffn",
    )(block_e, n_used, xb, w1, b1.reshape(depth, ne, 1, 2 * D_FF), w2, b2.reshape(depth, ne, 1, d))


def _combine_kernel(dest_ref, gate_ref, x_ref, gt_ref, g_ref, yb_ref, o_ref, buf_ref, sem):
    tm = x_ref.shape[0]
    _row_copies(lambda t, k, r: yb_ref.at[pl.ds(r, 1), :],
                lambda t, k, r: buf_ref.at[k, pl.ds(t, 1), :], sem, dest_ref, tm)
    gates = gate_ref[...]
    gates = jnp.concatenate([gates, jnp.zeros((LANES - TOP_K, tm), F32)], axis=0).T
    for k in range(TOP_K):
        pltpu.make_async_copy(yb_ref.at[pl.ds(0, tm), :], buf_ref.at[k], sem).wait()
    y = gates[:, 0:1] * buf_ref[0]
    for k in range(1, TOP_K):
        y = y + gates[:, k:k + 1] * buf_ref[k]
    o_ref[...] = x_ref[...] + gt_ref[0] * _rms(y, g_ref[...])


def _combine(yb, dest_tiles, gate_t, x2, gt2, g_post, seq):
    t, d = x2.shape
    tm = ROUTE_TILE
    tiles_per_seq = seq // tm
    row = pl.BlockSpec((tm, d), lambda i: (i, 0))
    return pl.pallas_call(
        _combine_kernel,
        grid=(t // tm,),
        in_specs=[pl.BlockSpec((None, TOP_K, tm), lambda i: (i, 0, 0), memory_space=pltpu.SMEM),
                  pl.BlockSpec((TOP_K, tm), lambda i: (0, i)),
                  row, pl.BlockSpec((1, 1, d), lambda i: (i // tiles_per_seq, 0, 0)), _full((1, d)),
                  pl.BlockSpec(memory_space=pl.ANY)],
        out_specs=row,
        out_shape=jax.ShapeDtypeStruct((t, d), F32),
        scratch_shapes=[pltpu.VMEM((TOP_K, tm, d), F32), pltpu.SemaphoreType.DMA(())],
        compiler_params=_cparams("arbitrary"),
        name="expert_combine",
    )(dest_tiles, gate_t, x2, gt2, g_post.reshape(1, d), yb)


def _route_plan(idx_t, rank_t, counts):
    rb = EXPERT_ROWS
    k, t = idx_t.shape
    padded = (counts + rb - 1) // rb * rb
    pad_ends = jnp.cumsum(padded)
    pad_starts = pad_ends - padded
    experts = jnp.arange(N_EXPERTS, dtype=jnp.int32)
    start_of = jnp.sum(jnp.where(idx_t[None] == experts[:, None, None], pad_starts[:, None, None], 0), axis=0)
    dest = (start_of + rank_t).astype(jnp.int32)
    n_blocks = (k * t) // rb + N_EXPERTS
    blk_start = jnp.arange(n_blocks, dtype=jnp.int32) * rb
    block_e = jnp.minimum(jnp.sum(pad_ends[None, :] <= blk_start[:, None], axis=1), N_EXPERTS - 1)
    n_used = (pad_ends[-1] // rb).astype(jnp.int32).reshape(1)
    dest_tiles = dest.reshape(k, t // ROUTE_TILE, ROUTE_TILE).transpose(1, 0, 2)
    return dest_tiles, block_e.astype(jnp.int32), n_used, pad_ends.astype(jnp.int32), n_blocks * rb


def kernel(x, c, rel_bias, w_ada, b_ada, g_pre_mix, g_post_mix, g_pre_ffn, g_post_ffn, w_in, ssm_a_re, ssm_a_im, ssm_log_dt, ssm_b_re, ssm_b_im, ssm_c_re, ssm_c_im, ssm_d, w_glu, w_ssm_out, w_att_out, w_out, w_router, b_router, w_exp_in, b_exp_in, w_exp_out, b_exp_out):
    bsz, seq, d = x.shape
    depth = w_ada.shape[0]
    t = bsz * seq
    assert d == D_MODEL and seq % TOKEN_TILE == 0 and TOKEN_TILE % MOBA_BLOCK == 0
    assert seq % (SSM_CHUNK * SSM_CHUNKS_PER_STEP) == 0 and seq % ROUTE_TILE == 0

    mod = _ada_mod(c, w_ada, b_ada)
    bias_tiles, far_bias = _moba_bias_tiles(rel_bias.astype(F32))
    x2 = x.reshape(t, d)
    for l in range(depth):
        sh1, sc1, gt1, sh2, sc2, gt2 = [m.reshape(bsz, 1, d) for m in jnp.split(mod[l], N_MOD, axis=-1)]
        q4, k4, vt4, u_tm, ga, gs = _premix(x2, g_pre_mix[l], sc1, sh1, w_in[l], seq)
        att = _moba(q4, k4, vt4, bias_tiles, far_bias, bsz).reshape(t, ATT_WIDTH)
        ssm_w = _ssm_weights(ssm_a_re[l], ssm_a_im[l], ssm_log_dt[l], ssm_b_re[l], ssm_b_im[l],
                             ssm_c_re[l], ssm_c_im[l], ssm_d[l])
        ys_tm = _ssm(u_tm, ssm_w, bsz, seq)
        x1, h2, idx_t, gate_t, rank_t, cnt = _postmix(
            att, ys_tm, ga, gs, x2, gt1, sc2, sh2, g_post_mix[l], g_pre_ffn[l],
            w_glu[l], w_ssm_out[l], w_att_out[l], w_out[l], w_router[l], b_router[l], seq)
        dest_tiles, block_e, n_used, pad_ends, p_rows = _route_plan(idx_t, rank_t, cnt[:, 0])
        xb = _dispatch(h2, dest_tiles, pad_ends, p_rows)
        yb = _experts(xb, block_e, n_used, w_exp_in, b_exp_in, w_exp_out, b_exp_out, l)
        x2 = _combine(yb, dest_tiles, gate_t, x1, gt2, g_post_ffn[l], seq)
    return x2.reshape(bsz, seq, d)
```

```python
import functools
import math

import jax
import jax.numpy as jnp
from jax import lax
from jax.experimental import pallas as pl
from jax.experimental.pallas import tpu as pltpu
from jax.experimental.pallas import tpu_sc as plsc

F32 = jnp.float32
BF16 = jnp.bfloat16

D_MODEL = 1024
ATT_HEADS = 8
HEAD_DIM = 64
ATT_WIDTH = ATT_HEADS * HEAD_DIM
MOBA_BLOCK = 256
MOBA_TOPK = 3
REL_BUCKETS = 32
REL_MAX_DIST = 128
SSM_WIDTH = D_MODEL // 2
SSM_GROUP = 16
SSM_GROUPS = SSM_WIDTH // SSM_GROUP
SSM_STATE = 64
N_EXPERTS = 32
TOP_K = 4
D_FF = D_MODEL
SWIGLU_ALPHA = 1.702
SWIGLU_LIMIT = 7.0
RMS_EPS = 1e-6
N_MOD = 6

SSM_CHUNK = 8
SSM_CHUNKS_PER_STEP = 16
LANES = 128
MXU_DIM = 256
TOKEN_TILE = 512
ROUTE_TILE = 256
ROW_COPY_UNROLL = 4
SC_CORES = 2
SC_SUBCORES = 16
SC_GATHER_ROWS = 32
EXPERT_ROWS = 256
MASK_NEG = -1e30
VMEM_LIMIT = 56 * 1024 * 1024

_NT = (((1,), (1,)), ((), ()))


def _cparams(*sem):
    return pltpu.CompilerParams(dimension_semantics=sem, vmem_limit_bytes=VMEM_LIMIT)


def _rms(x, g):
    return x * lax.rsqrt(jnp.mean(x * x, axis=-1, keepdims=True) + RMS_EPS) * g


def _full(shape, **kw):
    n = len(shape)
    return pl.BlockSpec(shape, lambda *_: (0,) * n, **kw)


def _ada_kernel(c_ref, w_ref, b_ref, o_ref):
    c = c_ref[...]
    cond = c * jax.nn.sigmoid(c)
    o_ref[0] = jnp.dot(cond, w_ref[0], preferred_element_type=F32,
                       precision=lax.Precision.HIGHEST) + b_ref[0]


def _ada_mod(c, w_ada, b_ada):
    depth, d, nd = w_ada.shape
    bsz = c.shape[0]
    return pl.pallas_call(
        _ada_kernel,
        grid=(depth, nd // d),
        in_specs=[
            pl.BlockSpec((bsz, d), lambda l, j: (0, 0)),
            pl.BlockSpec((1, d, d), lambda l, j: (l, 0, j)),
            pl.BlockSpec((1, 1, d), lambda l, j: (l, 0, j)),
        ],
        out_specs=pl.BlockSpec((1, bsz, d), lambda l, j: (l, 0, j)),
        out_shape=jax.ShapeDtypeStruct((depth, bsz, nd), F32),
        compiler_params=_cparams("arbitrary", "arbitrary"),
        name="ada_mod",
    )(c, w_ada, b_ada.reshape(depth, 1, nd))


def _premix_kernel(x_ref, g_ref, sc_ref, sh_ref, wq_ref, wk_ref, wvt_ref, wu_ref, wga_ref, wgs_ref,
                   q_ref, k_ref, vt_ref, u_ref, ga_ref, gs_ref):
    x = x_ref[...]
    h = _rms(x, g_ref[...]) * (1.0 + sc_ref[0]) + sh_ref[0]
    hb = h.astype(BF16)
    nblk = q_ref.shape[0]
    q = (jnp.dot(hb, wq_ref[...], preferred_element_type=F32) * (HEAD_DIM ** -0.5)).astype(BF16)
    k = jnp.dot(hb, wk_ref[...], preferred_element_type=F32).astype(BF16)
    vt = lax.dot_general(wvt_ref[...], hb, _NT, preferred_element_type=F32).astype(BF16)
    for r in range(nblk):
        q_ref[r] = q[r * MOBA_BLOCK:(r + 1) * MOBA_BLOCK]
        k_ref[r] = k[r * MOBA_BLOCK:(r + 1) * MOBA_BLOCK]
        vt_ref[r] = vt[:, r * MOBA_BLOCK:(r + 1) * MOBA_BLOCK]
    u_ref[...] = jnp.dot(hb, wu_ref[...], preferred_element_type=F32).astype(BF16)
    ga = jnp.dot(hb, wga_ref[...], preferred_element_type=F32)
    ga_ref[...] = jax.nn.sigmoid(ga).astype(BF16)
    gs = jnp.dot(hb, wgs_ref[...], preferred_element_type=F32)
    gs_ref[...] = jax.nn.sigmoid(gs).astype(BF16)


def _premix(x2, g, sc, sh, w_in, seq):
    t, d = x2.shape
    tm = TOKEN_TILE
    tiles_per_seq = seq // tm
    nblk = tm // MOBA_BLOCK
    a = ATT_WIDTH
    wb = w_in.astype(BF16)
    wq, wk, wv = wb[:, :a], wb[:, a:2 * a], wb[:, 2 * a:3 * a]
    wu = wb[:, 3 * a:3 * a + SSM_WIDTH]
    wga = wb[:, 3 * a + SSM_WIDTH:3 * a + SSM_WIDTH + d]
    wgs = wb[:, 3 * a + SSM_WIDTH + d:]
    mod_spec = pl.BlockSpec((1, 1, d), lambda i: (i // tiles_per_seq, 0, 0))
    blk3 = lambda rows, cols: pl.BlockSpec((nblk, rows, cols), lambda i: (i, 0, 0))
    row = lambda cols: pl.BlockSpec((tm, cols), lambda i: (i, 0))
    time_major = pl.BlockSpec((tm, SSM_WIDTH), lambda i: (i % tiles_per_seq, i // tiles_per_seq))
    nb_total = t // MOBA_BLOCK
    return pl.pallas_call(
        _premix_kernel,
        grid=(t // tm,),
        in_specs=[row(d), _full((1, d)), mod_spec, mod_spec,
                  _full(wq.shape), _full(wk.shape), _full((a, d)), _full(wu.shape),
                  _full(wga.shape), _full(wgs.shape)],
        out_specs=[blk3(MOBA_BLOCK, a), blk3(MOBA_BLOCK, a), blk3(a, MOBA_BLOCK),
                   time_major, row(d), row(d)],
        out_shape=[jax.ShapeDtypeStruct((nb_total, MOBA_BLOCK, a), BF16),
                   jax.ShapeDtypeStruct((nb_total, MOBA_BLOCK, a), BF16),
                   jax.ShapeDtypeStruct((nb_total, a, MOBA_BLOCK), BF16),
                   jax.ShapeDtypeStruct((seq, (t // seq) * SSM_WIDTH), BF16),
                   jax.ShapeDtypeStruct((t, d), BF16),
                   jax.ShapeDtypeStruct((t, d), BF16)],
        compiler_params=_cparams("arbitrary"),
        name="premix_inproj",
    )(x2, g.reshape(1, d), sc, sh, wq, wk, wv.T, wu, wga, wgs)


def _moba_kernel(far_ref, q_ref, k_ref, vt_ref, bias_ref, o_ref, kmean_ref, neg_ref, *state_refs):
    m_refs = state_refs[0::3]
    l_refs = state_refs[1::3]
    acc_refs = state_refs[2::3]
    nb = k_ref.shape[0]
    blk = MOBA_BLOCK
    i = pl.program_id(1)

    @pl.when(i == 0)
    def _():
        for n in range(nb):
            kmean_ref[n:n + 1, :] = jnp.mean(k_ref[n].astype(F32), axis=0, keepdims=True)

    lane = lax.broadcasted_iota(jnp.int32, (blk, 2 * HEAD_DIM), 1)
    nidx = lax.broadcasted_iota(jnp.int32, (nb, blk), 0)
    valid = nidx < i
    pair_w = 2 * HEAD_DIM

    def pair_cols(head):
        return slice((head // 2) * pair_w, (head // 2 + 1) * pair_w)

    def head_rows(head):
        return slice(head * HEAD_DIM, (head + 1) * HEAD_DIM)

    def q_head(head):
        q2 = q_ref[0, :, pair_cols(head)]
        lo = (head % 2) * HEAD_DIM
        return jnp.where((lane >= lo) & (lane < lo + HEAD_DIM), q2, jnp.zeros_like(q2))

    for head in range(ATT_HEADS):
        km = kmean_ref[:, pair_cols(head)]
        km_hi = km.astype(BF16)
        km_lo = (km - km_hi.astype(F32)).astype(BF16)
        qh = q_head(head)
        gate = (lax.dot_general(km_hi, qh, _NT, preferred_element_type=F32)
                + lax.dot_general(km_lo, qh, _NT, preferred_element_type=F32))
        g = jnp.where(valid, gate, -jnp.inf)
        sel = jnp.zeros((nb, blk), jnp.bool_)
        for _ in range(MOBA_TOPK):
            mx = jnp.max(g, axis=0, keepdims=True)
            first = jnp.min(jnp.where(g == mx, nidx, nb), axis=0, keepdims=True)
            pick = nidx == first
            sel = sel | pick
            g = jnp.where(pick, -jnp.inf, g)
        neg_ref[head * nb:(head + 1) * nb, :] = jnp.where(sel & valid, 0.0, MASK_NEG)

    def scores(head, j, add):
        kb = k_ref[j, :, pair_cols(head)]
        return lax.dot_general(kb, q_head(head), _NT, preferred_element_type=F32) + add

    def weighted_values(head, j, p):
        return jnp.dot(vt_ref[j, head_rows(head), :], p.astype(BF16), preferred_element_type=F32)

    def start(head, j, st):
        m = jnp.max(st, axis=0, keepdims=True)
        p = jnp.exp(st - m)
        m_refs[head][...] = m
        l_refs[head][...] = jnp.sum(p, axis=0, keepdims=True)
        acc_refs[head][...] = weighted_values(head, j, p)

    def update(head, j, st):
        m = m_refs[head][...]
        m_new = jnp.maximum(m, jnp.max(st, axis=0, keepdims=True))
        alpha = jnp.exp(m - m_new)
        p = jnp.exp(st - m_new)
        m_refs[head][...] = m_new
        l_refs[head][...] = alpha * l_refs[head][...] + jnp.sum(p, axis=0, keepdims=True)
        acc_refs[head][...] = alpha * acc_refs[head][...] + weighted_values(head, j, p)

    def sweep(j, add_of, absorb):
        st = scores(0, j, add_of(0))
        for head in range(ATT_HEADS):
            nxt = scores(head + 1, j, add_of(head + 1)) if head + 1 < ATT_HEADS else None
            absorb(head, j, st)
            st = nxt

    sweep(i, lambda h: bias_ref[h, 0], start)

    @pl.when(i > 0)
    def _():
        sweep(i - 1, lambda h: bias_ref[h, 1] + neg_ref[pl.ds(h * nb + i - 1, 1), :], update)

    def far_block(j, carry):
        sweep(j, lambda h: neg_ref[pl.ds(h * nb + j, 1), :] + far_ref[h], update)
        return carry

    lax.fori_loop(0, jnp.maximum(i - 1, 0), far_block, 0)

    for hp in range(ATT_HEADS // 2):
        outs = [acc_refs[h][...] / l_refs[h][...] for h in (2 * hp, 2 * hp + 1)]
        pair = jnp.concatenate(outs, axis=0)
        o_ref[0, :, hp * pair_w:(hp + 1) * pair_w] = pair.T.astype(BF16)


def _rel_bucket(dist):
    n = jnp.maximum(dist, 0)
    max_exact = REL_BUCKETS // 2
    nf = jnp.maximum(n, 1).astype(F32)
    large = max_exact + (jnp.log(nf / max_exact) / math.log(REL_MAX_DIST / max_exact)
                         * (REL_BUCKETS - max_exact)).astype(jnp.int32)
    large = jnp.minimum(large, REL_BUCKETS - 1)
    return jnp.where(n < max_exact, n, large)


def _moba_bias_tiles(rel_bias):
    blk = MOBA_BLOCK
    span = 2 * blk
    vec = rel_bias[_rel_bucket(jnp.arange(span))].T.astype(F32)
    masked = jnp.full_like(vec[:, :blk], MASK_NEG)
    ring_own = jnp.concatenate([vec[:, :blk], masked], axis=1)
    ring_prev = jnp.concatenate([vec[:, blk:], vec[:, :blk]], axis=1)

    def toeplitz(ring):
        flat = jnp.tile(ring, (1, blk))[:, :blk * (span - 1)]
        return flat.reshape(-1, blk, span - 1)[:, :, :blk]

    tiles = jnp.stack([toeplitz(ring_own), toeplitz(ring_prev)], axis=1)
    far = rel_bias[_rel_bucket(jnp.array(span))]
    return tiles, far.astype(F32)


def _moba(q4, k4, vt4, bias_tiles, far_bias, bsz):
    nb_total, blk, a = q4.shape
    nb = nb_total // bsz
    return pl.pallas_call(
        _moba_kernel,
        grid=(bsz, nb),
        in_specs=[
            pl.BlockSpec(memory_space=pltpu.SMEM),
            pl.BlockSpec((1, blk, a), lambda b, i: (b * nb + i, 0, 0)),
            pl.BlockSpec((nb, blk, a), lambda b, i: (b, 0, 0)),
            pl.BlockSpec((nb, a, blk), lambda b, i: (b, 0, 0)),
            _full(bias_tiles.shape),
        ],
        out_specs=pl.BlockSpec((1, blk, a), lambda b, i: (b * nb + i, 0, 0)),
        out_shape=jax.ShapeDtypeStruct((nb_total, blk, a), BF16),
        scratch_shapes=[pltpu.VMEM((nb, a), F32), pltpu.VMEM((ATT_HEADS * nb, blk), F32)]
        + [pltpu.VMEM((1, blk), F32), pltpu.VMEM((1, blk), F32), pltpu.VMEM((HEAD_DIM, blk), F32)] * ATT_HEADS,
        compiler_params=_cparams("arbitrary", "arbitrary"),
        name="moba_attention",
    )(far_bias, q4, k4, vt4, bias_tiles)


def _ssm_weights(a_re, a_im, log_dt, b_re, b_im, c_re, c_im, d_skip):
    g, p, cw, n = SSM_GROUPS, SSM_STATE, SSM_GROUP, SSM_CHUNK
    gs = LANES // cw
    nq = g // gs
    npair = n // 2
    pairs_per_set = gs // 2
    lam_re = jnp.minimum(a_re.astype(F32), -1e-4)
    lam_im = a_im.astype(F32)
    dt = jnp.exp(log_dt.astype(F32))[:, None]
    mag = jnp.exp(lam_re * dt)
    lb_re = mag * jnp.cos(lam_im * dt)
    lb_im = mag * jnp.sin(lam_im * dt)
    n_re = lb_re - 1.0
    n_im = lb_im
    den = lam_re * lam_re + lam_im * lam_im
    z_re = ((n_re * lam_re + n_im * lam_im) / den)[..., None]
    z_im = ((n_im * lam_re - n_re * lam_im) / den)[..., None]
    br, bi = b_re.astype(F32), b_im.astype(F32)
    bb_re = z_re * br - z_im * bi
    bb_im = z_re * bi + z_im * br
    pw_re, pw_im = [jnp.ones_like(lb_re)], [jnp.zeros_like(lb_im)]
    for _ in range(n):
        r, im = pw_re[-1], pw_im[-1]
        pw_re.append(r * lb_re - im * lb_im)
        pw_im.append(r * lb_im + im * lb_re)
    pw_re, pw_im = jnp.stack(pw_re), jnp.stack(pw_im)
    cr, ci = c_re.astype(F32), c_im.astype(F32)
    hi = lax.Precision.HIGHEST
    rev_re, rev_im = pw_re[n - 1::-1], pw_im[n - 1::-1]
    sb_re = jnp.einsum('sgp,gpc->sgcp', rev_re, bb_re) - jnp.einsum('sgp,gpc->sgcp', rev_im, bb_im)
    sb_im = jnp.einsum('sgp,gpc->sgcp', rev_re, bb_im) + jnp.einsum('sgp,gpc->sgcp', rev_im, bb_re)
    cl_re = jnp.einsum('gcp,tgp->tgpc', cr, pw_re[1:]) - jnp.einsum('gcp,tgp->tgpc', ci, pw_im[1:])
    cl_im = jnp.einsum('gcp,tgp->tgpc', cr, pw_im[1:]) + jnp.einsum('gcp,tgp->tgpc', ci, pw_re[1:])
    cp_re = jnp.einsum('gcp,tgp->tgcp', cr, pw_re[:n]) - jnp.einsum('gcp,tgp->tgcp', ci, pw_im[:n])
    cp_im = jnp.einsum('gcp,tgp->tgcp', cr, pw_im[:n]) + jnp.einsum('gcp,tgp->tgcp', ci, pw_re[:n])
    kern = (jnp.einsum('tgcp,gpd->gtcd', cp_re, bb_re, precision=hi)
            - jnp.einsum('tgcp,gpd->gtcd', cp_im, bb_im, precision=hi))

    kp = jnp.concatenate([jnp.zeros_like(kern[:, :1]), kern], axis=1)
    dl = jnp.arange(npair)[:, None, None]
    s0 = jnp.arange(2)[None, :, None]
    t0 = jnp.arange(2)[None, None, :]
    lag = 2 * dl + t0 - s0
    kl = kp[:, lag + 1].reshape(nq, gs, npair, 2, 2, cw, cw)
    base = kl.transpose(0, 2, 3, 1, 6, 4, 5)
    eye = jnp.eye(gs, dtype=F32)
    w_toe = (base[:, :, :, :, :, :, None, :] * eye[None, None, None, :, None, None, :, None])
    w_toe = w_toe.reshape(nq, npair, MXU_DIM, MXU_DIM).astype(BF16)

    member = (jnp.arange(gs)[None, :, None]
              == 2 * jnp.arange(pairs_per_set)[:, None, None] + jnp.arange(2)[None, None, :]).astype(F32)
    sbs = jnp.stack([sb_re, sb_im]).reshape(2, npair, 2, nq, gs, cw, p)
    sbs = sbs.transpose(3, 1, 2, 4, 5, 0, 6)
    w_in_state = (sbs[:, None, :, :, :, :, :, None, :]
                  * member[None, :, None, None, :, None, None, :, None])
    w_in_state = w_in_state.reshape(g // 2, npair, MXU_DIM, MXU_DIM).astype(BF16)
    sos = jnp.stack([cl_re, -cl_im]).reshape(2, npair, 2, nq, gs, p, cw)
    sos = sos.transpose(3, 1, 0, 5, 2, 4, 6)
    member_t = member.transpose(0, 2, 1)
    w_state_out = (sos[:, None, :, :, None, :, :, :, :]
                   * member_t[None, :, None, None, :, None, None, :, None])
    w_state_out = w_state_out.reshape(g // 2, npair, MXU_DIM, MXU_DIM).astype(BF16)
    decay = jnp.stack([pw_re[n].reshape(g // 2, 2 * p), pw_im[n].reshape(g // 2, 2 * p)])
    dvec = d_skip.astype(F32).reshape(1, g * cw)
    return w_toe, w_in_state, w_state_out, decay, dvec


def _ssm_kernel(u_ref, wt_ref, wi_ref, wo_ref, dec_ref, d_ref, y_ref, s_ref, xp_ref, x_ref, *, bsz):
    kt = u_ref.shape[0]
    n = u_ref.shape[1] // bsz
    rows = kt * bsz
    npair = n // 2
    nq = wt_ref.shape[0]
    npairs_g = wi_ref.shape[0]
    per_set = npairs_g // nq
    half = MXU_DIM // 2

    @pl.when(pl.program_id(0) == 0)
    def _():
        x_ref[...] = jnp.zeros_like(x_ref)

    def piece(s, q):
        return u_ref[:, s * bsz:(s + 1) * bsz, q * LANES:(q + 1) * LANES].reshape(rows, LANES)

    lhs = {(sp, q): jnp.concatenate([piece(2 * sp, q), piece(2 * sp + 1, q)], axis=-1)
           for sp in range(npair) for q in range(nq)}

    for gp in range(npairs_g):
        q = gp // per_set
        acc = jnp.dot(lhs[0, q], wi_ref[gp, 0], preferred_element_type=F32)
        for sp in range(1, npair):
            acc = acc + jnp.dot(lhs[sp, q], wi_ref[gp, sp], preferred_element_type=F32)
        s_ref[:, gp * MXU_DIM:(gp + 1) * MXU_DIM] = acc

    for gp in range(npairs_g):
        re_cols = slice(gp * MXU_DIM, gp * MXU_DIM + half)
        im_cols = slice(gp * MXU_DIM + half, (gp + 1) * MXU_DIM)
        ar = dec_ref[0, gp:gp + 1, :]
        ai = dec_ref[1, gp:gp + 1, :]
        xr = x_ref[:, re_cols]
        xi = x_ref[:, im_cols]
        for kc in range(kt):
            rs = slice(kc * bsz, (kc + 1) * bsz)
            xp_ref[rs, re_cols] = xr.astype(BF16)
            xp_ref[rs, im_cols] = xi.astype(BF16)
            sr = s_ref[rs, re_cols]
            si = s_ref[rs, im_cols]
            xr, xi = ar * xr - ai * xi + sr, ar * xi + ai * xr + si
        x_ref[:, re_cols] = xr
        x_ref[:, im_cols] = xi

    for q in range(nq):
        for tp in range(npair):
            acc = jnp.dot(lhs[0, q], wt_ref[q, tp], preferred_element_type=F32)
            for sp in range(1, tp + 1):
                acc = acc + jnp.dot(lhs[sp, q], wt_ref[q, tp - sp], preferred_element_type=F32)
            for gp in range(q * per_set, (q + 1) * per_set):
                acc = acc + jnp.dot(xp_ref[:, gp * MXU_DIM:(gp + 1) * MXU_DIM], wo_ref[gp, tp],
                                    preferred_element_type=F32)
            for t0 in range(2):
                s = 2 * tp + t0
                y = acc[:, t0 * half:(t0 + 1) * half] + d_ref[:, q * LANES:(q + 1) * LANES] * piece(s, q).astype(F32)
                y_ref[:, s * bsz:(s + 1) * bsz, q * LANES:(q + 1) * LANES] = (
                    jax.nn.gelu(y).astype(BF16).reshape(kt, bsz, LANES))


def _ssm(u_tm, weights, bsz, seq):
    w_toe, w_in_state, w_state_out, decay, dvec = weights
    n = SSM_CHUNK
    kt = SSM_CHUNKS_PER_STEP
    nchunk = seq // n
    rows = kt * bsz
    state_w = SSM_GROUPS * 2 * SSM_STATE
    u3 = u_tm.reshape(nchunk, n * bsz, SSM_WIDTH)
    once = pl.Buffered(1)
    blk = pl.BlockSpec((kt, n * bsz, SSM_WIDTH), lambda i: (i, 0, 0))
    y3 = pl.pallas_call(
        functools.partial(_ssm_kernel, bsz=bsz),
        grid=(nchunk // kt,),
        in_specs=[blk, _full(w_toe.shape, pipeline_mode=once), _full(w_in_state.shape, pipeline_mode=once),
                  _full(w_state_out.shape, pipeline_mode=once), _full(decay.shape), _full(dvec.shape)],
        out_specs=blk,
        out_shape=jax.ShapeDtypeStruct(u3.shape, BF16),
        scratch_shapes=[pltpu.VMEM((rows, state_w), F32), pltpu.VMEM((rows, state_w), BF16),
                        pltpu.VMEM((bsz, state_w), F32)],
        compiler_params=_cparams("arbitrary"),
        name="s5_chunked_scan",
    )(u3, w_toe, w_in_state, w_state_out, decay, dvec)
    return y3.reshape(seq, bsz * SSM_WIDTH)


def _postmix_kernel(att_ref, ys_ref, ga_ref, gs_ref, x_ref, gt_ref, sc_ref, sh_ref, gpost_ref, gpre_ref,
                    wglu_ref, wso_ref, wao_ref, wo_ref, wrh_ref, wrl_ref, br_ref,
                    x1_ref, h2_ref, idx_ref, gate_ref, rank_ref, cnt_ref, run_ref):
    tm = x_ref.shape[0]
    ne = wrh_ref.shape[0]

    @pl.when(pl.program_id(0) == 0)
    def _():
        run_ref[...] = jnp.zeros_like(run_ref)

    glu = jnp.dot(ys_ref[...], wglu_ref[...], preferred_element_type=F32)
    sg = (glu[:, :SSM_WIDTH] * jax.nn.sigmoid(glu[:, SSM_WIDTH:])).astype(BF16)
    ssm = jnp.dot(sg, wso_ref[...], preferred_element_type=F32)
    att = jnp.dot(att_ref[...], wao_ref[...], preferred_element_type=F32)
    merged = (ga_ref[...].astype(F32) * att + gs_ref[...].astype(F32) * ssm).astype(BF16)
    y = jnp.dot(merged, wo_ref[...], preferred_element_type=F32)
    x1 = x_ref[...] + gt_ref[0] * _rms(y, gpost_ref[...])
    x1_ref[...] = x1
    h2 = _rms(x1, gpre_ref[...]) * (1.0 + sc_ref[0]) + sh_ref[0]
    h2_ref[...] = h2
    h_hi = h2.astype(BF16)
    h_lo = (h2 - h_hi.astype(F32)).astype(BF16)
    logits = (lax.dot_general(wrh_ref[...], h_hi, _NT, preferred_element_type=F32)
              + lax.dot_general(wrh_ref[...], h_lo, _NT, preferred_element_type=F32)
              + lax.dot_general(wrl_ref[...], h_hi, _NT, preferred_element_type=F32)
              + br_ref[...])
    eidx = lax.broadcasted_iota(jnp.int32, (ne, tm), 0)
    lg = logits
    vals, idxs = [], []
    for _ in range(TOP_K):
        mx = jnp.max(lg, axis=0, keepdims=True)
        first = jnp.min(jnp.where(lg == mx, eidx, ne), axis=0, keepdims=True)
        vals.append(mx)
        idxs.append(first)
        lg = jnp.where(eidx == first, -jnp.inf, lg)
    exps = [jnp.exp(v - vals[0]) for v in vals]
    denom = exps[0] + exps[1] + exps[2] + exps[3]
    gate_ref[...] = jnp.concatenate([e / denom for e in exps], axis=0)
    idx_ref[...] = jnp.concatenate(idxs, axis=0)
    onehot = jnp.where(lg == -jnp.inf, 1.0, 0.0)
    t_row = lax.broadcasted_iota(jnp.int32, (tm, tm), 0)
    t_col = lax.broadcasted_iota(jnp.int32, (tm, tm), 1)
    earlier = jnp.where(t_row < t_col, 1.0, 0.0).astype(BF16)
    before = jnp.dot(onehot.astype(BF16), earlier, preferred_element_type=F32) + run_ref[:, 0:1]
    ranks = [jnp.sum(jnp.where(eidx == ix, before, 0.0), axis=0, keepdims=True) for ix in idxs]
    rank_ref[...] = jnp.concatenate(ranks, axis=0).astype(jnp.int32)
    run_ref[...] = run_ref[...] + jnp.sum(onehot, axis=1, keepdims=True)
    cnt_ref[...] = run_ref[...].astype(jnp.int32)


def _postmix(att2, ys_tm, ga, gs, x2, gt1, sc2, sh2, g_post, g_pre, w_glu, w_ssm_out, w_att_out, w_out,
             w_router, b_router, seq):
    t, d = x2.shape
    tm = TOKEN_TILE
    tiles_per_seq = seq // tm
    ne = w_router.shape[1]
    wr_t = w_router.T.astype(F32)
    wr_hi = wr_t.astype(BF16)
    wr_lo = (wr_t - wr_hi.astype(F32)).astype(BF16)
    mod_spec = pl.BlockSpec((1, 1, d), lambda i: (i // tiles_per_seq, 0, 0))
    row = lambda cols: pl.BlockSpec((tm, cols), lambda i: (i, 0))
    time_major = pl.BlockSpec((tm, SSM_WIDTH), lambda i: (i % tiles_per_seq, i // tiles_per_seq))
    col = pl.BlockSpec((TOP_K, tm), lambda i: (0, i))
    weights = [w_glu.astype(BF16), w_ssm_out.astype(BF16), w_att_out.astype(BF16), w_out.astype(BF16),
               wr_hi, wr_lo, b_router.astype(F32).reshape(ne, 1)]
    return pl.pallas_call(
        _postmix_kernel,
        grid=(t // tm,),
        in_specs=[row(ATT_WIDTH), time_major, row(d), row(d), row(d), mod_spec, mod_spec, mod_spec,
                  _full((1, d)), _full((1, d))] + [_full(w.shape) for w in weights],
        out_specs=[row(d), row(d), col, col, col, _full((ne, 128))],
        out_shape=[jax.ShapeDtypeStruct((t, d), F32), jax.ShapeDtypeStruct((t, d), F32),
                   jax.ShapeDtypeStruct((TOP_K, t), jnp.int32), jax.ShapeDtypeStruct((TOP_K, t), F32),
                   jax.ShapeDtypeStruct((TOP_K, t), jnp.int32), jax.ShapeDtypeStruct((ne, 128), jnp.int32)],
        scratch_shapes=[pltpu.VMEM((ne, 128), F32)],
        compiler_params=_cparams("arbitrary"),
        name="postmix_router",
    )(att2, ys_tm, ga, gs, x2, gt1, sc2, sh2, g_post.reshape(1, d), g_pre.reshape(1, d), *weights)


def _row_copies(src_row, dst_row, sem, dest_ref, tm):
    def body(it, carry):
        for u in range(ROW_COPY_UNROLL):
            t = it * ROW_COPY_UNROLL + u
            for k in range(TOP_K):
                r = dest_ref[k, t]
                pltpu.make_async_copy(src_row(t, k, r), dst_row(t, k, r), sem).start()
        return carry
    lax.fori_loop(0, tm // ROW_COPY_UNROLL, body, 0)


def _dispatch_kernel(ends_ref, dest_ref, h_ref, xb_ref, zero_ref, sem, zsem):
    tm = h_ref.shape[0]
    rb = zero_ref.shape[0]

    @pl.when(pl.program_id(0) == 0)
    def _():
        zero_ref[...] = jnp.zeros_like(zero_ref)

        def tail(e):
            start = pl.multiple_of(ends_ref[e] - rb, rb)
            return pltpu.make_async_copy(zero_ref, xb_ref.at[pl.ds(start, rb), :], zsem)

        for e in range(ends_ref.shape[0]):
            pl.when(ends_ref[e] > 0)(lambda e=e: tail(e).start())
        for e in range(ends_ref.shape[0]):
            pl.when(ends_ref[e] > 0)(lambda e=e: tail(e).wait())

        def unused(b):
            return pltpu.make_async_copy(zero_ref, xb_ref.at[pl.ds(pl.multiple_of(b * rb, rb), rb), :], zsem)

        first_unused = ends_ref[ends_ref.shape[0] - 1] // rb
        n_blocks = xb_ref.shape[0] // rb
        lax.fori_loop(first_unused, n_blocks, lambda b, c: (unused(b).start(), c)[1], 0)
        lax.fori_loop(first_unused, n_blocks, lambda b, c: (unused(b).wait(), c)[1], 0)

    _row_copies(lambda t, k, r: h_ref.at[pl.ds(t, 1), :],
                lambda t, k, r: xb_ref.at[pl.ds(r, 1), :], sem, dest_ref, tm)
    for k in range(TOP_K):
        pltpu.make_async_copy(h_ref, xb_ref.at[pl.ds(0, tm), :], sem).wait()


def _dispatch(h2, dest_tiles, pad_ends, p_rows):
    t, d = h2.shape
    tm = ROUTE_TILE
    return pl.pallas_call(
        _dispatch_kernel,
        grid=(t // tm,),
        in_specs=[pl.BlockSpec(memory_space=pltpu.SMEM),
                  pl.BlockSpec((None, TOP_K, tm), lambda i: (i, 0, 0), memory_space=pltpu.SMEM),
                  pl.BlockSpec((tm, d), lambda i: (i, 0))],
        out_specs=pl.BlockSpec(memory_space=pl.ANY),
        out_shape=jax.ShapeDtypeStruct((p_rows, d), h2.dtype),
        scratch_shapes=[pltpu.VMEM((EXPERT_ROWS, d), h2.dtype), pltpu.SemaphoreType.DMA(()),
                        pltpu.SemaphoreType.DMA(())],
        compiler_params=_cparams("arbitrary"),
        name="expert_dispatch",
    )(pad_ends, dest_tiles, h2)


def _experts_kernel(be_ref, nused_ref, x_ref, w1_ref, b1_ref, w2_ref, b2_ref, y_ref, w1b_ref, w2b_ref):
    i = pl.program_id(0)
    prev = be_ref[jnp.maximum(i - 1, 0)]
    fresh = (i == 0) | (be_ref[i] != prev)

    @pl.when(fresh)
    def _():
        w1b_ref[...] = w1_ref[...].astype(BF16)
        w2b_ref[...] = w2_ref[...].astype(BF16)

    @pl.when(i < nused_ref[0])
    def _():
        gu = jnp.dot(x_ref[...].astype(BF16), w1b_ref[...], preferred_element_type=F32) + b1_ref[...]
        g = jnp.minimum(gu[:, :D_FF], SWIGLU_LIMIT)
        up = jnp.clip(gu[:, D_FF:], -SWIGLU_LIMIT, SWIGLU_LIMIT)
        act = ((up + 1.0) * g * jax.nn.sigmoid(SWIGLU_ALPHA * g)).astype(BF16)
        y_ref[...] = jnp.dot(act, w2b_ref[...], preferred_element_type=F32) + b2_ref[...]

    @pl.when(i >= nused_ref[0])
    def _():
        y_ref[...] = jnp.zeros_like(y_ref)


def _experts(xb, block_e, n_used, w1, b1, w2, b2, layer):
    p_rows, d = xb.shape
    depth, ne = w1.shape[:2]
    rb = EXPERT_ROWS
    wmap = lambda i, be, nu: (layer, be[i], 0, 0)
    grid_spec = pltpu.PrefetchScalarGridSpec(
        num_scalar_prefetch=2,
        grid=(p_rows // rb,),
        in_specs=[
            pl.BlockSpec((rb, d), lambda i, be, nu: (i, 0)),
            pl.BlockSpec((None, None, d, 2 * D_FF), wmap),
            pl.BlockSpec((None, None, 1, 2 * D_FF), wmap),
            pl.BlockSpec((None, None, D_FF, d), wmap),
            pl.BlockSpec((None, None, 1, d), wmap),
        ],
        out_specs=pl.BlockSpec((rb, d), lambda i, be, nu: (i, 0)),
        scratch_shapes=[pltpu.VMEM((d, 2 * D_FF), BF16), pltpu.VMEM((D_FF, d), BF16)],
    )
    return pl.pallas_call(
        _experts_kernel,
        grid_spec=grid_spec,
        out_shape=jax.ShapeDtypeStruct((p_rows, d), F32),
        compiler_params=_cparams("arbitrary"),
        name="expert_ffn",
    )(block_e, n_used, xb, w1, b1.reshape(depth, ne, 1, 2 * D_FF), w2, b2.reshape(depth, ne, 1, d))


def _sc_gather_kernel(table_hbm, idx_hbm, out_hbm, idx_v, rows_v, sems, *, per_worker):
    chunk = rows_v.shape[1]
    n_chunks = per_worker // chunk
    base = (lax.axis_index("s") * SC_CORES + lax.axis_index("c")) * per_worker

    def gather(c, b):
        off = pl.multiple_of(base + c * chunk, chunk)
        pltpu.sync_copy(idx_hbm.at[pl.ds(off, chunk)], idx_v.at[b])
        return pltpu.make_async_copy(table_hbm.at[idx_v.at[b]], rows_v.at[b], sems.at[b])

    gather(0, 0).start()

    @pl.loop(0, n_chunks, step=2)
    def _(c0):
        for b in range(2):
            c = c0 + b

            @pl.when(c + 1 < n_chunks)
            def _():
                gather(c + 1, 1 - b).start()

            pltpu.make_async_copy(table_hbm.at[idx_v.at[b]], rows_v.at[b], sems.at[b]).wait()
            off = pl.multiple_of(base + c * chunk, chunk)
            pltpu.sync_copy(rows_v.at[b], out_hbm.at[pl.ds(off, chunk)])


def _sc_gather_rows(table, idx):
    n = idx.shape[0]
    d = table.shape[1]
    workers = SC_CORES * SC_SUBCORES
    per_worker = n // workers
    assert n % workers == 0 and per_worker % (2 * SC_GATHER_ROWS) == 0
    mesh = plsc.VectorSubcoreMesh(core_axis_name="c", subcore_axis_name="s",
                                  num_cores=SC_CORES, num_subcores=SC_SUBCORES)
    return pl.kernel(
        functools.partial(_sc_gather_kernel, per_worker=per_worker),
        out_type=jax.ShapeDtypeStruct((n, d), table.dtype),
        mesh=mesh,
        scratch_types=[pltpu.VMEM((2, SC_GATHER_ROWS), jnp.int32),
                       pltpu.VMEM((2, SC_GATHER_ROWS, d), table.dtype),
                       pltpu.SemaphoreType.DMA((2,))],
        name="sc_row_gather",
    )(table, idx)


def _combine_kernel(gate_ref, x_ref, gt_ref, g_ref, y0_ref, y1_ref, y2_ref, y3_ref, o_ref):
    tm = x_ref.shape[0]
    gates = gate_ref[...]
    gates = jnp.concatenate([gates, jnp.zeros((LANES - TOP_K, tm), F32)], axis=0).T
    y = gates[:, 0:1] * y0_ref[...]
    for k, y_ref in enumerate((y1_ref, y2_ref, y3_ref), start=1):
        y = y + gates[:, k:k + 1] * y_ref[...]
    o_ref[...] = x_ref[...] + gt_ref[0] * _rms(y, g_ref[...])


def _combine(y4, gate_t, x2, gt2, g_post, seq):
    t, d = x2.shape
    tm = TOKEN_TILE
    tiles = t // tm
    tiles_per_seq = seq // tm
    row = pl.BlockSpec((tm, d), lambda i: (i, 0))
    slot = lambda k: pl.BlockSpec((tm, d), lambda i: (k * tiles + i, 0))
    return pl.pallas_call(
        _combine_kernel,
        grid=(tiles,),
        in_specs=[pl.BlockSpec((TOP_K, tm), lambda i: (0, i)),
                  row, pl.BlockSpec((1, 1, d), lambda i: (i // tiles_per_seq, 0, 0)), _full((1, d))]
        + [slot(k) for k in range(TOP_K)],
        out_specs=row,
        out_shape=jax.ShapeDtypeStruct((t, d), F32),
        compiler_params=_cparams("arbitrary"),
        name="expert_combine",
    )(gate_t, x2, gt2, g_post.reshape(1, d), y4, y4, y4, y4)


def _route_plan(idx_t, rank_t, counts):
    rb = EXPERT_ROWS
    k, t = idx_t.shape
    padded = (counts + rb - 1) // rb * rb
    pad_ends = jnp.cumsum(padded)
    pad_starts = pad_ends - padded
    experts = jnp.arange(N_EXPERTS, dtype=jnp.int32)
    start_of = jnp.sum(jnp.where(idx_t[None] == experts[:, None, None], pad_starts[:, None, None], 0), axis=0)
    dest = (start_of + rank_t).astype(jnp.int32)
    n_blocks = (k * t) // rb + N_EXPERTS
    blk_start = jnp.arange(n_blocks, dtype=jnp.int32) * rb
    block_e = jnp.minimum(jnp.sum(pad_ends[None, :] <= blk_start[:, None], axis=1), N_EXPERTS - 1)
    n_used = (pad_ends[-1] // rb).astype(jnp.int32).reshape(1)
    dest_tiles = dest.reshape(k, t // ROUTE_TILE, ROUTE_TILE).transpose(1, 0, 2)
    return dest_tiles, dest.reshape(-1), block_e.astype(jnp.int32), n_used, pad_ends.astype(jnp.int32), n_blocks * rb


def kernel(x, c, rel_bias, w_ada, b_ada, g_pre_mix, g_post_mix, g_pre_ffn, g_post_ffn, w_in, ssm_a_re, ssm_a_im, ssm_log_dt, ssm_b_re, ssm_b_im, ssm_c_re, ssm_c_im, ssm_d, w_glu, w_ssm_out, w_att_out, w_out, w_router, b_router, w_exp_in, b_exp_in, w_exp_out, b_exp_out):
    bsz, seq, d = x.shape
    depth = w_ada.shape[0]
    t = bsz * seq
    assert d == D_MODEL and seq % TOKEN_TILE == 0 and TOKEN_TILE % MOBA_BLOCK == 0
    assert seq % (SSM_CHUNK * SSM_CHUNKS_PER_STEP) == 0 and seq % ROUTE_TILE == 0

    mod = _ada_mod(c, w_ada, b_ada)
    bias_tiles, far_bias = _moba_bias_tiles(rel_bias.astype(F32))
    x2 = x.reshape(t, d)
    for l in range(depth):
        sh1, sc1, gt1, sh2, sc2, gt2 = [m.reshape(bsz, 1, d) for m in jnp.split(mod[l], N_MOD, axis=-1)]
        q4, k4, vt4, u_tm, ga, gs = _premix(x2, g_pre_mix[l], sc1, sh1, w_in[l], seq)
        att = _moba(q4, k4, vt4, bias_tiles, far_bias, bsz).reshape(t, ATT_WIDTH)
        ssm_w = _ssm_weights(ssm_a_re[l], ssm_a_im[l], ssm_log_dt[l], ssm_b_re[l], ssm_b_im[l],
                             ssm_c_re[l], ssm_c_im[l], ssm_d[l])
        ys_tm = _ssm(u_tm, ssm_w, bsz, seq)
        x1, h2, idx_t, gate_t, rank_t, cnt = _postmix(
            att, ys_tm, ga, gs, x2, gt1, sc2, sh2, g_post_mix[l], g_pre_ffn[l],
            w_glu[l], w_ssm_out[l], w_att_out[l], w_out[l], w_router[l], b_router[l], seq)
        dest_tiles, dest_flat, block_e, n_used, pad_ends, p_rows = _route_plan(idx_t, rank_t, cnt[:, 0])
        xb = _dispatch(h2, dest_tiles, pad_ends, p_rows)
        yb = _experts(xb, block_e, n_used, w_exp_in, b_exp_in, w_exp_out, b_exp_out, l)
        y4 = _sc_gather_rows(yb, dest_flat)
        x2 = _combine(y4, gate_t, x1, gt2, g_post_ffn[l], seq)
    return x2.reshape(bsz, seq, d)
```

```python
import functools
import math

import jax
import jax.numpy as jnp
from jax import lax
from jax.experimental import pallas as pl
from jax.experimental.pallas import tpu as pltpu
from jax.experimental.pallas import tpu_sc as plsc

F32 = jnp.float32
BF16 = jnp.bfloat16

D_MODEL = 1024
ATT_HEADS = 8
HEAD_DIM = 64
ATT_WIDTH = ATT_HEADS * HEAD_DIM
MOBA_BLOCK = 256
MOBA_TOPK = 3
REL_BUCKETS = 32
REL_MAX_DIST = 128
SSM_WIDTH = D_MODEL // 2
SSM_GROUP = 16
SSM_GROUPS = SSM_WIDTH // SSM_GROUP
SSM_STATE = 64
N_EXPERTS = 32
TOP_K = 4
D_FF = D_MODEL
SWIGLU_ALPHA = 1.702
SWIGLU_LIMIT = 7.0
RMS_EPS = 1e-6
N_MOD = 6

SSM_CHUNK = 8
SSM_CHUNKS_PER_STEP = 16
LANES = 128
MXU_DIM = 256
TOKEN_TILE = 512
SC_CORES = 2
SC_SUBCORES = 16
SC_GATHER_ROWS = 32
EXPERT_ROWS = 256
MASK_NEG = -1e30
VMEM_LIMIT = 56 * 1024 * 1024

_NT = (((1,), (1,)), ((), ()))


def _cparams(*sem):
    return pltpu.CompilerParams(dimension_semantics=sem, vmem_limit_bytes=VMEM_LIMIT)


def _rms(x, g):
    return x * lax.rsqrt(jnp.mean(x * x, axis=-1, keepdims=True) + RMS_EPS) * g


def _full(shape, **kw):
    n = len(shape)
    return pl.BlockSpec(shape, lambda *_: (0,) * n, **kw)


def _ada_kernel(c_ref, w_ref, b_ref, o_ref):
    c = c_ref[...]
    cond = c * jax.nn.sigmoid(c)
    o_ref[0] = jnp.dot(cond, w_ref[0], preferred_element_type=F32,
                       precision=lax.Precision.HIGHEST) + b_ref[0]


def _ada_mod(c, w_ada, b_ada):
    depth, d, nd = w_ada.shape
    bsz = c.shape[0]
    return pl.pallas_call(
        _ada_kernel,
        grid=(depth, nd // d),
        in_specs=[
            pl.BlockSpec((bsz, d), lambda l, j: (0, 0)),
            pl.BlockSpec((1, d, d), lambda l, j: (l, 0, j)),
            pl.BlockSpec((1, 1, d), lambda l, j: (l, 0, j)),
        ],
        out_specs=pl.BlockSpec((1, bsz, d), lambda l, j: (l, 0, j)),
        out_shape=jax.ShapeDtypeStruct((depth, bsz, nd), F32),
        compiler_params=_cparams("arbitrary", "arbitrary"),
        name="ada_mod",
    )(c, w_ada, b_ada.reshape(depth, 1, nd))


def _premix_kernel(x_ref, g_ref, sc_ref, sh_ref, wq_ref, wk_ref, wvt_ref, wu_ref, wga_ref, wgs_ref,
                   q_ref, k_ref, vt_ref, u_ref, ga_ref, gs_ref):
    x = x_ref[...]
    h = _rms(x, g_ref[...]) * (1.0 + sc_ref[0]) + sh_ref[0]
    hb = h.astype(BF16)
    nblk = q_ref.shape[0]
    q = (jnp.dot(hb, wq_ref[...], preferred_element_type=F32) * (HEAD_DIM ** -0.5)).astype(BF16)
    k = jnp.dot(hb, wk_ref[...], preferred_element_type=F32).astype(BF16)
    vt = lax.dot_general(wvt_ref[...], hb, _NT, preferred_element_type=F32).astype(BF16)
    for r in range(nblk):
        q_ref[r] = q[r * MOBA_BLOCK:(r + 1) * MOBA_BLOCK]
        k_ref[r] = k[r * MOBA_BLOCK:(r + 1) * MOBA_BLOCK]
        vt_ref[r] = vt[:, r * MOBA_BLOCK:(r + 1) * MOBA_BLOCK]
    u_ref[...] = jnp.dot(hb, wu_ref[...], preferred_element_type=F32).astype(BF16)
    ga = jnp.dot(hb, wga_ref[...], preferred_element_type=F32)
    ga_ref[...] = jax.nn.sigmoid(ga).astype(BF16)
    gs = jnp.dot(hb, wgs_ref[...], preferred_element_type=F32)
    gs_ref[...] = jax.nn.sigmoid(gs).astype(BF16)


def _premix(x2, g, sc, sh, w_in, seq):
    t, d = x2.shape
    tm = TOKEN_TILE
    tiles_per_seq = seq // tm
    nblk = tm // MOBA_BLOCK
    a = ATT_WIDTH
    wb = w_in.astype(BF16)
    wq, wk, wv = wb[:, :a], wb[:, a:2 * a], wb[:, 2 * a:3 * a]
    wu = wb[:, 3 * a:3 * a + SSM_WIDTH]
    wga = wb[:, 3 * a + SSM_WIDTH:3 * a + SSM_WIDTH + d]
    wgs = wb[:, 3 * a + SSM_WIDTH + d:]
    mod_spec = pl.BlockSpec((1, 1, d), lambda i: (i // tiles_per_seq, 0, 0))
    blk3 = lambda rows, cols: pl.BlockSpec((nblk, rows, cols), lambda i: (i, 0, 0))
    row = lambda cols: pl.BlockSpec((tm, cols), lambda i: (i, 0))
    time_major = pl.BlockSpec((tm, SSM_WIDTH), lambda i: (i % tiles_per_seq, i // tiles_per_seq))
    nb_total = t // MOBA_BLOCK
    return pl.pallas_call(
        _premix_kernel,
        grid=(t // tm,),
        in_specs=[row(d), _full((1, d)), mod_spec, mod_spec,
                  _full(wq.shape), _full(wk.shape), _full((a, d)), _full(wu.shape),
                  _full(wga.shape), _full(wgs.shape)],
        out_specs=[blk3(MOBA_BLOCK, a), blk3(MOBA_BLOCK, a), blk3(a, MOBA_BLOCK),
                   time_major, row(d), row(d)],
        out_shape=[jax.ShapeDtypeStruct((nb_total, MOBA_BLOCK, a), BF16),
                   jax.ShapeDtypeStruct((nb_total, MOBA_BLOCK, a), BF16),
                   jax.ShapeDtypeStruct((nb_total, a, MOBA_BLOCK), BF16),
                   jax.ShapeDtypeStruct((seq, (t // seq) * SSM_WIDTH), BF16),
                   jax.ShapeDtypeStruct((t, d), BF16),
                   jax.ShapeDtypeStruct((t, d), BF16)],
        compiler_params=_cparams("arbitrary"),
        name="premix_inproj",
    )(x2, g.reshape(1, d), sc, sh, wq, wk, wv.T, wu, wga, wgs)


def _moba_kernel(far_ref, q_ref, k_ref, vt_ref, bias_ref, o_ref, kmean_ref, neg_ref, *state_refs):
    m_refs = state_refs[0::3]
    l_refs = state_refs[1::3]
    acc_refs = state_refs[2::3]
    nb = k_ref.shape[0]
    blk = MOBA_BLOCK
    i = pl.program_id(1)

    @pl.when(i == 0)
    def _():
        for n in range(nb):
            kmean_ref[n:n + 1, :] = jnp.mean(k_ref[n].astype(F32), axis=0, keepdims=True)

    lane = lax.broadcasted_iota(jnp.int32, (blk, 2 * HEAD_DIM), 1)
    nidx = lax.broadcasted_iota(jnp.int32, (nb, blk), 0)
    valid = nidx < i
    pair_w = 2 * HEAD_DIM

    def pair_cols(head):
        return slice((head // 2) * pair_w, (head // 2 + 1) * pair_w)

    def head_rows(head):
        return slice(head * HEAD_DIM, (head + 1) * HEAD_DIM)

    def q_head(head):
        q2 = q_ref[0, :, pair_cols(head)]
        lo = (head % 2) * HEAD_DIM
        return jnp.where((lane >= lo) & (lane < lo + HEAD_DIM), q2, jnp.zeros_like(q2))

    for head in range(ATT_HEADS):
        km = kmean_ref[:, pair_cols(head)]
        km_hi = km.astype(BF16)
        km_lo = (km - km_hi.astype(F32)).astype(BF16)
        qh = q_head(head)
        gate = (lax.dot_general(km_hi, qh, _NT, preferred_element_type=F32)
                + lax.dot_general(km_lo, qh, _NT, preferred_element_type=F32))
        g = jnp.where(valid, gate, -jnp.inf)
        sel = jnp.zeros((nb, blk), jnp.bool_)
        for _ in range(MOBA_TOPK):
            mx = jnp.max(g, axis=0, keepdims=True)
            first = jnp.min(jnp.where(g == mx, nidx, nb), axis=0, keepdims=True)
            pick = nidx == first
            sel = sel | pick
            g = jnp.where(pick, -jnp.inf, g)
        neg_ref[head * nb:(head + 1) * nb, :] = jnp.where(sel & valid, 0.0, MASK_NEG)

    def scores(head, j, add):
        kb = k_ref[j, :, pair_cols(head)]
        return lax.dot_general(kb, q_head(head), _NT, preferred_element_type=F32) + add

    def weighted_values(head, j, p):
        return jnp.dot(vt_ref[j, head_rows(head), :], p.astype(BF16), preferred_element_type=F32)

    def start(head, j, st):
        m = jnp.max(st, axis=0, keepdims=True)
        p = jnp.exp(st - m)
        m_refs[head][...] = m
        l_refs[head][...] = jnp.sum(p, axis=0, keepdims=True)
        acc_refs[head][...] = weighted_values(head, j, p)

    def update(head, j, st):
        m = m_refs[head][...]
        m_new = jnp.maximum(m, jnp.max(st, axis=0, keepdims=True))
        alpha = jnp.exp(m - m_new)
        p = jnp.exp(st - m_new)
        m_refs[head][...] = m_new
        l_refs[head][...] = alpha * l_refs[head][...] + jnp.sum(p, axis=0, keepdims=True)
        acc_refs[head][...] = alpha * acc_refs[head][...] + weighted_values(head, j, p)

    def sweep(j, add_of, absorb):
        st = scores(0, j, add_of(0))
        for head in range(ATT_HEADS):
            nxt = scores(head + 1, j, add_of(head + 1)) if head + 1 < ATT_HEADS else None
            absorb(head, j, st)
            st = nxt

    sweep(i, lambda h: bias_ref[h, 0], start)

    @pl.when(i > 0)
    def _():
        sweep(i - 1, lambda h: bias_ref[h, 1] + neg_ref[pl.ds(h * nb + i - 1, 1), :], update)

    def far_block(j, carry):
        sweep(j, lambda h: neg_ref[pl.ds(h * nb + j, 1), :] + far_ref[h], update)
        return carry

    lax.fori_loop(0, jnp.maximum(i - 1, 0), far_block, 0)

    for hp in range(ATT_HEADS // 2):
        outs = [acc_refs[h][...] / l_refs[h][...] for h in (2 * hp, 2 * hp + 1)]
        pair = jnp.concatenate(outs, axis=0)
        o_ref[0, :, hp * pair_w:(hp + 1) * pair_w] = pair.T.astype(BF16)


def _rel_bucket(dist):
    n = jnp.maximum(dist, 0)
    max_exact = REL_BUCKETS // 2
    nf = jnp.maximum(n, 1).astype(F32)
    large = max_exact + (jnp.log(nf / max_exact) / math.log(REL_MAX_DIST / max_exact)
                         * (REL_BUCKETS - max_exact)).astype(jnp.int32)
    large = jnp.minimum(large, REL_BUCKETS - 1)
    return jnp.where(n < max_exact, n, large)


def _moba_bias_tiles(rel_bias):
    blk = MOBA_BLOCK
    span = 2 * blk
    vec = rel_bias[_rel_bucket(jnp.arange(span))].T.astype(F32)
    masked = jnp.full_like(vec[:, :blk], MASK_NEG)
    ring_own = jnp.concatenate([vec[:, :blk], masked], axis=1)
    ring_prev = jnp.concatenate([vec[:, blk:], vec[:, :blk]], axis=1)

    def toeplitz(ring):
        flat = jnp.tile(ring, (1, blk))[:, :blk * (span - 1)]
        return flat.reshape(-1, blk, span - 1)[:, :, :blk]

    tiles = jnp.stack([toeplitz(ring_own), toeplitz(ring_prev)], axis=1)
    far = rel_bias[_rel_bucket(jnp.array(span))]
    return tiles, far.astype(F32)


def _moba(q4, k4, vt4, bias_tiles, far_bias, bsz):
    nb_total, blk, a = q4.shape
    nb = nb_total // bsz
    return pl.pallas_call(
        _moba_kernel,
        grid=(bsz, nb),
        in_specs=[
            pl.BlockSpec(memory_space=pltpu.SMEM),
            pl.BlockSpec((1, blk, a), lambda b, i: (b * nb + i, 0, 0)),
            pl.BlockSpec((nb, blk, a), lambda b, i: (b, 0, 0)),
            pl.BlockSpec((nb, a, blk), lambda b, i: (b, 0, 0)),
            _full(bias_tiles.shape),
        ],
        out_specs=pl.BlockSpec((1, blk, a), lambda b, i: (b * nb + i, 0, 0)),
        out_shape=jax.ShapeDtypeStruct((nb_total, blk, a), BF16),
        scratch_shapes=[pltpu.VMEM((nb, a), F32), pltpu.VMEM((ATT_HEADS * nb, blk), F32)]
        + [pltpu.VMEM((1, blk), F32), pltpu.VMEM((1, blk), F32), pltpu.VMEM((HEAD_DIM, blk), F32)] * ATT_HEADS,
        compiler_params=_cparams("arbitrary", "arbitrary"),
        name="moba_attention",
    )(far_bias, q4, k4, vt4, bias_tiles)


def _ssm_weights(a_re, a_im, log_dt, b_re, b_im, c_re, c_im, d_skip):
    g, p, cw, n = SSM_GROUPS, SSM_STATE, SSM_GROUP, SSM_CHUNK
    gs = LANES // cw
    nq = g // gs
    npair = n // 2
    pairs_per_set = gs // 2
    lam_re = jnp.minimum(a_re.astype(F32), -1e-4)
    lam_im = a_im.astype(F32)
    dt = jnp.exp(log_dt.astype(F32))[:, None]
    mag = jnp.exp(lam_re * dt)
    lb_re = mag * jnp.cos(lam_im * dt)
    lb_im = mag * jnp.sin(lam_im * dt)
    n_re = lb_re - 1.0
    n_im = lb_im
    den = lam_re * lam_re + lam_im * lam_im
    z_re = ((n_re * lam_re + n_im * lam_im) / den)[..., None]
    z_im = ((n_im * lam_re - n_re * lam_im) / den)[..., None]
    br, bi = b_re.astype(F32), b_im.astype(F32)
    bb_re = z_re * br - z_im * bi
    bb_im = z_re * bi + z_im * br
    pw_re, pw_im = [jnp.ones_like(lb_re)], [jnp.zeros_like(lb_im)]
    for _ in range(n):
        r, im = pw_re[-1], pw_im[-1]
        pw_re.append(r * lb_re - im * lb_im)
        pw_im.append(r * lb_im + im * lb_re)
    pw_re, pw_im = jnp.stack(pw_re), jnp.stack(pw_im)
    cr, ci = c_re.astype(F32), c_im.astype(F32)
    hi = lax.Precision.HIGHEST
    rev_re, rev_im = pw_re[n - 1::-1], pw_im[n - 1::-1]
    sb_re = jnp.einsum('sgp,gpc->sgcp', rev_re, bb_re) - jnp.einsum('sgp,gpc->sgcp', rev_im, bb_im)
    sb_im = jnp.einsum('sgp,gpc->sgcp', rev_re, bb_im) + jnp.einsum('sgp,gpc->sgcp', rev_im, bb_re)
    cl_re = jnp.einsum('gcp,tgp->tgpc', cr, pw_re[1:]) - jnp.einsum('gcp,tgp->tgpc', ci, pw_im[1:])
    cl_im = jnp.einsum('gcp,tgp->tgpc', cr, pw_im[1:]) + jnp.einsum('gcp,tgp->tgpc', ci, pw_re[1:])
    cp_re = jnp.einsum('gcp,tgp->tgcp', cr, pw_re[:n]) - jnp.einsum('gcp,tgp->tgcp', ci, pw_im[:n])
    cp_im = jnp.einsum('gcp,tgp->tgcp', cr, pw_im[:n]) + jnp.einsum('gcp,tgp->tgcp', ci, pw_re[:n])
    kern = (jnp.einsum('tgcp,gpd->gtcd', cp_re, bb_re, precision=hi)
            - jnp.einsum('tgcp,gpd->gtcd', cp_im, bb_im, precision=hi))

    kp = jnp.concatenate([jnp.zeros_like(kern[:, :1]), kern], axis=1)
    dl = jnp.arange(npair)[:, None, None]
    s0 = jnp.arange(2)[None, :, None]
    t0 = jnp.arange(2)[None, None, :]
    lag = 2 * dl + t0 - s0
    kl = kp[:, lag + 1].reshape(nq, gs, npair, 2, 2, cw, cw)
    base = kl.transpose(0, 2, 3, 1, 6, 4, 5)
    eye = jnp.eye(gs, dtype=F32)
    w_toe = (base[:, :, :, :, :, :, None, :] * eye[None, None, None, :, None, None, :, None])
    w_toe = w_toe.reshape(nq, npair, MXU_DIM, MXU_DIM).astype(BF16)

    member = (jnp.arange(gs)[None, :, None]
              == 2 * jnp.arange(pairs_per_set)[:, None, None] + jnp.arange(2)[None, None, :]).astype(F32)
    sbs = jnp.stack([sb_re, sb_im]).reshape(2, npair, 2, nq, gs, cw, p)
    sbs = sbs.transpose(3, 1, 2, 4, 5, 0, 6)
    w_in_state = (sbs[:, None, :, :, :, :, :, None, :]
                  * member[None, :, None, None, :, None, None, :, None])
    w_in_state = w_in_state.reshape(g // 2, npair, MXU_DIM, MXU_DIM).astype(BF16)
    sos = jnp.stack([cl_re, -cl_im]).reshape(2, npair, 2, nq, gs, p, cw)
    sos = sos.transpose(3, 1, 0, 5, 2, 4, 6)
    member_t = member.transpose(0, 2, 1)
    w_state_out = (sos[:, None, :, :, None, :, :, :, :]
                   * member_t[None, :, None, None, :, None, None, :, None])
    w_state_out = w_state_out.reshape(g // 2, npair, MXU_DIM, MXU_DIM).astype(BF16)
    decay = jnp.stack([pw_re[n].reshape(g // 2, 2 * p), pw_im[n].reshape(g // 2, 2 * p)])
    dvec = d_skip.astype(F32).reshape(1, g * cw)
    return w_toe, w_in_state, w_state_out, decay, dvec


def _ssm_kernel(u_ref, wt_ref, wi_ref, wo_ref, dec_ref, d_ref, y_ref, s_ref, xp_ref, x_ref, *, bsz):
    kt = u_ref.shape[0]
    n = u_ref.shape[1] // bsz
    rows = kt * bsz
    npair = n // 2
    nq = wt_ref.shape[0]
    npairs_g = wi_ref.shape[0]
    per_set = npairs_g // nq
    half = MXU_DIM // 2

    @pl.when(pl.program_id(0) == 0)
    def _():
        x_ref[...] = jnp.zeros_like(x_ref)

    def piece(s, q):
        return u_ref[:, s * bsz:(s + 1) * bsz, q * LANES:(q + 1) * LANES].reshape(rows, LANES)

    lhs = {(sp, q): jnp.concatenate([piece(2 * sp, q), piece(2 * sp + 1, q)], axis=-1)
           for sp in range(npair) for q in range(nq)}

    for gp in range(npairs_g):
        q = gp // per_set
        acc = jnp.dot(lhs[0, q], wi_ref[gp, 0], preferred_element_type=F32)
        for sp in range(1, npair):
            acc = acc + jnp.dot(lhs[sp, q], wi_ref[gp, sp], preferred_element_type=F32)
        s_ref[:, gp * MXU_DIM:(gp + 1) * MXU_DIM] = acc

    for gp in range(npairs_g):
        re_cols = slice(gp * MXU_DIM, gp * MXU_DIM + half)
        im_cols = slice(gp * MXU_DIM + half, (gp + 1) * MXU_DIM)
        ar = dec_ref[0, gp:gp + 1, :]
        ai = dec_ref[1, gp:gp + 1, :]
        xr = x_ref[:, re_cols]
        xi = x_ref[:, im_cols]
        for kc in range(kt):
            rs = slice(kc * bsz, (kc + 1) * bsz)
            xp_ref[rs, re_cols] = xr.astype(BF16)
            xp_ref[rs, im_cols] = xi.astype(BF16)
            sr = s_ref[rs, re_cols]
            si = s_ref[rs, im_cols]
            xr, xi = ar * xr - ai * xi + sr, ar * xi + ai * xr + si
        x_ref[:, re_cols] = xr
        x_ref[:, im_cols] = xi

    for q in range(nq):
        for tp in range(npair):
            acc = jnp.dot(lhs[0, q], wt_ref[q, tp], preferred_element_type=F32)
            for sp in range(1, tp + 1):
                acc = acc + jnp.dot(lhs[sp, q], wt_ref[q, tp - sp], preferred_element_type=F32)
            for gp in range(q * per_set, (q + 1) * per_set):
                acc = acc + jnp.dot(xp_ref[:, gp * MXU_DIM:(gp + 1) * MXU_DIM], wo_ref[gp, tp],
                                    preferred_element_type=F32)
            for t0 in range(2):
                s = 2 * tp + t0
                y = acc[:, t0 * half:(t0 + 1) * half] + d_ref[:, q * LANES:(q + 1) * LANES] * piece(s, q).astype(F32)
                y_ref[:, s * bsz:(s + 1) * bsz, q * LANES:(q + 1) * LANES] = (
                    jax.nn.gelu(y).astype(BF16).reshape(kt, bsz, LANES))


def _ssm(u_tm, weights, bsz, seq):
    w_toe, w_in_state, w_state_out, decay, dvec = weights
    n = SSM_CHUNK
    kt = SSM_CHUNKS_PER_STEP
    nchunk = seq // n
    rows = kt * bsz
    state_w = SSM_GROUPS * 2 * SSM_STATE
    u3 = u_tm.reshape(nchunk, n * bsz, SSM_WIDTH)
    once = pl.Buffered(1)
    blk = pl.BlockSpec((kt, n * bsz, SSM_WIDTH), lambda i: (i, 0, 0))
    y3 = pl.pallas_call(
        functools.partial(_ssm_kernel, bsz=bsz),
        grid=(nchunk // kt,),
        in_specs=[blk, _full(w_toe.shape, pipeline_mode=once), _full(w_in_state.shape, pipeline_mode=once),
                  _full(w_state_out.shape, pipeline_mode=once), _full(decay.shape), _full(dvec.shape)],
        out_specs=blk,
        out_shape=jax.ShapeDtypeStruct(u3.shape, BF16),
        scratch_shapes=[pltpu.VMEM((rows, state_w), F32), pltpu.VMEM((rows, state_w), BF16),
                        pltpu.VMEM((bsz, state_w), F32)],
        compiler_params=_cparams("arbitrary"),
        name="s5_chunked_scan",
    )(u3, w_toe, w_in_state, w_state_out, decay, dvec)
    return y3.reshape(seq, bsz * SSM_WIDTH)


def _postmix_kernel(att_ref, ys_ref, ga_ref, gs_ref, x_ref, gt_ref, sc_ref, sh_ref, gpost_ref, gpre_ref,
                    wglu_ref, wso_ref, wao_ref, wo_ref, wrh_ref, wrl_ref, br_ref,
                    x1_ref, h2_ref, idx_ref, gate_ref, rank_ref, cnt_ref, run_ref):
    tm = x_ref.shape[0]
    ne = wrh_ref.shape[0]

    @pl.when(pl.program_id(0) == 0)
    def _():
        run_ref[...] = jnp.zeros_like(run_ref)

    glu = jnp.dot(ys_ref[...], wglu_ref[...], preferred_element_type=F32)
    sg = (glu[:, :SSM_WIDTH] * jax.nn.sigmoid(glu[:, SSM_WIDTH:])).astype(BF16)
    ssm = jnp.dot(sg, wso_ref[...], preferred_element_type=F32)
    att = jnp.dot(att_ref[...], wao_ref[...], preferred_element_type=F32)
    merged = (ga_ref[...].astype(F32) * att + gs_ref[...].astype(F32) * ssm).astype(BF16)
    y = jnp.dot(merged, wo_ref[...], preferred_element_type=F32)
    x1 = x_ref[...] + gt_ref[0] * _rms(y, gpost_ref[...])
    x1_ref[...] = x1
    h2 = _rms(x1, gpre_ref[...]) * (1.0 + sc_ref[0]) + sh_ref[0]
    h2_ref[...] = h2
    h_hi = h2.astype(BF16)
    h_lo = (h2 - h_hi.astype(F32)).astype(BF16)
    logits = (lax.dot_general(wrh_ref[...], h_hi, _NT, preferred_element_type=F32)
              + lax.dot_general(wrh_ref[...], h_lo, _NT, preferred_element_type=F32)
              + lax.dot_general(wrl_ref[...], h_hi, _NT, preferred_element_type=F32)
              + br_ref[...])
    eidx = lax.broadcasted_iota(jnp.int32, (ne, tm), 0)
    lg = logits
    vals, idxs = [], []
    for _ in range(TOP_K):
        mx = jnp.max(lg, axis=0, keepdims=True)
        first = jnp.min(jnp.where(lg == mx, eidx, ne), axis=0, keepdims=True)
        vals.append(mx)
        idxs.append(first)
        lg = jnp.where(eidx == first, -jnp.inf, lg)
    exps = [jnp.exp(v - vals[0]) for v in vals]
    denom = exps[0] + exps[1] + exps[2] + exps[3]
    gate_ref[...] = jnp.concatenate([e / denom for e in exps], axis=0)
    idx_ref[...] = jnp.concatenate(idxs, axis=0)
    onehot = jnp.where(lg == -jnp.inf, 1.0, 0.0)
    t_row = lax.broadcasted_iota(jnp.int32, (tm, tm), 0)
    t_col = lax.broadcasted_iota(jnp.int32, (tm, tm), 1)
    earlier = jnp.where(t_row < t_col, 1.0, 0.0).astype(BF16)
    before = jnp.dot(onehot.astype(BF16), earlier, preferred_element_type=F32) + run_ref[:, 0:1]
    ranks = [jnp.sum(jnp.where(eidx == ix, before, 0.0), axis=0, keepdims=True) for ix in idxs]
    rank_ref[...] = jnp.concatenate(ranks, axis=0).astype(jnp.int32)
    run_ref[...] = run_ref[...] + jnp.sum(onehot, axis=1, keepdims=True)
    cnt_ref[...] = run_ref[...].astype(jnp.int32)


def _postmix(att2, ys_tm, ga, gs, x2, gt1, sc2, sh2, g_post, g_pre, w_glu, w_ssm_out, w_att_out, w_out,
             w_router, b_router, seq):
    t, d = x2.shape
    tm = TOKEN_TILE
    tiles_per_seq = seq // tm
    ne = w_router.shape[1]
    wr_t = w_router.T.astype(F32)
    wr_hi = wr_t.astype(BF16)
    wr_lo = (wr_t - wr_hi.astype(F32)).astype(BF16)
    mod_spec = pl.BlockSpec((1, 1, d), lambda i: (i // tiles_per_seq, 0, 0))
    row = lambda cols: pl.BlockSpec((tm, cols), lambda i: (i, 0))
    time_major = pl.BlockSpec((tm, SSM_WIDTH), lambda i: (i % tiles_per_seq, i // tiles_per_seq))
    col = pl.BlockSpec((TOP_K, tm), lambda i: (0, i))
    weights = [w_glu.astype(BF16), w_ssm_out.astype(BF16), w_att_out.astype(BF16), w_out.astype(BF16),
               wr_hi, wr_lo, b_router.astype(F32).reshape(ne, 1)]
    return pl.pallas_call(
        _postmix_kernel,
        grid=(t // tm,),
        in_specs=[row(ATT_WIDTH), time_major, row(d), row(d), row(d), mod_spec, mod_spec, mod_spec,
                  _full((1, d)), _full((1, d))] + [_full(w.shape) for w in weights],
        out_specs=[row(d), row(d), col, col, col, _full((ne, 128))],
        out_shape=[jax.ShapeDtypeStruct((t, d), F32), jax.ShapeDtypeStruct((t, d), F32),
                   jax.ShapeDtypeStruct((TOP_K, t), jnp.int32), jax.ShapeDtypeStruct((TOP_K, t), F32),
                   jax.ShapeDtypeStruct((TOP_K, t), jnp.int32), jax.ShapeDtypeStruct((ne, 128), jnp.int32)],
        scratch_shapes=[pltpu.VMEM((ne, 128), F32)],
        compiler_params=_cparams("arbitrary"),
        name="postmix_router",
    )(att2, ys_tm, ga, gs, x2, gt1, sc2, sh2, g_post.reshape(1, d), g_pre.reshape(1, d), *weights)


def _sc_worker_base(per_worker):
    return (lax.axis_index("s") * SC_CORES + lax.axis_index("c")) * per_worker


def _sc_mesh():
    return plsc.VectorSubcoreMesh(core_axis_name="c", subcore_axis_name="s",
                                  num_cores=SC_CORES, num_subcores=SC_SUBCORES)


def _sc_scatter_kernel(rows_hbm, idx_hbm, out_hbm, idx_v, rows_v, load_sems, store_sems, *, per_worker):
    chunk = rows_v.shape[1]
    n_tokens = rows_hbm.shape[0]
    n_chunks = per_worker // chunk
    base = _sc_worker_base(per_worker)

    def load(c, b):
        off = pl.multiple_of(base + c * chunk, chunk)
        for k in range(TOP_K):
            pltpu.sync_copy(idx_hbm.at[pl.ds(k * n_tokens + off, chunk)], idx_v.at[b, k])
        return pltpu.make_async_copy(rows_hbm.at[pl.ds(off, chunk)], rows_v.at[b], load_sems.at[b])

    def stores(b):
        return [pltpu.make_async_copy(rows_v.at[b], out_hbm.at[idx_v.at[b, k]], store_sems.at[b])
                for k in range(TOP_K)]

    load(0, 0).start()

    @pl.loop(0, n_chunks, step=2)
    def _(c0):
        for b in range(2):
            c = c0 + b

            @pl.when(c > 0)
            def _():
                for cp in stores(1 - b):
                    cp.wait()

            @pl.when(c + 1 < n_chunks)
            def _():
                load(c + 1, 1 - b).start()

            pltpu.make_async_copy(rows_hbm.at[pl.ds(0, chunk)], rows_v.at[b], load_sems.at[b]).wait()
            for cp in stores(b):
                cp.start()

    for cp in stores((n_chunks - 1) % 2):
        cp.wait()


def _sc_scatter_rows(rows, idx, n_out):
    t, d = rows.shape
    workers = SC_CORES * SC_SUBCORES
    per_worker = t // workers
    assert t % workers == 0 and per_worker % (2 * SC_GATHER_ROWS) == 0
    return pl.kernel(
        functools.partial(_sc_scatter_kernel, per_worker=per_worker),
        out_type=jax.ShapeDtypeStruct((n_out, d), rows.dtype),
        mesh=_sc_mesh(),
        scratch_types=[pltpu.VMEM((2, TOP_K, SC_GATHER_ROWS), jnp.int32),
                       pltpu.VMEM((2, SC_GATHER_ROWS, d), rows.dtype),
                       pltpu.SemaphoreType.DMA((2,)), pltpu.SemaphoreType.DMA((2,))],
        name="sc_row_scatter",
    )(rows, idx)


def _sc_gather_kernel(table_hbm, idx_hbm, out_hbm, idx_v, rows_v, sems, *, per_worker):
    chunk = rows_v.shape[1]
    n_chunks = per_worker // chunk
    base = _sc_worker_base(per_worker)

    def gather(c, b):
        off = pl.multiple_of(base + c * chunk, chunk)
        pltpu.sync_copy(idx_hbm.at[pl.ds(off, chunk)], idx_v.at[b])
        return pltpu.make_async_copy(table_hbm.at[idx_v.at[b]], rows_v.at[b], sems.at[b])

    gather(0, 0).start()

    @pl.loop(0, n_chunks, step=2)
    def _(c0):
        for b in range(2):
            c = c0 + b

            @pl.when(c + 1 < n_chunks)
            def _():
                gather(c + 1, 1 - b).start()

            pltpu.make_async_copy(table_hbm.at[idx_v.at[b]], rows_v.at[b], sems.at[b]).wait()
            off = pl.multiple_of(base + c * chunk, chunk)
            pltpu.sync_copy(rows_v.at[b], out_hbm.at[pl.ds(off, chunk)])


def _sc_gather_rows(table, idx):
    n = idx.shape[0]
    d = table.shape[1]
    workers = SC_CORES * SC_SUBCORES
    per_worker = n // workers
    assert n % workers == 0 and per_worker % (2 * SC_GATHER_ROWS) == 0
    return pl.kernel(
        functools.partial(_sc_gather_kernel, per_worker=per_worker),
        out_type=jax.ShapeDtypeStruct((n, d), table.dtype),
        mesh=_sc_mesh(),
        scratch_types=[pltpu.VMEM((2, SC_GATHER_ROWS), jnp.int32),
                       pltpu.VMEM((2, SC_GATHER_ROWS, d), table.dtype),
                       pltpu.SemaphoreType.DMA((2,))],
        name="sc_row_gather",
    )(table, idx)


def _experts_kernel(be_ref, nused_ref, valid_ref, x_ref, w1_ref, b1_ref, w2_ref, b2_ref, y_ref,
                    w1b_ref, w2b_ref):
    i = pl.program_id(0)
    prev = be_ref[jnp.maximum(i - 1, 0)]
    fresh = (i == 0) | (be_ref[i] != prev)

    @pl.when(fresh)
    def _():
        w1b_ref[...] = w1_ref[...].astype(BF16)
        w2b_ref[...] = w2_ref[...].astype(BF16)

    @pl.when(i < nused_ref[0])
    def _():
        row = lax.broadcasted_iota(jnp.int32, x_ref.shape, 0)
        x = jnp.where(row < valid_ref[i], x_ref[...], 0.0).astype(BF16)
        gu = jnp.dot(x, w1b_ref[...], preferred_element_type=F32) + b1_ref[...]
        g = jnp.minimum(gu[:, :D_FF], SWIGLU_LIMIT)
        up = jnp.clip(gu[:, D_FF:], -SWIGLU_LIMIT, SWIGLU_LIMIT)
        act = ((up + 1.0) * g * jax.nn.sigmoid(SWIGLU_ALPHA * g)).astype(BF16)
        y_ref[...] = jnp.dot(act, w2b_ref[...], preferred_element_type=F32) + b2_ref[...]

    @pl.when(i >= nused_ref[0])
    def _():
        y_ref[...] = jnp.zeros_like(y_ref)


def _experts(xb, block_e, n_used, valid, w1, b1, w2, b2, layer):
    p_rows, d = xb.shape
    depth, ne = w1.shape[:2]
    rb = EXPERT_ROWS
    wmap = lambda i, be, nu, va: (layer, be[i], 0, 0)
    grid_spec = pltpu.PrefetchScalarGridSpec(
        num_scalar_prefetch=3,
        grid=(p_rows // rb,),
        in_specs=[
            pl.BlockSpec((rb, d), lambda i, be, nu, va: (i, 0)),
            pl.BlockSpec((None, None, d, 2 * D_FF), wmap),
            pl.BlockSpec((None, None, 1, 2 * D_FF), wmap),
            pl.BlockSpec((None, None, D_FF, d), wmap),
            pl.BlockSpec((None, None, 1, d), wmap),
        ],
        out_specs=pl.BlockSpec((rb, d), lambda i, be, nu, va: (i, 0)),
        scratch_shapes=[pltpu.VMEM((d, 2 * D_FF), BF16), pltpu.VMEM((D_FF, d), BF16)],
    )
    return pl.pallas_call(
        _experts_kernel,
        grid_spec=grid_spec,
        out_shape=jax.ShapeDtypeStruct((p_rows, d), F32),
        compiler_params=_cparams("arbitrary"),
        name="expert_ffn",
    )(block_e, n_used, valid, xb, w1, b1.reshape(depth, ne, 1, 2 * D_FF), w2, b2.reshape(depth, ne, 1, d))


def _combine_kernel(gate_ref, x_ref, gt_ref, g_ref, y0_ref, y1_ref, y2_ref, y3_ref, o_ref):
    tm = x_ref.shape[0]
    gates = gate_ref[...]
    gates = jnp.concatenate([gates, jnp.zeros((LANES - TOP_K, tm), F32)], axis=0).T
    y = gates[:, 0:1] * y0_ref[...]
    for k, y_ref in enumerate((y1_ref, y2_ref, y3_ref), start=1):
        y = y + gates[:, k:k + 1] * y_ref[...]
    o_ref[...] = x_ref[...] + gt_ref[0] * _rms(y, g_ref[...])


def _combine(y4, gate_t, x2, gt2, g_post, seq):
    t, d = x2.shape
    tm = TOKEN_TILE
    tiles = t // tm
    tiles_per_seq = seq // tm
    row = pl.BlockSpec((tm, d), lambda i: (i, 0))
    slot = lambda k: pl.BlockSpec((tm, d), lambda i: (k * tiles + i, 0))
    return pl.pallas_call(
        _combine_kernel,
        grid=(tiles,),
        in_specs=[pl.BlockSpec((TOP_K, tm), lambda i: (0, i)),
                  row, pl.BlockSpec((1, 1, d), lambda i: (i // tiles_per_seq, 0, 0)), _full((1, d))]
        + [slot(k) for k in range(TOP_K)],
        out_specs=row,
        out_shape=jax.ShapeDtypeStruct((t, d), F32),
        compiler_params=_cparams("arbitrary"),
        name="expert_combine",
    )(gate_t, x2, gt2, g_post.reshape(1, d), y4, y4, y4, y4)


def _route_plan(idx_t, rank_t, counts):
    rb = EXPERT_ROWS
    k, t = idx_t.shape
    padded = (counts + rb - 1) // rb * rb
    pad_ends = jnp.cumsum(padded)
    pad_starts = pad_ends - padded
    experts = jnp.arange(N_EXPERTS, dtype=jnp.int32)
    start_of = jnp.sum(jnp.where(idx_t[None] == experts[:, None, None], pad_starts[:, None, None], 0), axis=0)
    dest = (start_of + rank_t).astype(jnp.int32)
    n_blocks = (k * t) // rb + N_EXPERTS
    blk_start = jnp.arange(n_blocks, dtype=jnp.int32) * rb
    block_e = jnp.minimum(jnp.sum(pad_ends[None, :] <= blk_start[:, None], axis=1), N_EXPERTS - 1)
    onehot_e = block_e[:, None] == experts[None, :]
    cnt_b = jnp.sum(jnp.where(onehot_e, counts[None, :], 0), axis=1)
    start_b = jnp.sum(jnp.where(onehot_e, pad_starts[None, :], 0), axis=1)
    valid = jnp.clip(cnt_b - (blk_start - start_b), 0, rb).astype(jnp.int32)
    n_used = (pad_ends[-1] // rb).astype(jnp.int32).reshape(1)
    return dest.reshape(-1), block_e.astype(jnp.int32), n_used, valid, n_blocks * rb


def kernel(x, c, rel_bias, w_ada, b_ada, g_pre_mix, g_post_mix, g_pre_ffn, g_post_ffn, w_in, ssm_a_re, ssm_a_im, ssm_log_dt, ssm_b_re, ssm_b_im, ssm_c_re, ssm_c_im, ssm_d, w_glu, w_ssm_out, w_att_out, w_out, w_router, b_router, w_exp_in, b_exp_in, w_exp_out, b_exp_out):
    bsz, seq, d = x.shape
    depth = w_ada.shape[0]
    t = bsz * seq
    assert d == D_MODEL and seq % TOKEN_TILE == 0 and TOKEN_TILE % MOBA_BLOCK == 0
    assert seq % (SSM_CHUNK * SSM_CHUNKS_PER_STEP) == 0

    mod = _ada_mod(c, w_ada, b_ada)
    bias_tiles, far_bias = _moba_bias_tiles(rel_bias.astype(F32))
    x2 = x.reshape(t, d)
    for l in range(depth):
        sh1, sc1, gt1, sh2, sc2, gt2 = [m.reshape(bsz, 1, d) for m in jnp.split(mod[l], N_MOD, axis=-1)]
        q4, k4, vt4, u_tm, ga, gs = _premix(x2, g_pre_mix[l], sc1, sh1, w_in[l], seq)
        att = _moba(q4, k4, vt4, bias_tiles, far_bias, bsz).reshape(t, ATT_WIDTH)
        ssm_w = _ssm_weights(ssm_a_re[l], ssm_a_im[l], ssm_log_dt[l], ssm_b_re[l], ssm_b_im[l],
                             ssm_c_re[l], ssm_c_im[l], ssm_d[l])
        ys_tm = _ssm(u_tm, ssm_w, bsz, seq)
        x1, h2, idx_t, gate_t, rank_t, cnt = _postmix(
            att, ys_tm, ga, gs, x2, gt1, sc2, sh2, g_post_mix[l], g_pre_ffn[l],
            w_glu[l], w_ssm_out[l], w_att_out[l], w_out[l], w_router[l], b_router[l], seq)
        dest_flat, block_e, n_used, valid, p_rows = _route_plan(idx_t, rank_t, cnt[:, 0])
        xb = _sc_scatter_rows(h2, dest_flat, p_rows)
        yb = _experts(xb, block_e, n_used, valid, w_exp_in, b_exp_in, w_exp_out, b_exp_out, l)
        y4 = _sc_gather_rows(yb, dest_flat)
        x2 = _combine(y4, gate_t, x1, gt2, g_post_ffn[l], seq)
    return x2.reshape(bsz, seq, d)
```

```python
import functools
import math

import jax
import jax.numpy as jnp
from jax import lax
from jax.experimental import pallas as pl
from jax.experimental.pallas import tpu as pltpu
from jax.experimental.pallas import tpu_sc as plsc

F32 = jnp.float32
BF16 = jnp.bfloat16

D_MODEL = 1024
ATT_HEADS = 8
HEAD_DIM = 64
ATT_WIDTH = ATT_HEADS * HEAD_DIM
MOBA_BLOCK = 256
MOBA_TOPK = 3
REL_BUCKETS = 32
REL_MAX_DIST = 128
SSM_WIDTH = D_MODEL // 2
SSM_GROUP = 16
SSM_GROUPS = SSM_WIDTH // SSM_GROUP
SSM_STATE = 64
N_EXPERTS = 32
TOP_K = 4
D_FF = D_MODEL
SWIGLU_ALPHA = 1.702
SWIGLU_LIMIT = 7.0
RMS_EPS = 1e-6
N_MOD = 6

SSM_CHUNK = 8
SSM_CHUNKS_PER_STEP = 16
LANES = 128
MXU_DIM = 256
TOKEN_TILE = 512
SC_CORES = 2
SC_SUBCORES = 16
SC_GATHER_ROWS = 64
EXPERT_ROWS = 256
MASK_NEG = -1e30
VMEM_LIMIT = 56 * 1024 * 1024

_NT = (((1,), (1,)), ((), ()))


def _cparams(*sem):
    return pltpu.CompilerParams(dimension_semantics=sem, vmem_limit_bytes=VMEM_LIMIT)


def _pack_bf16_pairs(x):
    n = x.shape[1] // 2
    bits = lax.bitcast_convert_type(x.astype(BF16).astype(F32), jnp.uint32)
    return lax.shift_right_logical(bits[:, :n], jnp.uint32(16)) | (bits[:, n:] & jnp.uint32(0xFFFF0000))


def _unpack_bf16_pairs(w):
    lo = lax.bitcast_convert_type(lax.shift_left(w, jnp.uint32(16)), F32)
    hi = lax.bitcast_convert_type(w & jnp.uint32(0xFFFF0000), F32)
    return lo, hi


def _rms(x, g):
    return x * lax.rsqrt(jnp.mean(x * x, axis=-1, keepdims=True) + RMS_EPS) * g


def _full(shape, **kw):
    n = len(shape)
    return pl.BlockSpec(shape, lambda *_: (0,) * n, **kw)


def _ada_kernel(c_ref, w_ref, b_ref, o_ref):
    c = c_ref[...]
    cond = c * jax.nn.sigmoid(c)
    o_ref[0] = jnp.dot(cond, w_ref[0], preferred_element_type=F32,
                       precision=lax.Precision.HIGHEST) + b_ref[0]


def _ada_mod(c, w_ada, b_ada):
    depth, d, nd = w_ada.shape
    bsz = c.shape[0]
    return pl.pallas_call(
        _ada_kernel,
        grid=(depth, nd // d),
        in_specs=[
            pl.BlockSpec((bsz, d), lambda l, j: (0, 0)),
            pl.BlockSpec((1, d, d), lambda l, j: (l, 0, j)),
            pl.BlockSpec((1, 1, d), lambda l, j: (l, 0, j)),
        ],
        out_specs=pl.BlockSpec((1, bsz, d), lambda l, j: (l, 0, j)),
        out_shape=jax.ShapeDtypeStruct((depth, bsz, nd), F32),
        compiler_params=_cparams("arbitrary", "arbitrary"),
        name="ada_mod",
    )(c, w_ada, b_ada.reshape(depth, 1, nd))


def _premix_kernel(x_ref, g_ref, sc_ref, sh_ref, wq_ref, wk_ref, wvt_ref, wu_ref, wga_ref, wgs_ref,
                   q_ref, k_ref, vt_ref, u_ref, ga_ref, gs_ref):
    x = x_ref[...]
    h = _rms(x, g_ref[...]) * (1.0 + sc_ref[0]) + sh_ref[0]
    hb = h.astype(BF16)
    nblk = q_ref.shape[0]
    q = (jnp.dot(hb, wq_ref[...], preferred_element_type=F32) * (HEAD_DIM ** -0.5)).astype(BF16)
    k = jnp.dot(hb, wk_ref[...], preferred_element_type=F32).astype(BF16)
    vt = lax.dot_general(wvt_ref[...], hb, _NT, preferred_element_type=F32).astype(BF16)
    for r in range(nblk):
        q_ref[r] = q[r * MOBA_BLOCK:(r + 1) * MOBA_BLOCK]
        k_ref[r] = k[r * MOBA_BLOCK:(r + 1) * MOBA_BLOCK]
        vt_ref[r] = vt[:, r * MOBA_BLOCK:(r + 1) * MOBA_BLOCK]
    u_ref[...] = jnp.dot(hb, wu_ref[...], preferred_element_type=F32).astype(BF16)
    ga = jnp.dot(hb, wga_ref[...], preferred_element_type=F32)
    ga_ref[...] = jax.nn.sigmoid(ga).astype(BF16)
    gs = jnp.dot(hb, wgs_ref[...], preferred_element_type=F32)
    gs_ref[...] = jax.nn.sigmoid(gs).astype(BF16)


def _premix(x2, g, sc, sh, w_in, seq):
    t, d = x2.shape
    tm = TOKEN_TILE
    tiles_per_seq = seq // tm
    nblk = tm // MOBA_BLOCK
    a = ATT_WIDTH
    wb = w_in.astype(BF16)
    wq, wk, wv = wb[:, :a], wb[:, a:2 * a], wb[:, 2 * a:3 * a]
    wu = wb[:, 3 * a:3 * a + SSM_WIDTH]
    wga = wb[:, 3 * a + SSM_WIDTH:3 * a + SSM_WIDTH + d]
    wgs = wb[:, 3 * a + SSM_WIDTH + d:]
    mod_spec = pl.BlockSpec((1, 1, d), lambda i: (i // tiles_per_seq, 0, 0))
    blk3 = lambda rows, cols: pl.BlockSpec((nblk, rows, cols), lambda i: (i, 0, 0))
    row = lambda cols: pl.BlockSpec((tm, cols), lambda i: (i, 0))
    time_major = pl.BlockSpec((tm, SSM_WIDTH), lambda i: (i % tiles_per_seq, i // tiles_per_seq))
    nb_total = t // MOBA_BLOCK
    return pl.pallas_call(
        _premix_kernel,
        grid=(t // tm,),
        in_specs=[row(d), _full((1, d)), mod_spec, mod_spec,
                  _full(wq.shape), _full(wk.shape), _full((a, d)), _full(wu.shape),
                  _full(wga.shape), _full(wgs.shape)],
        out_specs=[blk3(MOBA_BLOCK, a), blk3(MOBA_BLOCK, a), blk3(a, MOBA_BLOCK),
                   time_major, row(d), row(d)],
        out_shape=[jax.ShapeDtypeStruct((nb_total, MOBA_BLOCK, a), BF16),
                   jax.ShapeDtypeStruct((nb_total, MOBA_BLOCK, a), BF16),
                   jax.ShapeDtypeStruct((nb_total, a, MOBA_BLOCK), BF16),
                   jax.ShapeDtypeStruct((seq, (t // seq) * SSM_WIDTH), BF16),
                   jax.ShapeDtypeStruct((t, d), BF16),
                   jax.ShapeDtypeStruct((t, d), BF16)],
        compiler_params=_cparams("arbitrary"),
        name="premix_inproj",
    )(x2, g.reshape(1, d), sc, sh, wq, wk, wv.T, wu, wga, wgs)


def _moba_kernel(far_ref, q_ref, k_ref, vt_ref, bias_ref, o_ref, kmean_ref, neg_ref, *state_refs):
    m_refs = state_refs[0::3]
    l_refs = state_refs[1::3]
    acc_refs = state_refs[2::3]
    nb = k_ref.shape[0]
    blk = MOBA_BLOCK
    i = pl.program_id(1)

    @pl.when(i == 0)
    def _():
        for n in range(nb):
            kmean_ref[n:n + 1, :] = jnp.mean(k_ref[n].astype(F32), axis=0, keepdims=True)

    lane = lax.broadcasted_iota(jnp.int32, (blk, 2 * HEAD_DIM), 1)
    nidx = lax.broadcasted_iota(jnp.int32, (nb, blk), 0)
    valid = nidx < i
    pair_w = 2 * HEAD_DIM

    def pair_cols(head):
        return slice((head // 2) * pair_w, (head // 2 + 1) * pair_w)

    def head_rows(head):
        return slice(head * HEAD_DIM, (head + 1) * HEAD_DIM)

    def q_head(head):
        q2 = q_ref[0, :, pair_cols(head)]
        lo = (head % 2) * HEAD_DIM
        return jnp.where((lane >= lo) & (lane < lo + HEAD_DIM), q2, jnp.zeros_like(q2))

    for head in range(ATT_HEADS):
        km = kmean_ref[:, pair_cols(head)]
        km_hi = km.astype(BF16)
        km_lo = (km - km_hi.astype(F32)).astype(BF16)
        qh = q_head(head)
        gate = (lax.dot_general(km_hi, qh, _NT, preferred_element_type=F32)
                + lax.dot_general(km_lo, qh, _NT, preferred_element_type=F32))
        g = jnp.where(valid, gate, -jnp.inf)
        sel = jnp.zeros((nb, blk), jnp.bool_)
        for _ in range(MOBA_TOPK):
            mx = jnp.max(g, axis=0, keepdims=True)
            first = jnp.min(jnp.where(g == mx, nidx, nb), axis=0, keepdims=True)
            pick = nidx == first
            sel = sel | pick
            g = jnp.where(pick, -jnp.inf, g)
        neg_ref[head * nb:(head + 1) * nb, :] = jnp.where(sel & valid, 0.0, MASK_NEG)

    def scores(head, j, add):
        kb = k_ref[j, :, pair_cols(head)]
        return lax.dot_general(kb, q_head(head), _NT, preferred_element_type=F32) + add

    def weighted_values(head, j, p):
        return jnp.dot(vt_ref[j, head_rows(head), :], p.astype(BF16), preferred_element_type=F32)

    def start(head, j, st):
        m = jnp.max(st, axis=0, keepdims=True)
        p = jnp.exp(st - m)
        m_refs[head][...] = m
        l_refs[head][...] = jnp.sum(p, axis=0, keepdims=True)
        acc_refs[head][...] = weighted_values(head, j, p)

    def update(head, j, st):
        m = m_refs[head][...]
        m_new = jnp.maximum(m, jnp.max(st, axis=0, keepdims=True))
        alpha = jnp.exp(m - m_new)
        p = jnp.exp(st - m_new)
        m_refs[head][...] = m_new
        l_refs[head][...] = alpha * l_refs[head][...] + jnp.sum(p, axis=0, keepdims=True)
        acc_refs[head][...] = alpha * acc_refs[head][...] + weighted_values(head, j, p)

    def sweep(j, add_of, absorb):
        st = scores(0, j, add_of(0))
        for head in range(ATT_HEADS):
            nxt = scores(head + 1, j, add_of(head + 1)) if head + 1 < ATT_HEADS else None
            absorb(head, j, st)
            st = nxt

    sweep(i, lambda h: bias_ref[h, 0], start)

    @pl.when(i > 0)
    def _():
        sweep(i - 1, lambda h: bias_ref[h, 1] + neg_ref[pl.ds(h * nb + i - 1, 1), :], update)

    def far_block(j, carry):
        sweep(j, lambda h: neg_ref[pl.ds(h * nb + j, 1), :] + far_ref[h], update)
        return carry

    lax.fori_loop(0, jnp.maximum(i - 1, 0), far_block, 0)

    for hp in range(ATT_HEADS // 2):
        outs = [acc_refs[h][...] / l_refs[h][...] for h in (2 * hp, 2 * hp + 1)]
        pair = jnp.concatenate(outs, axis=0)
        o_ref[0, :, hp * pair_w:(hp + 1) * pair_w] = pair.T.astype(BF16)


def _rel_bucket(dist):
    n = jnp.maximum(dist, 0)
    max_exact = REL_BUCKETS // 2
    nf = jnp.maximum(n, 1).astype(F32)
    large = max_exact + (jnp.log(nf / max_exact) / math.log(REL_MAX_DIST / max_exact)
                         * (REL_BUCKETS - max_exact)).astype(jnp.int32)
    large = jnp.minimum(large, REL_BUCKETS - 1)
    return jnp.where(n < max_exact, n, large)


def _moba_bias_tiles(rel_bias):
    blk = MOBA_BLOCK
    span = 2 * blk
    vec = rel_bias[_rel_bucket(jnp.arange(span))].T.astype(F32)
    masked = jnp.full_like(vec[:, :blk], MASK_NEG)
    ring_own = jnp.concatenate([vec[:, :blk], masked], axis=1)
    ring_prev = jnp.concatenate([vec[:, blk:], vec[:, :blk]], axis=1)

    def toeplitz(ring):
        flat = jnp.tile(ring, (1, blk))[:, :blk * (span - 1)]
        return flat.reshape(-1, blk, span - 1)[:, :, :blk]

    tiles = jnp.stack([toeplitz(ring_own), toeplitz(ring_prev)], axis=1)
    far = rel_bias[_rel_bucket(jnp.array(span))]
    return tiles, far.astype(F32)


def _moba(q4, k4, vt4, bias_tiles, far_bias, bsz):
    nb_total, blk, a = q4.shape
    nb = nb_total // bsz
    return pl.pallas_call(
        _moba_kernel,
        grid=(bsz, nb),
        in_specs=[
            pl.BlockSpec(memory_space=pltpu.SMEM),
            pl.BlockSpec((1, blk, a), lambda b, i: (b * nb + i, 0, 0)),
            pl.BlockSpec((nb, blk, a), lambda b, i: (b, 0, 0)),
            pl.BlockSpec((nb, a, blk), lambda b, i: (b, 0, 0)),
            _full(bias_tiles.shape),
        ],
        out_specs=pl.BlockSpec((1, blk, a), lambda b, i: (b * nb + i, 0, 0)),
        out_shape=jax.ShapeDtypeStruct((nb_total, blk, a), BF16),
        scratch_shapes=[pltpu.VMEM((nb, a), F32), pltpu.VMEM((ATT_HEADS * nb, blk), F32)]
        + [pltpu.VMEM((1, blk), F32), pltpu.VMEM((1, blk), F32), pltpu.VMEM((HEAD_DIM, blk), F32)] * ATT_HEADS,
        compiler_params=_cparams("arbitrary", "arbitrary"),
        name="moba_attention",
    )(far_bias, q4, k4, vt4, bias_tiles)


def _ssm_weights(a_re, a_im, log_dt, b_re, b_im, c_re, c_im, d_skip):
    g, p, cw, n = SSM_GROUPS, SSM_STATE, SSM_GROUP, SSM_CHUNK
    gs = LANES // cw
    nq = g // gs
    npair = n // 2
    pairs_per_set = gs // 2
    lam_re = jnp.minimum(a_re.astype(F32), -1e-4)
    lam_im = a_im.astype(F32)
    dt = jnp.exp(log_dt.astype(F32))[:, None]
    mag = jnp.exp(lam_re * dt)
    lb_re = mag * jnp.cos(lam_im * dt)
    lb_im = mag * jnp.sin(lam_im * dt)
    n_re = lb_re - 1.0
    n_im = lb_im
    den = lam_re * lam_re + lam_im * lam_im
    z_re = ((n_re * lam_re + n_im * lam_im) / den)[..., None]
    z_im = ((n_im * lam_re - n_re * lam_im) / den)[..., None]
    br, bi = b_re.astype(F32), b_im.astype(F32)
    bb_re = z_re * br - z_im * bi
    bb_im = z_re * bi + z_im * br
    pw_re, pw_im = [jnp.ones_like(lb_re)], [jnp.zeros_like(lb_im)]
    for _ in range(n):
        r, im = pw_re[-1], pw_im[-1]
        pw_re.append(r * lb_re - im * lb_im)
        pw_im.append(r * lb_im + im * lb_re)
    pw_re, pw_im = jnp.stack(pw_re), jnp.stack(pw_im)
    cr, ci = c_re.astype(F32), c_im.astype(F32)
    hi = lax.Precision.HIGHEST
    rev_re, rev_im = pw_re[n - 1::-1], pw_im[n - 1::-1]
    sb_re = jnp.einsum('sgp,gpc->sgcp', rev_re, bb_re) - jnp.einsum('sgp,gpc->sgcp', rev_im, bb_im)
    sb_im = jnp.einsum('sgp,gpc->sgcp', rev_re, bb_im) + jnp.einsum('sgp,gpc->sgcp', rev_im, bb_re)
    cl_re = jnp.einsum('gcp,tgp->tgpc', cr, pw_re[1:]) - jnp.einsum('gcp,tgp->tgpc', ci, pw_im[1:])
    cl_im = jnp.einsum('gcp,tgp->tgpc', cr, pw_im[1:]) + jnp.einsum('gcp,tgp->tgpc', ci, pw_re[1:])
    cp_re = jnp.einsum('gcp,tgp->tgcp', cr, pw_re[:n]) - jnp.einsum('gcp,tgp->tgcp', ci, pw_im[:n])
    cp_im = jnp.einsum('gcp,tgp->tgcp', cr, pw_im[:n]) + jnp.einsum('gcp,tgp->tgcp', ci, pw_re[:n])
    kern = (jnp.einsum('tgcp,gpd->gtcd', cp_re, bb_re, precision=hi)
            - jnp.einsum('tgcp,gpd->gtcd', cp_im, bb_im, precision=hi))

    kp = jnp.concatenate([jnp.zeros_like(kern[:, :1]), kern], axis=1)
    dl = jnp.arange(npair)[:, None, None]
    s0 = jnp.arange(2)[None, :, None]
    t0 = jnp.arange(2)[None, None, :]
    lag = 2 * dl + t0 - s0
    kl = kp[:, lag + 1].reshape(nq, gs, npair, 2, 2, cw, cw)
    base = kl.transpose(0, 2, 3, 1, 6, 4, 5)
    same_group = jnp.eye(gs, dtype=F32)
    w_toe = jnp.concatenate(
        [base[..., t0, :] * same_group[:, g1][None, None, None, :, None, None]
         for t0 in range(2) for g1 in range(gs)], axis=-1)
    w_toe = w_toe.reshape(nq, npair, MXU_DIM, MXU_DIM).astype(BF16)

    member = (jnp.arange(gs)[None, :, None]
              == 2 * jnp.arange(pairs_per_set)[:, None, None] + jnp.arange(2)[None, None, :]).astype(F32)

    def pair_tiles(x):
        cols = [x[ri][:, None] * member[:, :, gl][None, :, None, None, :, None, None]
                for ri in range(2) for gl in range(2)]
        return jnp.concatenate(cols, axis=-1).reshape(g // 2, npair, MXU_DIM, MXU_DIM)

    sbs = jnp.stack([sb_re, sb_im]).reshape(2, npair, 2, nq, gs, cw, p)
    w_in_state = pair_tiles(sbs.transpose(0, 3, 1, 2, 4, 5, 6)).astype(BF16)
    sos = jnp.stack([cl_re, -cl_im]).reshape(2, npair, 2, nq, gs, p, cw)
    w_state_out = jnp.swapaxes(pair_tiles(sos.transpose(0, 3, 1, 2, 4, 6, 5)), -1, -2).astype(BF16)
    decay = jnp.stack([pw_re[n].reshape(g // 2, 2 * p), pw_im[n].reshape(g // 2, 2 * p)])
    dvec = d_skip.astype(F32).reshape(1, g * cw)
    return w_toe, w_in_state, w_state_out, decay, dvec


def _ssm_kernel(u_ref, wt_ref, wi_ref, wo_ref, dec_ref, d_ref, y_ref, s_ref, xp_ref, x_ref, *, bsz):
    kt = u_ref.shape[0]
    n = u_ref.shape[1] // bsz
    rows = kt * bsz
    npair = n // 2
    nq = wt_ref.shape[0]
    npairs_g = wi_ref.shape[0]
    per_set = npairs_g // nq
    half = MXU_DIM // 2

    @pl.when(pl.program_id(0) == 0)
    def _():
        x_ref[...] = jnp.zeros_like(x_ref)

    def piece(s, q):
        return u_ref[:, s * bsz:(s + 1) * bsz, q * LANES:(q + 1) * LANES].reshape(rows, LANES)

    lhs = {(sp, q): jnp.concatenate([piece(2 * sp, q), piece(2 * sp + 1, q)], axis=-1)
           for sp in range(npair) for q in range(nq)}

    for gp in range(npairs_g):
        q = gp // per_set
        acc = jnp.dot(lhs[0, q], wi_ref[gp, 0], preferred_element_type=F32)
        for sp in range(1, npair):
            acc = acc + jnp.dot(lhs[sp, q], wi_ref[gp, sp], preferred_element_type=F32)
        s_ref[:, gp * MXU_DIM:(gp + 1) * MXU_DIM] = acc

    for gp in range(npairs_g):
        re_cols = slice(gp * MXU_DIM, gp * MXU_DIM + half)
        im_cols = slice(gp * MXU_DIM + half, (gp + 1) * MXU_DIM)
        ar = dec_ref[0, gp:gp + 1, :]
        ai = dec_ref[1, gp:gp + 1, :]
        xr = x_ref[:, re_cols]
        xi = x_ref[:, im_cols]
        for kc in range(kt):
            rs = slice(kc * bsz, (kc + 1) * bsz)
            xp_ref[rs, re_cols] = xr.astype(BF16)
            xp_ref[rs, im_cols] = xi.astype(BF16)
            sr = s_ref[rs, re_cols]
            si = s_ref[rs, im_cols]
            xr, xi = ar * xr - ai * xi + sr, ar * xi + ai * xr + si
        x_ref[:, re_cols] = xr
        x_ref[:, im_cols] = xi

    for q in range(nq):
        for tp in range(npair):
            acc = jnp.dot(lhs[0, q], wt_ref[q, tp], preferred_element_type=F32)
            for sp in range(1, tp + 1):
                acc = acc + jnp.dot(lhs[sp, q], wt_ref[q, tp - sp], preferred_element_type=F32)
            for gp in range(q * per_set, (q + 1) * per_set):
                acc = acc + jnp.dot(xp_ref[:, gp * MXU_DIM:(gp + 1) * MXU_DIM], wo_ref[gp, tp],
                                    preferred_element_type=F32)
            for t0 in range(2):
                s = 2 * tp + t0
                y = acc[:, t0 * half:(t0 + 1) * half] + d_ref[:, q * LANES:(q + 1) * LANES] * piece(s, q).astype(F32)
                y_ref[:, s * bsz:(s + 1) * bsz, q * LANES:(q + 1) * LANES] = (
                    jax.nn.gelu(y).astype(BF16).reshape(kt, bsz, LANES))


def _ssm(u_tm, weights, bsz, seq):
    w_toe, w_in_state, w_state_out, decay, dvec = weights
    n = SSM_CHUNK
    kt = SSM_CHUNKS_PER_STEP
    nchunk = seq // n
    rows = kt * bsz
    state_w = SSM_GROUPS * 2 * SSM_STATE
    u3 = u_tm.reshape(nchunk, n * bsz, SSM_WIDTH)
    once = pl.Buffered(1)
    blk = pl.BlockSpec((kt, n * bsz, SSM_WIDTH), lambda i: (i, 0, 0))
    y3 = pl.pallas_call(
        functools.partial(_ssm_kernel, bsz=bsz),
        grid=(nchunk // kt,),
        in_specs=[blk, _full(w_toe.shape, pipeline_mode=once), _full(w_in_state.shape, pipeline_mode=once),
                  _full(w_state_out.shape, pipeline_mode=once), _full(decay.shape), _full(dvec.shape)],
        out_specs=blk,
        out_shape=jax.ShapeDtypeStruct(u3.shape, BF16),
        scratch_shapes=[pltpu.VMEM((rows, state_w), F32), pltpu.VMEM((rows, state_w), BF16),
                        pltpu.VMEM((bsz, state_w), F32)],
        compiler_params=_cparams("arbitrary"),
        name="s5_chunked_scan",
    )(u3, w_toe, w_in_state, w_state_out, decay, dvec)
    return y3.reshape(seq, bsz * SSM_WIDTH)


def _postmix_kernel(att_ref, ys_ref, ga_ref, gs_ref, x_ref, gt_ref, sc_ref, sh_ref, gpost_ref, gpre_ref,
                    wglu_ref, wso_ref, wao_ref, wo_ref, wrh_ref, wrl_ref, br_ref,
                    x1_ref, h2_ref, idx_ref, gate_ref, rank_ref, cnt_ref, run_ref):
    tm = x_ref.shape[0]
    ne = wrh_ref.shape[0]

    @pl.when(pl.program_id(0) == 0)
    def _():
        run_ref[...] = jnp.zeros_like(run_ref)

    glu = jnp.dot(ys_ref[...], wglu_ref[...], preferred_element_type=F32)
    sg = (glu[:, :SSM_WIDTH] * jax.nn.sigmoid(glu[:, SSM_WIDTH:])).astype(BF16)
    ssm = jnp.dot(sg, wso_ref[...], preferred_element_type=F32)
    att = jnp.dot(att_ref[...], wao_ref[...], preferred_element_type=F32)
    merged = (ga_ref[...].astype(F32) * att + gs_ref[...].astype(F32) * ssm).astype(BF16)
    y = jnp.dot(merged, wo_ref[...], preferred_element_type=F32)
    x1 = x_ref[...] + gt_ref[0] * _rms(y, gpost_ref[...])
    x1_ref[...] = x1
    h2 = _rms(x1, gpre_ref[...]) * (1.0 + sc_ref[0]) + sh_ref[0]
    h2_ref[...] = _pack_bf16_pairs(h2)
    h_hi = h2.astype(BF16)
    h_lo = (h2 - h_hi.astype(F32)).astype(BF16)
    logits = (lax.dot_general(wrh_ref[...], h_hi, _NT, preferred_element_type=F32)
              + lax.dot_general(wrh_ref[...], h_lo, _NT, preferred_element_type=F32)
              + lax.dot_general(wrl_ref[...], h_hi, _NT, preferred_element_type=F32)
              + br_ref[...])
    eidx = lax.broadcasted_iota(jnp.int32, (ne, tm), 0)
    lg = logits
    vals, idxs = [], []
    for _ in range(TOP_K):
        mx = jnp.max(lg, axis=0, keepdims=True)
        first = jnp.min(jnp.where(lg == mx, eidx, ne), axis=0, keepdims=True)
        vals.append(mx)
        idxs.append(first)
        lg = jnp.where(eidx == first, -jnp.inf, lg)
    exps = [jnp.exp(v - vals[0]) for v in vals]
    denom = exps[0] + exps[1] + exps[2] + exps[3]
    gate_ref[...] = jnp.concatenate([e / denom for e in exps], axis=0)
    idx_ref[...] = jnp.concatenate(idxs, axis=0)
    onehot = jnp.where(lg == -jnp.inf, 1.0, 0.0)
    t_row = lax.broadcasted_iota(jnp.int32, (tm, tm), 0)
    t_col = lax.broadcasted_iota(jnp.int32, (tm, tm), 1)
    earlier = jnp.where(t_row < t_col, 1.0, 0.0).astype(BF16)
    before = jnp.dot(onehot.astype(BF16), earlier, preferred_element_type=F32) + run_ref[:, 0:1]
    ranks = [jnp.sum(jnp.where(eidx == ix, before, 0.0), axis=0, keepdims=True) for ix in idxs]
    rank_ref[...] = jnp.concatenate(ranks, axis=0).astype(jnp.int32)
    run_ref[...] = run_ref[...] + jnp.sum(onehot, axis=1, keepdims=True)
    cnt_ref[...] = run_ref[...].astype(jnp.int32)


def _postmix(att2, ys_tm, ga, gs, x2, gt1, sc2, sh2, g_post, g_pre, w_glu, w_ssm_out, w_att_out, w_out,
             w_router, b_router, seq):
    t, d = x2.shape
    tm = TOKEN_TILE
    tiles_per_seq = seq // tm
    ne = w_router.shape[1]
    wr_t = w_router.T.astype(F32)
    wr_hi = wr_t.astype(BF16)
    wr_lo = (wr_t - wr_hi.astype(F32)).astype(BF16)
    mod_spec = pl.BlockSpec((1, 1, d), lambda i: (i // tiles_per_seq, 0, 0))
    row = lambda cols: pl.BlockSpec((tm, cols), lambda i: (i, 0))
    time_major = pl.BlockSpec((tm, SSM_WIDTH), lambda i: (i % tiles_per_seq, i // tiles_per_seq))
    col = pl.BlockSpec((TOP_K, tm), lambda i: (0, i))
    weights = [w_glu.astype(BF16), w_ssm_out.astype(BF16), w_att_out.astype(BF16), w_out.astype(BF16),
               wr_hi, wr_lo, b_router.astype(F32).reshape(ne, 1)]
    return pl.pallas_call(
        _postmix_kernel,
        grid=(t // tm,),
        in_specs=[row(ATT_WIDTH), time_major, row(d), row(d), row(d), mod_spec, mod_spec, mod_spec,
                  _full((1, d)), _full((1, d))] + [_full(w.shape) for w in weights],
        out_specs=[row(d), row(d // 2), col, col, col, _full((ne, 128))],
        out_shape=[jax.ShapeDtypeStruct((t, d), F32), jax.ShapeDtypeStruct((t, d // 2), jnp.uint32),
                   jax.ShapeDtypeStruct((TOP_K, t), jnp.int32), jax.ShapeDtypeStruct((TOP_K, t), F32),
                   jax.ShapeDtypeStruct((TOP_K, t), jnp.int32), jax.ShapeDtypeStruct((ne, 128), jnp.int32)],
        scratch_shapes=[pltpu.VMEM((ne, 128), F32)],
        compiler_params=_cparams("arbitrary"),
        name="postmix_router",
    )(att2, ys_tm, ga, gs, x2, gt1, sc2, sh2, g_post.reshape(1, d), g_pre.reshape(1, d), *weights)


def _sc_worker_base(per_worker):
    return (lax.axis_index("s") * SC_CORES + lax.axis_index("c")) * per_worker


def _sc_mesh():
    return plsc.VectorSubcoreMesh(core_axis_name="c", subcore_axis_name="s",
                                  num_cores=SC_CORES, num_subcores=SC_SUBCORES)


def _sc_scatter_kernel(rows_hbm, idx_hbm, out_hbm, idx_v, rows_v, load_sems, store_sems, *, per_worker):
    chunk = rows_v.shape[1]
    n_tokens = rows_hbm.shape[0]
    n_chunks = per_worker // chunk
    base = _sc_worker_base(per_worker)

    def load(c, b):
        off = pl.multiple_of(base + c * chunk, chunk)
        for k in range(TOP_K):
            pltpu.sync_copy(idx_hbm.at[pl.ds(k * n_tokens + off, chunk)], idx_v.at[b, k])
        return pltpu.make_async_copy(rows_hbm.at[pl.ds(off, chunk)], rows_v.at[b], load_sems.at[b])

    def stores(b):
        return [pltpu.make_async_copy(rows_v.at[b], out_hbm.at[idx_v.at[b, k]], store_sems.at[b])
                for k in range(TOP_K)]

    load(0, 0).start()

    @pl.loop(0, n_chunks, step=2)
    def _(c0):
        for b in range(2):
            c = c0 + b

            @pl.when(c > 0)
            def _():
                for cp in stores(1 - b):
                    cp.wait()

            @pl.when(c + 1 < n_chunks)
            def _():
                load(c + 1, 1 - b).start()

            pltpu.make_async_copy(rows_hbm.at[pl.ds(0, chunk)], rows_v.at[b], load_sems.at[b]).wait()
            for cp in stores(b):
                cp.start()

    for cp in stores((n_chunks - 1) % 2):
        cp.wait()


def _sc_scatter_rows(rows, idx, n_out):
    t, d = rows.shape
    workers = SC_CORES * SC_SUBCORES
    per_worker = t // workers
    assert t % workers == 0 and per_worker % (2 * SC_GATHER_ROWS) == 0
    return pl.kernel(
        functools.partial(_sc_scatter_kernel, per_worker=per_worker),
        out_type=jax.ShapeDtypeStruct((n_out, d), rows.dtype),
        mesh=_sc_mesh(),
        scratch_types=[pltpu.VMEM((2, TOP_K, SC_GATHER_ROWS), jnp.int32),
                       pltpu.VMEM((2, SC_GATHER_ROWS, d), rows.dtype),
                       pltpu.SemaphoreType.DMA((2,)), pltpu.SemaphoreType.DMA((2,))],
        name="sc_row_scatter",
    )(rows, idx)


def _sc_gather_kernel(table_hbm, idx_hbm, out_hbm, idx_v, rows_v, sems, *, per_worker):
    chunk = rows_v.shape[1]
    n_chunks = per_worker // chunk
    base = _sc_worker_base(per_worker)

    def gather(c, b):
        off = pl.multiple_of(base + c * chunk, chunk)
        pltpu.sync_copy(idx_hbm.at[pl.ds(off, chunk)], idx_v.at[b])
        return pltpu.make_async_copy(table_hbm.at[idx_v.at[b]], rows_v.at[b], sems.at[b])

    gather(0, 0).start()

    @pl.loop(0, n_chunks, step=2)
    def _(c0):
        for b in range(2):
            c = c0 + b

            @pl.when(c + 1 < n_chunks)
            def _():
                gather(c + 1, 1 - b).start()

            pltpu.make_async_copy(table_hbm.at[idx_v.at[b]], rows_v.at[b], sems.at[b]).wait()
            off = pl.multiple_of(base + c * chunk, chunk)
            pltpu.sync_copy(rows_v.at[b], out_hbm.at[pl.ds(off, chunk)])


def _sc_gather_rows(table, idx):
    n = idx.shape[0]
    d = table.shape[1]
    workers = SC_CORES * SC_SUBCORES
    per_worker = n // workers
    assert n % workers == 0 and per_worker % (2 * SC_GATHER_ROWS) == 0
    return pl.kernel(
        functools.partial(_sc_gather_kernel, per_worker=per_worker),
        out_type=jax.ShapeDtypeStruct((n, d), table.dtype),
        mesh=_sc_mesh(),
        scratch_types=[pltpu.VMEM((2, SC_GATHER_ROWS), jnp.int32),
                       pltpu.VMEM((2, SC_GATHER_ROWS, d), table.dtype),
                       pltpu.SemaphoreType.DMA((2,))],
        name="sc_row_gather",
    )(table, idx)


def _experts_kernel(be_ref, nused_ref, valid_ref, x_ref, w1_ref, b1_ref, w2_ref, b2_ref, y_ref,
                    w1b_ref, w2b_ref):
    i = pl.program_id(0)
    prev = be_ref[jnp.maximum(i - 1, 0)]
    fresh = (i == 0) | (be_ref[i] != prev)

    @pl.when(fresh)
    def _():
        w1b_ref[...] = w1_ref[...].astype(BF16)
        w2b_ref[...] = w2_ref[...].astype(BF16)

    @pl.when(i < nused_ref[0])
    def _():
        row = lax.broadcasted_iota(jnp.int32, x_ref.shape, 0)
        x_lo, x_hi = _unpack_bf16_pairs(jnp.where(row < valid_ref[i], x_ref[...], jnp.uint32(0)))
        x = jnp.concatenate([x_lo.astype(BF16), x_hi.astype(BF16)], axis=1)
        gu = jnp.dot(x, w1b_ref[...], preferred_element_type=F32) + b1_ref[...]
        g = jnp.minimum(gu[:, :D_FF], SWIGLU_LIMIT)
        up = jnp.clip(gu[:, D_FF:], -SWIGLU_LIMIT, SWIGLU_LIMIT)
        act = ((up + 1.0) * g * jax.nn.sigmoid(SWIGLU_ALPHA * g)).astype(BF16)
        y = jnp.dot(act, w2b_ref[...], preferred_element_type=F32) + b2_ref[...]
        y_ref[...] = _pack_bf16_pairs(y)

    @pl.when(i >= nused_ref[0])
    def _():
        y_ref[...] = jnp.zeros_like(y_ref)


def _experts(xb, block_e, n_used, valid, w1, b1, w2, b2, layer):
    p_rows, packed_w = xb.shape
    d = 2 * packed_w
    depth, ne = w1.shape[:2]
    rb = EXPERT_ROWS
    wmap = lambda i, be, nu, va: (layer, be[i], 0, 0)
    rows = pl.BlockSpec((rb, packed_w), lambda i, be, nu, va: (i, 0))
    grid_spec = pltpu.PrefetchScalarGridSpec(
        num_scalar_prefetch=3,
        grid=(p_rows // rb,),
        in_specs=[
            rows,
            pl.BlockSpec((None, None, d, 2 * D_FF), wmap),
            pl.BlockSpec((None, None, 1, 2 * D_FF), wmap),
            pl.BlockSpec((None, None, D_FF, d), wmap),
            pl.BlockSpec((None, None, 1, d), wmap),
        ],
        out_specs=rows,
        scratch_shapes=[pltpu.VMEM((d, 2 * D_FF), BF16), pltpu.VMEM((D_FF, d), BF16)],
    )
    return pl.pallas_call(
        _experts_kernel,
        grid_spec=grid_spec,
        out_shape=jax.ShapeDtypeStruct((p_rows, packed_w), jnp.uint32),
        compiler_params=_cparams("arbitrary"),
        name="expert_ffn",
    )(block_e, n_used, valid, xb, w1, b1.reshape(depth, ne, 1, 2 * D_FF), w2, b2.reshape(depth, ne, 1, d))


def _combine_kernel(gate_ref, x_ref, gt_ref, g_ref, y0_ref, y1_ref, y2_ref, y3_ref, o_ref):
    tm = x_ref.shape[0]
    gates = gate_ref[...]
    gates = jnp.concatenate([gates, jnp.zeros((LANES - TOP_K, tm), F32)], axis=0).T
    lo, hi = _unpack_bf16_pairs(y0_ref[...])
    y_lo, y_hi = gates[:, 0:1] * lo, gates[:, 0:1] * hi
    for k, y_ref in enumerate((y1_ref, y2_ref, y3_ref), start=1):
        lo, hi = _unpack_bf16_pairs(y_ref[...])
        y_lo, y_hi = y_lo + gates[:, k:k + 1] * lo, y_hi + gates[:, k:k + 1] * hi
    y = jnp.concatenate([y_lo, y_hi], axis=1)
    o_ref[...] = x_ref[...] + gt_ref[0] * _rms(y, g_ref[...])


def _combine(y4, gate_t, x2, gt2, g_post, seq):
    t, d = x2.shape
    tm = TOKEN_TILE
    tiles = t // tm
    tiles_per_seq = seq // tm
    row = pl.BlockSpec((tm, d), lambda i: (i, 0))
    slot = lambda k: pl.BlockSpec((tm, y4.shape[1]), lambda i: (k * tiles + i, 0))
    return pl.pallas_call(
        _combine_kernel,
        grid=(tiles,),
        in_specs=[pl.BlockSpec((TOP_K, tm), lambda i: (0, i)),
                  row, pl.BlockSpec((1, 1, d), lambda i: (i // tiles_per_seq, 0, 0)), _full((1, d))]
        + [slot(k) for k in range(TOP_K)],
        out_specs=row,
        out_shape=jax.ShapeDtypeStruct((t, d), F32),
        compiler_params=_cparams("arbitrary"),
        name="expert_combine",
    )(gate_t, x2, gt2, g_post.reshape(1, d), y4, y4, y4, y4)


def _route_plan(idx_t, rank_t, counts):
    rb = EXPERT_ROWS
    k, t = idx_t.shape
    padded = (counts + rb - 1) // rb * rb
    pad_ends = jnp.cumsum(padded)
    pad_starts = pad_ends - padded
    experts = jnp.arange(N_EXPERTS, dtype=jnp.int32)
    start_of = jnp.sum(jnp.where(idx_t[None] == experts[:, None, None], pad_starts[:, None, None], 0), axis=0)
    dest = (start_of + rank_t).astype(jnp.int32)
    n_blocks = (k * t) // rb + N_EXPERTS
    blk_start = jnp.arange(n_blocks, dtype=jnp.int32) * rb
    block_e = jnp.minimum(jnp.sum(pad_ends[None, :] <= blk_start[:, None], axis=1), N_EXPERTS - 1)
    onehot_e = block_e[:, None] == experts[None, :]
    cnt_b = jnp.sum(jnp.where(onehot_e, counts[None, :], 0), axis=1)
    start_b = jnp.sum(jnp.where(onehot_e, pad_starts[None, :], 0), axis=1)
    valid = jnp.clip(cnt_b - (blk_start - start_b), 0, rb).astype(jnp.int32)
    n_used = (pad_ends[-1] // rb).astype(jnp.int32).reshape(1)
    return dest.reshape(-1), block_e.astype(jnp.int32), n_used, valid, n_blocks * rb


def kernel(x, c, rel_bias, w_ada, b_ada, g_pre_mix, g_post_mix, g_pre_ffn, g_post_ffn, w_in, ssm_a_re, ssm_a_im, ssm_log_dt, ssm_b_re, ssm_b_im, ssm_c_re, ssm_c_im, ssm_d, w_glu, w_ssm_out, w_att_out, w_out, w_router, b_router, w_exp_in, b_exp_in, w_exp_out, b_exp_out):
    bsz, seq, d = x.shape
    depth = w_ada.shape[0]
    t = bsz * seq
    assert d == D_MODEL and seq % TOKEN_TILE == 0 and TOKEN_TILE % MOBA_BLOCK == 0
    assert seq % (SSM_CHUNK * SSM_CHUNKS_PER_STEP) == 0

    mod = _ada_mod(c, w_ada, b_ada)
    bias_tiles, far_bias = _moba_bias_tiles(rel_bias.astype(F32))
    x2 = x.reshape(t, d)
    for l in range(depth):
        sh1, sc1, gt1, sh2, sc2, gt2 = [m.reshape(bsz, 1, d) for m in jnp.split(mod[l], N_MOD, axis=-1)]
        q4, k4, vt4, u_tm, ga, gs = _premix(x2, g_pre_mix[l], sc1, sh1, w_in[l], seq)
        att = _moba(q4, k4, vt4, bias_tiles, far_bias, bsz).reshape(t, ATT_WIDTH)
        ssm_w = _ssm_weights(ssm_a_re[l], ssm_a_im[l], ssm_log_dt[l], ssm_b_re[l], ssm_b_im[l],
                             ssm_c_re[l], ssm_c_im[l], ssm_d[l])
        ys_tm = _ssm(u_tm, ssm_w, bsz, seq)
        x1, h2, idx_t, gate_t, rank_t, cnt = _postmix(
            att, ys_tm, ga, gs, x2, gt1, sc2, sh2, g_post_mix[l], g_pre_ffn[l],
            w_glu[l], w_ssm_out[l], w_att_out[l], w_out[l], w_router[l], b_router[l], seq)
        dest_flat, block_e, n_used, valid, p_rows = _route_plan(idx_t, rank_t, cnt[:, 0])
        xb = _sc_scatter_rows(h2, dest_flat, p_rows)
        yb = _experts(xb, block_e, n_used, valid, w_exp_in, b_exp_in, w_exp_out, b_exp_out, l)
        y4 = _sc_gather_rows(yb, dest_flat)
        x2 = _combine(y4, gate_t, x1, gt2, g_post_ffn[l], seq)
    return x2.reshape(bsz, seq, d)
```

```python
import functools
import math

import jax
import jax.numpy as jnp
from jax import lax
from jax.experimental import pallas as pl
from jax.experimental.pallas import tpu as pltpu
from jax.experimental.pallas import tpu_sc as plsc

F32 = jnp.float32
BF16 = jnp.bfloat16

D_MODEL = 1024
ATT_HEADS = 8
HEAD_DIM = 64
ATT_WIDTH = ATT_HEADS * HEAD_DIM
MOBA_BLOCK = 256
MOBA_TOPK = 3
REL_BUCKETS = 32
REL_MAX_DIST = 128
SSM_WIDTH = D_MODEL // 2
SSM_GROUP = 16
SSM_GROUPS = SSM_WIDTH // SSM_GROUP
SSM_STATE = 64
N_EXPERTS = 32
TOP_K = 4
D_FF = D_MODEL
SWIGLU_ALPHA = 1.702
SWIGLU_LIMIT = 7.0
RMS_EPS = 1e-6
N_MOD = 6

SSM_CHUNK = 8
SSM_CHUNKS_PER_STEP = 16
LANES = 128
MXU_DIM = 256
TOKEN_TILE = 512
SC_CORES = 2
SC_SUBCORES = 16
SC_GATHER_ROWS = 64
EXPERT_ROWS = 512
MASK_NEG = -1e30
LOG2E = math.log2(math.e)
SUM_ROWS = 16
VMEM_LIMIT = 56 * 1024 * 1024

_NT = (((1,), (1,)), ((), ()))


def _cparams(*sem):
    return pltpu.CompilerParams(dimension_semantics=sem, vmem_limit_bytes=VMEM_LIMIT)


def _pack_bf16_pairs(x):
    n = x.shape[1] // 2
    bits = lax.bitcast_convert_type(x.astype(BF16).astype(F32), jnp.uint32)
    return lax.shift_right_logical(bits[:, :n], jnp.uint32(16)) | (bits[:, n:] & jnp.uint32(0xFFFF0000))


def _unpack_bf16_pairs(w):
    lo = lax.bitcast_convert_type(lax.shift_left(w, jnp.uint32(16)), F32)
    hi = lax.bitcast_convert_type(w & jnp.uint32(0xFFFF0000), F32)
    return lo, hi


def _rms(x, g):
    return x * lax.rsqrt(jnp.mean(x * x, axis=-1, keepdims=True) + RMS_EPS) * g


def _full(shape, **kw):
    n = len(shape)
    return pl.BlockSpec(shape, lambda *_: (0,) * n, **kw)


def _ada_kernel(c_ref, w_ref, b_ref, o_ref):
    c = c_ref[...]
    cond = c * jax.nn.sigmoid(c)
    o_ref[0] = jnp.dot(cond, w_ref[0], preferred_element_type=F32,
                       precision=lax.Precision.HIGHEST) + b_ref[0]


def _ada_mod(c, w_ada, b_ada):
    depth, d, nd = w_ada.shape
    bsz = c.shape[0]
    return pl.pallas_call(
        _ada_kernel,
        grid=(depth, nd // d),
        in_specs=[
            pl.BlockSpec((bsz, d), lambda l, j: (0, 0)),
            pl.BlockSpec((1, d, d), lambda l, j: (l, 0, j)),
            pl.BlockSpec((1, 1, d), lambda l, j: (l, 0, j)),
        ],
        out_specs=pl.BlockSpec((1, bsz, d), lambda l, j: (l, 0, j)),
        out_shape=jax.ShapeDtypeStruct((depth, bsz, nd), F32),
        compiler_params=_cparams("arbitrary", "arbitrary"),
        name="ada_mod",
    )(c, w_ada, b_ada.reshape(depth, 1, nd))


def _premix_kernel(x_ref, g_ref, sc_ref, sh_ref, wq_ref, wk_ref, wvt_ref, wu_ref, wga_ref, wgs_ref,
                   q_ref, k_ref, vt_ref, u_ref, ga_ref, gs_ref):
    x = x_ref[...]
    h = _rms(x, g_ref[...]) * (1.0 + sc_ref[0]) + sh_ref[0]
    hb = h.astype(BF16)
    nblk = q_ref.shape[0]
    q = (jnp.dot(hb, wq_ref[...], preferred_element_type=F32) * (HEAD_DIM ** -0.5 * LOG2E)).astype(BF16)
    k = jnp.dot(hb, wk_ref[...], preferred_element_type=F32).astype(BF16)
    vt = lax.dot_general(wvt_ref[...], hb, _NT, preferred_element_type=F32).astype(BF16)
    for r in range(nblk):
        q_ref[r] = q[r * MOBA_BLOCK:(r + 1) * MOBA_BLOCK]
        k_ref[r] = k[r * MOBA_BLOCK:(r + 1) * MOBA_BLOCK]
        vt_ref[r] = vt[:, r * MOBA_BLOCK:(r + 1) * MOBA_BLOCK]
    u_ref[...] = jnp.dot(hb, wu_ref[...], preferred_element_type=F32).astype(BF16)
    ga = jnp.dot(hb, wga_ref[...], preferred_element_type=F32)
    ga_ref[...] = jax.nn.sigmoid(ga).astype(BF16)
    gs = jnp.dot(hb, wgs_ref[...], preferred_element_type=F32)
    gs_ref[...] = jax.nn.sigmoid(gs).astype(BF16)


def _premix(x2, g, sc, sh, w_in, seq):
    t, d = x2.shape
    tm = TOKEN_TILE
    tiles_per_seq = seq // tm
    nblk = tm // MOBA_BLOCK
    a = ATT_WIDTH
    wb = w_in.astype(BF16)
    wq, wk, wv = wb[:, :a], wb[:, a:2 * a], wb[:, 2 * a:3 * a]
    wu = wb[:, 3 * a:3 * a + SSM_WIDTH]
    wga = wb[:, 3 * a + SSM_WIDTH:3 * a + SSM_WIDTH + d]
    wgs = wb[:, 3 * a + SSM_WIDTH + d:]
    mod_spec = pl.BlockSpec((1, 1, d), lambda i: (i // tiles_per_seq, 0, 0))
    blk3 = lambda rows, cols: pl.BlockSpec((nblk, rows, cols), lambda i: (i, 0, 0))
    row = lambda cols: pl.BlockSpec((tm, cols), lambda i: (i, 0))
    time_major = pl.BlockSpec((tm, SSM_WIDTH), lambda i: (i % tiles_per_seq, i // tiles_per_seq))
    nb_total = t // MOBA_BLOCK
    return pl.pallas_call(
        _premix_kernel,
        grid=(t // tm,),
        in_specs=[row(d), _full((1, d)), mod_spec, mod_spec,
                  _full(wq.shape), _full(wk.shape), _full((a, d)), _full(wu.shape),
                  _full(wga.shape), _full(wgs.shape)],
        out_specs=[blk3(MOBA_BLOCK, a), blk3(MOBA_BLOCK, a), blk3(a, MOBA_BLOCK),
                   time_major, row(d), row(d)],
        out_shape=[jax.ShapeDtypeStruct((nb_total, MOBA_BLOCK, a), BF16),
                   jax.ShapeDtypeStruct((nb_total, MOBA_BLOCK, a), BF16),
                   jax.ShapeDtypeStruct((nb_total, a, MOBA_BLOCK), BF16),
                   jax.ShapeDtypeStruct((seq, (t // seq) * SSM_WIDTH), BF16),
                   jax.ShapeDtypeStruct((t, d), BF16),
                   jax.ShapeDtypeStruct((t, d), BF16)],
        compiler_params=_cparams("arbitrary"),
        name="premix_inproj",
    )(x2, g.reshape(1, d), sc, sh, wq, wk, wv.T, wu, wga, wgs)


def _moba_kernel(far_ref, q_ref, k_ref, vt_ref, bias_ref, o_ref, kmean_ref, neg_ref, *state_refs):
    m_refs = state_refs[0::2]
    acc_refs = state_refs[1::2]
    nb = k_ref.shape[0]
    blk = MOBA_BLOCK
    i = pl.program_id(1)

    @pl.when(i == 0)
    def _():
        for n in range(nb):
            kmean_ref[n:n + 1, :] = jnp.mean(k_ref[n].astype(F32), axis=0, keepdims=True)

    lane = lax.broadcasted_iota(jnp.int32, (blk, 2 * HEAD_DIM), 1)
    nidx = lax.broadcasted_iota(jnp.int32, (nb, blk), 0)
    valid = nidx < i
    pair_w = 2 * HEAD_DIM

    def pair_cols(head):
        return slice((head // 2) * pair_w, (head // 2 + 1) * pair_w)

    def head_rows(head):
        return slice(head * HEAD_DIM, (head + 1) * HEAD_DIM)

    def q_head(head):
        q2 = q_ref[0, :, pair_cols(head)]
        lo = (head % 2) * HEAD_DIM
        return jnp.where((lane >= lo) & (lane < lo + HEAD_DIM), q2, jnp.zeros_like(q2))

    for head in range(ATT_HEADS):
        km = kmean_ref[:, pair_cols(head)]
        km_hi = km.astype(BF16)
        km_lo = (km - km_hi.astype(F32)).astype(BF16)
        qh = q_head(head)
        gate = (lax.dot_general(km_hi, qh, _NT, preferred_element_type=F32)
                + lax.dot_general(km_lo, qh, _NT, preferred_element_type=F32))
        g = jnp.where(valid, gate, -jnp.inf)
        sel = jnp.zeros((nb, blk), jnp.bool_)
        for _ in range(MOBA_TOPK):
            mx = jnp.max(g, axis=0, keepdims=True)
            first = jnp.min(jnp.where(g == mx, nidx, nb), axis=0, keepdims=True)
            pick = nidx == first
            sel = sel | pick
            g = jnp.where(pick, -jnp.inf, g)
        neg_ref[head * nb:(head + 1) * nb, :] = jnp.where(sel & valid, 0.0, MASK_NEG)

    def scores(head, j, add):
        kb = k_ref[j, :, pair_cols(head)]
        return lax.dot_general(kb, q_head(head), _NT, preferred_element_type=F32) + add

    ones_rows = jnp.ones((SUM_ROWS, blk), BF16)

    def weighted_values(head, j, p):
        vb = jnp.concatenate([vt_ref[j, head_rows(head), :], ones_rows], axis=0)
        return jnp.dot(vb, p.astype(BF16), preferred_element_type=F32)

    def start(head, j, st):
        m = jnp.max(st, axis=0, keepdims=True)
        m_refs[head][...] = m
        acc_refs[head][...] = weighted_values(head, j, jnp.exp2(st - m))

    def update(head, j, st):
        m = m_refs[head][...]
        m_new = jnp.maximum(m, jnp.max(st, axis=0, keepdims=True))
        alpha = jnp.exp2(m - m_new)
        m_refs[head][...] = m_new
        acc_refs[head][...] = alpha * acc_refs[head][...] + weighted_values(head, j, jnp.exp2(st - m_new))

    def sweep(j, add_of, absorb):
        st = scores(0, j, add_of(0))
        for head in range(ATT_HEADS):
            nxt = scores(head + 1, j, add_of(head + 1)) if head + 1 < ATT_HEADS else None
            absorb(head, j, st)
            st = nxt

    sweep(i, lambda h: bias_ref[h, 0], start)

    @pl.when(i > 0)
    def _():
        sweep(i - 1, lambda h: bias_ref[h, 1] + neg_ref[pl.ds(h * nb + i - 1, 1), :], update)

    def far_block(j, carry):
        sweep(j, lambda h: neg_ref[pl.ds(h * nb + j, 1), :] + far_ref[h], update)
        return carry

    lax.fori_loop(0, jnp.maximum(i - 1, 0), far_block, 0)

    for hp in range(ATT_HEADS // 2):
        outs = [acc_refs[h][0:HEAD_DIM, :] / acc_refs[h][HEAD_DIM:HEAD_DIM + 1, :] for h in (2 * hp, 2 * hp + 1)]
        pair = jnp.concatenate(outs, axis=0)
        o_ref[0, :, hp * pair_w:(hp + 1) * pair_w] = pair.T.astype(BF16)


def _rel_bucket(dist):
    n = jnp.maximum(dist, 0)
    max_exact = REL_BUCKETS // 2
    nf = jnp.maximum(n, 1).astype(F32)
    large = max_exact + (jnp.log(nf / max_exact) / math.log(REL_MAX_DIST / max_exact)
                         * (REL_BUCKETS - max_exact)).astype(jnp.int32)
    large = jnp.minimum(large, REL_BUCKETS - 1)
    return jnp.where(n < max_exact, n, large)


def _moba_bias_tiles(rel_bias):
    blk = MOBA_BLOCK
    span = 2 * blk
    rel_bias = rel_bias * LOG2E
    vec = rel_bias[_rel_bucket(jnp.arange(span))].T.astype(F32)
    masked = jnp.full_like(vec[:, :blk], MASK_NEG)
    ring_own = jnp.concatenate([vec[:, :blk], masked], axis=1)
    ring_prev = jnp.concatenate([vec[:, blk:], vec[:, :blk]], axis=1)

    def toeplitz(ring):
        flat = jnp.tile(ring, (1, blk))[:, :blk * (span - 1)]
        return flat.reshape(-1, blk, span - 1)[:, :, :blk]

    tiles = jnp.stack([toeplitz(ring_own), toeplitz(ring_prev)], axis=1)
    far = rel_bias[_rel_bucket(jnp.array(span))]
    return tiles, far.astype(F32)


def _moba(q4, k4, vt4, bias_tiles, far_bias, bsz):
    nb_total, blk, a = q4.shape
    nb = nb_total // bsz
    return pl.pallas_call(
        _moba_kernel,
        grid=(bsz, nb),
        in_specs=[
            pl.BlockSpec(memory_space=pltpu.SMEM),
            pl.BlockSpec((1, blk, a), lambda b, i: (b * nb + i, 0, 0)),
            pl.BlockSpec((nb, blk, a), lambda b, i: (b, 0, 0)),
            pl.BlockSpec((nb, a, blk), lambda b, i: (b, 0, 0)),
            _full(bias_tiles.shape),
        ],
        out_specs=pl.BlockSpec((1, blk, a), lambda b, i: (b * nb + i, 0, 0)),
        out_shape=jax.ShapeDtypeStruct((nb_total, blk, a), BF16),
        scratch_shapes=[pltpu.VMEM((nb, a), F32), pltpu.VMEM((ATT_HEADS * nb, blk), F32)]
        + [pltpu.VMEM((1, blk), F32), pltpu.VMEM((HEAD_DIM + SUM_ROWS, blk), F32)] * ATT_HEADS,
        compiler_params=_cparams("arbitrary", "arbitrary"),
        name="moba_attention",
    )(far_bias, q4, k4, vt4, bias_tiles)


def _ssm_weights(a_re, a_im, log_dt, b_re, b_im, c_re, c_im, d_skip):
    g, p, cw, n = SSM_GROUPS, SSM_STATE, SSM_GROUP, SSM_CHUNK
    gs = LANES // cw
    nq = g // gs
    npair = n // 2
    pairs_per_set = gs // 2
    lam_re = jnp.minimum(a_re.astype(F32), -1e-4)
    lam_im = a_im.astype(F32)
    dt = jnp.exp(log_dt.astype(F32))[:, None]
    mag = jnp.exp(lam_re * dt)
    lb_re = mag * jnp.cos(lam_im * dt)
    lb_im = mag * jnp.sin(lam_im * dt)
    n_re = lb_re - 1.0
    n_im = lb_im
    den = lam_re * lam_re + lam_im * lam_im
    z_re = ((n_re * lam_re + n_im * lam_im) / den)[..., None]
    z_im = ((n_im * lam_re - n_re * lam_im) / den)[..., None]
    br, bi = b_re.astype(F32), b_im.astype(F32)
    bb_re = z_re * br - z_im * bi
    bb_im = z_re * bi + z_im * br
    pw_re, pw_im = [jnp.ones_like(lb_re)], [jnp.zeros_like(lb_im)]
    for _ in range(n):
        r, im = pw_re[-1], pw_im[-1]
        pw_re.append(r * lb_re - im * lb_im)
        pw_im.append(r * lb_im + im * lb_re)
    pw_re, pw_im = jnp.stack(pw_re), jnp.stack(pw_im)
    cr, ci = c_re.astype(F32), c_im.astype(F32)
    hi = lax.Precision.HIGHEST
    rev_re, rev_im = pw_re[n - 1::-1], pw_im[n - 1::-1]
    sb_re = jnp.einsum('sgp,gpc->sgcp', rev_re, bb_re) - jnp.einsum('sgp,gpc->sgcp', rev_im, bb_im)
    sb_im = jnp.einsum('sgp,gpc->sgcp', rev_re, bb_im) + jnp.einsum('sgp,gpc->sgcp', rev_im, bb_re)
    cl_re = jnp.einsum('gcp,tgp->tgpc', cr, pw_re[1:]) - jnp.einsum('gcp,tgp->tgpc', ci, pw_im[1:])
    cl_im = jnp.einsum('gcp,tgp->tgpc', cr, pw_im[1:]) + jnp.einsum('gcp,tgp->tgpc', ci, pw_re[1:])
    cp_re = jnp.einsum('gcp,tgp->tgcp', cr, pw_re[:n]) - jnp.einsum('gcp,tgp->tgcp', ci, pw_im[:n])
    cp_im = jnp.einsum('gcp,tgp->tgcp', cr, pw_im[:n]) + jnp.einsum('gcp,tgp->tgcp', ci, pw_re[:n])
    kern = (jnp.einsum('tgcp,gpd->gtcd', cp_re, bb_re, precision=hi)
            - jnp.einsum('tgcp,gpd->gtcd', cp_im, bb_im, precision=hi))

    kp = jnp.concatenate([jnp.zeros_like(kern[:, :1]), kern], axis=1)
    dl = jnp.arange(npair)[:, None, None]
    s0 = jnp.arange(2)[None, :, None]
    t0 = jnp.arange(2)[None, None, :]
    lag = 2 * dl + t0 - s0
    kl = kp[:, lag + 1].reshape(nq, gs, npair, 2, 2, cw, cw)
    base = kl.transpose(0, 2, 3, 1, 6, 4, 5)
    same_group = jnp.eye(gs, dtype=F32)
    w_toe = jnp.concatenate(
        [base[..., t0, :] * same_group[:, g1][None, None, None, :, None, None]
         for t0 in range(2) for g1 in range(gs)], axis=-1)
    w_toe = w_toe.reshape(nq, npair, MXU_DIM, MXU_DIM).astype(BF16)

    member = (jnp.arange(gs)[None, :, None]
              == 2 * jnp.arange(pairs_per_set)[:, None, None] + jnp.arange(2)[None, None, :]).astype(F32)

    def pair_tiles(x):
        cols = [x[ri][:, None] * member[:, :, gl][None, :, None, None, :, None, None]
                for ri in range(2) for gl in range(2)]
        return jnp.concatenate(cols, axis=-1).reshape(g // 2, npair, MXU_DIM, MXU_DIM)

    sbs = jnp.stack([sb_re, sb_im]).reshape(2, npair, 2, nq, gs, cw, p)
    w_in_state = pair_tiles(sbs.transpose(0, 3, 1, 2, 4, 5, 6)).astype(BF16)
    sos = jnp.stack([cl_re, -cl_im]).reshape(2, npair, 2, nq, gs, p, cw)
    w_state_out = jnp.swapaxes(pair_tiles(sos.transpose(0, 3, 1, 2, 4, 6, 5)), -1, -2).astype(BF16)
    decay = jnp.stack([pw_re[n].reshape(g // 2, 2 * p), pw_im[n].reshape(g // 2, 2 * p)])
    dvec = d_skip.astype(F32).reshape(1, g * cw)
    return w_toe, w_in_state, w_state_out, decay, dvec


def _ssm_kernel(u_ref, wt_ref, wi_ref, wo_ref, dec_ref, d_ref, y_ref, s_ref, xp_ref, x_ref, *, bsz):
    kt = u_ref.shape[0]
    n = u_ref.shape[1] // bsz
    rows = kt * bsz
    npair = n // 2
    nq = wt_ref.shape[0]
    npairs_g = wi_ref.shape[0]
    per_set = npairs_g // nq
    half = MXU_DIM // 2

    @pl.when(pl.program_id(0) == 0)
    def _():
        x_ref[...] = jnp.zeros_like(x_ref)

    def piece(s, q):
        return u_ref[:, s * bsz:(s + 1) * bsz, q * LANES:(q + 1) * LANES].reshape(rows, LANES)

    lhs = {(sp, q): jnp.concatenate([piece(2 * sp, q), piece(2 * sp + 1, q)], axis=-1)
           for sp in range(npair) for q in range(nq)}

    for gp in range(npairs_g):
        q = gp // per_set
        acc = jnp.dot(lhs[0, q], wi_ref[gp, 0], preferred_element_type=F32)
        for sp in range(1, npair):
            acc = acc + jnp.dot(lhs[sp, q], wi_ref[gp, sp], preferred_element_type=F32)
        s_ref[:, gp * MXU_DIM:(gp + 1) * MXU_DIM] = acc

    for gp in range(npairs_g):
        re_cols = slice(gp * MXU_DIM, gp * MXU_DIM + half)
        im_cols = slice(gp * MXU_DIM + half, (gp + 1) * MXU_DIM)
        ar = dec_ref[0, gp:gp + 1, :]
        ai = dec_ref[1, gp:gp + 1, :]
        xr = x_ref[:, re_cols]
        xi = x_ref[:, im_cols]
        for kc in range(kt):
            rs = slice(kc * bsz, (kc + 1) * bsz)
            xp_ref[rs, re_cols] = xr.astype(BF16)
            xp_ref[rs, im_cols] = xi.astype(BF16)
            sr = s_ref[rs, re_cols]
            si = s_ref[rs, im_cols]
            xr, xi = ar * xr - ai * xi + sr, ar * xi + ai * xr + si
        x_ref[:, re_cols] = xr
        x_ref[:, im_cols] = xi

    for q in range(nq):
        for tp in range(npair):
            acc = jnp.dot(lhs[0, q], wt_ref[q, tp], preferred_element_type=F32)
            for sp in range(1, tp + 1):
                acc = acc + jnp.dot(lhs[sp, q], wt_ref[q, tp - sp], preferred_element_type=F32)
            for gp in range(q * per_set, (q + 1) * per_set):
                acc = acc + jnp.dot(xp_ref[:, gp * MXU_DIM:(gp + 1) * MXU_DIM], wo_ref[gp, tp],
                                    preferred_element_type=F32)
            for t0 in range(2):
                s = 2 * tp + t0
                y = acc[:, t0 * half:(t0 + 1) * half] + d_ref[:, q * LANES:(q + 1) * LANES] * piece(s, q).astype(F32)
                y_ref[:, s * bsz:(s + 1) * bsz, q * LANES:(q + 1) * LANES] = (
                    jax.nn.gelu(y).astype(BF16).reshape(kt, bsz, LANES))


def _ssm(u_tm, weights, bsz, seq):
    w_toe, w_in_state, w_state_out, decay, dvec = weights
    n = SSM_CHUNK
    kt = SSM_CHUNKS_PER_STEP
    nchunk = seq // n
    rows = kt * bsz
    state_w = SSM_GROUPS * 2 * SSM_STATE
    u3 = u_tm.reshape(nchunk, n * bsz, SSM_WIDTH)
    once = pl.Buffered(1)
    blk = pl.BlockSpec((kt, n * bsz, SSM_WIDTH), lambda i: (i, 0, 0))
    y3 = pl.pallas_call(
        functools.partial(_ssm_kernel, bsz=bsz),
        grid=(nchunk // kt,),
        in_specs=[blk, _full(w_toe.shape, pipeline_mode=once), _full(w_in_state.shape, pipeline_mode=once),
                  _full(w_state_out.shape, pipeline_mode=once), _full(decay.shape), _full(dvec.shape)],
        out_specs=blk,
        out_shape=jax.ShapeDtypeStruct(u3.shape, BF16),
        scratch_shapes=[pltpu.VMEM((rows, state_w), F32), pltpu.VMEM((rows, state_w), BF16),
                        pltpu.VMEM((bsz, state_w), F32)],
        compiler_params=_cparams("arbitrary"),
        name="s5_chunked_scan",
    )(u3, w_toe, w_in_state, w_state_out, decay, dvec)
    return y3.reshape(seq, bsz * SSM_WIDTH)


def _postmix_kernel(att_ref, ys_ref, ga_ref, gs_ref, x_ref, gt_ref, sc_ref, sh_ref, gpost_ref, gpre_ref,
                    wglu_ref, wso_ref, wao_ref, wo_ref, wrh_ref, wrl_ref, br_ref,
                    x1_ref, h2_ref, idx_ref, gate_ref, rank_ref, cnt_ref, run_ref):
    tm = x_ref.shape[0]
    ne = wrh_ref.shape[0]

    @pl.when(pl.program_id(0) == 0)
    def _():
        run_ref[...] = jnp.zeros_like(run_ref)

    glu = jnp.dot(ys_ref[...], wglu_ref[...], preferred_element_type=F32)
    sg = (glu[:, :SSM_WIDTH] * jax.nn.sigmoid(glu[:, SSM_WIDTH:])).astype(BF16)
    ssm = jnp.dot(sg, wso_ref[...], preferred_element_type=F32)
    att = jnp.dot(att_ref[...], wao_ref[...], preferred_element_type=F32)
    merged = (ga_ref[...].astype(F32) * att + gs_ref[...].astype(F32) * ssm).astype(BF16)
    y = jnp.dot(merged, wo_ref[...], preferred_element_type=F32)
    x1 = x_ref[...] + gt_ref[0] * _rms(y, gpost_ref[...])
    x1_ref[...] = x1
    h2 = _rms(x1, gpre_ref[...]) * (1.0 + sc_ref[0]) + sh_ref[0]
    h2_ref[...] = _pack_bf16_pairs(h2)
    h_hi = h2.astype(BF16)
    h_lo = (h2 - h_hi.astype(F32)).astype(BF16)
    logits = (lax.dot_general(wrh_ref[...], h_hi, _NT, preferred_element_type=F32)
              + lax.dot_general(wrh_ref[...], h_lo, _NT, preferred_element_type=F32)
              + lax.dot_general(wrl_ref[...], h_hi, _NT, preferred_element_type=F32)
              + br_ref[...])
    eidx = lax.broadcasted_iota(jnp.int32, (ne, tm), 0)
    lg = logits
    vals, idxs = [], []
    for _ in range(TOP_K):
        mx = jnp.max(lg, axis=0, keepdims=True)
        first = jnp.min(jnp.where(lg == mx, eidx, ne), axis=0, keepdims=True)
        vals.append(mx)
        idxs.append(first)
        lg = jnp.where(eidx == first, -jnp.inf, lg)
    exps = [jnp.exp(v - vals[0]) for v in vals]
    denom = exps[0] + exps[1] + exps[2] + exps[3]
    gate_ref[...] = jnp.concatenate([e / denom for e in exps], axis=0)
    idx_ref[...] = jnp.concatenate(idxs, axis=0)
    onehot = jnp.where(lg == -jnp.inf, 1.0, 0.0)
    t_row = lax.broadcasted_iota(jnp.int32, (tm, tm), 0)
    t_col = lax.broadcasted_iota(jnp.int32, (tm, tm), 1)
    earlier = jnp.where(t_row < t_col, 1.0, 0.0).astype(BF16)
    before = jnp.dot(onehot.astype(BF16), earlier, preferred_element_type=F32) + run_ref[:, 0:1]
    ranks = [jnp.sum(jnp.where(eidx == ix, before, 0.0), axis=0, keepdims=True) for ix in idxs]
    rank_ref[...] = jnp.concatenate(ranks, axis=0).astype(jnp.int32)
    run_ref[...] = run_ref[...] + jnp.sum(onehot, axis=1, keepdims=True)
    cnt_ref[...] = run_ref[...].astype(jnp.int32)


def _postmix(att2, ys_tm, ga, gs, x2, gt1, sc2, sh2, g_post, g_pre, w_glu, w_ssm_out, w_att_out, w_out,
             w_router, b_router, seq):
    t, d = x2.shape
    tm = TOKEN_TILE
    tiles_per_seq = seq // tm
    ne = w_router.shape[1]
    wr_t = w_router.T.astype(F32)
    wr_hi = wr_t.astype(BF16)
    wr_lo = (wr_t - wr_hi.astype(F32)).astype(BF16)
    mod_spec = pl.BlockSpec((1, 1, d), lambda i: (i // tiles_per_seq, 0, 0))
    row = lambda cols: pl.BlockSpec((tm, cols), lambda i: (i, 0))
    time_major = pl.BlockSpec((tm, SSM_WIDTH), lambda i: (i % tiles_per_seq, i // tiles_per_seq))
    col = pl.BlockSpec((TOP_K, tm), lambda i: (0, i))
    weights = [w_glu.astype(BF16), w_ssm_out.astype(BF16), w_att_out.astype(BF16), w_out.astype(BF16),
               wr_hi, wr_lo, b_router.astype(F32).reshape(ne, 1)]
    return pl.pallas_call(
        _postmix_kernel,
        grid=(t // tm,),
        in_specs=[row(ATT_WIDTH), time_major, row(d), row(d), row(d), mod_spec, mod_spec, mod_spec,
                  _full((1, d)), _full((1, d))] + [_full(w.shape) for w in weights],
        out_specs=[row(d), row(d // 2), col, col, col, _full((ne, 128))],
        out_shape=[jax.ShapeDtypeStruct((t, d), F32), jax.ShapeDtypeStruct((t, d // 2), jnp.uint32),
                   jax.ShapeDtypeStruct((TOP_K, t), jnp.int32), jax.ShapeDtypeStruct((TOP_K, t), F32),
                   jax.ShapeDtypeStruct((TOP_K, t), jnp.int32), jax.ShapeDtypeStruct((ne, 128), jnp.int32)],
        scratch_shapes=[pltpu.VMEM((ne, 128), F32)],
        compiler_params=_cparams("arbitrary"),
        name="postmix_router",
    )(att2, ys_tm, ga, gs, x2, gt1, sc2, sh2, g_post.reshape(1, d), g_pre.reshape(1, d), *weights)


def _sc_worker_base(per_worker):
    return (lax.axis_index("s") * SC_CORES + lax.axis_index("c")) * per_worker


def _sc_mesh():
    return plsc.VectorSubcoreMesh(core_axis_name="c", subcore_axis_name="s",
                                  num_cores=SC_CORES, num_subcores=SC_SUBCORES)


def _sc_scatter_kernel(rows_hbm, idx_hbm, out_hbm, idx_v, rows_v, load_sems, store_sems, *, per_worker):
    chunk = rows_v.shape[1]
    n_tokens = rows_hbm.shape[0]
    n_chunks = per_worker // chunk
    base = _sc_worker_base(per_worker)

    def load(c, b):
        off = pl.multiple_of(base + c * chunk, chunk)
        for k in range(TOP_K):
            pltpu.sync_copy(idx_hbm.at[pl.ds(k * n_tokens + off, chunk)], idx_v.at[b, k])
        return pltpu.make_async_copy(rows_hbm.at[pl.ds(off, chunk)], rows_v.at[b], load_sems.at[b])

    def stores(b):
        return [pltpu.make_async_copy(rows_v.at[b], out_hbm.at[idx_v.at[b, k]], store_sems.at[b])
                for k in range(TOP_K)]

    load(0, 0).start()

    @pl.loop(0, n_chunks, step=2)
    def _(c0):
        for b in range(2):
            c = c0 + b

            @pl.when(c > 0)
            def _():
                for cp in stores(1 - b):
                    cp.wait()

            @pl.when(c + 1 < n_chunks)
            def _():
                load(c + 1, 1 - b).start()

            pltpu.make_async_copy(rows_hbm.at[pl.ds(0, chunk)], rows_v.at[b], load_sems.at[b]).wait()
            for cp in stores(b):
                cp.start()

    for cp in stores((n_chunks - 1) % 2):
        cp.wait()


def _sc_scatter_rows(rows, idx, n_out):
    t, d = rows.shape
    workers = SC_CORES * SC_SUBCORES
    per_worker = t // workers
    assert t % workers == 0 and per_worker % (2 * SC_GATHER_ROWS) == 0
    return pl.kernel(
        functools.partial(_sc_scatter_kernel, per_worker=per_worker),
        out_type=jax.ShapeDtypeStruct((n_out, d), rows.dtype),
        mesh=_sc_mesh(),
        scratch_types=[pltpu.VMEM((2, TOP_K, SC_GATHER_ROWS), jnp.int32),
                       pltpu.VMEM((2, SC_GATHER_ROWS, d), rows.dtype),
                       pltpu.SemaphoreType.DMA((2,)), pltpu.SemaphoreType.DMA((2,))],
        name="sc_row_scatter",
    )(rows, idx)


def _sc_gather_kernel(table_hbm, idx_hbm, out_hbm, idx_v, rows_v, sems, *, per_worker):
    chunk = rows_v.shape[1]
    n_chunks = per_worker // chunk
    base = _sc_worker_base(per_worker)

    def gather(c, b):
        off = pl.multiple_of(base + c * chunk, chunk)
        pltpu.sync_copy(idx_hbm.at[pl.ds(off, chunk)], idx_v.at[b])
        return pltpu.make_async_copy(table_hbm.at[idx_v.at[b]], rows_v.at[b], sems.at[b])

    gather(0, 0).start()

    @pl.loop(0, n_chunks, step=2)
    def _(c0):
        for b in range(2):
            c = c0 + b

            @pl.when(c + 1 < n_chunks)
            def _():
                gather(c + 1, 1 - b).start()

            pltpu.make_async_copy(table_hbm.at[idx_v.at[b]], rows_v.at[b], sems.at[b]).wait()
            off = pl.multiple_of(base + c * chunk, chunk)
            pltpu.sync_copy(rows_v.at[b], out_hbm.at[pl.ds(off, chunk)])


def _sc_gather_rows(table, idx):
    n = idx.shape[0]
    d = table.shape[1]
    workers = SC_CORES * SC_SUBCORES
    per_worker = n // workers
    assert n % workers == 0 and per_worker % (2 * SC_GATHER_ROWS) == 0
    return pl.kernel(
        functools.partial(_sc_gather_kernel, per_worker=per_worker),
        out_type=jax.ShapeDtypeStruct((n, d), table.dtype),
        mesh=_sc_mesh(),
        scratch_types=[pltpu.VMEM((2, SC_GATHER_ROWS), jnp.int32),
                       pltpu.VMEM((2, SC_GATHER_ROWS, d), table.dtype),
                       pltpu.SemaphoreType.DMA((2,))],
        name="sc_row_gather",
    )(table, idx)


def _experts_kernel(be_ref, nused_ref, valid_ref, x_ref, w1_ref, b1_ref, w2_ref, b2_ref, y_ref,
                    w1b_ref, w2b_ref):
    i = pl.program_id(0)
    prev = be_ref[jnp.maximum(i - 1, 0)]
    fresh = (i == 0) | (be_ref[i] != prev)

    @pl.when(fresh)
    def _():
        w1b_ref[...] = w1_ref[...].astype(BF16)
        w2b_ref[...] = w2_ref[...].astype(BF16)

    @pl.when(i < nused_ref[0])
    def _():
        row = lax.broadcasted_iota(jnp.int32, x_ref.shape, 0)
        x_lo, x_hi = _unpack_bf16_pairs(jnp.where(row < valid_ref[i], x_ref[...], jnp.uint32(0)))
        x = jnp.concatenate([x_lo.astype(BF16), x_hi.astype(BF16)], axis=1)
        gu = jnp.dot(x, w1b_ref[...], preferred_element_type=F32) + b1_ref[...]
        g = jnp.minimum(gu[:, :D_FF], SWIGLU_LIMIT)
        up = jnp.clip(gu[:, D_FF:], -SWIGLU_LIMIT, SWIGLU_LIMIT)
        act = ((up + 1.0) * g * jax.nn.sigmoid(SWIGLU_ALPHA * g)).astype(BF16)
        y = jnp.dot(act, w2b_ref[...], preferred_element_type=F32) + b2_ref[...]
        y_ref[...] = _pack_bf16_pairs(y)

    @pl.when(i >= nused_ref[0])
    def _():
        y_ref[...] = jnp.zeros_like(y_ref)


def _experts(xb, block_e, n_used, valid, w1, b1, w2, b2, layer):
    p_rows, packed_w = xb.shape
    d = 2 * packed_w
    depth, ne = w1.shape[:2]
    rb = EXPERT_ROWS
    wmap = lambda i, be, nu, va: (layer, be[i], 0, 0)
    rows = pl.BlockSpec((rb, packed_w), lambda i, be, nu, va: (i, 0))
    grid_spec = pltpu.PrefetchScalarGridSpec(
        num_scalar_prefetch=3,
        grid=(p_rows // rb,),
        in_specs=[
            rows,
            pl.BlockSpec((None, None, d, 2 * D_FF), wmap),
            pl.BlockSpec((None, None, 1, 2 * D_FF), wmap),
            pl.BlockSpec((None, None, D_FF, d), wmap),
            pl.BlockSpec((None, None, 1, d), wmap),
        ],
        out_specs=rows,
        scratch_shapes=[pltpu.VMEM((d, 2 * D_FF), BF16), pltpu.VMEM((D_FF, d), BF16)],
    )
    return pl.pallas_call(
        _experts_kernel,
        grid_spec=grid_spec,
        out_shape=jax.ShapeDtypeStruct((p_rows, packed_w), jnp.uint32),
        compiler_params=_cparams("arbitrary"),
        name="expert_ffn",
    )(block_e, n_used, valid, xb, w1, b1.reshape(depth, ne, 1, 2 * D_FF), w2, b2.reshape(depth, ne, 1, d))


def _combine_kernel(gate_ref, x_ref, gt_ref, g_ref, y0_ref, y1_ref, y2_ref, y3_ref, o_ref):
    tm = x_ref.shape[0]
    gates = gate_ref[...]
    gates = jnp.concatenate([gates, jnp.zeros((LANES - TOP_K, tm), F32)], axis=0).T
    lo, hi = _unpack_bf16_pairs(y0_ref[...])
    y_lo, y_hi = gates[:, 0:1] * lo, gates[:, 0:1] * hi
    for k, y_ref in enumerate((y1_ref, y2_ref, y3_ref), start=1):
        lo, hi = _unpack_bf16_pairs(y_ref[...])
        y_lo, y_hi = y_lo + gates[:, k:k + 1] * lo, y_hi + gates[:, k:k + 1] * hi
    y = jnp.concatenate([y_lo, y_hi], axis=1)
    o_ref[...] = x_ref[...] + gt_ref[0] * _rms(y, g_ref[...])


def _combine(y4, gate_t, x2, gt2, g_post, seq):
    t, d = x2.shape
    tm = TOKEN_TILE
    tiles = t // tm
    tiles_per_seq = seq // tm
    row = pl.BlockSpec((tm, d), lambda i: (i, 0))
    slot = lambda k: pl.BlockSpec((tm, y4.shape[1]), lambda i: (k * tiles + i, 0))
    return pl.pallas_call(
        _combine_kernel,
        grid=(tiles,),
        in_specs=[pl.BlockSpec((TOP_K, tm), lambda i: (0, i)),
                  row, pl.BlockSpec((1, 1, d), lambda i: (i // tiles_per_seq, 0, 0)), _full((1, d))]
        + [slot(k) for k in range(TOP_K)],
        out_specs=row,
        out_shape=jax.ShapeDtypeStruct((t, d), F32),
        compiler_params=_cparams("arbitrary"),
        name="expert_combine",
    )(gate_t, x2, gt2, g_post.reshape(1, d), y4, y4, y4, y4)


def _route_plan(idx_t, rank_t, counts):
    rb = EXPERT_ROWS
    k, t = idx_t.shape
    padded = (counts + rb - 1) // rb * rb
    pad_ends = jnp.cumsum(padded)
    pad_starts = pad_ends - padded
    experts = jnp.arange(N_EXPERTS, dtype=jnp.int32)
    start_of = jnp.sum(jnp.where(idx_t[None] == experts[:, None, None], pad_starts[:, None, None], 0), axis=0)
    dest = (start_of + rank_t).astype(jnp.int32)
    n_blocks = (k * t) // rb + N_EXPERTS
    blk_start = jnp.arange(n_blocks, dtype=jnp.int32) * rb
    block_e = jnp.minimum(jnp.sum(pad_ends[None, :] <= blk_start[:, None], axis=1), N_EXPERTS - 1)
    onehot_e = block_e[:, None] == experts[None, :]
    cnt_b = jnp.sum(jnp.where(onehot_e, counts[None, :], 0), axis=1)
    start_b = jnp.sum(jnp.where(onehot_e, pad_starts[None, :], 0), axis=1)
    valid = jnp.clip(cnt_b - (blk_start - start_b), 0, rb).astype(jnp.int32)
    n_used = (pad_ends[-1] // rb).astype(jnp.int32).reshape(1)
    return dest.reshape(-1), block_e.astype(jnp.int32), n_used, valid, n_blocks * rb


def kernel(x, c, rel_bias, w_ada, b_ada, g_pre_mix, g_post_mix, g_pre_ffn, g_post_ffn, w_in, ssm_a_re, ssm_a_im, ssm_log_dt, ssm_b_re, ssm_b_im, ssm_c_re, ssm_c_im, ssm_d, w_glu, w_ssm_out, w_att_out, w_out, w_router, b_router, w_exp_in, b_exp_in, w_exp_out, b_exp_out):
    bsz, seq, d = x.shape
    depth = w_ada.shape[0]
    t = bsz * seq
    assert d == D_MODEL and seq % TOKEN_TILE == 0 and TOKEN_TILE % MOBA_BLOCK == 0
    assert seq % (SSM_CHUNK * SSM_CHUNKS_PER_STEP) == 0

    mod = _ada_mod(c, w_ada, b_ada)
    bias_tiles, far_bias = _moba_bias_tiles(rel_bias.astype(F32))
    x2 = x.reshape(t, d)
    for l in range(depth):
        sh1, sc1, gt1, sh2, sc2, gt2 = [m.reshape(bsz, 1, d) for m in jnp.split(mod[l], N_MOD, axis=-1)]
        q4, k4, vt4, u_tm, ga, gs = _premix(x2, g_pre_mix[l], sc1, sh1, w_in[l], seq)
        att = _moba(q4, k4, vt4, bias_tiles, far_bias, bsz).reshape(t, ATT_WIDTH)
        ssm_w = _ssm_weights(ssm_a_re[l], ssm_a_im[l], ssm_log_dt[l], ssm_b_re[l], ssm_b_im[l],
                             ssm_c_re[l], ssm_c_im[l], ssm_d[l])
        ys_tm = _ssm(u_tm, ssm_w, bsz, seq)
        x1, h2, idx_t, gate_t, rank_t, cnt = _postmix(
            att, ys_tm, ga, gs, x2, gt1, sc2, sh2, g_post_mix[l], g_pre_ffn[l],
            w_glu[l], w_ssm_out[l], w_att_out[l], w_out[l], w_router[l], b_router[l], seq)
        dest_flat, block_e, n_used, valid, p_rows = _route_plan(idx_t, rank_t, cnt[:, 0])
        xb = _sc_scatter_rows(h2, dest_flat, p_rows)
        yb = _experts(xb, block_e, n_used, valid, w_exp_in, b_exp_in, w_exp_out, b_exp_out, l)
        y4 = _sc_gather_rows(yb, dest_flat)
        x2 = _combine(y4, gate_t, x1, gt2, g_post_ffn[l], seq)
    return x2.reshape(bsz, seq, d)
```

```python
import functools
import math

import jax
import jax.numpy as jnp
from jax import lax
from jax.experimental import pallas as pl
from jax.experimental.pallas import tpu as pltpu
from jax.experimental.pallas import tpu_sc as plsc

F32 = jnp.float32
BF16 = jnp.bfloat16

D_MODEL = 1024
ATT_HEADS = 8
HEAD_DIM = 64
ATT_WIDTH = ATT_HEADS * HEAD_DIM
MOBA_BLOCK = 256
MOBA_TOPK = 3
REL_BUCKETS = 32
REL_MAX_DIST = 128
SSM_WIDTH = D_MODEL // 2
SSM_GROUP = 16
SSM_GROUPS = SSM_WIDTH // SSM_GROUP
SSM_STATE = 64
N_EXPERTS = 32
TOP_K = 4
D_FF = D_MODEL
SWIGLU_ALPHA = 1.702
SWIGLU_LIMIT = 7.0
RMS_EPS = 1e-6
N_MOD = 6

SSM_CHUNK = 8
SSM_CHUNKS_PER_STEP = 16
LANES = 128
MXU_DIM = 256
TOKEN_TILE = 512
SC_CORES = 2
SC_SUBCORES = 16
SC_GATHER_ROWS = 64
EXPERT_ROWS = 512
MASK_NEG = -1e30
LOG2E = math.log2(math.e)
SUM_ROWS = 16
VMEM_LIMIT = 56 * 1024 * 1024

_NT = (((1,), (1,)), ((), ()))


def _cparams(*sem):
    return pltpu.CompilerParams(dimension_semantics=sem, vmem_limit_bytes=VMEM_LIMIT)


def _pack_bf16_pairs(x):
    n = x.shape[1] // 2
    bits = lax.bitcast_convert_type(x.astype(BF16).astype(F32), jnp.uint32)
    return lax.shift_right_logical(bits[:, :n], jnp.uint32(16)) | (bits[:, n:] & jnp.uint32(0xFFFF0000))


def _unpack_bf16_pairs(w):
    lo = lax.bitcast_convert_type(lax.shift_left(w, jnp.uint32(16)), F32)
    hi = lax.bitcast_convert_type(w & jnp.uint32(0xFFFF0000), F32)
    return lo, hi


def _rms(x, g):
    return x * lax.rsqrt(jnp.mean(x * x, axis=-1, keepdims=True) + RMS_EPS) * g


def _full(shape, **kw):
    n = len(shape)
    return pl.BlockSpec(shape, lambda *_: (0,) * n, **kw)


def _ada_kernel(c_ref, w_ref, b_ref, o_ref):
    c = c_ref[...]
    cond = c * jax.nn.sigmoid(c)
    o_ref[0] = jnp.dot(cond, w_ref[0], preferred_element_type=F32,
                       precision=lax.Precision.HIGHEST) + b_ref[0]


def _ada_mod(c, w_ada, b_ada):
    depth, d, nd = w_ada.shape
    bsz = c.shape[0]
    return pl.pallas_call(
        _ada_kernel,
        grid=(depth, nd // d),
        in_specs=[
            pl.BlockSpec((bsz, d), lambda l, j: (0, 0)),
            pl.BlockSpec((1, d, d), lambda l, j: (l, 0, j)),
            pl.BlockSpec((1, 1, d), lambda l, j: (l, 0, j)),
        ],
        out_specs=pl.BlockSpec((1, bsz, d), lambda l, j: (l, 0, j)),
        out_shape=jax.ShapeDtypeStruct((depth, bsz, nd), F32),
        compiler_params=_cparams("arbitrary", "arbitrary"),
        name="ada_mod",
    )(c, w_ada, b_ada.reshape(depth, 1, nd))


def _premix_kernel(x_ref, g_ref, sc_ref, sh_ref, wq_ref, wk_ref, wvt_ref, wu_ref, wga_ref, wgs_ref,
                   q_ref, k_ref, vt_ref, u_ref, ga_ref, gs_ref):
    x = x_ref[...]
    h = _rms(x, g_ref[...]) * (1.0 + sc_ref[0]) + sh_ref[0]
    hb = h.astype(BF16)
    nblk = q_ref.shape[0]
    q = (jnp.dot(hb, wq_ref[...], preferred_element_type=F32) * (HEAD_DIM ** -0.5 * LOG2E)).astype(BF16)
    k = jnp.dot(hb, wk_ref[...], preferred_element_type=F32).astype(BF16)
    vt = lax.dot_general(wvt_ref[...], hb, _NT, preferred_element_type=F32).astype(BF16)
    for r in range(nblk):
        q_ref[r] = q[r * MOBA_BLOCK:(r + 1) * MOBA_BLOCK]
        k_ref[r] = k[r * MOBA_BLOCK:(r + 1) * MOBA_BLOCK]
        vt_ref[r] = vt[:, r * MOBA_BLOCK:(r + 1) * MOBA_BLOCK]
    u_ref[...] = jnp.dot(hb, wu_ref[...], preferred_element_type=F32).astype(BF16)
    ga = jnp.dot(hb, wga_ref[...], preferred_element_type=F32)
    ga_ref[...] = jax.nn.sigmoid(ga).astype(BF16)
    gs = jnp.dot(hb, wgs_ref[...], preferred_element_type=F32)
    gs_ref[...] = jax.nn.sigmoid(gs).astype(BF16)


def _premix(x2, g, sc, sh, w_in, seq):
    t, d = x2.shape
    tm = TOKEN_TILE
    tiles_per_seq = seq // tm
    nblk = tm // MOBA_BLOCK
    a = ATT_WIDTH
    wb = w_in.astype(BF16)
    wq, wk, wv = wb[:, :a], wb[:, a:2 * a], wb[:, 2 * a:3 * a]
    wu = wb[:, 3 * a:3 * a + SSM_WIDTH]
    wga = wb[:, 3 * a + SSM_WIDTH:3 * a + SSM_WIDTH + d]
    wgs = wb[:, 3 * a + SSM_WIDTH + d:]
    mod_spec = pl.BlockSpec((1, 1, d), lambda i: (i // tiles_per_seq, 0, 0))
    blk3 = lambda rows, cols: pl.BlockSpec((nblk, rows, cols), lambda i: (i, 0, 0))
    row = lambda cols: pl.BlockSpec((tm, cols), lambda i: (i, 0))
    time_major = pl.BlockSpec((tm, SSM_WIDTH), lambda i: (i % tiles_per_seq, i // tiles_per_seq))
    nb_total = t // MOBA_BLOCK
    return pl.pallas_call(
        _premix_kernel,
        grid=(t // tm,),
        in_specs=[row(d), _full((1, d)), mod_spec, mod_spec,
                  _full(wq.shape), _full(wk.shape), _full((a, d)), _full(wu.shape),
                  _full(wga.shape), _full(wgs.shape)],
        out_specs=[blk3(MOBA_BLOCK, a), blk3(MOBA_BLOCK, a), blk3(a, MOBA_BLOCK),
                   time_major, row(d), row(d)],
        out_shape=[jax.ShapeDtypeStruct((nb_total, MOBA_BLOCK, a), BF16),
                   jax.ShapeDtypeStruct((nb_total, MOBA_BLOCK, a), BF16),
                   jax.ShapeDtypeStruct((nb_total, a, MOBA_BLOCK), BF16),
                   jax.ShapeDtypeStruct((seq, (t // seq) * SSM_WIDTH), BF16),
                   jax.ShapeDtypeStruct((t, d), BF16),
                   jax.ShapeDtypeStruct((t, d), BF16)],
        compiler_params=_cparams("arbitrary"),
        name="premix_inproj",
    )(x2, g.reshape(1, d), sc, sh, wq, wk, wv.T, wu, wga, wgs)


def _moba_kernel(far_ref, q_ref, k_ref, vt_ref, bias_ref, o_ref, kmean_ref, neg_ref, *state_refs):
    m_refs = state_refs[0::2]
    acc_refs = state_refs[1::2]
    nb = k_ref.shape[0]
    blk = MOBA_BLOCK
    i = pl.program_id(1)

    @pl.when(i == 0)
    def _():
        for n in range(nb):
            kmean_ref[n:n + 1, :] = jnp.mean(k_ref[n].astype(F32), axis=0, keepdims=True)

    lane = lax.broadcasted_iota(jnp.int32, (blk, 2 * HEAD_DIM), 1)
    nidx = lax.broadcasted_iota(jnp.int32, (nb, blk), 0)
    valid = nidx < i
    pair_w = 2 * HEAD_DIM

    def pair_cols(head):
        return slice((head // 2) * pair_w, (head // 2 + 1) * pair_w)

    def head_rows(head):
        return slice(head * HEAD_DIM, (head + 1) * HEAD_DIM)

    def q_head(head):
        q2 = q_ref[0, :, pair_cols(head)]
        lo = (head % 2) * HEAD_DIM
        return jnp.where((lane >= lo) & (lane < lo + HEAD_DIM), q2, jnp.zeros_like(q2))

    for head in range(ATT_HEADS):
        km = kmean_ref[:, pair_cols(head)]
        km_hi = km.astype(BF16)
        km_lo = (km - km_hi.astype(F32)).astype(BF16)
        qh = q_head(head)
        gate = (lax.dot_general(km_hi, qh, _NT, preferred_element_type=F32)
                + lax.dot_general(km_lo, qh, _NT, preferred_element_type=F32))
        g = jnp.where(valid, gate, -jnp.inf)
        sel = jnp.zeros((nb, blk), jnp.bool_)
        for _ in range(MOBA_TOPK):
            mx = jnp.max(g, axis=0, keepdims=True)
            first = jnp.min(jnp.where(g == mx, nidx, nb), axis=0, keepdims=True)
            pick = nidx == first
            sel = sel | pick
            g = jnp.where(pick, -jnp.inf, g)
        neg_ref[head * nb:(head + 1) * nb, :] = jnp.where(sel & valid, 0.0, MASK_NEG)

    def scores(head, j, add):
        kb = k_ref[j, :, pair_cols(head)]
        return lax.dot_general(kb, q_head(head), _NT, preferred_element_type=F32) + add

    ones_rows = jnp.ones((SUM_ROWS, blk), BF16)

    def weighted_values(head, j, p):
        vb = jnp.concatenate([vt_ref[j, head_rows(head), :], ones_rows], axis=0)
        return jnp.dot(vb, p.astype(BF16), preferred_element_type=F32)

    def start(head, j, st):
        m = jnp.max(st, axis=0, keepdims=True)
        m_refs[head][...] = m
        acc_refs[head][...] = weighted_values(head, j, jnp.exp2(st - m))

    def update(head, j, st):
        m = m_refs[head][...]
        m_new = jnp.maximum(m, jnp.max(st, axis=0, keepdims=True))
        alpha = jnp.exp2(m - m_new)
        m_refs[head][...] = m_new
        acc_refs[head][...] = alpha * acc_refs[head][...] + weighted_values(head, j, jnp.exp2(st - m_new))

    def sweep(j, add_of, absorb):
        st = scores(0, j, add_of(0))
        for head in range(ATT_HEADS):
            nxt = scores(head + 1, j, add_of(head + 1)) if head + 1 < ATT_HEADS else None
            absorb(head, j, st)
            st = nxt

    sweep(i, lambda h: bias_ref[h, 0], start)

    @pl.when(i > 0)
    def _():
        sweep(i - 1, lambda h: bias_ref[h, 1] + neg_ref[pl.ds(h * nb + i - 1, 1), :], update)

    def far_block(j, carry):
        sweep(j, lambda h: neg_ref[pl.ds(h * nb + j, 1), :] + far_ref[h], update)
        return carry

    lax.fori_loop(0, jnp.maximum(i - 1, 0), far_block, 0)

    for hp in range(ATT_HEADS // 2):
        outs = [acc_refs[h][0:HEAD_DIM, :] / acc_refs[h][HEAD_DIM:HEAD_DIM + 1, :] for h in (2 * hp, 2 * hp + 1)]
        pair = jnp.concatenate(outs, axis=0)
        o_ref[0, :, hp * pair_w:(hp + 1) * pair_w] = pair.T.astype(BF16)


def _rel_bucket(dist):
    n = jnp.maximum(dist, 0)
    max_exact = REL_BUCKETS // 2
    nf = jnp.maximum(n, 1).astype(F32)
    large = max_exact + (jnp.log(nf / max_exact) / math.log(REL_MAX_DIST / max_exact)
                         * (REL_BUCKETS - max_exact)).astype(jnp.int32)
    large = jnp.minimum(large, REL_BUCKETS - 1)
    return jnp.where(n < max_exact, n, large)


def _moba_bias_tiles(rel_bias):
    blk = MOBA_BLOCK
    span = 2 * blk
    rel_bias = rel_bias * LOG2E
    vec = rel_bias[_rel_bucket(jnp.arange(span))].T.astype(F32)
    masked = jnp.full_like(vec[:, :blk], MASK_NEG)
    ring_own = jnp.concatenate([vec[:, :blk], masked], axis=1)
    ring_prev = jnp.concatenate([vec[:, blk:], vec[:, :blk]], axis=1)

    def toeplitz(ring):
        flat = jnp.tile(ring, (1, blk))[:, :blk * (span - 1)]
        return flat.reshape(-1, blk, span - 1)[:, :, :blk]

    tiles = jnp.stack([toeplitz(ring_own), toeplitz(ring_prev)], axis=1)
    far = rel_bias[_rel_bucket(jnp.array(span))]
    return tiles, far.astype(F32)


def _moba(q4, k4, vt4, bias_tiles, far_bias, bsz):
    nb_total, blk, a = q4.shape
    nb = nb_total // bsz
    return pl.pallas_call(
        _moba_kernel,
        grid=(bsz, nb),
        in_specs=[
            pl.BlockSpec(memory_space=pltpu.SMEM),
            pl.BlockSpec((1, blk, a), lambda b, i: (b * nb + i, 0, 0)),
            pl.BlockSpec((nb, blk, a), lambda b, i: (b, 0, 0)),
            pl.BlockSpec((nb, a, blk), lambda b, i: (b, 0, 0)),
            _full(bias_tiles.shape),
        ],
        out_specs=pl.BlockSpec((1, blk, a), lambda b, i: (b * nb + i, 0, 0)),
        out_shape=jax.ShapeDtypeStruct((nb_total, blk, a), BF16),
        scratch_shapes=[pltpu.VMEM((nb, a), F32), pltpu.VMEM((ATT_HEADS * nb, blk), F32)]
        + [pltpu.VMEM((1, blk), F32), pltpu.VMEM((HEAD_DIM + SUM_ROWS, blk), F32)] * ATT_HEADS,
        compiler_params=_cparams("arbitrary", "arbitrary"),
        name="moba_attention",
    )(far_bias, q4, k4, vt4, bias_tiles)


def _ssm_weights(a_re, a_im, log_dt, b_re, b_im, c_re, c_im, d_skip):
    g, p, cw, n = SSM_GROUPS, SSM_STATE, SSM_GROUP, SSM_CHUNK
    gs = LANES // cw
    nq = g // gs
    npair = n // 2
    pairs_per_set = gs // 2
    lam_re = jnp.minimum(a_re.astype(F32), -1e-4)
    lam_im = a_im.astype(F32)
    dt = jnp.exp(log_dt.astype(F32))[:, None]
    mag = jnp.exp(lam_re * dt)
    lb_re = mag * jnp.cos(lam_im * dt)
    lb_im = mag * jnp.sin(lam_im * dt)
    n_re = lb_re - 1.0
    n_im = lb_im
    den = lam_re * lam_re + lam_im * lam_im
    z_re = ((n_re * lam_re + n_im * lam_im) / den)[..., None]
    z_im = ((n_im * lam_re - n_re * lam_im) / den)[..., None]
    br, bi = b_re.astype(F32), b_im.astype(F32)
    bb_re = z_re * br - z_im * bi
    bb_im = z_re * bi + z_im * br
    pw_re, pw_im = [jnp.ones_like(lb_re)], [jnp.zeros_like(lb_im)]
    for _ in range(n):
        r, im = pw_re[-1], pw_im[-1]
        pw_re.append(r * lb_re - im * lb_im)
        pw_im.append(r * lb_im + im * lb_re)
    pw_re, pw_im = jnp.stack(pw_re), jnp.stack(pw_im)
    cr, ci = c_re.astype(F32), c_im.astype(F32)
    hi = lax.Precision.HIGHEST
    rev_re, rev_im = pw_re[n - 1::-1], pw_im[n - 1::-1]
    sb_re = jnp.einsum('sgp,gpc->sgcp', rev_re, bb_re) - jnp.einsum('sgp,gpc->sgcp', rev_im, bb_im)
    sb_im = jnp.einsum('sgp,gpc->sgcp', rev_re, bb_im) + jnp.einsum('sgp,gpc->sgcp', rev_im, bb_re)
    cl_re = jnp.einsum('gcp,tgp->tgpc', cr, pw_re[1:]) - jnp.einsum('gcp,tgp->tgpc', ci, pw_im[1:])
    cl_im = jnp.einsum('gcp,tgp->tgpc', cr, pw_im[1:]) + jnp.einsum('gcp,tgp->tgpc', ci, pw_re[1:])
    cp_re = jnp.einsum('gcp,tgp->tgcp', cr, pw_re[:n]) - jnp.einsum('gcp,tgp->tgcp', ci, pw_im[:n])
    cp_im = jnp.einsum('gcp,tgp->tgcp', cr, pw_im[:n]) + jnp.einsum('gcp,tgp->tgcp', ci, pw_re[:n])
    kern = (jnp.einsum('tgcp,gpd->gtcd', cp_re, bb_re, precision=hi)
            - jnp.einsum('tgcp,gpd->gtcd', cp_im, bb_im, precision=hi))

    kp = jnp.concatenate([jnp.zeros_like(kern[:, :1]), kern], axis=1)
    dl = jnp.arange(npair)[:, None, None]
    s0 = jnp.arange(2)[None, :, None]
    t0 = jnp.arange(2)[None, None, :]
    lag = 2 * dl + t0 - s0
    kl = kp[:, lag + 1].reshape(nq, gs, npair, 2, 2, cw, cw)
    base = kl.transpose(0, 2, 3, 1, 6, 4, 5)
    same_group = jnp.eye(gs, dtype=F32)
    w_toe = jnp.concatenate(
        [base[..., t0, :] * same_group[:, g1][None, None, None, :, None, None]
         for t0 in range(2) for g1 in range(gs)], axis=-1)
    w_toe = w_toe.reshape(nq, npair, MXU_DIM, MXU_DIM).astype(BF16)

    member = (jnp.arange(gs)[None, :, None]
              == 2 * jnp.arange(pairs_per_set)[:, None, None] + jnp.arange(2)[None, None, :]).astype(F32)

    def pair_tiles(x):
        cols = [x[ri][:, None] * member[:, :, gl][None, :, None, None, :, None, None]
                for ri in range(2) for gl in range(2)]
        return jnp.concatenate(cols, axis=-1).reshape(g // 2, npair, MXU_DIM, MXU_DIM)

    sbs = jnp.stack([sb_re, sb_im]).reshape(2, npair, 2, nq, gs, cw, p)
    w_in_state = pair_tiles(sbs.transpose(0, 3, 1, 2, 4, 5, 6)).astype(BF16)
    sos = jnp.stack([cl_re, -cl_im]).reshape(2, npair, 2, nq, gs, p, cw)
    w_state_out = jnp.swapaxes(pair_tiles(sos.transpose(0, 3, 1, 2, 4, 6, 5)), -1, -2).astype(BF16)
    decay = jnp.stack([pw_re[n].reshape(g // 2, 2 * p), pw_im[n].reshape(g // 2, 2 * p)])
    dvec = d_skip.astype(F32).reshape(1, g * cw)
    return w_toe, w_in_state, w_state_out, decay, dvec


def _ssm_kernel(u_ref, wt_ref, wi_ref, wo_ref, dec_ref, d_ref, y_ref, s_ref, xp_ref, x_ref, *, bsz):
    kt = u_ref.shape[0]
    n = u_ref.shape[1] // bsz
    rows = kt * bsz
    npair = n // 2
    nq = wt_ref.shape[0]
    npairs_g = wi_ref.shape[0]
    per_set = npairs_g // nq
    half = MXU_DIM // 2

    @pl.when(pl.program_id(0) == 0)
    def _():
        x_ref[...] = jnp.zeros_like(x_ref)

    def piece(s, q):
        return u_ref[:, s * bsz:(s + 1) * bsz, q * LANES:(q + 1) * LANES].reshape(rows, LANES)

    lhs = {(sp, q): jnp.concatenate([piece(2 * sp, q), piece(2 * sp + 1, q)], axis=-1)
           for sp in range(npair) for q in range(nq)}

    for gp in range(npairs_g):
        q = gp // per_set
        acc = jnp.dot(lhs[0, q], wi_ref[gp, 0], preferred_element_type=F32)
        for sp in range(1, npair):
            acc = acc + jnp.dot(lhs[sp, q], wi_ref[gp, sp], preferred_element_type=F32)
        s_ref[:, gp * MXU_DIM:(gp + 1) * MXU_DIM] = acc

    for gp in range(npairs_g):
        re_cols = slice(gp * MXU_DIM, gp * MXU_DIM + half)
        im_cols = slice(gp * MXU_DIM + half, (gp + 1) * MXU_DIM)
        ar = dec_ref[0, gp:gp + 1, :]
        ai = dec_ref[1, gp:gp + 1, :]
        xr = x_ref[:, re_cols]
        xi = x_ref[:, im_cols]
        for kc in range(kt):
            rs = slice(kc * bsz, (kc + 1) * bsz)
            xp_ref[rs, re_cols] = xr.astype(BF16)
            xp_ref[rs, im_cols] = xi.astype(BF16)
            sr = s_ref[rs, re_cols]
            si = s_ref[rs, im_cols]
            xr, xi = ar * xr - ai * xi + sr, ar * xi + ai * xr + si
        x_ref[:, re_cols] = xr
        x_ref[:, im_cols] = xi

    for q in range(nq):
        for tp in range(npair):
            acc = jnp.dot(lhs[0, q], wt_ref[q, tp], preferred_element_type=F32)
            for sp in range(1, tp + 1):
                acc = acc + jnp.dot(lhs[sp, q], wt_ref[q, tp - sp], preferred_element_type=F32)
            for gp in range(q * per_set, (q + 1) * per_set):
                acc = acc + jnp.dot(xp_ref[:, gp * MXU_DIM:(gp + 1) * MXU_DIM], wo_ref[gp, tp],
                                    preferred_element_type=F32)
            for t0 in range(2):
                s = 2 * tp + t0
                y = acc[:, t0 * half:(t0 + 1) * half] + d_ref[:, q * LANES:(q + 1) * LANES] * piece(s, q).astype(F32)
                y_ref[:, s * bsz:(s + 1) * bsz, q * LANES:(q + 1) * LANES] = (
                    jax.nn.gelu(y).astype(BF16).reshape(kt, bsz, LANES))


def _ssm(u_tm, weights, bsz, seq):
    w_toe, w_in_state, w_state_out, decay, dvec = weights
    n = SSM_CHUNK
    kt = SSM_CHUNKS_PER_STEP
    nchunk = seq // n
    rows = kt * bsz
    state_w = SSM_GROUPS * 2 * SSM_STATE
    u3 = u_tm.reshape(nchunk, n * bsz, SSM_WIDTH)
    once = pl.Buffered(1)
    blk = pl.BlockSpec((kt, n * bsz, SSM_WIDTH), lambda i: (i, 0, 0))
    y3 = pl.pallas_call(
        functools.partial(_ssm_kernel, bsz=bsz),
        grid=(nchunk // kt,),
        in_specs=[blk, _full(w_toe.shape, pipeline_mode=once), _full(w_in_state.shape, pipeline_mode=once),
                  _full(w_state_out.shape, pipeline_mode=once), _full(decay.shape), _full(dvec.shape)],
        out_specs=blk,
        out_shape=jax.ShapeDtypeStruct(u3.shape, BF16),
        scratch_shapes=[pltpu.VMEM((rows, state_w), F32), pltpu.VMEM((rows, state_w), BF16),
                        pltpu.VMEM((bsz, state_w), F32)],
        compiler_params=_cparams("arbitrary"),
        name="s5_chunked_scan",
    )(u3, w_toe, w_in_state, w_state_out, decay, dvec)
    return y3.reshape(seq, bsz * SSM_WIDTH)


def _postmix_kernel(att_ref, ys_ref, ga_ref, gs_ref, x_ref, gt_ref, sc_ref, sh_ref, gpost_ref, gpre_ref,
                    wglu_ref, wso_ref, wao_ref, wo_ref, wrh_ref, wrl_ref, br_ref,
                    x1_ref, h2_ref, idx_ref, gate_ref, rank_ref, cnt_ref, run_ref):
    tm = x_ref.shape[0]
    ne = wrh_ref.shape[0]

    @pl.when(pl.program_id(0) == 0)
    def _():
        run_ref[...] = jnp.zeros_like(run_ref)

    halves = [slice(r * (tm // 2), (r + 1) * (tm // 2)) for r in range(2)]
    dot = functools.partial(jnp.dot, preferred_element_type=F32)
    att = [dot(att_ref[r, :], wao_ref[...]) for r in halves]
    glu = [dot(ys_ref[r, :], wglu_ref[...]) for r in halves]
    ssm = []
    for g in glu:
        sg = (g[:, :SSM_WIDTH] * jax.nn.sigmoid(g[:, SSM_WIDTH:])).astype(BF16)
        ssm.append(dot(sg, wso_ref[...]))
    y = []
    for r, a, s in zip(halves, att, ssm):
        merged = (ga_ref[r, :].astype(F32) * a + gs_ref[r, :].astype(F32) * s).astype(BF16)
        y.append(dot(merged, wo_ref[...]))
    logits = []
    for r, yh in zip(halves, y):
        x1 = x_ref[r, :] + gt_ref[0] * _rms(yh, gpost_ref[...])
        x1_ref[r, :] = x1
        h2 = _rms(x1, gpre_ref[...]) * (1.0 + sc_ref[0]) + sh_ref[0]
        h2_ref[r, :] = _pack_bf16_pairs(h2)
        h_hi = h2.astype(BF16)
        h_lo = (h2 - h_hi.astype(F32)).astype(BF16)
        logits.append(lax.dot_general(wrh_ref[...], h_hi, _NT, preferred_element_type=F32)
                      + lax.dot_general(wrh_ref[...], h_lo, _NT, preferred_element_type=F32)
                      + lax.dot_general(wrl_ref[...], h_hi, _NT, preferred_element_type=F32))
    logits = jnp.concatenate(logits, axis=1) + br_ref[...]
    eidx = lax.broadcasted_iota(jnp.int32, (ne, tm), 0)
    lg = logits
    vals, idxs = [], []
    for _ in range(TOP_K):
        mx = jnp.max(lg, axis=0, keepdims=True)
        first = jnp.min(jnp.where(lg == mx, eidx, ne), axis=0, keepdims=True)
        vals.append(mx)
        idxs.append(first)
        lg = jnp.where(eidx == first, -jnp.inf, lg)
    exps = [jnp.exp(v - vals[0]) for v in vals]
    denom = exps[0] + exps[1] + exps[2] + exps[3]
    gate_ref[...] = jnp.concatenate([e / denom for e in exps], axis=0)
    idx_ref[...] = jnp.concatenate(idxs, axis=0)
    onehot = jnp.where(lg == -jnp.inf, 1.0, 0.0)
    t_row = lax.broadcasted_iota(jnp.int32, (tm, tm), 0)
    t_col = lax.broadcasted_iota(jnp.int32, (tm, tm), 1)
    earlier = jnp.where(t_row < t_col, 1.0, 0.0).astype(BF16)
    before = jnp.dot(onehot.astype(BF16), earlier, preferred_element_type=F32) + run_ref[:, 0:1]
    ranks = [jnp.sum(jnp.where(eidx == ix, before, 0.0), axis=0, keepdims=True) for ix in idxs]
    rank_ref[...] = jnp.concatenate(ranks, axis=0).astype(jnp.int32)
    run_ref[...] = run_ref[...] + jnp.sum(onehot, axis=1, keepdims=True)
    cnt_ref[...] = run_ref[...].astype(jnp.int32)


def _postmix(att2, ys_tm, ga, gs, x2, gt1, sc2, sh2, g_post, g_pre, w_glu, w_ssm_out, w_att_out, w_out,
             w_router, b_router, seq):
    t, d = x2.shape
    tm = TOKEN_TILE
    tiles_per_seq = seq // tm
    ne = w_router.shape[1]
    wr_t = w_router.T.astype(F32)
    wr_hi = wr_t.astype(BF16)
    wr_lo = (wr_t - wr_hi.astype(F32)).astype(BF16)
    mod_spec = pl.BlockSpec((1, 1, d), lambda i: (i // tiles_per_seq, 0, 0))
    row = lambda cols: pl.BlockSpec((tm, cols), lambda i: (i, 0))
    time_major = pl.BlockSpec((tm, SSM_WIDTH), lambda i: (i % tiles_per_seq, i // tiles_per_seq))
    col = pl.BlockSpec((TOP_K, tm), lambda i: (0, i))
    weights = [w_glu.astype(BF16), w_ssm_out.astype(BF16), w_att_out.astype(BF16), w_out.astype(BF16),
               wr_hi, wr_lo, b_router.astype(F32).reshape(ne, 1)]
    return pl.pallas_call(
        _postmix_kernel,
        grid=(t // tm,),
        in_specs=[row(ATT_WIDTH), time_major, row(d), row(d), row(d), mod_spec, mod_spec, mod_spec,
                  _full((1, d)), _full((1, d))] + [_full(w.shape) for w in weights],
        out_specs=[row(d), row(d // 2), col, col, col, _full((ne, 128))],
        out_shape=[jax.ShapeDtypeStruct((t, d), F32), jax.ShapeDtypeStruct((t, d // 2), jnp.uint32),
                   jax.ShapeDtypeStruct((TOP_K, t), jnp.int32), jax.ShapeDtypeStruct((TOP_K, t), F32),
                   jax.ShapeDtypeStruct((TOP_K, t), jnp.int32), jax.ShapeDtypeStruct((ne, 128), jnp.int32)],
        scratch_shapes=[pltpu.VMEM((ne, 128), F32)],
        compiler_params=_cparams("arbitrary"),
        name="postmix_router",
    )(att2, ys_tm, ga, gs, x2, gt1, sc2, sh2, g_post.reshape(1, d), g_pre.reshape(1, d), *weights)


def _sc_worker_base(per_worker):
    return (lax.axis_index("s") * SC_CORES + lax.axis_index("c")) * per_worker


def _sc_mesh():
    return plsc.VectorSubcoreMesh(core_axis_name="c", subcore_axis_name="s",
                                  num_cores=SC_CORES, num_subcores=SC_SUBCORES)


def _sc_scatter_kernel(rows_hbm, idx_hbm, out_hbm, idx_v, rows_v, load_sems, store_sems, *, per_worker):
    chunk = rows_v.shape[1]
    n_tokens = rows_hbm.shape[0]
    n_chunks = per_worker // chunk
    base = _sc_worker_base(per_worker)

    def load(c, b):
        off = pl.multiple_of(base + c * chunk, chunk)
        for k in range(TOP_K):
            pltpu.sync_copy(idx_hbm.at[pl.ds(k * n_tokens + off, chunk)], idx_v.at[b, k])
        return pltpu.make_async_copy(rows_hbm.at[pl.ds(off, chunk)], rows_v.at[b], load_sems.at[b])

    def stores(b):
        return [pltpu.make_async_copy(rows_v.at[b], out_hbm.at[idx_v.at[b, k]], store_sems.at[b])
                for k in range(TOP_K)]

    load(0, 0).start()

    @pl.loop(0, n_chunks, step=2)
    def _(c0):
        for b in range(2):
            c = c0 + b

            @pl.when(c > 0)
            def _():
                for cp in stores(1 - b):
                    cp.wait()

            @pl.when(c + 1 < n_chunks)
            def _():
                load(c + 1, 1 - b).start()

            pltpu.make_async_copy(rows_hbm.at[pl.ds(0, chunk)], rows_v.at[b], load_sems.at[b]).wait()
            for cp in stores(b):
                cp.start()

    for cp in stores((n_chunks - 1) % 2):
        cp.wait()


def _sc_scatter_rows(rows, idx, n_out):
    t, d = rows.shape
    workers = SC_CORES * SC_SUBCORES
    per_worker = t // workers
    assert t % workers == 0 and per_worker % (2 * SC_GATHER_ROWS) == 0
    return pl.kernel(
        functools.partial(_sc_scatter_kernel, per_worker=per_worker),
        out_type=jax.ShapeDtypeStruct((n_out, d), rows.dtype),
        mesh=_sc_mesh(),
        scratch_types=[pltpu.VMEM((2, TOP_K, SC_GATHER_ROWS), jnp.int32),
                       pltpu.VMEM((2, SC_GATHER_ROWS, d), rows.dtype),
                       pltpu.SemaphoreType.DMA((2,)), pltpu.SemaphoreType.DMA((2,))],
        name="sc_row_scatter",
    )(rows, idx)


def _sc_gather_kernel(table_hbm, idx_hbm, out_hbm, idx_v, rows_v, sems, *, per_worker):
    chunk = rows_v.shape[1]
    n_chunks = per_worker // chunk
    base = _sc_worker_base(per_worker)

    def gather(c, b):
        off = pl.multiple_of(base + c * chunk, chunk)
        pltpu.sync_copy(idx_hbm.at[pl.ds(off, chunk)], idx_v.at[b])
        return pltpu.make_async_copy(table_hbm.at[idx_v.at[b]], rows_v.at[b], sems.at[b])

    gather(0, 0).start()

    @pl.loop(0, n_chunks, step=2)
    def _(c0):
        for b in range(2):
            c = c0 + b

            @pl.when(c + 1 < n_chunks)
            def _():
                gather(c + 1, 1 - b).start()

            pltpu.make_async_copy(table_hbm.at[idx_v.at[b]], rows_v.at[b], sems.at[b]).wait()
            off = pl.multiple_of(base + c * chunk, chunk)
            pltpu.sync_copy(rows_v.at[b], out_hbm.at[pl.ds(off, chunk)])


def _sc_gather_rows(table, idx):
    n = idx.shape[0]
    d = table.shape[1]
    workers = SC_CORES * SC_SUBCORES
    per_worker = n // workers
    assert n % workers == 0 and per_worker % (2 * SC_GATHER_ROWS) == 0
    return pl.kernel(
        functools.partial(_sc_gather_kernel, per_worker=per_worker),
        out_type=jax.ShapeDtypeStruct((n, d), table.dtype),
        mesh=_sc_mesh(),
        scratch_types=[pltpu.VMEM((2, SC_GATHER_ROWS), jnp.int32),
                       pltpu.VMEM((2, SC_GATHER_ROWS, d), table.dtype),
                       pltpu.SemaphoreType.DMA((2,))],
        name="sc_row_gather",
    )(table, idx)


def _experts_kernel(be_ref, nused_ref, valid_ref, next_ref, x_ref, b1_ref, b2_ref, w1_hbm, w2_hbm, y_ref,
                    w1s_ref, w2s_ref, w1b_ref, w2b_ref, sems, *, layer):
    i = pl.program_id(0)
    prev = be_ref[jnp.maximum(i - 1, 0)]
    fresh = (i < nused_ref[0]) & ((i == 0) | (be_ref[i] != prev))

    def fetch(e):
        return (pltpu.make_async_copy(w1_hbm.at[layer, e], w1s_ref, sems.at[0]),
                pltpu.make_async_copy(w2_hbm.at[layer, e], w2s_ref, sems.at[1]))

    @pl.when(i == 0)
    def _():
        for cp in fetch(be_ref[0]):
            cp.start()

    @pl.when(fresh)
    def _():
        for cp in fetch(be_ref[i]):
            cp.wait()
        w1b_ref[...] = w1s_ref[...].astype(BF16)
        w2b_ref[...] = w2s_ref[...].astype(BF16)

        @pl.when(next_ref[i] >= 0)
        def _():
            for cp in fetch(next_ref[i]):
                cp.start()

    @pl.when(i < nused_ref[0])
    def _():
        row = lax.broadcasted_iota(jnp.int32, x_ref.shape, 0)
        x_lo, x_hi = _unpack_bf16_pairs(jnp.where(row < valid_ref[i], x_ref[...], jnp.uint32(0)))
        x = jnp.concatenate([x_lo.astype(BF16), x_hi.astype(BF16)], axis=1)
        gu = jnp.dot(x, w1b_ref[...], preferred_element_type=F32) + b1_ref[...]
        g = jnp.minimum(gu[:, :D_FF], SWIGLU_LIMIT)
        up = jnp.clip(gu[:, D_FF:], -SWIGLU_LIMIT, SWIGLU_LIMIT)
        act = ((up + 1.0) * g * jax.nn.sigmoid(SWIGLU_ALPHA * g)).astype(BF16)
        y = jnp.dot(act, w2b_ref[...], preferred_element_type=F32) + b2_ref[...]
        y_ref[...] = _pack_bf16_pairs(y)

    @pl.when(i >= nused_ref[0])
    def _():
        y_ref[...] = jnp.zeros_like(y_ref)


def _experts(xb, block_e, n_used, valid, next_e, w1, b1, w2, b2, layer):
    p_rows, packed_w = xb.shape
    d = 2 * packed_w
    depth, ne = w1.shape[:2]
    rb = EXPERT_ROWS
    bmap = lambda i, be, nu, va, nx: (layer, be[i], 0, 0)
    rows = pl.BlockSpec((rb, packed_w), lambda i, be, nu, va, nx: (i, 0))
    grid_spec = pltpu.PrefetchScalarGridSpec(
        num_scalar_prefetch=4,
        grid=(p_rows // rb,),
        in_specs=[
            rows,
            pl.BlockSpec((None, None, 1, 2 * D_FF), bmap),
            pl.BlockSpec((None, None, 1, d), bmap),
            pl.BlockSpec(memory_space=pl.ANY),
            pl.BlockSpec(memory_space=pl.ANY),
        ],
        out_specs=rows,
        scratch_shapes=[pltpu.VMEM((d, 2 * D_FF), w1.dtype), pltpu.VMEM((D_FF, d), w2.dtype),
                        pltpu.VMEM((d, 2 * D_FF), BF16), pltpu.VMEM((D_FF, d), BF16),
                        pltpu.SemaphoreType.DMA((2,))],
    )
    return pl.pallas_call(
        functools.partial(_experts_kernel, layer=layer),
        grid_spec=grid_spec,
        out_shape=jax.ShapeDtypeStruct((p_rows, packed_w), jnp.uint32),
        compiler_params=_cparams("arbitrary"),
        name="expert_ffn",
    )(block_e, n_used, valid, next_e, xb, b1.reshape(depth, ne, 1, 2 * D_FF), b2.reshape(depth, ne, 1, d), w1, w2)


def _combine_kernel(gate_ref, x_ref, gt_ref, g_ref, y0_ref, y1_ref, y2_ref, y3_ref, o_ref):
    tm = x_ref.shape[0]
    gates = gate_ref[...]
    gates = jnp.concatenate([gates, jnp.zeros((LANES - TOP_K, tm), F32)], axis=0).T
    lo, hi = _unpack_bf16_pairs(y0_ref[...])
    y_lo, y_hi = gates[:, 0:1] * lo, gates[:, 0:1] * hi
    for k, y_ref in enumerate((y1_ref, y2_ref, y3_ref), start=1):
        lo, hi = _unpack_bf16_pairs(y_ref[...])
        y_lo, y_hi = y_lo + gates[:, k:k + 1] * lo, y_hi + gates[:, k:k + 1] * hi
    y = jnp.concatenate([y_lo, y_hi], axis=1)
    o_ref[...] = x_ref[...] + gt_ref[0] * _rms(y, g_ref[...])


def _combine(y4, gate_t, x2, gt2, g_post, seq):
    t, d = x2.shape
    tm = TOKEN_TILE
    tiles = t // tm
    tiles_per_seq = seq // tm
    row = pl.BlockSpec((tm, d), lambda i: (i, 0))
    slot = lambda k: pl.BlockSpec((tm, y4.shape[1]), lambda i: (k * tiles + i, 0))
    return pl.pallas_call(
        _combine_kernel,
        grid=(tiles,),
        in_specs=[pl.BlockSpec((TOP_K, tm), lambda i: (0, i)),
                  row, pl.BlockSpec((1, 1, d), lambda i: (i // tiles_per_seq, 0, 0)), _full((1, d))]
        + [slot(k) for k in range(TOP_K)],
        out_specs=row,
        out_shape=jax.ShapeDtypeStruct((t, d), F32),
        compiler_params=_cparams("arbitrary"),
        name="expert_combine",
    )(gate_t, x2, gt2, g_post.reshape(1, d), y4, y4, y4, y4)


def _route_plan(idx_t, rank_t, counts):
    rb = EXPERT_ROWS
    k, t = idx_t.shape
    padded = (counts + rb - 1) // rb * rb
    pad_ends = jnp.cumsum(padded)
    pad_starts = pad_ends - padded
    experts = jnp.arange(N_EXPERTS, dtype=jnp.int32)
    start_of = jnp.sum(jnp.where(idx_t[None] == experts[:, None, None], pad_starts[:, None, None], 0), axis=0)
    dest = (start_of + rank_t).astype(jnp.int32)
    n_blocks = (k * t) // rb + N_EXPERTS
    blk_start = jnp.arange(n_blocks, dtype=jnp.int32) * rb
    block_e = jnp.minimum(jnp.sum(pad_ends[None, :] <= blk_start[:, None], axis=1), N_EXPERTS - 1)
    onehot_e = block_e[:, None] == experts[None, :]
    cnt_b = jnp.sum(jnp.where(onehot_e, counts[None, :], 0), axis=1)
    start_b = jnp.sum(jnp.where(onehot_e, pad_starts[None, :], 0), axis=1)
    valid = jnp.clip(cnt_b - (blk_start - start_b), 0, rb).astype(jnp.int32)
    n_used = (pad_ends[-1] // rb).astype(jnp.int32).reshape(1)
    later_nonempty = (experts[None, :] > experts[:, None]) & (counts[None, :] > 0)
    next_nonempty = jnp.min(jnp.where(later_nonempty, experts[None, :], N_EXPERTS), axis=1)
    next_nonempty = jnp.where(next_nonempty == N_EXPERTS, -1, next_nonempty)
    next_e = jnp.sum(jnp.where(onehot_e, next_nonempty[None, :], 0), axis=1).astype(jnp.int32)
    return dest.reshape(-1), block_e.astype(jnp.int32), n_used, valid, next_e, n_blocks * rb


def kernel(x, c, rel_bias, w_ada, b_ada, g_pre_mix, g_post_mix, g_pre_ffn, g_post_ffn, w_in, ssm_a_re, ssm_a_im, ssm_log_dt, ssm_b_re, ssm_b_im, ssm_c_re, ssm_c_im, ssm_d, w_glu, w_ssm_out, w_att_out, w_out, w_router, b_router, w_exp_in, b_exp_in, w_exp_out, b_exp_out):
    bsz, seq, d = x.shape
    depth = w_ada.shape[0]
    t = bsz * seq
    assert d == D_MODEL and seq % TOKEN_TILE == 0 and TOKEN_TILE % MOBA_BLOCK == 0
    assert seq % (SSM_CHUNK * SSM_CHUNKS_PER_STEP) == 0

    mod = _ada_mod(c, w_ada, b_ada)
    bias_tiles, far_bias = _moba_bias_tiles(rel_bias.astype(F32))
    x2 = x.reshape(t, d)
    for l in range(depth):
        sh1, sc1, gt1, sh2, sc2, gt2 = [m.reshape(bsz, 1, d) for m in jnp.split(mod[l], N_MOD, axis=-1)]
        q4, k4, vt4, u_tm, ga, gs = _premix(x2, g_pre_mix[l], sc1, sh1, w_in[l], seq)
        att = _moba(q4, k4, vt4, bias_tiles, far_bias, bsz).reshape(t, ATT_WIDTH)
        ssm_w = _ssm_weights(ssm_a_re[l], ssm_a_im[l], ssm_log_dt[l], ssm_b_re[l], ssm_b_im[l],
                             ssm_c_re[l], ssm_c_im[l], ssm_d[l])
        ys_tm = _ssm(u_tm, ssm_w, bsz, seq)
        x1, h2, idx_t, gate_t, rank_t, cnt = _postmix(
            att, ys_tm, ga, gs, x2, gt1, sc2, sh2, g_post_mix[l], g_pre_ffn[l],
            w_glu[l], w_ssm_out[l], w_att_out[l], w_out[l], w_router[l], b_router[l], seq)
        dest_flat, block_e, n_used, valid, next_e, p_rows = _route_plan(idx_t, rank_t, cnt[:, 0])
        xb = _sc_scatter_rows(h2, dest_flat, p_rows)
        yb = _experts(xb, block_e, n_used, valid, next_e, w_exp_in, b_exp_in, w_exp_out, b_exp_out, l)
        y4 = _sc_gather_rows(yb, dest_flat)
        x2 = _combine(y4, gate_t, x1, gt2, g_post_ffn[l], seq)
    return x2.reshape(bsz, seq, d)
```

```python
import functools
import math

import jax
import jax.numpy as jnp
from jax import lax
from jax.experimental import pallas as pl
from jax.experimental.pallas import tpu as pltpu
from jax.experimental.pallas import tpu_sc as plsc

F32 = jnp.float32
BF16 = jnp.bfloat16

D_MODEL = 1024
ATT_HEADS = 8
HEAD_DIM = 64
ATT_WIDTH = ATT_HEADS * HEAD_DIM
MOBA_BLOCK = 256
MOBA_TOPK = 3
REL_BUCKETS = 32
REL_MAX_DIST = 128
SSM_WIDTH = D_MODEL // 2
SSM_GROUP = 16
SSM_GROUPS = SSM_WIDTH // SSM_GROUP
SSM_STATE = 64
N_EXPERTS = 32
TOP_K = 4
D_FF = D_MODEL
SWIGLU_ALPHA = 1.702
SWIGLU_LIMIT = 7.0
RMS_EPS = 1e-6
N_MOD = 6

SSM_CHUNK = 8
SSM_CHUNKS_PER_STEP = 16
LANES = 128
MXU_DIM = 256
TOKEN_TILE = 512
SC_CORES = 2
SC_SUBCORES = 16
SC_GATHER_ROWS = 64
EXPERT_ROWS = 512
MASK_NEG = -1e30
LOG2E = math.log2(math.e)
SUM_ROWS = 16
VMEM_LIMIT = 56 * 1024 * 1024

_NT = (((1,), (1,)), ((), ()))


def _cparams(*sem):
    return pltpu.CompilerParams(dimension_semantics=sem, vmem_limit_bytes=VMEM_LIMIT)


def _pack_bf16_pairs(x):
    n = x.shape[1] // 2
    bits = lax.bitcast_convert_type(x.astype(BF16).astype(F32), jnp.uint32)
    return lax.shift_right_logical(bits[:, :n], jnp.uint32(16)) | (bits[:, n:] & jnp.uint32(0xFFFF0000))


def _unpack_bf16_pairs(w):
    lo = lax.bitcast_convert_type(lax.shift_left(w, jnp.uint32(16)), F32)
    hi = lax.bitcast_convert_type(w & jnp.uint32(0xFFFF0000), F32)
    return lo, hi


def _rms(x, g):
    return x * lax.rsqrt(jnp.mean(x * x, axis=-1, keepdims=True) + RMS_EPS) * g


def _full(shape, **kw):
    n = len(shape)
    return pl.BlockSpec(shape, lambda *_: (0,) * n, **kw)


def _ada_kernel(c_ref, w_ref, b_ref, o_ref):
    c = c_ref[...]
    cond = c * jax.nn.sigmoid(c)
    o_ref[0] = jnp.dot(cond, w_ref[0], preferred_element_type=F32,
                       precision=lax.Precision.HIGHEST) + b_ref[0]


def _ada_mod(c, w_ada, b_ada):
    depth, d, nd = w_ada.shape
    bsz = c.shape[0]
    return pl.pallas_call(
        _ada_kernel,
        grid=(depth, nd // d),
        in_specs=[
            pl.BlockSpec((bsz, d), lambda l, j: (0, 0)),
            pl.BlockSpec((1, d, d), lambda l, j: (l, 0, j)),
            pl.BlockSpec((1, 1, d), lambda l, j: (l, 0, j)),
        ],
        out_specs=pl.BlockSpec((1, bsz, d), lambda l, j: (l, 0, j)),
        out_shape=jax.ShapeDtypeStruct((depth, bsz, nd), F32),
        compiler_params=_cparams("arbitrary", "arbitrary"),
        name="ada_mod",
    )(c, w_ada, b_ada.reshape(depth, 1, nd))


def _premix_kernel(x_ref, g_ref, sc_ref, sh_ref, wq_ref, wk_ref, wvt_ref, wu_ref, wga_ref, wgs_ref,
                   q_ref, k_ref, vt_ref, u_ref, ga_ref, gs_ref):
    x = x_ref[...]
    h = _rms(x, g_ref[...]) * (1.0 + sc_ref[0]) + sh_ref[0]
    hb = h.astype(BF16)
    nblk = q_ref.shape[0]
    q = (jnp.dot(hb, wq_ref[...], preferred_element_type=F32) * (HEAD_DIM ** -0.5 * LOG2E)).astype(BF16)
    k = jnp.dot(hb, wk_ref[...], preferred_element_type=F32).astype(BF16)
    vt = lax.dot_general(wvt_ref[...], hb, _NT, preferred_element_type=F32).astype(BF16)
    for r in range(nblk):
        q_ref[r] = q[r * MOBA_BLOCK:(r + 1) * MOBA_BLOCK]
        k_ref[r] = k[r * MOBA_BLOCK:(r + 1) * MOBA_BLOCK]
        vt_ref[r] = vt[:, r * MOBA_BLOCK:(r + 1) * MOBA_BLOCK]
    u_ref[...] = jnp.dot(hb, wu_ref[...], preferred_element_type=F32).astype(BF16)
    ga = jnp.dot(hb, wga_ref[...], preferred_element_type=F32)
    ga_ref[...] = jax.nn.sigmoid(ga).astype(BF16)
    gs = jnp.dot(hb, wgs_ref[...], preferred_element_type=F32)
    gs_ref[...] = jax.nn.sigmoid(gs).astype(BF16)


def _premix(x2, g, sc, sh, w_in, seq):
    t, d = x2.shape
    tm = TOKEN_TILE
    tiles_per_seq = seq // tm
    nblk = tm // MOBA_BLOCK
    a = ATT_WIDTH
    wb = w_in.astype(BF16)
    wq, wk, wv = wb[:, :a], wb[:, a:2 * a], wb[:, 2 * a:3 * a]
    wu = wb[:, 3 * a:3 * a + SSM_WIDTH]
    wga = wb[:, 3 * a + SSM_WIDTH:3 * a + SSM_WIDTH + d]
    wgs = wb[:, 3 * a + SSM_WIDTH + d:]
    mod_spec = pl.BlockSpec((1, 1, d), lambda i: (i // tiles_per_seq, 0, 0))
    blk3 = lambda rows, cols: pl.BlockSpec((nblk, rows, cols), lambda i: (i, 0, 0))
    row = lambda cols: pl.BlockSpec((tm, cols), lambda i: (i, 0))
    time_major = pl.BlockSpec((tm, SSM_WIDTH), lambda i: (i % tiles_per_seq, i // tiles_per_seq))
    nb_total = t // MOBA_BLOCK
    return pl.pallas_call(
        _premix_kernel,
        grid=(t // tm,),
        in_specs=[row(d), _full((1, d)), mod_spec, mod_spec,
                  _full(wq.shape), _full(wk.shape), _full((a, d)), _full(wu.shape),
                  _full(wga.shape), _full(wgs.shape)],
        out_specs=[blk3(MOBA_BLOCK, a), blk3(MOBA_BLOCK, a), blk3(a, MOBA_BLOCK),
                   time_major, row(d), row(d)],
        out_shape=[jax.ShapeDtypeStruct((nb_total, MOBA_BLOCK, a), BF16),
                   jax.ShapeDtypeStruct((nb_total, MOBA_BLOCK, a), BF16),
                   jax.ShapeDtypeStruct((nb_total, a, MOBA_BLOCK), BF16),
                   jax.ShapeDtypeStruct((seq, (t // seq) * SSM_WIDTH), BF16),
                   jax.ShapeDtypeStruct((t, d), BF16),
                   jax.ShapeDtypeStruct((t, d), BF16)],
        compiler_params=_cparams("arbitrary"),
        name="premix_inproj",
    )(x2, g.reshape(1, d), sc, sh, wq, wk, wv.T, wu, wga, wgs)


def _moba_kernel(far_ref, q_ref, k_ref, vt_ref, bias_ref, o_ref, kmean_ref, neg_ref, *state_refs):
    m_refs = state_refs[0::2]
    acc_refs = state_refs[1::2]
    nb = k_ref.shape[0]
    blk = MOBA_BLOCK
    i = pl.program_id(1)

    @pl.when(i == 0)
    def _():
        for n in range(nb):
            kmean_ref[n:n + 1, :] = jnp.mean(k_ref[n].astype(F32), axis=0, keepdims=True)

    lane = lax.broadcasted_iota(jnp.int32, (blk, 2 * HEAD_DIM), 1)
    nidx = lax.broadcasted_iota(jnp.int32, (nb, blk), 0)
    valid = nidx < i
    pair_w = 2 * HEAD_DIM

    def pair_cols(head):
        return slice((head // 2) * pair_w, (head // 2 + 1) * pair_w)

    def head_rows(head):
        return slice(head * HEAD_DIM, (head + 1) * HEAD_DIM)

    def q_head(head):
        q2 = q_ref[0, :, pair_cols(head)]
        lo = (head % 2) * HEAD_DIM
        return jnp.where((lane >= lo) & (lane < lo + HEAD_DIM), q2, jnp.zeros_like(q2))

    for head in range(ATT_HEADS):
        km = kmean_ref[:, pair_cols(head)]
        km_hi = km.astype(BF16)
        km_lo = (km - km_hi.astype(F32)).astype(BF16)
        qh = q_head(head)
        gate = (lax.dot_general(km_hi, qh, _NT, preferred_element_type=F32)
                + lax.dot_general(km_lo, qh, _NT, preferred_element_type=F32))
        g = jnp.where(valid, gate, -jnp.inf)
        sel = jnp.zeros((nb, blk), jnp.bool_)
        for _ in range(MOBA_TOPK):
            mx = jnp.max(g, axis=0, keepdims=True)
            first = jnp.min(jnp.where(g == mx, nidx, nb), axis=0, keepdims=True)
            pick = nidx == first
            sel = sel | pick
            g = jnp.where(pick, -jnp.inf, g)
        neg_ref[head * nb:(head + 1) * nb, :] = jnp.where(sel & valid, 0.0, MASK_NEG)

    def scores(head, js, adds):
        qh = q_head(head)
        parts = [lax.dot_general(k_ref[j, :, pair_cols(head)], qh, _NT, preferred_element_type=F32) + a
                 for j, a in zip(js, adds)]
        return parts[0] if len(parts) == 1 else jnp.concatenate(parts, axis=0)

    def weighted_values(head, js, p):
        vals = [vt_ref[j, head_rows(head), :] for j in js]
        vb = vals[0] if len(vals) == 1 else jnp.concatenate(vals, axis=1)
        vb = jnp.concatenate([vb, jnp.ones((SUM_ROWS, vb.shape[1]), BF16)], axis=0)
        return jnp.dot(vb, p.astype(BF16), preferred_element_type=F32)

    def start(head, js, st):
        m = jnp.max(st, axis=0, keepdims=True)
        m_refs[head][...] = m
        acc_refs[head][...] = weighted_values(head, js, jnp.exp2(st - m))

    def update(head, js, st):
        m = m_refs[head][...]
        m_new = jnp.maximum(m, jnp.max(st, axis=0, keepdims=True))
        alpha = jnp.exp2(m - m_new)
        m_refs[head][...] = m_new
        acc_refs[head][...] = alpha * acc_refs[head][...] + weighted_values(head, js, jnp.exp2(st - m_new))

    def sweep(js, adds_of, absorb):
        st = scores(0, js, adds_of(0))
        for head in range(ATT_HEADS):
            nxt = scores(head + 1, js, adds_of(head + 1)) if head + 1 < ATT_HEADS else None
            absorb(head, js, st)
            st = nxt

    def far_add(head, j):
        return neg_ref[pl.ds(head * nb + j, 1), :] + far_ref[head]

    @pl.when(i == 0)
    def _():
        sweep([i], lambda h: [bias_ref[h, 0]], start)

    @pl.when(i > 0)
    def _():
        sweep([i, i - 1],
              lambda h: [bias_ref[h, 0], bias_ref[h, 1] + neg_ref[pl.ds(h * nb + i - 1, 1), :]], start)

    n_far = jnp.maximum(i - 1, 0)

    def far_pair(jj, carry):
        sweep([2 * jj, 2 * jj + 1], lambda h: [far_add(h, 2 * jj), far_add(h, 2 * jj + 1)], update)
        return carry

    lax.fori_loop(0, n_far // 2, far_pair, 0)

    @pl.when(n_far % 2 == 1)
    def _():
        sweep([n_far - 1], lambda h: [far_add(h, n_far - 1)], update)

    for hp in range(ATT_HEADS // 2):
        outs = [acc_refs[h][0:HEAD_DIM, :] / acc_refs[h][HEAD_DIM:HEAD_DIM + 1, :] for h in (2 * hp, 2 * hp + 1)]
        pair = jnp.concatenate(outs, axis=0)
        o_ref[0, :, hp * pair_w:(hp + 1) * pair_w] = pair.T.astype(BF16)


def _rel_bucket(dist):
    n = jnp.maximum(dist, 0)
    max_exact = REL_BUCKETS // 2
    nf = jnp.maximum(n, 1).astype(F32)
    large = max_exact + (jnp.log(nf / max_exact) / math.log(REL_MAX_DIST / max_exact)
                         * (REL_BUCKETS - max_exact)).astype(jnp.int32)
    large = jnp.minimum(large, REL_BUCKETS - 1)
    return jnp.where(n < max_exact, n, large)


def _moba_bias_tiles(rel_bias):
    blk = MOBA_BLOCK
    span = 2 * blk
    rel_bias = rel_bias * LOG2E
    vec = rel_bias[_rel_bucket(jnp.arange(span))].T.astype(F32)
    masked = jnp.full_like(vec[:, :blk], MASK_NEG)
    ring_own = jnp.concatenate([vec[:, :blk], masked], axis=1)
    ring_prev = jnp.concatenate([vec[:, blk:], vec[:, :blk]], axis=1)

    def toeplitz(ring):
        flat = jnp.tile(ring, (1, blk))[:, :blk * (span - 1)]
        return flat.reshape(-1, blk, span - 1)[:, :, :blk]

    tiles = jnp.stack([toeplitz(ring_own), toeplitz(ring_prev)], axis=1)
    far = rel_bias[_rel_bucket(jnp.array(span))]
    return tiles, far.astype(F32)


def _moba(q4, k4, vt4, bias_tiles, far_bias, bsz):
    nb_total, blk, a = q4.shape
    nb = nb_total // bsz
    return pl.pallas_call(
        _moba_kernel,
        grid=(bsz, nb),
        in_specs=[
            pl.BlockSpec(memory_space=pltpu.SMEM),
            pl.BlockSpec((1, blk, a), lambda b, i: (b * nb + i, 0, 0)),
            pl.BlockSpec((nb, blk, a), lambda b, i: (b, 0, 0)),
            pl.BlockSpec((nb, a, blk), lambda b, i: (b, 0, 0)),
            _full(bias_tiles.shape),
        ],
        out_specs=pl.BlockSpec((1, blk, a), lambda b, i: (b * nb + i, 0, 0)),
        out_shape=jax.ShapeDtypeStruct((nb_total, blk, a), BF16),
        scratch_shapes=[pltpu.VMEM((nb, a), F32), pltpu.VMEM((ATT_HEADS * nb, blk), F32)]
        + [pltpu.VMEM((1, blk), F32), pltpu.VMEM((HEAD_DIM + SUM_ROWS, blk), F32)] * ATT_HEADS,
        compiler_params=_cparams("arbitrary", "arbitrary"),
        name="moba_attention",
    )(far_bias, q4, k4, vt4, bias_tiles)


def _ssm_weights(a_re, a_im, log_dt, b_re, b_im, c_re, c_im, d_skip):
    g, p, cw, n = SSM_GROUPS, SSM_STATE, SSM_GROUP, SSM_CHUNK
    gs = LANES // cw
    nq = g // gs
    npair = n // 2
    pairs_per_set = gs // 2
    lam_re = jnp.minimum(a_re.astype(F32), -1e-4)
    lam_im = a_im.astype(F32)
    dt = jnp.exp(log_dt.astype(F32))[:, None]
    mag = jnp.exp(lam_re * dt)
    lb_re = mag * jnp.cos(lam_im * dt)
    lb_im = mag * jnp.sin(lam_im * dt)
    n_re = lb_re - 1.0
    n_im = lb_im
    den = lam_re * lam_re + lam_im * lam_im
    z_re = ((n_re * lam_re + n_im * lam_im) / den)[..., None]
    z_im = ((n_im * lam_re - n_re * lam_im) / den)[..., None]
    br, bi = b_re.astype(F32), b_im.astype(F32)
    bb_re = z_re * br - z_im * bi
    bb_im = z_re * bi + z_im * br
    pw_re, pw_im = [jnp.ones_like(lb_re)], [jnp.zeros_like(lb_im)]
    for _ in range(n):
        r, im = pw_re[-1], pw_im[-1]
        pw_re.append(r * lb_re - im * lb_im)
        pw_im.append(r * lb_im + im * lb_re)
    pw_re, pw_im = jnp.stack(pw_re), jnp.stack(pw_im)
    cr, ci = c_re.astype(F32), c_im.astype(F32)
    hi = lax.Precision.HIGHEST
    rev_re, rev_im = pw_re[n - 1::-1], pw_im[n - 1::-1]
    sb_re = jnp.einsum('sgp,gpc->sgcp', rev_re, bb_re) - jnp.einsum('sgp,gpc->sgcp', rev_im, bb_im)
    sb_im = jnp.einsum('sgp,gpc->sgcp', rev_re, bb_im) + jnp.einsum('sgp,gpc->sgcp', rev_im, bb_re)
    cl_re = jnp.einsum('gcp,tgp->tgpc', cr, pw_re[1:]) - jnp.einsum('gcp,tgp->tgpc', ci, pw_im[1:])
    cl_im = jnp.einsum('gcp,tgp->tgpc', cr, pw_im[1:]) + jnp.einsum('gcp,tgp->tgpc', ci, pw_re[1:])
    cp_re = jnp.einsum('gcp,tgp->tgcp', cr, pw_re[:n]) - jnp.einsum('gcp,tgp->tgcp', ci, pw_im[:n])
    cp_im = jnp.einsum('gcp,tgp->tgcp', cr, pw_im[:n]) + jnp.einsum('gcp,tgp->tgcp', ci, pw_re[:n])
    kern = (jnp.einsum('tgcp,gpd->gtcd', cp_re, bb_re, precision=hi)
            - jnp.einsum('tgcp,gpd->gtcd', cp_im, bb_im, precision=hi))

    kp = jnp.concatenate([jnp.zeros_like(kern[:, :1]), kern], axis=1)
    dl = jnp.arange(npair)[:, None, None]
    s0 = jnp.arange(2)[None, :, None]
    t0 = jnp.arange(2)[None, None, :]
    lag = 2 * dl + t0 - s0
    kl = kp[:, lag + 1].reshape(nq, gs, npair, 2, 2, cw, cw)
    base = kl.transpose(0, 2, 3, 1, 6, 4, 5)
    same_group = jnp.eye(gs, dtype=F32)
    w_toe = jnp.concatenate(
        [base[..., t0, :] * same_group[:, g1][None, None, None, :, None, None]
         for t0 in range(2) for g1 in range(gs)], axis=-1)
    w_toe = w_toe.reshape(nq, npair, MXU_DIM, MXU_DIM).astype(BF16)

    member = (jnp.arange(gs)[None, :, None]
              == 2 * jnp.arange(pairs_per_set)[:, None, None] + jnp.arange(2)[None, None, :]).astype(F32)

    def pair_tiles(x):
        cols = [x[ri][:, None] * member[:, :, gl][None, :, None, None, :, None, None]
                for ri in range(2) for gl in range(2)]
        return jnp.concatenate(cols, axis=-1).reshape(g // 2, npair, MXU_DIM, MXU_DIM)

    sbs = jnp.stack([sb_re, sb_im]).reshape(2, npair, 2, nq, gs, cw, p)
    w_in_state = pair_tiles(sbs.transpose(0, 3, 1, 2, 4, 5, 6)).astype(BF16)
    sos = jnp.stack([cl_re, -cl_im]).reshape(2, npair, 2, nq, gs, p, cw)
    w_state_out = jnp.swapaxes(pair_tiles(sos.transpose(0, 3, 1, 2, 4, 6, 5)), -1, -2).astype(BF16)
    decay = jnp.stack([pw_re[n].reshape(g // 2, 2 * p), pw_im[n].reshape(g // 2, 2 * p)])
    dvec = d_skip.astype(F32).reshape(1, g * cw)
    return w_toe, w_in_state, w_state_out, decay, dvec


def _ssm_kernel(u_ref, wt_ref, wi_ref, wo_ref, dec_ref, d_ref, y_ref, s_ref, xp_ref, x_ref, *, bsz):
    kt = u_ref.shape[0]
    n = u_ref.shape[1] // bsz
    rows = kt * bsz
    npair = n // 2
    nq = wt_ref.shape[0]
    npairs_g = wi_ref.shape[0]
    per_set = npairs_g // nq
    half = MXU_DIM // 2

    @pl.when(pl.program_id(0) == 0)
    def _():
        x_ref[...] = jnp.zeros_like(x_ref)

    def piece(s, q):
        return u_ref[:, s * bsz:(s + 1) * bsz, q * LANES:(q + 1) * LANES].reshape(rows, LANES)

    lhs = {(sp, q): jnp.concatenate([piece(2 * sp, q), piece(2 * sp + 1, q)], axis=-1)
           for sp in range(npair) for q in range(nq)}

    for gp in range(npairs_g):
        q = gp // per_set
        acc = jnp.dot(lhs[0, q], wi_ref[gp, 0], preferred_element_type=F32)
        for sp in range(1, npair):
            acc = acc + jnp.dot(lhs[sp, q], wi_ref[gp, sp], preferred_element_type=F32)
        s_ref[:, gp * MXU_DIM:(gp + 1) * MXU_DIM] = acc

    for gp in range(npairs_g):
        re_cols = slice(gp * MXU_DIM, gp * MXU_DIM + half)
        im_cols = slice(gp * MXU_DIM + half, (gp + 1) * MXU_DIM)
        ar = dec_ref[0, gp:gp + 1, :]
        ai = dec_ref[1, gp:gp + 1, :]
        xr = x_ref[:, re_cols]
        xi = x_ref[:, im_cols]
        for kc in range(kt):
            rs = slice(kc * bsz, (kc + 1) * bsz)
            xp_ref[rs, re_cols] = xr.astype(BF16)
            xp_ref[rs, im_cols] = xi.astype(BF16)
            sr = s_ref[rs, re_cols]
            si = s_ref[rs, im_cols]
            xr, xi = ar * xr - ai * xi + sr, ar * xi + ai * xr + si
        x_ref[:, re_cols] = xr
        x_ref[:, im_cols] = xi

    for q in range(nq):
        for tp in range(npair):
            acc = jnp.dot(lhs[0, q], wt_ref[q, tp], preferred_element_type=F32)
            for sp in range(1, tp + 1):
                acc = acc + jnp.dot(lhs[sp, q], wt_ref[q, tp - sp], preferred_element_type=F32)
            for gp in range(q * per_set, (q + 1) * per_set):
                acc = acc + jnp.dot(xp_ref[:, gp * MXU_DIM:(gp + 1) * MXU_DIM], wo_ref[gp, tp],
                                    preferred_element_type=F32)
            for t0 in range(2):
                s = 2 * tp + t0
                y = acc[:, t0 * half:(t0 + 1) * half] + d_ref[:, q * LANES:(q + 1) * LANES] * piece(s, q).astype(F32)
                y_ref[:, s * bsz:(s + 1) * bsz, q * LANES:(q + 1) * LANES] = (
                    jax.nn.gelu(y).astype(BF16).reshape(kt, bsz, LANES))


def _ssm(u_tm, weights, bsz, seq):
    w_toe, w_in_state, w_state_out, decay, dvec = weights
    n = SSM_CHUNK
    kt = SSM_CHUNKS_PER_STEP
    nchunk = seq // n
    rows = kt * bsz
    state_w = SSM_GROUPS * 2 * SSM_STATE
    u3 = u_tm.reshape(nchunk, n * bsz, SSM_WIDTH)
    once = pl.Buffered(1)
    blk = pl.BlockSpec((kt, n * bsz, SSM_WIDTH), lambda i: (i, 0, 0))
    y3 = pl.pallas_call(
        functools.partial(_ssm_kernel, bsz=bsz),
        grid=(nchunk // kt,),
        in_specs=[blk, _full(w_toe.shape, pipeline_mode=once), _full(w_in_state.shape, pipeline_mode=once),
                  _full(w_state_out.shape, pipeline_mode=once), _full(decay.shape), _full(dvec.shape)],
        out_specs=blk,
        out_shape=jax.ShapeDtypeStruct(u3.shape, BF16),
        scratch_shapes=[pltpu.VMEM((rows, state_w), F32), pltpu.VMEM((rows, state_w), BF16),
                        pltpu.VMEM((bsz, state_w), F32)],
        compiler_params=_cparams("arbitrary"),
        name="s5_chunked_scan",
    )(u3, w_toe, w_in_state, w_state_out, decay, dvec)
    return y3.reshape(seq, bsz * SSM_WIDTH)


def _postmix_kernel(att_ref, ys_ref, ga_ref, gs_ref, x_ref, gt_ref, sc_ref, sh_ref, gpost_ref, gpre_ref,
                    wglu_ref, wso_ref, wao_ref, wo_ref, wrh_ref, wrl_ref, br_ref,
                    x1_ref, h2_ref, idx_ref, gate_ref, rank_ref, cnt_ref, run_ref):
    tm = x_ref.shape[0]
    ne = wrh_ref.shape[0]

    @pl.when(pl.program_id(0) == 0)
    def _():
        run_ref[...] = jnp.zeros_like(run_ref)

    halves = [slice(r * (tm // 2), (r + 1) * (tm // 2)) for r in range(2)]
    dot = functools.partial(jnp.dot, preferred_element_type=F32)
    att = [dot(att_ref[r, :], wao_ref[...]) for r in halves]
    glu = [dot(ys_ref[r, :], wglu_ref[...]) for r in halves]
    ssm = []
    for g in glu:
        sg = (g[:, :SSM_WIDTH] * jax.nn.sigmoid(g[:, SSM_WIDTH:])).astype(BF16)
        ssm.append(dot(sg, wso_ref[...]))
    y = []
    for r, a, s in zip(halves, att, ssm):
        merged = (ga_ref[r, :].astype(F32) * a + gs_ref[r, :].astype(F32) * s).astype(BF16)
        y.append(dot(merged, wo_ref[...]))
    logits = []
    for r, yh in zip(halves, y):
        x1 = x_ref[r, :] + gt_ref[0] * _rms(yh, gpost_ref[...])
        x1_ref[r, :] = x1
        h2 = _rms(x1, gpre_ref[...]) * (1.0 + sc_ref[0]) + sh_ref[0]
        h2_ref[r, :] = _pack_bf16_pairs(h2)
        h_hi = h2.astype(BF16)
        h_lo = (h2 - h_hi.astype(F32)).astype(BF16)
        logits.append(lax.dot_general(wrh_ref[...], h_hi, _NT, preferred_element_type=F32)
                      + lax.dot_general(wrh_ref[...], h_lo, _NT, preferred_element_type=F32)
                      + lax.dot_general(wrl_ref[...], h_hi, _NT, preferred_element_type=F32))
    logits = jnp.concatenate(logits, axis=1) + br_ref[...]
    eidx = lax.broadcasted_iota(jnp.int32, (ne, tm), 0)
    lg = logits
    vals, idxs = [], []
    for _ in range(TOP_K):
        mx = jnp.max(lg, axis=0, keepdims=True)
        first = jnp.min(jnp.where(lg == mx, eidx, ne), axis=0, keepdims=True)
        vals.append(mx)
        idxs.append(first)
        lg = jnp.where(eidx == first, -jnp.inf, lg)
    exps = [jnp.exp(v - vals[0]) for v in vals]
    denom = exps[0] + exps[1] + exps[2] + exps[3]
    gate_ref[...] = jnp.concatenate([e / denom for e in exps], axis=0)
    idx_ref[...] = jnp.concatenate(idxs, axis=0)
    onehot = jnp.where(lg == -jnp.inf, 1.0, 0.0)
    t_row = lax.broadcasted_iota(jnp.int32, (tm, tm), 0)
    t_col = lax.broadcasted_iota(jnp.int32, (tm, tm), 1)
    earlier = jnp.where(t_row < t_col, 1.0, 0.0).astype(BF16)
    before = jnp.dot(onehot.astype(BF16), earlier, preferred_element_type=F32) + run_ref[:, 0:1]
    ranks = [jnp.sum(jnp.where(eidx == ix, before, 0.0), axis=0, keepdims=True) for ix in idxs]
    rank_ref[...] = jnp.concatenate(ranks, axis=0).astype(jnp.int32)
    run_ref[...] = run_ref[...] + jnp.sum(onehot, axis=1, keepdims=True)
    cnt_ref[...] = run_ref[...].astype(jnp.int32)


def _postmix(att2, ys_tm, ga, gs, x2, gt1, sc2, sh2, g_post, g_pre, w_glu, w_ssm_out, w_att_out, w_out,
             w_router, b_router, seq):
    t, d = x2.shape
    tm = TOKEN_TILE
    tiles_per_seq = seq // tm
    ne = w_router.shape[1]
    wr_t = w_router.T.astype(F32)
    wr_hi = wr_t.astype(BF16)
    wr_lo = (wr_t - wr_hi.astype(F32)).astype(BF16)
    mod_spec = pl.BlockSpec((1, 1, d), lambda i: (i // tiles_per_seq, 0, 0))
    row = lambda cols: pl.BlockSpec((tm, cols), lambda i: (i, 0))
    time_major = pl.BlockSpec((tm, SSM_WIDTH), lambda i: (i % tiles_per_seq, i // tiles_per_seq))
    col = pl.BlockSpec((TOP_K, tm), lambda i: (0, i))
    weights = [w_glu.astype(BF16), w_ssm_out.astype(BF16), w_att_out.astype(BF16), w_out.astype(BF16),
               wr_hi, wr_lo, b_router.astype(F32).reshape(ne, 1)]
    return pl.pallas_call(
        _postmix_kernel,
        grid=(t // tm,),
        in_specs=[row(ATT_WIDTH), time_major, row(d), row(d), row(d), mod_spec, mod_spec, mod_spec,
                  _full((1, d)), _full((1, d))] + [_full(w.shape) for w in weights],
        out_specs=[row(d), row(d // 2), col, col, col, _full((ne, 128))],
        out_shape=[jax.ShapeDtypeStruct((t, d), F32), jax.ShapeDtypeStruct((t, d // 2), jnp.uint32),
                   jax.ShapeDtypeStruct((TOP_K, t), jnp.int32), jax.ShapeDtypeStruct((TOP_K, t), F32),
                   jax.ShapeDtypeStruct((TOP_K, t), jnp.int32), jax.ShapeDtypeStruct((ne, 128), jnp.int32)],
        scratch_shapes=[pltpu.VMEM((ne, 128), F32)],
        compiler_params=_cparams("arbitrary"),
        name="postmix_router",
    )(att2, ys_tm, ga, gs, x2, gt1, sc2, sh2, g_post.reshape(1, d), g_pre.reshape(1, d), *weights)


def _sc_worker_base(per_worker):
    return (lax.axis_index("s") * SC_CORES + lax.axis_index("c")) * per_worker


def _sc_mesh():
    return plsc.VectorSubcoreMesh(core_axis_name="c", subcore_axis_name="s",
                                  num_cores=SC_CORES, num_subcores=SC_SUBCORES)


def _sc_scatter_kernel(rows_hbm, idx_hbm, out_hbm, idx_v, rows_v, load_sems, store_sems, *, per_worker):
    chunk = rows_v.shape[1]
    n_tokens = rows_hbm.shape[0]
    n_chunks = per_worker // chunk
    base = _sc_worker_base(per_worker)

    def load(c, b):
        off = pl.multiple_of(base + c * chunk, chunk)
        for k in range(TOP_K):
            pltpu.sync_copy(idx_hbm.at[pl.ds(k * n_tokens + off, chunk)], idx_v.at[b, k])
        return pltpu.make_async_copy(rows_hbm.at[pl.ds(off, chunk)], rows_v.at[b], load_sems.at[b])

    def stores(b):
        return [pltpu.make_async_copy(rows_v.at[b], out_hbm.at[idx_v.at[b, k]], store_sems.at[b])
                for k in range(TOP_K)]

    load(0, 0).start()

    @pl.loop(0, n_chunks, step=2)
    def _(c0):
        for b in range(2):
            c = c0 + b

            @pl.when(c > 0)
            def _():
                for cp in stores(1 - b):
                    cp.wait()

            @pl.when(c + 1 < n_chunks)
            def _():
                load(c + 1, 1 - b).start()

            pltpu.make_async_copy(rows_hbm.at[pl.ds(0, chunk)], rows_v.at[b], load_sems.at[b]).wait()
            for cp in stores(b):
                cp.start()

    for cp in stores((n_chunks - 1) % 2):
        cp.wait()


def _sc_scatter_rows(rows, idx, n_out):
    t, d = rows.shape
    workers = SC_CORES * SC_SUBCORES
    per_worker = t // workers
    assert t % workers == 0 and per_worker % (2 * SC_GATHER_ROWS) == 0
    return pl.kernel(
        functools.partial(_sc_scatter_kernel, per_worker=per_worker),
        out_type=jax.ShapeDtypeStruct((n_out, d), rows.dtype),
        mesh=_sc_mesh(),
        scratch_types=[pltpu.VMEM((2, TOP_K, SC_GATHER_ROWS), jnp.int32),
                       pltpu.VMEM((2, SC_GATHER_ROWS, d), rows.dtype),
                       pltpu.SemaphoreType.DMA((2,)), pltpu.SemaphoreType.DMA((2,))],
        name="sc_row_scatter",
    )(rows, idx)


def _sc_gather_kernel(table_hbm, idx_hbm, out_hbm, idx_v, rows_v, sems, *, per_worker):
    chunk = rows_v.shape[1]
    n_chunks = per_worker // chunk
    base = _sc_worker_base(per_worker)

    def gather(c, b):
        off = pl.multiple_of(base + c * chunk, chunk)
        pltpu.sync_copy(idx_hbm.at[pl.ds(off, chunk)], idx_v.at[b])
        return pltpu.make_async_copy(table_hbm.at[idx_v.at[b]], rows_v.at[b], sems.at[b])

    gather(0, 0).start()

    @pl.loop(0, n_chunks, step=2)
    def _(c0):
        for b in range(2):
            c = c0 + b

            @pl.when(c + 1 < n_chunks)
            def _():
                gather(c + 1, 1 - b).start()

            pltpu.make_async_copy(table_hbm.at[idx_v.at[b]], rows_v.at[b], sems.at[b]).wait()
            off = pl.multiple_of(base + c * chunk, chunk)
            pltpu.sync_copy(rows_v.at[b], out_hbm.at[pl.ds(off, chunk)])


def _sc_gather_rows(table, idx):
    n = idx.shape[0]
    d = table.shape[1]
    workers = SC_CORES * SC_SUBCORES
    per_worker = n // workers
    assert n % workers == 0 and per_worker % (2 * SC_GATHER_ROWS) == 0
    return pl.kernel(
        functools.partial(_sc_gather_kernel, per_worker=per_worker),
        out_type=jax.ShapeDtypeStruct((n, d), table.dtype),
        mesh=_sc_mesh(),
        scratch_types=[pltpu.VMEM((2, SC_GATHER_ROWS), jnp.int32),
                       pltpu.VMEM((2, SC_GATHER_ROWS, d), table.dtype),
                       pltpu.SemaphoreType.DMA((2,))],
        name="sc_row_gather",
    )(table, idx)


def _experts_kernel(be_ref, nused_ref, valid_ref, next_ref, x_ref, b1_ref, b2_ref, w1_hbm, w2_hbm, y_ref,
                    w1s_ref, w2s_ref, w1b_ref, w2b_ref, sems, *, layer):
    i = pl.program_id(0)
    prev = be_ref[jnp.maximum(i - 1, 0)]
    fresh = (i < nused_ref[0]) & ((i == 0) | (be_ref[i] != prev))

    def fetch(e):
        return (pltpu.make_async_copy(w1_hbm.at[layer, e], w1s_ref, sems.at[0]),
                pltpu.make_async_copy(w2_hbm.at[layer, e], w2s_ref, sems.at[1]))

    @pl.when(i == 0)
    def _():
        for cp in fetch(be_ref[0]):
            cp.start()

    @pl.when(fresh)
    def _():
        for cp in fetch(be_ref[i]):
            cp.wait()
        w1b_ref[...] = w1s_ref[...].astype(BF16)
        w2b_ref[...] = w2s_ref[...].astype(BF16)

        @pl.when(next_ref[i] >= 0)
        def _():
            for cp in fetch(next_ref[i]):
                cp.start()

    @pl.when(i < nused_ref[0])
    def _():
        row = lax.broadcasted_iota(jnp.int32, x_ref.shape, 0)
        x_lo, x_hi = _unpack_bf16_pairs(jnp.where(row < valid_ref[i], x_ref[...], jnp.uint32(0)))
        x = jnp.concatenate([x_lo.astype(BF16), x_hi.astype(BF16)], axis=1)
        gu = jnp.dot(x, w1b_ref[...], preferred_element_type=F32) + b1_ref[...]
        g = jnp.minimum(gu[:, :D_FF], SWIGLU_LIMIT)
        up = jnp.clip(gu[:, D_FF:], -SWIGLU_LIMIT, SWIGLU_LIMIT)
        act = ((up + 1.0) * g * jax.nn.sigmoid(SWIGLU_ALPHA * g)).astype(BF16)
        y = jnp.dot(act, w2b_ref[...], preferred_element_type=F32) + b2_ref[...]
        y_ref[...] = _pack_bf16_pairs(y)

    @pl.when(i >= nused_ref[0])
    def _():
        y_ref[...] = jnp.zeros_like(y_ref)


def _experts(xb, block_e, n_used, valid, next_e, w1, b1, w2, b2, layer):
    p_rows, packed_w = xb.shape
    d = 2 * packed_w
    depth, ne = w1.shape[:2]
    rb = EXPERT_ROWS
    bmap = lambda i, be, nu, va, nx: (layer, be[i], 0, 0)
    rows = pl.BlockSpec((rb, packed_w), lambda i, be, nu, va, nx: (i, 0))
    grid_spec = pltpu.PrefetchScalarGridSpec(
        num_scalar_prefetch=4,
        grid=(p_rows // rb,),
        in_specs=[
            rows,
            pl.BlockSpec((None, None, 1, 2 * D_FF), bmap),
            pl.BlockSpec((None, None, 1, d), bmap),
            pl.BlockSpec(memory_space=pl.ANY),
            pl.BlockSpec(memory_space=pl.ANY),
        ],
        out_specs=rows,
        scratch_shapes=[pltpu.VMEM((d, 2 * D_FF), w1.dtype), pltpu.VMEM((D_FF, d), w2.dtype),
                        pltpu.VMEM((d, 2 * D_FF), BF16), pltpu.VMEM((D_FF, d), BF16),
                        pltpu.SemaphoreType.DMA((2,))],
    )
    return pl.pallas_call(
        functools.partial(_experts_kernel, layer=layer),
        grid_spec=grid_spec,
        out_shape=jax.ShapeDtypeStruct((p_rows, packed_w), jnp.uint32),
        compiler_params=_cparams("arbitrary"),
        name="expert_ffn",
    )(block_e, n_used, valid, next_e, xb, b1.reshape(depth, ne, 1, 2 * D_FF), b2.reshape(depth, ne, 1, d), w1, w2)


def _combine_kernel(gate_ref, x_ref, gt_ref, g_ref, y0_ref, y1_ref, y2_ref, y3_ref, o_ref):
    tm = x_ref.shape[0]
    gates = gate_ref[...]
    gates = jnp.concatenate([gates, jnp.zeros((LANES - TOP_K, tm), F32)], axis=0).T
    lo, hi = _unpack_bf16_pairs(y0_ref[...])
    y_lo, y_hi = gates[:, 0:1] * lo, gates[:, 0:1] * hi
    for k, y_ref in enumerate((y1_ref, y2_ref, y3_ref), start=1):
        lo, hi = _unpack_bf16_pairs(y_ref[...])
        y_lo, y_hi = y_lo + gates[:, k:k + 1] * lo, y_hi + gates[:, k:k + 1] * hi
    y = jnp.concatenate([y_lo, y_hi], axis=1)
    o_ref[...] = x_ref[...] + gt_ref[0] * _rms(y, g_ref[...])


def _combine(y4, gate_t, x2, gt2, g_post, seq):
    t, d = x2.shape
    tm = TOKEN_TILE
    tiles = t // tm
    tiles_per_seq = seq // tm
    row = pl.BlockSpec((tm, d), lambda i: (i, 0))
    slot = lambda k: pl.BlockSpec((tm, y4.shape[1]), lambda i: (k * tiles + i, 0))
    return pl.pallas_call(
        _combine_kernel,
        grid=(tiles,),
        in_specs=[pl.BlockSpec((TOP_K, tm), lambda i: (0, i)),
                  row, pl.BlockSpec((1, 1, d), lambda i: (i // tiles_per_seq, 0, 0)), _full((1, d))]
        + [slot(k) for k in range(TOP_K)],
        out_specs=row,
        out_shape=jax.ShapeDtypeStruct((t, d), F32),
        compiler_params=_cparams("arbitrary"),
        name="expert_combine",
    )(gate_t, x2, gt2, g_post.reshape(1, d), y4, y4, y4, y4)


def _route_plan(idx_t, rank_t, counts):
    rb = EXPERT_ROWS
    k, t = idx_t.shape
    padded = (counts + rb - 1) // rb * rb
    pad_ends = jnp.cumsum(padded)
    pad_starts = pad_ends - padded
    experts = jnp.arange(N_EXPERTS, dtype=jnp.int32)
    start_of = jnp.sum(jnp.where(idx_t[None] == experts[:, None, None], pad_starts[:, None, None], 0), axis=0)
    dest = (start_of + rank_t).astype(jnp.int32)
    n_blocks = (k * t) // rb + N_EXPERTS
    blk_start = jnp.arange(n_blocks, dtype=jnp.int32) * rb
    block_e = jnp.minimum(jnp.sum(pad_ends[None, :] <= blk_start[:, None], axis=1), N_EXPERTS - 1)
    onehot_e = block_e[:, None] == experts[None, :]
    cnt_b = jnp.sum(jnp.where(onehot_e, counts[None, :], 0), axis=1)
    start_b = jnp.sum(jnp.where(onehot_e, pad_starts[None, :], 0), axis=1)
    valid = jnp.clip(cnt_b - (blk_start - start_b), 0, rb).astype(jnp.int32)
    n_used = (pad_ends[-1] // rb).astype(jnp.int32).reshape(1)
    later_nonempty = (experts[None, :] > experts[:, None]) & (counts[None, :] > 0)
    next_nonempty = jnp.min(jnp.where(later_nonempty, experts[None, :], N_EXPERTS), axis=1)
    next_nonempty = jnp.where(next_nonempty == N_EXPERTS, -1, next_nonempty)
    next_e = jnp.sum(jnp.where(onehot_e, next_nonempty[None, :], 0), axis=1).astype(jnp.int32)
    return dest.reshape(-1), block_e.astype(jnp.int32), n_used, valid, next_e, n_blocks * rb


def kernel(x, c, rel_bias, w_ada, b_ada, g_pre_mix, g_post_mix, g_pre_ffn, g_post_ffn, w_in, ssm_a_re, ssm_a_im, ssm_log_dt, ssm_b_re, ssm_b_im, ssm_c_re, ssm_c_im, ssm_d, w_glu, w_ssm_out, w_att_out, w_out, w_router, b_router, w_exp_in, b_exp_in, w_exp_out, b_exp_out):
    bsz, seq, d = x.shape
    depth = w_ada.shape[0]
    t = bsz * seq
    assert d == D_MODEL and seq % TOKEN_TILE == 0 and TOKEN_TILE % MOBA_BLOCK == 0
    assert seq % (SSM_CHUNK * SSM_CHUNKS_PER_STEP) == 0

    mod = _ada_mod(c, w_ada, b_ada)
    bias_tiles, far_bias = _moba_bias_tiles(rel_bias.astype(F32))
    x2 = x.reshape(t, d)
    for l in range(depth):
        sh1, sc1, gt1, sh2, sc2, gt2 = [m.reshape(bsz, 1, d) for m in jnp.split(mod[l], N_MOD, axis=-1)]
        q4, k4, vt4, u_tm, ga, gs = _premix(x2, g_pre_mix[l], sc1, sh1, w_in[l], seq)
        att = _moba(q4, k4, vt4, bias_tiles, far_bias, bsz).reshape(t, ATT_WIDTH)
        ssm_w = _ssm_weights(ssm_a_re[l], ssm_a_im[l], ssm_log_dt[l], ssm_b_re[l], ssm_b_im[l],
                             ssm_c_re[l], ssm_c_im[l], ssm_d[l])
        ys_tm = _ssm(u_tm, ssm_w, bsz, seq)
        x1, h2, idx_t, gate_t, rank_t, cnt = _postmix(
            att, ys_tm, ga, gs, x2, gt1, sc2, sh2, g_post_mix[l], g_pre_ffn[l],
            w_glu[l], w_ssm_out[l], w_att_out[l], w_out[l], w_router[l], b_router[l], seq)
        dest_flat, block_e, n_used, valid, next_e, p_rows = _route_plan(idx_t, rank_t, cnt[:, 0])
        xb = _sc_scatter_rows(h2, dest_flat, p_rows)
        yb = _experts(xb, block_e, n_used, valid, next_e, w_exp_in, b_exp_in, w_exp_out, b_exp_out, l)
        y4 = _sc_gather_rows(yb, dest_flat)
        x2 = _combine(y4, gate_t, x1, gt2, g_post_ffn[l], seq)
    return x2.reshape(bsz, seq, d)
```

```python
import functools
import math

import jax
import jax.numpy as jnp
from jax import lax
from jax.experimental import pallas as pl
from jax.experimental.pallas import tpu as pltpu
from jax.experimental.pallas import tpu_sc as plsc

F32 = jnp.float32
BF16 = jnp.bfloat16

D_MODEL = 1024
ATT_HEADS = 8
HEAD_DIM = 64
ATT_WIDTH = ATT_HEADS * HEAD_DIM
MOBA_BLOCK = 256
MOBA_TOPK = 3
REL_BUCKETS = 32
REL_MAX_DIST = 128
SSM_WIDTH = D_MODEL // 2
SSM_GROUP = 16
SSM_GROUPS = SSM_WIDTH // SSM_GROUP
SSM_STATE = 64
N_EXPERTS = 32
TOP_K = 4
D_FF = D_MODEL
SWIGLU_ALPHA = 1.702
SWIGLU_LIMIT = 7.0
RMS_EPS = 1e-6
N_MOD = 6

SSM_CHUNK = 8
SSM_CHUNKS_PER_STEP = 16
LANES = 128
MXU_DIM = 256
TOKEN_TILE = 512
SC_CORES = 2
SC_SUBCORES = 16
SC_GATHER_ROWS = 64
EXPERT_ROWS = 512
MASK_NEG = -1e30
LOG2E = math.log2(math.e)
SUM_ROWS = 16
VMEM_LIMIT = 56 * 1024 * 1024

_NT = (((1,), (1,)), ((), ()))


def _cparams(*sem):
    return pltpu.CompilerParams(dimension_semantics=sem, vmem_limit_bytes=VMEM_LIMIT)


def _pack_bf16_pairs(x):
    n = x.shape[1] // 2
    bits = lax.bitcast_convert_type(x.astype(BF16).astype(F32), jnp.uint32)
    return lax.shift_right_logical(bits[:, :n], jnp.uint32(16)) | (bits[:, n:] & jnp.uint32(0xFFFF0000))


def _unpack_bf16_pairs(w):
    lo = lax.bitcast_convert_type(lax.shift_left(w, jnp.uint32(16)), F32)
    hi = lax.bitcast_convert_type(w & jnp.uint32(0xFFFF0000), F32)
    return lo, hi


def _rms(x, g):
    return x * lax.rsqrt(jnp.mean(x * x, axis=-1, keepdims=True) + RMS_EPS) * g


def _full(shape, **kw):
    n = len(shape)
    return pl.BlockSpec(shape, lambda *_: (0,) * n, **kw)


def _ada_kernel(c_ref, w_ref, b_ref, o_ref):
    c = c_ref[...]
    cond = c * jax.nn.sigmoid(c)
    o_ref[0] = jnp.dot(cond, w_ref[0], preferred_element_type=F32,
                       precision=lax.Precision.HIGHEST) + b_ref[0]


def _ada_mod(c, w_ada, b_ada):
    depth, d, nd = w_ada.shape
    bsz = c.shape[0]
    return pl.pallas_call(
        _ada_kernel,
        grid=(depth, nd // d),
        in_specs=[
            pl.BlockSpec((bsz, d), lambda l, j: (0, 0)),
            pl.BlockSpec((1, d, d), lambda l, j: (l, 0, j)),
            pl.BlockSpec((1, 1, d), lambda l, j: (l, 0, j)),
        ],
        out_specs=pl.BlockSpec((1, bsz, d), lambda l, j: (l, 0, j)),
        out_shape=jax.ShapeDtypeStruct((depth, bsz, nd), F32),
        compiler_params=_cparams("arbitrary", "arbitrary"),
        name="ada_mod",
    )(c, w_ada, b_ada.reshape(depth, 1, nd))


def _premix_kernel(x_ref, g_ref, sc_ref, sh_ref, wq_ref, wk_ref, wvt_ref, wu_ref, wga_ref, wgs_ref,
                   q_ref, k_ref, vt_ref, u_ref, ga_ref, gs_ref):
    x = x_ref[...]
    h = _rms(x, g_ref[...]) * (1.0 + sc_ref[0]) + sh_ref[0]
    hb = h.astype(BF16)
    nblk = q_ref.shape[0]
    q = (jnp.dot(hb, wq_ref[...], preferred_element_type=F32) * (HEAD_DIM ** -0.5 * LOG2E)).astype(BF16)
    k = jnp.dot(hb, wk_ref[...], preferred_element_type=F32).astype(BF16)
    vt = lax.dot_general(wvt_ref[...], hb, _NT, preferred_element_type=F32).astype(BF16)
    for r in range(nblk):
        q_ref[r] = q[r * MOBA_BLOCK:(r + 1) * MOBA_BLOCK]
        k_ref[r] = k[r * MOBA_BLOCK:(r + 1) * MOBA_BLOCK]
        vt_ref[r] = vt[:, r * MOBA_BLOCK:(r + 1) * MOBA_BLOCK]
    u_ref[...] = jnp.dot(hb, wu_ref[...], preferred_element_type=F32).astype(BF16)
    ga = jnp.dot(hb, wga_ref[...], preferred_element_type=F32)
    ga_ref[...] = jax.nn.sigmoid(ga).astype(BF16)
    gs = jnp.dot(hb, wgs_ref[...], preferred_element_type=F32)
    gs_ref[...] = jax.nn.sigmoid(gs).astype(BF16)


def _premix(x2, g, sc, sh, w_in, seq):
    t, d = x2.shape
    tm = TOKEN_TILE
    tiles_per_seq = seq // tm
    nblk = tm // MOBA_BLOCK
    a = ATT_WIDTH
    wb = w_in.astype(BF16)
    wq, wk, wv = wb[:, :a], wb[:, a:2 * a], wb[:, 2 * a:3 * a]
    wu = wb[:, 3 * a:3 * a + SSM_WIDTH]
    wga = wb[:, 3 * a + SSM_WIDTH:3 * a + SSM_WIDTH + d]
    wgs = wb[:, 3 * a + SSM_WIDTH + d:]
    mod_spec = pl.BlockSpec((1, 1, d), lambda i: (i // tiles_per_seq, 0, 0))
    blk3 = lambda rows, cols: pl.BlockSpec((nblk, rows, cols), lambda i: (i, 0, 0))
    row = lambda cols: pl.BlockSpec((tm, cols), lambda i: (i, 0))
    time_major = pl.BlockSpec((tm, SSM_WIDTH), lambda i: (i % tiles_per_seq, i // tiles_per_seq))
    nb_total = t // MOBA_BLOCK
    return pl.pallas_call(
        _premix_kernel,
        grid=(t // tm,),
        in_specs=[row(d), _full((1, d)), mod_spec, mod_spec,
                  _full(wq.shape), _full(wk.shape), _full((a, d)), _full(wu.shape),
                  _full(wga.shape), _full(wgs.shape)],
        out_specs=[blk3(MOBA_BLOCK, a), blk3(MOBA_BLOCK, a), blk3(a, MOBA_BLOCK),
                   time_major, row(d), row(d)],
        out_shape=[jax.ShapeDtypeStruct((nb_total, MOBA_BLOCK, a), BF16),
                   jax.ShapeDtypeStruct((nb_total, MOBA_BLOCK, a), BF16),
                   jax.ShapeDtypeStruct((nb_total, a, MOBA_BLOCK), BF16),
                   jax.ShapeDtypeStruct((seq, (t // seq) * SSM_WIDTH), BF16),
                   jax.ShapeDtypeStruct((t, d), BF16),
                   jax.ShapeDtypeStruct((t, d), BF16)],
        compiler_params=_cparams("arbitrary"),
        name="premix_inproj",
    )(x2, g.reshape(1, d), sc, sh, wq, wk, wv.T, wu, wga, wgs)


def _moba_kernel(far_ref, q_ref, k_ref, vt_ref, bias_ref, o_ref, kmean_ref, neg_ref, *state_refs):
    m_refs = state_refs[0::2]
    acc_refs = state_refs[1::2]
    nb = k_ref.shape[0]
    blk = MOBA_BLOCK
    i = pl.program_id(1)

    @pl.when(i == 0)
    def _():
        for n in range(nb):
            kmean_ref[n:n + 1, :] = jnp.mean(k_ref[n].astype(F32), axis=0, keepdims=True)

    lane = lax.broadcasted_iota(jnp.int32, (blk, 2 * HEAD_DIM), 1)
    nidx = lax.broadcasted_iota(jnp.int32, (nb, blk), 0)
    valid = nidx < i
    pair_w = 2 * HEAD_DIM

    def pair_cols(head):
        return slice((head // 2) * pair_w, (head // 2 + 1) * pair_w)

    def head_rows(head):
        return slice(head * HEAD_DIM, (head + 1) * HEAD_DIM)

    def q_head(head):
        q2 = q_ref[0, :, pair_cols(head)]
        lo = (head % 2) * HEAD_DIM
        return jnp.where((lane >= lo) & (lane < lo + HEAD_DIM), q2, jnp.zeros_like(q2))

    for head in range(ATT_HEADS):
        km = kmean_ref[:, pair_cols(head)]
        km_hi = km.astype(BF16)
        km_lo = (km - km_hi.astype(F32)).astype(BF16)
        qh = q_head(head)
        gate = (lax.dot_general(km_hi, qh, _NT, preferred_element_type=F32)
                + lax.dot_general(km_lo, qh, _NT, preferred_element_type=F32))
        g = jnp.where(valid, gate, -jnp.inf)
        sel = jnp.zeros((nb, blk), jnp.bool_)
        for _ in range(MOBA_TOPK):
            mx = jnp.max(g, axis=0, keepdims=True)
            first = jnp.min(jnp.where(g == mx, nidx, nb), axis=0, keepdims=True)
            pick = nidx == first
            sel = sel | pick
            g = jnp.where(pick, -jnp.inf, g)
        neg_ref[head * nb:(head + 1) * nb, :] = jnp.where(sel & valid, 0.0, MASK_NEG)

    def scores(head, js, adds):
        qh = q_head(head)
        parts = [lax.dot_general(k_ref[j, :, pair_cols(head)], qh, _NT, preferred_element_type=F32) + a
                 for j, a in zip(js, adds)]
        return parts[0] if len(parts) == 1 else jnp.concatenate(parts, axis=0)

    def weighted_values(head, js, p):
        vals = [vt_ref[j, head_rows(head), :] for j in js]
        vb = vals[0] if len(vals) == 1 else jnp.concatenate(vals, axis=1)
        vb = jnp.concatenate([vb, jnp.ones((SUM_ROWS, vb.shape[1]), BF16)], axis=0)
        return jnp.dot(vb, p.astype(BF16), preferred_element_type=F32)

    def start(head, js, st):
        m = jnp.max(st, axis=0, keepdims=True)
        m_refs[head][...] = m
        acc_refs[head][...] = weighted_values(head, js, jnp.exp2(st - m))

    def update(head, js, st):
        m = m_refs[head][...]
        m_new = jnp.maximum(m, jnp.max(st, axis=0, keepdims=True))
        alpha = jnp.exp2(m - m_new)
        m_refs[head][...] = m_new
        acc_refs[head][...] = alpha * acc_refs[head][...] + weighted_values(head, js, jnp.exp2(st - m_new))

    def sweep(js, adds_of, absorb):
        st = scores(0, js, adds_of(0))
        for head in range(ATT_HEADS):
            nxt = scores(head + 1, js, adds_of(head + 1)) if head + 1 < ATT_HEADS else None
            absorb(head, js, st)
            st = nxt

    def far_add(head, j):
        return neg_ref[pl.ds(head * nb + j, 1), :] + far_ref[head]

    @pl.when(i == 0)
    def _():
        sweep([i], lambda h: [bias_ref[h, 0]], start)

    @pl.when(i > 0)
    def _():
        sweep([i, i - 1],
              lambda h: [bias_ref[h, 0], bias_ref[h, 1] + neg_ref[pl.ds(h * nb + i - 1, 1), :]], start)

    n_far = jnp.maximum(i - 1, 0)

    def far_step(first, count):
        js = [first + c for c in range(count)]
        sweep(js, lambda h: [far_add(h, j) for j in js], update)

    def far_quad(jj, carry):
        far_step(4 * jj, 4)
        return carry

    lax.fori_loop(0, n_far // 4, far_quad, 0)
    rest = n_far % 4
    pl.when(rest >= 2)(lambda: far_step(n_far - rest, 2))
    pl.when(rest % 2 == 1)(lambda: far_step(n_far - 1, 1))

    for hp in range(ATT_HEADS // 2):
        outs = [acc_refs[h][0:HEAD_DIM, :] / acc_refs[h][HEAD_DIM:HEAD_DIM + 1, :] for h in (2 * hp, 2 * hp + 1)]
        pair = jnp.concatenate(outs, axis=0)
        o_ref[0, :, hp * pair_w:(hp + 1) * pair_w] = pair.T.astype(BF16)


def _rel_bucket(dist):
    n = jnp.maximum(dist, 0)
    max_exact = REL_BUCKETS // 2
    nf = jnp.maximum(n, 1).astype(F32)
    large = max_exact + (jnp.log(nf / max_exact) / math.log(REL_MAX_DIST / max_exact)
                         * (REL_BUCKETS - max_exact)).astype(jnp.int32)
    large = jnp.minimum(large, REL_BUCKETS - 1)
    return jnp.where(n < max_exact, n, large)


def _moba_bias_tiles(rel_bias):
    blk = MOBA_BLOCK
    span = 2 * blk
    rel_bias = rel_bias * LOG2E
    vec = rel_bias[_rel_bucket(jnp.arange(span))].T.astype(F32)
    masked = jnp.full_like(vec[:, :blk], MASK_NEG)
    ring_own = jnp.concatenate([vec[:, :blk], masked], axis=1)
    ring_prev = jnp.concatenate([vec[:, blk:], vec[:, :blk]], axis=1)

    def toeplitz(ring):
        flat = jnp.tile(ring, (1, blk))[:, :blk * (span - 1)]
        return flat.reshape(-1, blk, span - 1)[:, :, :blk]

    tiles = jnp.stack([toeplitz(ring_own), toeplitz(ring_prev)], axis=1)
    far = rel_bias[_rel_bucket(jnp.array(span))]
    return tiles, far.astype(F32)


def _moba(q4, k4, vt4, bias_tiles, far_bias, bsz):
    nb_total, blk, a = q4.shape
    nb = nb_total // bsz
    return pl.pallas_call(
        _moba_kernel,
        grid=(bsz, nb),
        in_specs=[
            pl.BlockSpec(memory_space=pltpu.SMEM),
            pl.BlockSpec((1, blk, a), lambda b, i: (b * nb + i, 0, 0)),
            pl.BlockSpec((nb, blk, a), lambda b, i: (b, 0, 0)),
            pl.BlockSpec((nb, a, blk), lambda b, i: (b, 0, 0)),
            _full(bias_tiles.shape),
        ],
        out_specs=pl.BlockSpec((1, blk, a), lambda b, i: (b * nb + i, 0, 0)),
        out_shape=jax.ShapeDtypeStruct((nb_total, blk, a), BF16),
        scratch_shapes=[pltpu.VMEM((nb, a), F32), pltpu.VMEM((ATT_HEADS * nb, blk), F32)]
        + [pltpu.VMEM((1, blk), F32), pltpu.VMEM((HEAD_DIM + SUM_ROWS, blk), F32)] * ATT_HEADS,
        compiler_params=_cparams("arbitrary", "arbitrary"),
        name="moba_attention",
    )(far_bias, q4, k4, vt4, bias_tiles)


def _ssm_weights(a_re, a_im, log_dt, b_re, b_im, c_re, c_im, d_skip):
    g, p, cw, n = SSM_GROUPS, SSM_STATE, SSM_GROUP, SSM_CHUNK
    gs = LANES // cw
    nq = g // gs
    npair = n // 2
    pairs_per_set = gs // 2
    lam_re = jnp.minimum(a_re.astype(F32), -1e-4)
    lam_im = a_im.astype(F32)
    dt = jnp.exp(log_dt.astype(F32))[:, None]
    mag = jnp.exp(lam_re * dt)
    lb_re = mag * jnp.cos(lam_im * dt)
    lb_im = mag * jnp.sin(lam_im * dt)
    n_re = lb_re - 1.0
    n_im = lb_im
    den = lam_re * lam_re + lam_im * lam_im
    z_re = ((n_re * lam_re + n_im * lam_im) / den)[..., None]
    z_im = ((n_im * lam_re - n_re * lam_im) / den)[..., None]
    br, bi = b_re.astype(F32), b_im.astype(F32)
    bb_re = z_re * br - z_im * bi
    bb_im = z_re * bi + z_im * br
    pw_re, pw_im = [jnp.ones_like(lb_re)], [jnp.zeros_like(lb_im)]
    for _ in range(n):
        r, im = pw_re[-1], pw_im[-1]
        pw_re.append(r * lb_re - im * lb_im)
        pw_im.append(r * lb_im + im * lb_re)
    pw_re, pw_im = jnp.stack(pw_re), jnp.stack(pw_im)
    cr, ci = c_re.astype(F32), c_im.astype(F32)
    hi = lax.Precision.HIGHEST
    rev_re, rev_im = pw_re[n - 1::-1], pw_im[n - 1::-1]
    sb_re = jnp.einsum('sgp,gpc->sgcp', rev_re, bb_re) - jnp.einsum('sgp,gpc->sgcp', rev_im, bb_im)
    sb_im = jnp.einsum('sgp,gpc->sgcp', rev_re, bb_im) + jnp.einsum('sgp,gpc->sgcp', rev_im, bb_re)
    cl_re = jnp.einsum('gcp,tgp->tgpc', cr, pw_re[1:]) - jnp.einsum('gcp,tgp->tgpc', ci, pw_im[1:])
    cl_im = jnp.einsum('gcp,tgp->tgpc', cr, pw_im[1:]) + jnp.einsum('gcp,tgp->tgpc', ci, pw_re[1:])
    cp_re = jnp.einsum('gcp,tgp->tgcp', cr, pw_re[:n]) - jnp.einsum('gcp,tgp->tgcp', ci, pw_im[:n])
    cp_im = jnp.einsum('gcp,tgp->tgcp', cr, pw_im[:n]) + jnp.einsum('gcp,tgp->tgcp', ci, pw_re[:n])
    kern = (jnp.einsum('tgcp,gpd->gtcd', cp_re, bb_re, precision=hi)
            - jnp.einsum('tgcp,gpd->gtcd', cp_im, bb_im, precision=hi))

    kp = jnp.concatenate([jnp.zeros_like(kern[:, :1]), kern], axis=1)
    dl = jnp.arange(npair)[:, None, None]
    s0 = jnp.arange(2)[None, :, None]
    t0 = jnp.arange(2)[None, None, :]
    lag = 2 * dl + t0 - s0
    kl = kp[:, lag + 1].reshape(nq, gs, npair, 2, 2, cw, cw)
    base = kl.transpose(0, 2, 3, 1, 6, 4, 5)
    same_group = jnp.eye(gs, dtype=F32)
    w_toe = jnp.concatenate(
        [base[..., t0, :] * same_group[:, g1][None, None, None, :, None, None]
         for t0 in range(2) for g1 in range(gs)], axis=-1)
    w_toe = w_toe.reshape(nq, npair, MXU_DIM, MXU_DIM).astype(BF16)

    member = (jnp.arange(gs)[None, :, None]
              == 2 * jnp.arange(pairs_per_set)[:, None, None] + jnp.arange(2)[None, None, :]).astype(F32)

    def pair_tiles(x):
        cols = [x[ri][:, None] * member[:, :, gl][None, :, None, None, :, None, None]
                for ri in range(2) for gl in range(2)]
        return jnp.concatenate(cols, axis=-1).reshape(g // 2, npair, MXU_DIM, MXU_DIM)

    sbs = jnp.stack([sb_re, sb_im]).reshape(2, npair, 2, nq, gs, cw, p)
    w_in_state = pair_tiles(sbs.transpose(0, 3, 1, 2, 4, 5, 6)).astype(BF16)
    sos = jnp.stack([cl_re, -cl_im]).reshape(2, npair, 2, nq, gs, p, cw)
    w_state_out = jnp.swapaxes(pair_tiles(sos.transpose(0, 3, 1, 2, 4, 6, 5)), -1, -2).astype(BF16)
    decay = jnp.stack([pw_re[n].reshape(g // 2, 2 * p), pw_im[n].reshape(g // 2, 2 * p)])
    dvec = d_skip.astype(F32).reshape(1, g * cw)
    return w_toe, w_in_state, w_state_out, decay, dvec


def _ssm_kernel(u_ref, wt_ref, wi_ref, wo_ref, dec_ref, d_ref, y_ref, s_ref, xp_ref, x_ref, *, bsz):
    kt = u_ref.shape[0]
    n = u_ref.shape[1] // bsz
    rows = kt * bsz
    npair = n // 2
    nq = wt_ref.shape[0]
    npairs_g = wi_ref.shape[0]
    per_set = npairs_g // nq
    half = MXU_DIM // 2

    @pl.when(pl.program_id(0) == 0)
    def _():
        x_ref[...] = jnp.zeros_like(x_ref)

    def piece(s, q):
        return u_ref[:, s * bsz:(s + 1) * bsz, q * LANES:(q + 1) * LANES].reshape(rows, LANES)

    lhs = {(sp, q): jnp.concatenate([piece(2 * sp, q), piece(2 * sp + 1, q)], axis=-1)
           for sp in range(npair) for q in range(nq)}

    for gp in range(npairs_g):
        q = gp // per_set
        acc = jnp.dot(lhs[0, q], wi_ref[gp, 0], preferred_element_type=F32)
        for sp in range(1, npair):
            acc = acc + jnp.dot(lhs[sp, q], wi_ref[gp, sp], preferred_element_type=F32)
        s_ref[:, gp * MXU_DIM:(gp + 1) * MXU_DIM] = acc

    for gp in range(npairs_g):
        re_cols = slice(gp * MXU_DIM, gp * MXU_DIM + half)
        im_cols = slice(gp * MXU_DIM + half, (gp + 1) * MXU_DIM)
        ar = dec_ref[0, gp:gp + 1, :]
        ai = dec_ref[1, gp:gp + 1, :]
        xr = x_ref[:, re_cols]
        xi = x_ref[:, im_cols]
        for kc in range(kt):
            rs = slice(kc * bsz, (kc + 1) * bsz)
            xp_ref[rs, re_cols] = xr.astype(BF16)
            xp_ref[rs, im_cols] = xi.astype(BF16)
            sr = s_ref[rs, re_cols]
            si = s_ref[rs, im_cols]
            xr, xi = ar * xr - ai * xi + sr, ar * xi + ai * xr + si
        x_ref[:, re_cols] = xr
        x_ref[:, im_cols] = xi

    for q in range(nq):
        for tp in range(npair):
            acc = jnp.dot(lhs[0, q], wt_ref[q, tp], preferred_element_type=F32)
            for sp in range(1, tp + 1):
                acc = acc + jnp.dot(lhs[sp, q], wt_ref[q, tp - sp], preferred_element_type=F32)
            for gp in range(q * per_set, (q + 1) * per_set):
                acc = acc + jnp.dot(xp_ref[:, gp * MXU_DIM:(gp + 1) * MXU_DIM], wo_ref[gp, tp],
                                    preferred_element_type=F32)
            for t0 in range(2):
                s = 2 * tp + t0
                y = acc[:, t0 * half:(t0 + 1) * half] + d_ref[:, q * LANES:(q + 1) * LANES] * piece(s, q).astype(F32)
                y_ref[:, s * bsz:(s + 1) * bsz, q * LANES:(q + 1) * LANES] = (
                    jax.nn.gelu(y).astype(BF16).reshape(kt, bsz, LANES))


def _ssm(u_tm, weights, bsz, seq):
    n = SSM_CHUNK
    kt = SSM_CHUNKS_PER_STEP
    nchunk = seq // n
    rows = kt * bsz
    state_w = SSM_GROUPS * 2 * SSM_STATE
    u3 = u_tm.reshape(nchunk, n * bsz, SSM_WIDTH)
    blk = pl.BlockSpec((kt, n * bsz, SSM_WIDTH), lambda i: (i, 0, 0))

    once = pl.Buffered(1)
    y3 = pl.pallas_call(
        functools.partial(_ssm_kernel, bsz=bsz),
        grid=(nchunk // kt,),
        in_specs=[blk] + [_full(w.shape, pipeline_mode=once) for w in weights],
        out_specs=blk,
        out_shape=jax.ShapeDtypeStruct(u3.shape, BF16),
        scratch_shapes=[pltpu.VMEM((rows, state_w), F32), pltpu.VMEM((rows, state_w), BF16),
                        pltpu.VMEM((bsz, state_w), F32)],
        compiler_params=_cparams("arbitrary"),
        name="s5_chunked_scan",
    )(u3, *weights)
    return y3.reshape(seq, bsz * SSM_WIDTH)


def _postmix_kernel(att_ref, ys_ref, ga_ref, gs_ref, x_ref, gt_ref, sc_ref, sh_ref, gpost_ref, gpre_ref,
                    wglu_ref, wso_ref, wao_ref, wo_ref, wrh_ref, wrl_ref, br_ref,
                    x1_ref, h2_ref, idx_ref, gate_ref, rank_ref, cnt_ref, run_ref):
    tm = x_ref.shape[0]
    ne = wrh_ref.shape[0]

    @pl.when(pl.program_id(0) == 0)
    def _():
        run_ref[...] = jnp.zeros_like(run_ref)

    halves = [slice(r * (tm // 2), (r + 1) * (tm // 2)) for r in range(2)]
    dot = functools.partial(jnp.dot, preferred_element_type=F32)
    att = [dot(att_ref[r, :], wao_ref[...]) for r in halves]
    glu = [dot(ys_ref[r, :], wglu_ref[...]) for r in halves]
    ssm = []
    for g in glu:
        sg = (g[:, :SSM_WIDTH] * jax.nn.sigmoid(g[:, SSM_WIDTH:])).astype(BF16)
        ssm.append(dot(sg, wso_ref[...]))
    y = []
    for r, a, s in zip(halves, att, ssm):
        merged = (ga_ref[r, :].astype(F32) * a + gs_ref[r, :].astype(F32) * s).astype(BF16)
        y.append(dot(merged, wo_ref[...]))
    logits = []
    for r, yh in zip(halves, y):
        x1 = x_ref[r, :] + gt_ref[0] * _rms(yh, gpost_ref[...])
        x1_ref[r, :] = x1
        h2 = _rms(x1, gpre_ref[...]) * (1.0 + sc_ref[0]) + sh_ref[0]
        h2_ref[r, :] = _pack_bf16_pairs(h2)
        h_hi = h2.astype(BF16)
        h_lo = (h2 - h_hi.astype(F32)).astype(BF16)
        logits.append(lax.dot_general(wrh_ref[...], h_hi, _NT, preferred_element_type=F32)
                      + lax.dot_general(wrh_ref[...], h_lo, _NT, preferred_element_type=F32)
                      + lax.dot_general(wrl_ref[...], h_hi, _NT, preferred_element_type=F32))
    logits = jnp.concatenate(logits, axis=1) + br_ref[...]
    eidx = lax.broadcasted_iota(jnp.int32, (ne, tm), 0)
    lg = logits
    vals, idxs = [], []
    for _ in range(TOP_K):
        mx = jnp.max(lg, axis=0, keepdims=True)
        first = jnp.min(jnp.where(lg == mx, eidx, ne), axis=0, keepdims=True)
        vals.append(mx)
        idxs.append(first)
        lg = jnp.where(eidx == first, -jnp.inf, lg)
    exps = [jnp.exp(v - vals[0]) for v in vals]
    denom = exps[0] + exps[1] + exps[2] + exps[3]
    gate_ref[...] = jnp.concatenate([e / denom for e in exps], axis=0)
    idx_ref[...] = jnp.concatenate(idxs, axis=0)
    onehot = jnp.where(lg == -jnp.inf, 1.0, 0.0)
    t_row = lax.broadcasted_iota(jnp.int32, (tm, tm), 0)
    t_col = lax.broadcasted_iota(jnp.int32, (tm, tm), 1)
    earlier = jnp.where(t_row < t_col, 1.0, 0.0).astype(BF16)
    before = jnp.dot(onehot.astype(BF16), earlier, preferred_element_type=F32) + run_ref[:, 0:1]
    ranks = [jnp.sum(jnp.where(eidx == ix, before, 0.0), axis=0, keepdims=True) for ix in idxs]
    rank_ref[...] = jnp.concatenate(ranks, axis=0).astype(jnp.int32)
    run_ref[...] = run_ref[...] + jnp.sum(onehot, axis=1, keepdims=True)
    cnt_ref[...] = run_ref[...].astype(jnp.int32)


def _postmix(att2, ys_tm, ga, gs, x2, gt1, sc2, sh2, g_post, g_pre, w_glu, w_ssm_out, w_att_out, w_out,
             w_router, b_router, seq):
    t, d = x2.shape
    tm = TOKEN_TILE
    tiles_per_seq = seq // tm
    ne = w_router.shape[1]
    wr_t = w_router.T.astype(F32)
    wr_hi = wr_t.astype(BF16)
    wr_lo = (wr_t - wr_hi.astype(F32)).astype(BF16)
    mod_spec = pl.BlockSpec((1, 1, d), lambda i: (i // tiles_per_seq, 0, 0))
    row = lambda cols: pl.BlockSpec((tm, cols), lambda i: (i, 0))
    time_major = pl.BlockSpec((tm, SSM_WIDTH), lambda i: (i % tiles_per_seq, i // tiles_per_seq))
    col = pl.BlockSpec((TOP_K, tm), lambda i: (0, i))
    weights = [w_glu.astype(BF16), w_ssm_out.astype(BF16), w_att_out.astype(BF16), w_out.astype(BF16),
               wr_hi, wr_lo, b_router.astype(F32).reshape(ne, 1)]
    return pl.pallas_call(
        _postmix_kernel,
        grid=(t // tm,),
        in_specs=[row(ATT_WIDTH), time_major, row(d), row(d), row(d), mod_spec, mod_spec, mod_spec,
                  _full((1, d)), _full((1, d))] + [_full(w.shape) for w in weights],
        out_specs=[row(d), row(d // 2), col, col, col, _full((ne, 128))],
        out_shape=[jax.ShapeDtypeStruct((t, d), F32), jax.ShapeDtypeStruct((t, d // 2), jnp.uint32),
                   jax.ShapeDtypeStruct((TOP_K, t), jnp.int32), jax.ShapeDtypeStruct((TOP_K, t), F32),
                   jax.ShapeDtypeStruct((TOP_K, t), jnp.int32), jax.ShapeDtypeStruct((ne, 128), jnp.int32)],
        scratch_shapes=[pltpu.VMEM((ne, 128), F32)],
        compiler_params=_cparams("arbitrary"),
        name="postmix_router",
    )(att2, ys_tm, ga, gs, x2, gt1, sc2, sh2, g_post.reshape(1, d), g_pre.reshape(1, d), *weights)


def _sc_worker_base(per_worker):
    return (lax.axis_index("s") * SC_CORES + lax.axis_index("c")) * per_worker


def _sc_mesh():
    return plsc.VectorSubcoreMesh(core_axis_name="c", subcore_axis_name="s",
                                  num_cores=SC_CORES, num_subcores=SC_SUBCORES)


def _sc_scatter_kernel(rows_hbm, idx_hbm, out_hbm, idx_v, rows_v, load_sems, store_sems, *, per_worker):
    chunk = rows_v.shape[1]
    n_tokens = rows_hbm.shape[0]
    n_chunks = per_worker // chunk
    base = _sc_worker_base(per_worker)

    def load(c, b):
        off = pl.multiple_of(base + c * chunk, chunk)
        for k in range(TOP_K):
            pltpu.sync_copy(idx_hbm.at[pl.ds(k * n_tokens + off, chunk)], idx_v.at[b, k])
        return pltpu.make_async_copy(rows_hbm.at[pl.ds(off, chunk)], rows_v.at[b], load_sems.at[b])

    def stores(b):
        return [pltpu.make_async_copy(rows_v.at[b], out_hbm.at[idx_v.at[b, k]], store_sems.at[b])
                for k in range(TOP_K)]

    load(0, 0).start()

    @pl.loop(0, n_chunks, step=2)
    def _(c0):
        for b in range(2):
            c = c0 + b

            @pl.when(c > 0)
            def _():
                for cp in stores(1 - b):
                    cp.wait()

            @pl.when(c + 1 < n_chunks)
            def _():
                load(c + 1, 1 - b).start()

            pltpu.make_async_copy(rows_hbm.at[pl.ds(0, chunk)], rows_v.at[b], load_sems.at[b]).wait()
            for cp in stores(b):
                cp.start()

    for cp in stores((n_chunks - 1) % 2):
        cp.wait()


def _sc_scatter_rows(rows, idx, n_out):
    t, d = rows.shape
    workers = SC_CORES * SC_SUBCORES
    per_worker = t // workers
    assert t % workers == 0 and per_worker % (2 * SC_GATHER_ROWS) == 0
    return pl.kernel(
        functools.partial(_sc_scatter_kernel, per_worker=per_worker),
        out_type=jax.ShapeDtypeStruct((n_out, d), rows.dtype),
        mesh=_sc_mesh(),
        scratch_types=[pltpu.VMEM((2, TOP_K, SC_GATHER_ROWS), jnp.int32),
                       pltpu.VMEM((2, SC_GATHER_ROWS, d), rows.dtype),
                       pltpu.SemaphoreType.DMA((2,)), pltpu.SemaphoreType.DMA((2,))],
        name="sc_row_scatter",
    )(rows, idx)


def _sc_gather_kernel(table_hbm, idx_hbm, out_hbm, idx_v, rows_v, sems, *, per_worker):
    chunk = rows_v.shape[1]
    n_chunks = per_worker // chunk
    base = _sc_worker_base(per_worker)

    def gather(c, b):
        off = pl.multiple_of(base + c * chunk, chunk)
        pltpu.sync_copy(idx_hbm.at[pl.ds(off, chunk)], idx_v.at[b])
        return pltpu.make_async_copy(table_hbm.at[idx_v.at[b]], rows_v.at[b], sems.at[b])

    gather(0, 0).start()

    @pl.loop(0, n_chunks, step=2)
    def _(c0):
        for b in range(2):
            c = c0 + b

            @pl.when(c + 1 < n_chunks)
            def _():
                gather(c + 1, 1 - b).start()

            pltpu.make_async_copy(table_hbm.at[idx_v.at[b]], rows_v.at[b], sems.at[b]).wait()
            off = pl.multiple_of(base + c * chunk, chunk)
            pltpu.sync_copy(rows_v.at[b], out_hbm.at[pl.ds(off, chunk)])


def _sc_gather_rows(table, idx):
    n = idx.shape[0]
    d = table.shape[1]
    workers = SC_CORES * SC_SUBCORES
    per_worker = n // workers
    assert n % workers == 0 and per_worker % (2 * SC_GATHER_ROWS) == 0
    return pl.kernel(
        functools.partial(_sc_gather_kernel, per_worker=per_worker),
        out_type=jax.ShapeDtypeStruct((n, d), table.dtype),
        mesh=_sc_mesh(),
        scratch_types=[pltpu.VMEM((2, SC_GATHER_ROWS), jnp.int32),
                       pltpu.VMEM((2, SC_GATHER_ROWS, d), table.dtype),
                       pltpu.SemaphoreType.DMA((2,))],
        name="sc_row_gather",
    )(table, idx)


def _experts_kernel(be_ref, nused_ref, valid_ref, next_ref, x_ref, b1_ref, b2_ref, w1_hbm, w2_hbm, y_ref,
                    w1s_ref, w2s_ref, w1b_ref, w2b_ref, sems, *, layer):
    i = pl.program_id(0)
    prev = be_ref[jnp.maximum(i - 1, 0)]
    fresh = (i < nused_ref[0]) & ((i == 0) | (be_ref[i] != prev))

    def fetch(e):
        return (pltpu.make_async_copy(w1_hbm.at[layer, e], w1s_ref, sems.at[0]),
                pltpu.make_async_copy(w2_hbm.at[layer, e], w2s_ref, sems.at[1]))

    @pl.when(i == 0)
    def _():
        for cp in fetch(be_ref[0]):
            cp.start()

    @pl.when(fresh)
    def _():
        for cp in fetch(be_ref[i]):
            cp.wait()
        w1b_ref[...] = w1s_ref[...].astype(BF16)
        w2b_ref[...] = w2s_ref[...].astype(BF16)

        @pl.when(next_ref[i] >= 0)
        def _():
            for cp in fetch(next_ref[i]):
                cp.start()

    @pl.when(i < nused_ref[0])
    def _():
        row = lax.broadcasted_iota(jnp.int32, x_ref.shape, 0)
        x_lo, x_hi = _unpack_bf16_pairs(jnp.where(row < valid_ref[i], x_ref[...], jnp.uint32(0)))
        x = jnp.concatenate([x_lo.astype(BF16), x_hi.astype(BF16)], axis=1)
        gu = jnp.dot(x, w1b_ref[...], preferred_element_type=F32) + b1_ref[...]
        g = jnp.minimum(gu[:, :D_FF], SWIGLU_LIMIT)
        up = jnp.clip(gu[:, D_FF:], -SWIGLU_LIMIT, SWIGLU_LIMIT)
        act = ((up + 1.0) * g * jax.nn.sigmoid(SWIGLU_ALPHA * g)).astype(BF16)
        y = jnp.dot(act, w2b_ref[...], preferred_element_type=F32) + b2_ref[...]
        y_ref[...] = _pack_bf16_pairs(y)

    @pl.when(i >= nused_ref[0])
    def _():
        y_ref[...] = jnp.zeros_like(y_ref)


def _experts(xb, block_e, n_used, valid, next_e, w1, b1, w2, b2, layer):
    p_rows, packed_w = xb.shape
    d = 2 * packed_w
    depth, ne = w1.shape[:2]
    rb = EXPERT_ROWS
    bmap = lambda i, be, nu, va, nx: (layer, be[i], 0, 0)
    rows = pl.BlockSpec((rb, packed_w), lambda i, be, nu, va, nx: (i, 0))
    grid_spec = pltpu.PrefetchScalarGridSpec(
        num_scalar_prefetch=4,
        grid=(p_rows // rb,),
        in_specs=[
            rows,
            pl.BlockSpec((None, None, 1, 2 * D_FF), bmap),
            pl.BlockSpec((None, None, 1, d), bmap),
            pl.BlockSpec(memory_space=pl.ANY),
            pl.BlockSpec(memory_space=pl.ANY),
        ],
        out_specs=rows,
        scratch_shapes=[pltpu.VMEM((d, 2 * D_FF), w1.dtype), pltpu.VMEM((D_FF, d), w2.dtype),
                        pltpu.VMEM((d, 2 * D_FF), BF16), pltpu.VMEM((D_FF, d), BF16),
                        pltpu.SemaphoreType.DMA((2,))],
    )
    return pl.pallas_call(
        functools.partial(_experts_kernel, layer=layer),
        grid_spec=grid_spec,
        out_shape=jax.ShapeDtypeStruct((p_rows, packed_w), jnp.uint32),
        compiler_params=_cparams("arbitrary"),
        name="expert_ffn",
    )(block_e, n_used, valid, next_e, xb, b1.reshape(depth, ne, 1, 2 * D_FF), b2.reshape(depth, ne, 1, d), w1, w2)


def _combine_kernel(gate_ref, x_ref, gt_ref, g_ref, y0_ref, y1_ref, y2_ref, y3_ref, o_ref):
    tm = x_ref.shape[0]
    gates = gate_ref[...]
    gates = jnp.concatenate([gates, jnp.zeros((LANES - TOP_K, tm), F32)], axis=0).T
    lo, hi = _unpack_bf16_pairs(y0_ref[...])
    y_lo, y_hi = gates[:, 0:1] * lo, gates[:, 0:1] * hi
    for k, y_ref in enumerate((y1_ref, y2_ref, y3_ref), start=1):
        lo, hi = _unpack_bf16_pairs(y_ref[...])
        y_lo, y_hi = y_lo + gates[:, k:k + 1] * lo, y_hi + gates[:, k:k + 1] * hi
    y = jnp.concatenate([y_lo, y_hi], axis=1)
    o_ref[...] = x_ref[...] + gt_ref[0] * _rms(y, g_ref[...])


def _combine(y4, gate_t, x2, gt2, g_post, seq):
    t, d = x2.shape
    tm = TOKEN_TILE
    tiles = t // tm
    tiles_per_seq = seq // tm
    row = pl.BlockSpec((tm, d), lambda i: (i, 0))
    slot = lambda k: pl.BlockSpec((tm, y4.shape[1]), lambda i: (k * tiles + i, 0))
    return pl.pallas_call(
        _combine_kernel,
        grid=(tiles,),
        in_specs=[pl.BlockSpec((TOP_K, tm), lambda i: (0, i)),
                  row, pl.BlockSpec((1, 1, d), lambda i: (i // tiles_per_seq, 0, 0)), _full((1, d))]
        + [slot(k) for k in range(TOP_K)],
        out_specs=row,
        out_shape=jax.ShapeDtypeStruct((t, d), F32),
        compiler_params=_cparams("arbitrary"),
        name="expert_combine",
    )(gate_t, x2, gt2, g_post.reshape(1, d), y4, y4, y4, y4)


def _route_plan(idx_t, rank_t, counts):
    rb = EXPERT_ROWS
    k, t = idx_t.shape
    padded = (counts + rb - 1) // rb * rb
    pad_ends = jnp.cumsum(padded)
    pad_starts = pad_ends - padded
    experts = jnp.arange(N_EXPERTS, dtype=jnp.int32)
    start_of = jnp.sum(jnp.where(idx_t[None] == experts[:, None, None], pad_starts[:, None, None], 0), axis=0)
    dest = (start_of + rank_t).astype(jnp.int32)
    n_blocks = (k * t) // rb + N_EXPERTS
    blk_start = jnp.arange(n_blocks, dtype=jnp.int32) * rb
    block_e = jnp.minimum(jnp.sum(pad_ends[None, :] <= blk_start[:, None], axis=1), N_EXPERTS - 1)
    onehot_e = block_e[:, None] == experts[None, :]
    cnt_b = jnp.sum(jnp.where(onehot_e, counts[None, :], 0), axis=1)
    start_b = jnp.sum(jnp.where(onehot_e, pad_starts[None, :], 0), axis=1)
    valid = jnp.clip(cnt_b - (blk_start - start_b), 0, rb).astype(jnp.int32)
    n_used = (pad_ends[-1] // rb).astype(jnp.int32).reshape(1)
    later_nonempty = (experts[None, :] > experts[:, None]) & (counts[None, :] > 0)
    next_nonempty = jnp.min(jnp.where(later_nonempty, experts[None, :], N_EXPERTS), axis=1)
    next_nonempty = jnp.where(next_nonempty == N_EXPERTS, -1, next_nonempty)
    next_e = jnp.sum(jnp.where(onehot_e, next_nonempty[None, :], 0), axis=1).astype(jnp.int32)
    return dest.reshape(-1), block_e.astype(jnp.int32), n_used, valid, next_e, n_blocks * rb


def kernel(x, c, rel_bias, w_ada, b_ada, g_pre_mix, g_post_mix, g_pre_ffn, g_post_ffn, w_in, ssm_a_re, ssm_a_im, ssm_log_dt, ssm_b_re, ssm_b_im, ssm_c_re, ssm_c_im, ssm_d, w_glu, w_ssm_out, w_att_out, w_out, w_router, b_router, w_exp_in, b_exp_in, w_exp_out, b_exp_out):
    bsz, seq, d = x.shape
    depth = w_ada.shape[0]
    t = bsz * seq
    assert d == D_MODEL and seq % TOKEN_TILE == 0 and TOKEN_TILE % MOBA_BLOCK == 0
    assert seq % (SSM_CHUNK * SSM_CHUNKS_PER_STEP) == 0

    mod = _ada_mod(c, w_ada, b_ada)
    bias_tiles, far_bias = _moba_bias_tiles(rel_bias.astype(F32))
    x2 = x.reshape(t, d)
    for l in range(depth):
        sh1, sc1, gt1, sh2, sc2, gt2 = [m.reshape(bsz, 1, d) for m in jnp.split(mod[l], N_MOD, axis=-1)]
        q4, k4, vt4, u_tm, ga, gs = _premix(x2, g_pre_mix[l], sc1, sh1, w_in[l], seq)
        att = _moba(q4, k4, vt4, bias_tiles, far_bias, bsz).reshape(t, ATT_WIDTH)
        ssm_w = _ssm_weights(ssm_a_re[l], ssm_a_im[l], ssm_log_dt[l], ssm_b_re[l], ssm_b_im[l],
                             ssm_c_re[l], ssm_c_im[l], ssm_d[l])
        ys_tm = _ssm(u_tm, ssm_w, bsz, seq)
        x1, h2, idx_t, gate_t, rank_t, cnt = _postmix(
            att, ys_tm, ga, gs, x2, gt1, sc2, sh2, g_post_mix[l], g_pre_ffn[l],
            w_glu[l], w_ssm_out[l], w_att_out[l], w_out[l], w_router[l], b_router[l], seq)
        dest_flat, block_e, n_used, valid, next_e, p_rows = _route_plan(idx_t, rank_t, cnt[:, 0])
        xb = _sc_scatter_rows(h2, dest_flat, p_rows)
        yb = _experts(xb, block_e, n_used, valid, next_e, w_exp_in, b_exp_in, w_exp_out, b_exp_out, l)
        y4 = _sc_gather_rows(yb, dest_flat)
        x2 = _combine(y4, gate_t, x1, gt2, g_post_ffn[l], seq)
    return x2.reshape(bsz, seq, d)
```

```python
import functools
import math

import jax
import jax.numpy as jnp
from jax import lax
from jax.experimental import pallas as pl
from jax.experimental.pallas import tpu as pltpu
from jax.experimental.pallas import tpu_sc as plsc

F32 = jnp.float32
BF16 = jnp.bfloat16

D_MODEL = 1024
ATT_HEADS = 8
HEAD_DIM = 64
ATT_WIDTH = ATT_HEADS * HEAD_DIM
MOBA_BLOCK = 256
MOBA_TOPK = 3
REL_BUCKETS = 32
REL_MAX_DIST = 128
SSM_WIDTH = D_MODEL // 2
SSM_GROUP = 16
SSM_GROUPS = SSM_WIDTH // SSM_GROUP
SSM_STATE = 64
N_EXPERTS = 32
TOP_K = 4
D_FF = D_MODEL
SWIGLU_ALPHA = 1.702
SWIGLU_LIMIT = 7.0
RMS_EPS = 1e-6
N_MOD = 6

SSM_CHUNK = 8
SSM_CHUNKS_PER_STEP = 16
LANES = 128
MXU_DIM = 256
TOKEN_TILE = 1024
SC_CORES = 2
SC_SUBCORES = 16
SC_GATHER_ROWS = 64
EXPERT_ROWS = 512
MASK_NEG = -1e30
LOG2E = math.log2(math.e)
SUM_ROWS = 16
VMEM_LIMIT = 56 * 1024 * 1024

_NT = (((1,), (1,)), ((), ()))


def _cparams(*sem):
    return pltpu.CompilerParams(dimension_semantics=sem, vmem_limit_bytes=VMEM_LIMIT)


def _pack_bf16_pairs(x):
    n = x.shape[1] // 2
    bits = lax.bitcast_convert_type(x.astype(BF16).astype(F32), jnp.uint32)
    return lax.shift_right_logical(bits[:, :n], jnp.uint32(16)) | (bits[:, n:] & jnp.uint32(0xFFFF0000))


def _unpack_bf16_pairs(w):
    lo = lax.bitcast_convert_type(lax.shift_left(w, jnp.uint32(16)), F32)
    hi = lax.bitcast_convert_type(w & jnp.uint32(0xFFFF0000), F32)
    return lo, hi


def _rms(x, g):
    return x * lax.rsqrt(jnp.mean(x * x, axis=-1, keepdims=True) + RMS_EPS) * g


def _full(shape, **kw):
    n = len(shape)
    return pl.BlockSpec(shape, lambda *_: (0,) * n, **kw)


def _ada_kernel(c_ref, w_ref, b_ref, o_ref):
    c = c_ref[...]
    cond = c * jax.nn.sigmoid(c)
    o_ref[0] = jnp.dot(cond, w_ref[0], preferred_element_type=F32,
                       precision=lax.Precision.HIGHEST) + b_ref[0]


def _ada_mod(c, w_ada, b_ada):
    depth, d, nd = w_ada.shape
    bsz = c.shape[0]
    return pl.pallas_call(
        _ada_kernel,
        grid=(depth, nd // d),
        in_specs=[
            pl.BlockSpec((bsz, d), lambda l, j: (0, 0)),
            pl.BlockSpec((1, d, d), lambda l, j: (l, 0, j)),
            pl.BlockSpec((1, 1, d), lambda l, j: (l, 0, j)),
        ],
        out_specs=pl.BlockSpec((1, bsz, d), lambda l, j: (l, 0, j)),
        out_shape=jax.ShapeDtypeStruct((depth, bsz, nd), F32),
        compiler_params=_cparams("arbitrary", "arbitrary"),
        name="ada_mod",
    )(c, w_ada, b_ada.reshape(depth, 1, nd))


def _premix_kernel(x_ref, g_ref, sc_ref, sh_ref, wq_ref, wk_ref, wvt_ref, wu_ref, wga_ref, wgs_ref,
                   q_ref, k_ref, vt_ref, u_ref, ga_ref, gs_ref):
    x = x_ref[...]
    h = _rms(x, g_ref[...]) * (1.0 + sc_ref[0]) + sh_ref[0]
    hb = h.astype(BF16)
    nblk = q_ref.shape[0]
    q = (jnp.dot(hb, wq_ref[...], preferred_element_type=F32) * (HEAD_DIM ** -0.5 * LOG2E)).astype(BF16)
    k = jnp.dot(hb, wk_ref[...], preferred_element_type=F32).astype(BF16)
    vt = lax.dot_general(wvt_ref[...], hb, _NT, preferred_element_type=F32).astype(BF16)
    for r in range(nblk):
        q_ref[r] = q[r * MOBA_BLOCK:(r + 1) * MOBA_BLOCK]
        k_ref[r] = k[r * MOBA_BLOCK:(r + 1) * MOBA_BLOCK]
        vt_ref[r] = vt[:, r * MOBA_BLOCK:(r + 1) * MOBA_BLOCK]
    u_ref[...] = jnp.dot(hb, wu_ref[...], preferred_element_type=F32).astype(BF16)
    ga = jnp.dot(hb, wga_ref[...], preferred_element_type=F32)
    ga_ref[...] = jax.nn.sigmoid(ga).astype(BF16)
    gs = jnp.dot(hb, wgs_ref[...], preferred_element_type=F32)
    gs_ref[...] = jax.nn.sigmoid(gs).astype(BF16)


def _premix(x2, g, sc, sh, w_in, seq):
    t, d = x2.shape
    tm = TOKEN_TILE
    tiles_per_seq = seq // tm
    nblk = tm // MOBA_BLOCK
    a = ATT_WIDTH
    wb = w_in.astype(BF16)
    wq, wk, wv = wb[:, :a], wb[:, a:2 * a], wb[:, 2 * a:3 * a]
    wu = wb[:, 3 * a:3 * a + SSM_WIDTH]
    wga = wb[:, 3 * a + SSM_WIDTH:3 * a + SSM_WIDTH + d]
    wgs = wb[:, 3 * a + SSM_WIDTH + d:]
    mod_spec = pl.BlockSpec((1, 1, d), lambda i: (i // tiles_per_seq, 0, 0))
    blk3 = lambda rows, cols: pl.BlockSpec((nblk, rows, cols), lambda i: (i, 0, 0))
    row = lambda cols: pl.BlockSpec((tm, cols), lambda i: (i, 0))
    time_major = pl.BlockSpec((tm, SSM_WIDTH), lambda i: (i % tiles_per_seq, i // tiles_per_seq))
    once = functools.partial(_full, pipeline_mode=pl.Buffered(1))
    nb_total = t // MOBA_BLOCK
    return pl.pallas_call(
        _premix_kernel,
        grid=(t // tm,),
        in_specs=[row(d), _full((1, d)), mod_spec, mod_spec,
                  once(wq.shape), once(wk.shape), once((a, d)), once(wu.shape),
                  once(wga.shape), once(wgs.shape)],
        out_specs=[blk3(MOBA_BLOCK, a), blk3(MOBA_BLOCK, a), blk3(a, MOBA_BLOCK),
                   time_major, row(d), row(d)],
        out_shape=[jax.ShapeDtypeStruct((nb_total, MOBA_BLOCK, a), BF16),
                   jax.ShapeDtypeStruct((nb_total, MOBA_BLOCK, a), BF16),
                   jax.ShapeDtypeStruct((nb_total, a, MOBA_BLOCK), BF16),
                   jax.ShapeDtypeStruct((seq, (t // seq) * SSM_WIDTH), BF16),
                   jax.ShapeDtypeStruct((t, d), BF16),
                   jax.ShapeDtypeStruct((t, d), BF16)],
        compiler_params=_cparams("arbitrary"),
        name="premix_inproj",
    )(x2, g.reshape(1, d), sc, sh, wq, wk, wv.T, wu, wga, wgs)


def _moba_kernel(far_ref, q_ref, k_ref, vt_ref, bias_ref, o_ref, kmean_ref, neg_ref, *state_refs):
    m_refs = state_refs[0::2]
    acc_refs = state_refs[1::2]
    nb = k_ref.shape[0]
    blk = MOBA_BLOCK
    i = pl.program_id(1)

    @pl.when(i == 0)
    def _():
        for n in range(nb):
            kmean_ref[n:n + 1, :] = jnp.mean(k_ref[n].astype(F32), axis=0, keepdims=True)

    lane = lax.broadcasted_iota(jnp.int32, (blk, 2 * HEAD_DIM), 1)
    nidx = lax.broadcasted_iota(jnp.int32, (nb, blk), 0)
    valid = nidx < i
    pair_w = 2 * HEAD_DIM

    def pair_cols(head):
        return slice((head // 2) * pair_w, (head // 2 + 1) * pair_w)

    def head_rows(head):
        return slice(head * HEAD_DIM, (head + 1) * HEAD_DIM)

    def q_head(head):
        q2 = q_ref[0, :, pair_cols(head)]
        lo = (head % 2) * HEAD_DIM
        return jnp.where((lane >= lo) & (lane < lo + HEAD_DIM), q2, jnp.zeros_like(q2))

    for head in range(ATT_HEADS):
        km = kmean_ref[:, pair_cols(head)]
        km_hi = km.astype(BF16)
        km_lo = (km - km_hi.astype(F32)).astype(BF16)
        qh = q_head(head)
        gate = (lax.dot_general(km_hi, qh, _NT, preferred_element_type=F32)
                + lax.dot_general(km_lo, qh, _NT, preferred_element_type=F32))
        g = jnp.where(valid, gate, -jnp.inf)
        sel = jnp.zeros((nb, blk), jnp.bool_)
        for _ in range(MOBA_TOPK):
            mx = jnp.max(g, axis=0, keepdims=True)
            first = jnp.min(jnp.where(g == mx, nidx, nb), axis=0, keepdims=True)
            pick = nidx == first
            sel = sel | pick
            g = jnp.where(pick, -jnp.inf, g)
        neg_ref[head * nb:(head + 1) * nb, :] = jnp.where(sel & valid, 0.0, MASK_NEG)

    def scores(head, js, adds):
        qh = q_head(head)
        parts = [lax.dot_general(k_ref[j, :, pair_cols(head)], qh, _NT, preferred_element_type=F32) + a
                 for j, a in zip(js, adds)]
        return parts[0] if len(parts) == 1 else jnp.concatenate(parts, axis=0)

    def weighted_values(head, js, p):
        vals = [vt_ref[j, head_rows(head), :] for j in js]
        vb = vals[0] if len(vals) == 1 else jnp.concatenate(vals, axis=1)
        vb = jnp.concatenate([vb, jnp.ones((SUM_ROWS, vb.shape[1]), BF16)], axis=0)
        return jnp.dot(vb, p.astype(BF16), preferred_element_type=F32)

    def start(head, js, st):
        m = jnp.max(st, axis=0, keepdims=True)
        m_refs[head][...] = m
        acc_refs[head][...] = weighted_values(head, js, jnp.exp2(st - m))

    def update(head, js, st):
        m = m_refs[head][...]
        m_new = jnp.maximum(m, jnp.max(st, axis=0, keepdims=True))
        alpha = jnp.exp2(m - m_new)
        m_refs[head][...] = m_new
        acc_refs[head][...] = alpha * acc_refs[head][...] + weighted_values(head, js, jnp.exp2(st - m_new))

    def sweep(js, adds_of, absorb):
        st = scores(0, js, adds_of(0))
        for head in range(ATT_HEADS):
            nxt = scores(head + 1, js, adds_of(head + 1)) if head + 1 < ATT_HEADS else None
            absorb(head, js, st)
            st = nxt

    def far_add(head, j):
        return neg_ref[pl.ds(head * nb + j, 1), :] + far_ref[head]

    @pl.when(i == 0)
    def _():
        sweep([i], lambda h: [bias_ref[h, 0]], start)

    @pl.when(i > 0)
    def _():
        sweep([i, i - 1],
              lambda h: [bias_ref[h, 0], bias_ref[h, 1] + neg_ref[pl.ds(h * nb + i - 1, 1), :]], start)

    n_far = jnp.maximum(i - 1, 0)

    def far_step(first, count):
        js = [first + c for c in range(count)]
        sweep(js, lambda h: [far_add(h, j) for j in js], update)

    def far_quad(jj, carry):
        far_step(4 * jj, 4)
        return carry

    lax.fori_loop(0, n_far // 4, far_quad, 0)
    rest = n_far % 4
    pl.when(rest >= 2)(lambda: far_step(n_far - rest, 2))
    pl.when(rest % 2 == 1)(lambda: far_step(n_far - 1, 1))

    for hp in range(ATT_HEADS // 2):
        outs = [acc_refs[h][0:HEAD_DIM, :] / acc_refs[h][HEAD_DIM:HEAD_DIM + 1, :] for h in (2 * hp, 2 * hp + 1)]
        pair = jnp.concatenate(outs, axis=0)
        o_ref[0, :, hp * pair_w:(hp + 1) * pair_w] = pair.T.astype(BF16)


def _rel_bucket(dist):
    n = jnp.maximum(dist, 0)
    max_exact = REL_BUCKETS // 2
    nf = jnp.maximum(n, 1).astype(F32)
    large = max_exact + (jnp.log(nf / max_exact) / math.log(REL_MAX_DIST / max_exact)
                         * (REL_BUCKETS - max_exact)).astype(jnp.int32)
    large = jnp.minimum(large, REL_BUCKETS - 1)
    return jnp.where(n < max_exact, n, large)


def _moba_bias_tiles(rel_bias):
    blk = MOBA_BLOCK
    span = 2 * blk
    rel_bias = rel_bias * LOG2E
    vec = rel_bias[_rel_bucket(jnp.arange(span))].T.astype(F32)
    masked = jnp.full_like(vec[:, :blk], MASK_NEG)
    ring_own = jnp.concatenate([vec[:, :blk], masked], axis=1)
    ring_prev = jnp.concatenate([vec[:, blk:], vec[:, :blk]], axis=1)

    def toeplitz(ring):
        flat = jnp.tile(ring, (1, blk))[:, :blk * (span - 1)]
        return flat.reshape(-1, blk, span - 1)[:, :, :blk]

    tiles = jnp.stack([toeplitz(ring_own), toeplitz(ring_prev)], axis=1)
    far = rel_bias[_rel_bucket(jnp.array(span))]
    return tiles, far.astype(F32)


def _moba(q4, k4, vt4, bias_tiles, far_bias, bsz):
    nb_total, blk, a = q4.shape
    nb = nb_total // bsz
    return pl.pallas_call(
        _moba_kernel,
        grid=(bsz, nb),
        in_specs=[
            pl.BlockSpec(memory_space=pltpu.SMEM),
            pl.BlockSpec((1, blk, a), lambda b, i: (b * nb + i, 0, 0)),
            pl.BlockSpec((nb, blk, a), lambda b, i: (b, 0, 0)),
            pl.BlockSpec((nb, a, blk), lambda b, i: (b, 0, 0)),
            _full(bias_tiles.shape),
        ],
        out_specs=pl.BlockSpec((1, blk, a), lambda b, i: (b * nb + i, 0, 0)),
        out_shape=jax.ShapeDtypeStruct((nb_total, blk, a), BF16),
        scratch_shapes=[pltpu.VMEM((nb, a), F32), pltpu.VMEM((ATT_HEADS * nb, blk), F32)]
        + [pltpu.VMEM((1, blk), F32), pltpu.VMEM((HEAD_DIM + SUM_ROWS, blk), F32)] * ATT_HEADS,
        compiler_params=_cparams("arbitrary", "arbitrary"),
        name="moba_attention",
    )(far_bias, q4, k4, vt4, bias_tiles)


def _ssm_weights(a_re, a_im, log_dt, b_re, b_im, c_re, c_im, d_skip):
    g, p, cw, n = SSM_GROUPS, SSM_STATE, SSM_GROUP, SSM_CHUNK
    gs = LANES // cw
    nq = g // gs
    npair = n // 2
    pairs_per_set = gs // 2
    lam_re = jnp.minimum(a_re.astype(F32), -1e-4)
    lam_im = a_im.astype(F32)
    dt = jnp.exp(log_dt.astype(F32))[:, None]
    mag = jnp.exp(lam_re * dt)
    lb_re = mag * jnp.cos(lam_im * dt)
    lb_im = mag * jnp.sin(lam_im * dt)
    n_re = lb_re - 1.0
    n_im = lb_im
    den = lam_re * lam_re + lam_im * lam_im
    z_re = ((n_re * lam_re + n_im * lam_im) / den)[..., None]
    z_im = ((n_im * lam_re - n_re * lam_im) / den)[..., None]
    br, bi = b_re.astype(F32), b_im.astype(F32)
    bb_re = z_re * br - z_im * bi
    bb_im = z_re * bi + z_im * br
    pw_re, pw_im = [jnp.ones_like(lb_re)], [jnp.zeros_like(lb_im)]
    for _ in range(n):
        r, im = pw_re[-1], pw_im[-1]
        pw_re.append(r * lb_re - im * lb_im)
        pw_im.append(r * lb_im + im * lb_re)
    pw_re, pw_im = jnp.stack(pw_re), jnp.stack(pw_im)
    cr, ci = c_re.astype(F32), c_im.astype(F32)
    hi = lax.Precision.HIGHEST
    rev_re, rev_im = pw_re[n - 1::-1], pw_im[n - 1::-1]
    sb_re = jnp.einsum('sgp,gpc->sgcp', rev_re, bb_re) - jnp.einsum('sgp,gpc->sgcp', rev_im, bb_im)
    sb_im = jnp.einsum('sgp,gpc->sgcp', rev_re, bb_im) + jnp.einsum('sgp,gpc->sgcp', rev_im, bb_re)
    cl_re = jnp.einsum('gcp,tgp->tgpc', cr, pw_re[1:]) - jnp.einsum('gcp,tgp->tgpc', ci, pw_im[1:])
    cl_im = jnp.einsum('gcp,tgp->tgpc', cr, pw_im[1:]) + jnp.einsum('gcp,tgp->tgpc', ci, pw_re[1:])
    cp_re = jnp.einsum('gcp,tgp->tgcp', cr, pw_re[:n]) - jnp.einsum('gcp,tgp->tgcp', ci, pw_im[:n])
    cp_im = jnp.einsum('gcp,tgp->tgcp', cr, pw_im[:n]) + jnp.einsum('gcp,tgp->tgcp', ci, pw_re[:n])
    kern = (jnp.einsum('tgcp,gpd->gtcd', cp_re, bb_re, precision=hi)
            - jnp.einsum('tgcp,gpd->gtcd', cp_im, bb_im, precision=hi))

    kp = jnp.concatenate([jnp.zeros_like(kern[:, :1]), kern], axis=1)
    dl = jnp.arange(npair)[:, None, None]
    s0 = jnp.arange(2)[None, :, None]
    t0 = jnp.arange(2)[None, None, :]
    lag = 2 * dl + t0 - s0
    kl = kp[:, lag + 1].reshape(nq, gs, npair, 2, 2, cw, cw)
    base = kl.transpose(0, 2, 3, 1, 6, 4, 5)
    same_group = jnp.eye(gs, dtype=F32)
    w_toe = jnp.concatenate(
        [base[..., t0, :] * same_group[:, g1][None, None, None, :, None, None]
         for t0 in range(2) for g1 in range(gs)], axis=-1)
    w_toe = w_toe.reshape(nq, npair, MXU_DIM, MXU_DIM).astype(BF16)

    member = (jnp.arange(gs)[None, :, None]
              == 2 * jnp.arange(pairs_per_set)[:, None, None] + jnp.arange(2)[None, None, :]).astype(F32)

    def pair_tiles(x):
        cols = [x[ri][:, None] * member[:, :, gl][None, :, None, None, :, None, None]
                for ri in range(2) for gl in range(2)]
        return jnp.concatenate(cols, axis=-1).reshape(g // 2, npair, MXU_DIM, MXU_DIM)

    sbs = jnp.stack([sb_re, sb_im]).reshape(2, npair, 2, nq, gs, cw, p)
    w_in_state = pair_tiles(sbs.transpose(0, 3, 1, 2, 4, 5, 6)).astype(BF16)
    sos = jnp.stack([cl_re, -cl_im]).reshape(2, npair, 2, nq, gs, p, cw)
    w_state_out = jnp.swapaxes(pair_tiles(sos.transpose(0, 3, 1, 2, 4, 6, 5)), -1, -2).astype(BF16)
    decay = jnp.stack([pw_re[n].reshape(g // 2, 2 * p), pw_im[n].reshape(g // 2, 2 * p)])
    dvec = d_skip.astype(F32).reshape(1, g * cw)
    return w_toe, w_in_state, w_state_out, decay, dvec


def _ssm_kernel(u_ref, wt_ref, wi_ref, wo_ref, dec_ref, d_ref, y_ref, s_ref, xp_ref, x_ref, *, bsz):
    kt = u_ref.shape[0]
    n = u_ref.shape[1] // bsz
    rows = kt * bsz
    npair = n // 2
    nq = wt_ref.shape[0]
    npairs_g = wi_ref.shape[0]
    per_set = npairs_g // nq
    half = MXU_DIM // 2

    @pl.when(pl.program_id(0) == 0)
    def _():
        x_ref[...] = jnp.zeros_like(x_ref)

    def piece(s, q):
        return u_ref[:, s * bsz:(s + 1) * bsz, q * LANES:(q + 1) * LANES].reshape(rows, LANES)

    lhs = {(sp, q): jnp.concatenate([piece(2 * sp, q), piece(2 * sp + 1, q)], axis=-1)
           for sp in range(npair) for q in range(nq)}

    for gp in range(npairs_g):
        q = gp // per_set
        acc = jnp.dot(lhs[0, q], wi_ref[gp, 0], preferred_element_type=F32)
        for sp in range(1, npair):
            acc = acc + jnp.dot(lhs[sp, q], wi_ref[gp, sp], preferred_element_type=F32)
        s_ref[:, gp * MXU_DIM:(gp + 1) * MXU_DIM] = acc

    for gp in range(npairs_g):
        re_cols = slice(gp * MXU_DIM, gp * MXU_DIM + half)
        im_cols = slice(gp * MXU_DIM + half, (gp + 1) * MXU_DIM)
        ar = dec_ref[0, gp:gp + 1, :]
        ai = dec_ref[1, gp:gp + 1, :]
        xr = x_ref[:, re_cols]
        xi = x_ref[:, im_cols]
        for kc in range(kt):
            rs = slice(kc * bsz, (kc + 1) * bsz)
            xp_ref[rs, re_cols] = xr.astype(BF16)
            xp_ref[rs, im_cols] = xi.astype(BF16)
            sr = s_ref[rs, re_cols]
            si = s_ref[rs, im_cols]
            xr, xi = ar * xr - ai * xi + sr, ar * xi + ai * xr + si
        x_ref[:, re_cols] = xr
        x_ref[:, im_cols] = xi

    for q in range(nq):
        for tp in range(npair):
            acc = jnp.dot(lhs[0, q], wt_ref[q, tp], preferred_element_type=F32)
            for sp in range(1, tp + 1):
                acc = acc + jnp.dot(lhs[sp, q], wt_ref[q, tp - sp], preferred_element_type=F32)
            for gp in range(q * per_set, (q + 1) * per_set):
                acc = acc + jnp.dot(xp_ref[:, gp * MXU_DIM:(gp + 1) * MXU_DIM], wo_ref[gp, tp],
                                    preferred_element_type=F32)
            for t0 in range(2):
                s = 2 * tp + t0
                y = acc[:, t0 * half:(t0 + 1) * half] + d_ref[:, q * LANES:(q + 1) * LANES] * piece(s, q).astype(F32)
                y_ref[:, s * bsz:(s + 1) * bsz, q * LANES:(q + 1) * LANES] = (
                    jax.nn.gelu(y).astype(BF16).reshape(kt, bsz, LANES))


def _ssm(u_tm, weights, bsz, seq):
    n = SSM_CHUNK
    kt = SSM_CHUNKS_PER_STEP
    nchunk = seq // n
    rows = kt * bsz
    state_w = SSM_GROUPS * 2 * SSM_STATE
    u3 = u_tm.reshape(nchunk, n * bsz, SSM_WIDTH)
    blk = pl.BlockSpec((kt, n * bsz, SSM_WIDTH), lambda i: (i, 0, 0))

    once = pl.Buffered(1)
    y3 = pl.pallas_call(
        functools.partial(_ssm_kernel, bsz=bsz),
        grid=(nchunk // kt,),
        in_specs=[blk] + [_full(w.shape, pipeline_mode=once) for w in weights],
        out_specs=blk,
        out_shape=jax.ShapeDtypeStruct(u3.shape, BF16),
        scratch_shapes=[pltpu.VMEM((rows, state_w), F32), pltpu.VMEM((rows, state_w), BF16),
                        pltpu.VMEM((bsz, state_w), F32)],
        compiler_params=_cparams("arbitrary"),
        name="s5_chunked_scan",
    )(u3, *weights)
    return y3.reshape(seq, bsz * SSM_WIDTH)


def _postmix_kernel(att_ref, ys_ref, ga_ref, gs_ref, x_ref, gt_ref, sc_ref, sh_ref, gpost_ref, gpre_ref,
                    wglu_ref, wso_ref, wao_ref, wo_ref, wrh_ref, wrl_ref, br_ref,
                    x1_ref, h2_ref, idx_ref, gate_ref, rank_ref, cnt_ref, run_ref):
    tm = x_ref.shape[0]
    ne = wrh_ref.shape[0]

    @pl.when(pl.program_id(0) == 0)
    def _():
        run_ref[...] = jnp.zeros_like(run_ref)

    halves = [slice(r * (tm // 2), (r + 1) * (tm // 2)) for r in range(2)]
    dot = functools.partial(jnp.dot, preferred_element_type=F32)
    att = [dot(att_ref[r, :], wao_ref[...]) for r in halves]
    glu = [dot(ys_ref[r, :], wglu_ref[...]) for r in halves]
    ssm = []
    for g in glu:
        sg = (g[:, :SSM_WIDTH] * jax.nn.sigmoid(g[:, SSM_WIDTH:])).astype(BF16)
        ssm.append(dot(sg, wso_ref[...]))
    y = []
    for r, a, s in zip(halves, att, ssm):
        merged = (ga_ref[r, :].astype(F32) * a + gs_ref[r, :].astype(F32) * s).astype(BF16)
        y.append(dot(merged, wo_ref[...]))
    logits = []
    for r, yh in zip(halves, y):
        x1 = x_ref[r, :] + gt_ref[0] * _rms(yh, gpost_ref[...])
        x1_ref[r, :] = x1
        h2 = _rms(x1, gpre_ref[...]) * (1.0 + sc_ref[0]) + sh_ref[0]
        h2_ref[r, :] = _pack_bf16_pairs(h2)
        h_hi = h2.astype(BF16)
        h_lo = (h2 - h_hi.astype(F32)).astype(BF16)
        logits.append(lax.dot_general(wrh_ref[...], h_hi, _NT, preferred_element_type=F32)
                      + lax.dot_general(wrh_ref[...], h_lo, _NT, preferred_element_type=F32)
                      + lax.dot_general(wrl_ref[...], h_hi, _NT, preferred_element_type=F32))
    logits = jnp.concatenate(logits, axis=1) + br_ref[...]
    eidx = lax.broadcasted_iota(jnp.int32, (ne, tm), 0)
    lg = logits
    vals, idxs = [], []
    for _ in range(TOP_K):
        mx = jnp.max(lg, axis=0, keepdims=True)
        first = jnp.min(jnp.where(lg == mx, eidx, ne), axis=0, keepdims=True)
        vals.append(mx)
        idxs.append(first)
        lg = jnp.where(eidx == first, -jnp.inf, lg)
    exps = [jnp.exp(v - vals[0]) for v in vals]
    denom = exps[0] + exps[1] + exps[2] + exps[3]
    gate_ref[...] = jnp.concatenate([e / denom for e in exps], axis=0)
    idx_ref[...] = jnp.concatenate(idxs, axis=0)
    onehot = jnp.where(lg == -jnp.inf, 1.0, 0.0)
    t_row = lax.broadcasted_iota(jnp.int32, (tm, tm), 0)
    t_col = lax.broadcasted_iota(jnp.int32, (tm, tm), 1)
    earlier = jnp.where(t_row < t_col, 1.0, 0.0).astype(BF16)
    before = jnp.dot(onehot.astype(BF16), earlier, preferred_element_type=F32) + run_ref[:, 0:1]
    ranks = [jnp.sum(jnp.where(eidx == ix, before, 0.0), axis=0, keepdims=True) for ix in idxs]
    rank_ref[...] = jnp.concatenate(ranks, axis=0).astype(jnp.int32)
    run_ref[...] = run_ref[...] + jnp.sum(onehot, axis=1, keepdims=True)
    cnt_ref[...] = run_ref[...].astype(jnp.int32)


def _postmix(att2, ys_tm, ga, gs, x2, gt1, sc2, sh2, g_post, g_pre, w_glu, w_ssm_out, w_att_out, w_out,
             w_router, b_router, seq):
    t, d = x2.shape
    tm = TOKEN_TILE
    tiles_per_seq = seq // tm
    ne = w_router.shape[1]
    wr_t = w_router.T.astype(F32)
    wr_hi = wr_t.astype(BF16)
    wr_lo = (wr_t - wr_hi.astype(F32)).astype(BF16)
    mod_spec = pl.BlockSpec((1, 1, d), lambda i: (i // tiles_per_seq, 0, 0))
    row = lambda cols: pl.BlockSpec((tm, cols), lambda i: (i, 0))
    time_major = pl.BlockSpec((tm, SSM_WIDTH), lambda i: (i % tiles_per_seq, i // tiles_per_seq))
    col = pl.BlockSpec((TOP_K, tm), lambda i: (0, i))
    weights = [w_glu.astype(BF16), w_ssm_out.astype(BF16), w_att_out.astype(BF16), w_out.astype(BF16),
               wr_hi, wr_lo, b_router.astype(F32).reshape(ne, 1)]
    return pl.pallas_call(
        _postmix_kernel,
        grid=(t // tm,),
        in_specs=[row(ATT_WIDTH), time_major, row(d), row(d), row(d), mod_spec, mod_spec, mod_spec,
                  _full((1, d)), _full((1, d))]
        + [_full(w.shape, pipeline_mode=pl.Buffered(1)) for w in weights],
        out_specs=[row(d), row(d // 2), col, col, col, _full((ne, 128))],
        out_shape=[jax.ShapeDtypeStruct((t, d), F32), jax.ShapeDtypeStruct((t, d // 2), jnp.uint32),
                   jax.ShapeDtypeStruct((TOP_K, t), jnp.int32), jax.ShapeDtypeStruct((TOP_K, t), F32),
                   jax.ShapeDtypeStruct((TOP_K, t), jnp.int32), jax.ShapeDtypeStruct((ne, 128), jnp.int32)],
        scratch_shapes=[pltpu.VMEM((ne, 128), F32)],
        compiler_params=_cparams("arbitrary"),
        name="postmix_router",
    )(att2, ys_tm, ga, gs, x2, gt1, sc2, sh2, g_post.reshape(1, d), g_pre.reshape(1, d), *weights)


def _sc_worker_base(per_worker):
    return (lax.axis_index("s") * SC_CORES + lax.axis_index("c")) * per_worker


def _sc_mesh():
    return plsc.VectorSubcoreMesh(core_axis_name="c", subcore_axis_name="s",
                                  num_cores=SC_CORES, num_subcores=SC_SUBCORES)


def _sc_scatter_kernel(rows_hbm, idx_hbm, out_hbm, idx_v, rows_v, load_sems, store_sems, *, per_worker):
    chunk = rows_v.shape[1]
    n_tokens = rows_hbm.shape[0]
    n_chunks = per_worker // chunk
    base = _sc_worker_base(per_worker)

    def load(c, b):
        off = pl.multiple_of(base + c * chunk, chunk)
        for k in range(TOP_K):
            pltpu.sync_copy(idx_hbm.at[pl.ds(k * n_tokens + off, chunk)], idx_v.at[b, k])
        return pltpu.make_async_copy(rows_hbm.at[pl.ds(off, chunk)], rows_v.at[b], load_sems.at[b])

    def stores(b):
        return [pltpu.make_async_copy(rows_v.at[b], out_hbm.at[idx_v.at[b, k]], store_sems.at[b])
                for k in range(TOP_K)]

    load(0, 0).start()

    @pl.loop(0, n_chunks, step=2)
    def _(c0):
        for b in range(2):
            c = c0 + b

            @pl.when(c > 0)
            def _():
                for cp in stores(1 - b):
                    cp.wait()

            @pl.when(c + 1 < n_chunks)
            def _():
                load(c + 1, 1 - b).start()

            pltpu.make_async_copy(rows_hbm.at[pl.ds(0, chunk)], rows_v.at[b], load_sems.at[b]).wait()
            for cp in stores(b):
                cp.start()

    for cp in stores((n_chunks - 1) % 2):
        cp.wait()


def _sc_scatter_rows(rows, idx, n_out):
    t, d = rows.shape
    workers = SC_CORES * SC_SUBCORES
    per_worker = t // workers
    assert t % workers == 0 and per_worker % (2 * SC_GATHER_ROWS) == 0
    return pl.kernel(
        functools.partial(_sc_scatter_kernel, per_worker=per_worker),
        out_type=jax.ShapeDtypeStruct((n_out, d), rows.dtype),
        mesh=_sc_mesh(),
        scratch_types=[pltpu.VMEM((2, TOP_K, SC_GATHER_ROWS), jnp.int32),
                       pltpu.VMEM((2, SC_GATHER_ROWS, d), rows.dtype),
                       pltpu.SemaphoreType.DMA((2,)), pltpu.SemaphoreType.DMA((2,))],
        name="sc_row_scatter",
    )(rows, idx)


def _sc_gather_kernel(table_hbm, idx_hbm, out_hbm, idx_v, rows_v, sems, *, per_worker):
    chunk = rows_v.shape[1]
    n_chunks = per_worker // chunk
    base = _sc_worker_base(per_worker)

    def gather(c, b):
        off = pl.multiple_of(base + c * chunk, chunk)
        pltpu.sync_copy(idx_hbm.at[pl.ds(off, chunk)], idx_v.at[b])
        return pltpu.make_async_copy(table_hbm.at[idx_v.at[b]], rows_v.at[b], sems.at[b])

    gather(0, 0).start()

    @pl.loop(0, n_chunks, step=2)
    def _(c0):
        for b in range(2):
            c = c0 + b

            @pl.when(c + 1 < n_chunks)
            def _():
                gather(c + 1, 1 - b).start()

            pltpu.make_async_copy(table_hbm.at[idx_v.at[b]], rows_v.at[b], sems.at[b]).wait()
            off = pl.multiple_of(base + c * chunk, chunk)
            pltpu.sync_copy(rows_v.at[b], out_hbm.at[pl.ds(off, chunk)])


def _sc_gather_rows(table, idx):
    n = idx.shape[0]
    d = table.shape[1]
    workers = SC_CORES * SC_SUBCORES
    per_worker = n // workers
    assert n % workers == 0 and per_worker % (2 * SC_GATHER_ROWS) == 0
    return pl.kernel(
        functools.partial(_sc_gather_kernel, per_worker=per_worker),
        out_type=jax.ShapeDtypeStruct((n, d), table.dtype),
        mesh=_sc_mesh(),
        scratch_types=[pltpu.VMEM((2, SC_GATHER_ROWS), jnp.int32),
                       pltpu.VMEM((2, SC_GATHER_ROWS, d), table.dtype),
                       pltpu.SemaphoreType.DMA((2,))],
        name="sc_row_gather",
    )(table, idx)


def _experts_kernel(be_ref, nused_ref, valid_ref, next_ref, x_ref, b1_ref, b2_ref, w1_hbm, w2_hbm, y_ref,
                    w1s_ref, w2s_ref, w1b_ref, w2b_ref, sems, *, layer):
    i = pl.program_id(0)
    prev = be_ref[jnp.maximum(i - 1, 0)]
    fresh = (i < nused_ref[0]) & ((i == 0) | (be_ref[i] != prev))

    def fetch(e):
        return (pltpu.make_async_copy(w1_hbm.at[layer, e], w1s_ref, sems.at[0]),
                pltpu.make_async_copy(w2_hbm.at[layer, e], w2s_ref, sems.at[1]))

    @pl.when(i == 0)
    def _():
        for cp in fetch(be_ref[0]):
            cp.start()

    @pl.when(fresh)
    def _():
        for cp in fetch(be_ref[i]):
            cp.wait()
        w1b_ref[...] = w1s_ref[...].astype(BF16)
        w2b_ref[...] = w2s_ref[...].astype(BF16)

        @pl.when(next_ref[i] >= 0)
        def _():
            for cp in fetch(next_ref[i]):
                cp.start()

    used = i < nused_ref[0]
    valid = valid_ref[i]
    half = x_ref.shape[0] // 2

    def ffn(n_rows):
        row = lax.broadcasted_iota(jnp.int32, (n_rows, x_ref.shape[1]), 0)
        x_lo, x_hi = _unpack_bf16_pairs(jnp.where(row < valid, x_ref[0:n_rows, :], jnp.uint32(0)))
        x = jnp.concatenate([x_lo.astype(BF16), x_hi.astype(BF16)], axis=1)
        gu = jnp.dot(x, w1b_ref[...], preferred_element_type=F32) + b1_ref[...]
        g = jnp.minimum(gu[:, :D_FF], SWIGLU_LIMIT)
        up = jnp.clip(gu[:, D_FF:], -SWIGLU_LIMIT, SWIGLU_LIMIT)
        act = ((up + 1.0) * g * jax.nn.sigmoid(SWIGLU_ALPHA * g)).astype(BF16)
        y = jnp.dot(act, w2b_ref[...], preferred_element_type=F32) + b2_ref[...]
        y_ref[0:n_rows, :] = _pack_bf16_pairs(y)

    @pl.when(used & (valid > half))
    def _():
        ffn(2 * half)

    @pl.when(used & (valid <= half))
    def _():
        ffn(half)
        y_ref[half:, :] = jnp.zeros((half, y_ref.shape[1]), y_ref.dtype)

    @pl.when(jnp.logical_not(used))
    def _():
        y_ref[...] = jnp.zeros_like(y_ref)


def _experts(xb, block_e, n_used, valid, next_e, w1, b1, w2, b2, layer):
    p_rows, packed_w = xb.shape
    d = 2 * packed_w
    depth, ne = w1.shape[:2]
    rb = EXPERT_ROWS
    bmap = lambda i, be, nu, va, nx: (layer, be[i], 0, 0)
    rows = pl.BlockSpec((rb, packed_w), lambda i, be, nu, va, nx: (i, 0))
    grid_spec = pltpu.PrefetchScalarGridSpec(
        num_scalar_prefetch=4,
        grid=(p_rows // rb,),
        in_specs=[
            rows,
            pl.BlockSpec((None, None, 1, 2 * D_FF), bmap),
            pl.BlockSpec((None, None, 1, d), bmap),
            pl.BlockSpec(memory_space=pl.ANY),
            pl.BlockSpec(memory_space=pl.ANY),
        ],
        out_specs=rows,
        scratch_shapes=[pltpu.VMEM((d, 2 * D_FF), w1.dtype), pltpu.VMEM((D_FF, d), w2.dtype),
                        pltpu.VMEM((d, 2 * D_FF), BF16), pltpu.VMEM((D_FF, d), BF16),
                        pltpu.SemaphoreType.DMA((2,))],
    )
    return pl.pallas_call(
        functools.partial(_experts_kernel, layer=layer),
        grid_spec=grid_spec,
        out_shape=jax.ShapeDtypeStruct((p_rows, packed_w), jnp.uint32),
        compiler_params=_cparams("arbitrary"),
        name="expert_ffn",
    )(block_e, n_used, valid, next_e, xb, b1.reshape(depth, ne, 1, 2 * D_FF), b2.reshape(depth, ne, 1, d), w1, w2)


def _combine_kernel(gate_ref, x_ref, gt_ref, g_ref, y0_ref, y1_ref, y2_ref, y3_ref, o_ref):
    tm = x_ref.shape[0]
    gates = gate_ref[...]
    gates = jnp.concatenate([gates, jnp.zeros((LANES - TOP_K, tm), F32)], axis=0).T
    lo, hi = _unpack_bf16_pairs(y0_ref[...])
    y_lo, y_hi = gates[:, 0:1] * lo, gates[:, 0:1] * hi
    for k, y_ref in enumerate((y1_ref, y2_ref, y3_ref), start=1):
        lo, hi = _unpack_bf16_pairs(y_ref[...])
        y_lo, y_hi = y_lo + gates[:, k:k + 1] * lo, y_hi + gates[:, k:k + 1] * hi
    y = jnp.concatenate([y_lo, y_hi], axis=1)
    o_ref[...] = x_ref[...] + gt_ref[0] * _rms(y, g_ref[...])


def _combine(y4, gate_t, x2, gt2, g_post, seq):
    t, d = x2.shape
    tm = TOKEN_TILE
    tiles = t // tm
    tiles_per_seq = seq // tm
    row = pl.BlockSpec((tm, d), lambda i: (i, 0))
    slot = lambda k: pl.BlockSpec((tm, y4.shape[1]), lambda i: (k * tiles + i, 0))
    return pl.pallas_call(
        _combine_kernel,
        grid=(tiles,),
        in_specs=[pl.BlockSpec((TOP_K, tm), lambda i: (0, i)),
                  row, pl.BlockSpec((1, 1, d), lambda i: (i // tiles_per_seq, 0, 0)), _full((1, d))]
        + [slot(k) for k in range(TOP_K)],
        out_specs=row,
        out_shape=jax.ShapeDtypeStruct((t, d), F32),
        compiler_params=_cparams("arbitrary"),
        name="expert_combine",
    )(gate_t, x2, gt2, g_post.reshape(1, d), y4, y4, y4, y4)


def _route_plan(idx_t, rank_t, counts):
    rb = EXPERT_ROWS
    k, t = idx_t.shape
    padded = (counts + rb - 1) // rb * rb
    pad_ends = jnp.cumsum(padded)
    pad_starts = pad_ends - padded
    experts = jnp.arange(N_EXPERTS, dtype=jnp.int32)
    start_of = jnp.sum(jnp.where(idx_t[None] == experts[:, None, None], pad_starts[:, None, None], 0), axis=0)
    dest = (start_of + rank_t).astype(jnp.int32)
    n_blocks = (k * t) // rb + N_EXPERTS
    blk_start = jnp.arange(n_blocks, dtype=jnp.int32) * rb
    block_e = jnp.minimum(jnp.sum(pad_ends[None, :] <= blk_start[:, None], axis=1), N_EXPERTS - 1)
    onehot_e = block_e[:, None] == experts[None, :]
    cnt_b = jnp.sum(jnp.where(onehot_e, counts[None, :], 0), axis=1)
    start_b = jnp.sum(jnp.where(onehot_e, pad_starts[None, :], 0), axis=1)
    valid = jnp.clip(cnt_b - (blk_start - start_b), 0, rb).astype(jnp.int32)
    n_used = (pad_ends[-1] // rb).astype(jnp.int32).reshape(1)
    later_nonempty = (experts[None, :] > experts[:, None]) & (counts[None, :] > 0)
    next_nonempty = jnp.min(jnp.where(later_nonempty, experts[None, :], N_EXPERTS), axis=1)
    next_nonempty = jnp.where(next_nonempty == N_EXPERTS, -1, next_nonempty)
    next_e = jnp.sum(jnp.where(onehot_e, next_nonempty[None, :], 0), axis=1).astype(jnp.int32)
    return dest.reshape(-1), block_e.astype(jnp.int32), n_used, valid, next_e, n_blocks * rb


def kernel(x, c, rel_bias, w_ada, b_ada, g_pre_mix, g_post_mix, g_pre_ffn, g_post_ffn, w_in, ssm_a_re, ssm_a_im, ssm_log_dt, ssm_b_re, ssm_b_im, ssm_c_re, ssm_c_im, ssm_d, w_glu, w_ssm_out, w_att_out, w_out, w_router, b_router, w_exp_in, b_exp_in, w_exp_out, b_exp_out):
    bsz, seq, d = x.shape
    depth = w_ada.shape[0]
    t = bsz * seq
    assert d == D_MODEL and seq % TOKEN_TILE == 0 and TOKEN_TILE % MOBA_BLOCK == 0
    assert seq % (SSM_CHUNK * SSM_CHUNKS_PER_STEP) == 0

    mod = _ada_mod(c, w_ada, b_ada)
    bias_tiles, far_bias = _moba_bias_tiles(rel_bias.astype(F32))
    x2 = x.reshape(t, d)
    for l in range(depth):
        sh1, sc1, gt1, sh2, sc2, gt2 = [m.reshape(bsz, 1, d) for m in jnp.split(mod[l], N_MOD, axis=-1)]
        q4, k4, vt4, u_tm, ga, gs = _premix(x2, g_pre_mix[l], sc1, sh1, w_in[l], seq)
        att = _moba(q4, k4, vt4, bias_tiles, far_bias, bsz).reshape(t, ATT_WIDTH)
        ssm_w = _ssm_weights(ssm_a_re[l], ssm_a_im[l], ssm_log_dt[l], ssm_b_re[l], ssm_b_im[l],
                             ssm_c_re[l], ssm_c_im[l], ssm_d[l])
        ys_tm = _ssm(u_tm, ssm_w, bsz, seq)
        x1, h2, idx_t, gate_t, rank_t, cnt = _postmix(
            att, ys_tm, ga, gs, x2, gt1, sc2, sh2, g_post_mix[l], g_pre_ffn[l],
            w_glu[l], w_ssm_out[l], w_att_out[l], w_out[l], w_router[l], b_router[l], seq)
        dest_flat, block_e, n_used, valid, next_e, p_rows = _route_plan(idx_t, rank_t, cnt[:, 0])
        xb = _sc_scatter_rows(h2, dest_flat, p_rows)
        yb = _experts(xb, block_e, n_used, valid, next_e, w_exp_in, b_exp_in, w_exp_out, b_exp_out, l)
        y4 = _sc_gather_rows(yb, dest_flat)
        x2 = _combine(y4, gate_t, x1, gt2, g_post_ffn[l], seq)
    return x2.reshape(bsz, seq, d)
```

```python
import functools
import math

import jax
import jax.numpy as jnp
from jax import lax
from jax.experimental import pallas as pl
from jax.experimental.pallas import tpu as pltpu
from jax.experimental.pallas import tpu_sc as plsc

F32 = jnp.float32
BF16 = jnp.bfloat16

D_MODEL = 1024
ATT_HEADS = 8
HEAD_DIM = 64
ATT_WIDTH = ATT_HEADS * HEAD_DIM
MOBA_BLOCK = 256
MOBA_TOPK = 3
REL_BUCKETS = 32
REL_MAX_DIST = 128
SSM_WIDTH = D_MODEL // 2
SSM_GROUP = 16
SSM_GROUPS = SSM_WIDTH // SSM_GROUP
SSM_STATE = 64
N_EXPERTS = 32
TOP_K = 4
D_FF = D_MODEL
SWIGLU_ALPHA = 1.702
SWIGLU_LIMIT = 7.0
RMS_EPS = 1e-6
N_MOD = 6

SSM_CHUNK = 8
SSM_CHUNKS_PER_STEP = 16
LANES = 128
MXU_DIM = 256
TOKEN_TILE = 1024
SC_CORES = 2
SC_SUBCORES = 16
SC_GATHER_ROWS = 64
EXPERT_ROWS = 512
MASK_NEG = -1e30
LOG2E = math.log2(math.e)
SUM_ROWS = 16
VMEM_LIMIT = 56 * 1024 * 1024

_NT = (((1,), (1,)), ((), ()))


def _cparams(*sem):
    return pltpu.CompilerParams(dimension_semantics=sem, vmem_limit_bytes=VMEM_LIMIT)


def _pack_bf16_pairs(x):
    n = x.shape[1] // 2
    bits = lax.bitcast_convert_type(x.astype(BF16).astype(F32), jnp.uint32)
    return lax.shift_right_logical(bits[:, :n], jnp.uint32(16)) | (bits[:, n:] & jnp.uint32(0xFFFF0000))


def _unpack_bf16_pairs(w):
    lo = lax.bitcast_convert_type(lax.shift_left(w, jnp.uint32(16)), F32)
    hi = lax.bitcast_convert_type(w & jnp.uint32(0xFFFF0000), F32)
    return lo, hi


def _rms(x, g):
    return x * lax.rsqrt(jnp.mean(x * x, axis=-1, keepdims=True) + RMS_EPS) * g


def _full(shape, **kw):
    n = len(shape)
    return pl.BlockSpec(shape, lambda *_: (0,) * n, **kw)


def _ada_kernel(c_ref, w_ref, b_ref, o_ref):
    c = c_ref[...]
    cond = c * jax.nn.sigmoid(c)
    o_ref[0] = jnp.dot(cond, w_ref[0], preferred_element_type=F32,
                       precision=lax.Precision.HIGHEST) + b_ref[0]


def _ada_mod(c, w_ada, b_ada):
    depth, d, nd = w_ada.shape
    bsz = c.shape[0]
    return pl.pallas_call(
        _ada_kernel,
        grid=(depth, nd // d),
        in_specs=[
            pl.BlockSpec((bsz, d), lambda l, j: (0, 0)),
            pl.BlockSpec((1, d, d), lambda l, j: (l, 0, j)),
            pl.BlockSpec((1, 1, d), lambda l, j: (l, 0, j)),
        ],
        out_specs=pl.BlockSpec((1, bsz, d), lambda l, j: (l, 0, j)),
        out_shape=jax.ShapeDtypeStruct((depth, bsz, nd), F32),
        compiler_params=_cparams("arbitrary", "arbitrary"),
        name="ada_mod",
    )(c, w_ada, b_ada.reshape(depth, 1, nd))


def _premix_kernel(x_ref, g_ref, sc_ref, sh_ref, wq_ref, wk_ref, wvt_ref, wu_ref, wga_ref, wgs_ref,
                   q_ref, k_ref, vt_ref, u_ref, ga_ref, gs_ref):
    x = x_ref[...]
    h = _rms(x, g_ref[...]) * (1.0 + sc_ref[0]) + sh_ref[0]
    hb = h.astype(BF16)
    nblk = q_ref.shape[0]
    q = (jnp.dot(hb, wq_ref[...], preferred_element_type=F32) * (HEAD_DIM ** -0.5 * LOG2E)).astype(BF16)
    k = jnp.dot(hb, wk_ref[...], preferred_element_type=F32).astype(BF16)
    vt = lax.dot_general(wvt_ref[...], hb, _NT, preferred_element_type=F32).astype(BF16)
    for r in range(nblk):
        q_ref[r] = q[r * MOBA_BLOCK:(r + 1) * MOBA_BLOCK]
        k_ref[r] = k[r * MOBA_BLOCK:(r + 1) * MOBA_BLOCK]
        vt_ref[r] = vt[:, r * MOBA_BLOCK:(r + 1) * MOBA_BLOCK]
    u_ref[...] = jnp.dot(hb, wu_ref[...], preferred_element_type=F32).astype(BF16)
    ga = jnp.dot(hb, wga_ref[...], preferred_element_type=F32)
    ga_ref[...] = jax.nn.sigmoid(ga).astype(BF16)
    gs = jnp.dot(hb, wgs_ref[...], preferred_element_type=F32)
    gs_ref[...] = jax.nn.sigmoid(gs).astype(BF16)


def _premix(x2, g, sc, sh, w_in, seq):
    t, d = x2.shape
    tm = TOKEN_TILE
    tiles_per_seq = seq // tm
    nblk = tm // MOBA_BLOCK
    a = ATT_WIDTH
    assert a == SSM_WIDTH and d == 2 * a and w_in.shape[1] == 4 * a + 2 * d
    wb = w_in.astype(BF16)
    wvt = wb[:, 2 * a:3 * a].T

    def cols(width, index):
        return pl.BlockSpec((d, width), lambda i: (0, index), pipeline_mode=pl.Buffered(1))

    mod_spec = pl.BlockSpec((1, 1, d), lambda i: (i // tiles_per_seq, 0, 0))
    blk3 = lambda rows, cols: pl.BlockSpec((nblk, rows, cols), lambda i: (i, 0, 0))
    row = lambda cols: pl.BlockSpec((tm, cols), lambda i: (i, 0))
    time_major = pl.BlockSpec((tm, SSM_WIDTH), lambda i: (i % tiles_per_seq, i // tiles_per_seq))
    once = functools.partial(_full, pipeline_mode=pl.Buffered(1))
    nb_total = t // MOBA_BLOCK
    return pl.pallas_call(
        _premix_kernel,
        grid=(t // tm,),
        in_specs=[row(d), _full((1, d)), mod_spec, mod_spec,
                  cols(a, 0), cols(a, 1), once((a, d)), cols(a, 3), cols(d, 2), cols(d, 3)],
        out_specs=[blk3(MOBA_BLOCK, a), blk3(MOBA_BLOCK, a), blk3(a, MOBA_BLOCK),
                   time_major, row(d), row(d)],
        out_shape=[jax.ShapeDtypeStruct((nb_total, MOBA_BLOCK, a), BF16),
                   jax.ShapeDtypeStruct((nb_total, MOBA_BLOCK, a), BF16),
                   jax.ShapeDtypeStruct((nb_total, a, MOBA_BLOCK), BF16),
                   jax.ShapeDtypeStruct((seq, (t // seq) * SSM_WIDTH), BF16),
                   jax.ShapeDtypeStruct((t, d), BF16),
                   jax.ShapeDtypeStruct((t, d), BF16)],
        compiler_params=_cparams("arbitrary"),
        name="premix_inproj",
    )(x2, g.reshape(1, d), sc, sh, wb, wb, wvt, wb, wb, wb)


def _moba_kernel(far_ref, q_ref, k_ref, vt_ref, bias_ref, o_ref, kmean_ref, neg_ref, *state_refs):
    m_refs = state_refs[0::2]
    acc_refs = state_refs[1::2]
    nb = k_ref.shape[0]
    blk = MOBA_BLOCK
    i = pl.program_id(1)

    @pl.when(i == 0)
    def _():
        for n in range(nb):
            kmean_ref[n:n + 1, :] = jnp.mean(k_ref[n].astype(F32), axis=0, keepdims=True)

    lane = lax.broadcasted_iota(jnp.int32, (blk, 2 * HEAD_DIM), 1)
    nidx = lax.broadcasted_iota(jnp.int32, (nb, blk), 0)
    valid = nidx < i
    pair_w = 2 * HEAD_DIM

    def pair_cols(head):
        return slice((head // 2) * pair_w, (head // 2 + 1) * pair_w)

    def head_rows(head):
        return slice(head * HEAD_DIM, (head + 1) * HEAD_DIM)

    def q_head(head):
        q2 = q_ref[0, :, pair_cols(head)]
        lo = (head % 2) * HEAD_DIM
        return jnp.where((lane >= lo) & (lane < lo + HEAD_DIM), q2, jnp.zeros_like(q2))

    for head in range(ATT_HEADS):
        km = kmean_ref[:, pair_cols(head)]
        km_hi = km.astype(BF16)
        km_lo = (km - km_hi.astype(F32)).astype(BF16)
        qh = q_head(head)
        gate = (lax.dot_general(km_hi, qh, _NT, preferred_element_type=F32)
                + lax.dot_general(km_lo, qh, _NT, preferred_element_type=F32))
        g = jnp.where(valid, gate, -jnp.inf)
        sel = jnp.zeros((nb, blk), jnp.bool_)
        for _ in range(MOBA_TOPK):
            mx = jnp.max(g, axis=0, keepdims=True)
            first = jnp.min(jnp.where(g == mx, nidx, nb), axis=0, keepdims=True)
            pick = nidx == first
            sel = sel | pick
            g = jnp.where(pick, -jnp.inf, g)
        neg_ref[head * nb:(head + 1) * nb, :] = jnp.where(sel & valid, 0.0, MASK_NEG)

    def scores(head, js, adds):
        qh = q_head(head)
        parts = [lax.dot_general(k_ref[j, :, pair_cols(head)], qh, _NT, preferred_element_type=F32) + a
                 for j, a in zip(js, adds)]
        return parts[0] if len(parts) == 1 else jnp.concatenate(parts, axis=0)

    def weighted_values(head, js, p):
        vals = [vt_ref[j, head_rows(head), :] for j in js]
        vb = vals[0] if len(vals) == 1 else jnp.concatenate(vals, axis=1)
        vb = jnp.concatenate([vb, jnp.ones((SUM_ROWS, vb.shape[1]), BF16)], axis=0)
        return jnp.dot(vb, p.astype(BF16), preferred_element_type=F32)

    def start(head, js, st):
        m = jnp.max(st, axis=0, keepdims=True)
        m_refs[head][...] = m
        acc_refs[head][...] = weighted_values(head, js, jnp.exp2(st - m))

    def update(head, js, st):
        m = m_refs[head][...]
        m_new = jnp.maximum(m, jnp.max(st, axis=0, keepdims=True))
        alpha = jnp.exp2(m - m_new)
        m_refs[head][...] = m_new
        acc_refs[head][...] = alpha * acc_refs[head][...] + weighted_values(head, js, jnp.exp2(st - m_new))

    def sweep(js, adds_of, absorb):
        st = scores(0, js, adds_of(0))
        for head in range(ATT_HEADS):
            nxt = scores(head + 1, js, adds_of(head + 1)) if head + 1 < ATT_HEADS else None
            absorb(head, js, st)
            st = nxt

    def far_add(head, j):
        return neg_ref[pl.ds(head * nb + j, 1), :] + far_ref[head]

    def near_adds(h):
        return [bias_ref[h, 0], bias_ref[h, 1] + neg_ref[pl.ds(h * nb + i - 1, 1), :]]

    @pl.when(i == 0)
    def _():
        sweep([i], lambda h: [bias_ref[h, 0]], start)

    @pl.when((i == 1) | (i == 2))
    def _():
        sweep([i, i - 1], near_adds, start)

    @pl.when(i >= 3)
    def _():
        sweep([i, i - 1, i - 2, i - 3],
              lambda h: near_adds(h) + [far_add(h, i - 2), far_add(h, i - 3)], start)

    n_far = jnp.where(i >= 3, i - 3, jnp.maximum(i - 1, 0))

    def far_step(first, count):
        js = [first + c for c in range(count)]
        sweep(js, lambda h: [far_add(h, j) for j in js], update)

    def far_quad(jj, carry):
        far_step(4 * jj, 4)
        return carry

    lax.fori_loop(0, n_far // 4, far_quad, 0)
    rest = n_far % 4
    pl.when(rest >= 2)(lambda: far_step(n_far - rest, 2))
    pl.when(rest % 2 == 1)(lambda: far_step(n_far - 1, 1))

    for hp in range(ATT_HEADS // 2):
        outs = [acc_refs[h][0:HEAD_DIM, :] / acc_refs[h][HEAD_DIM:HEAD_DIM + 1, :] for h in (2 * hp, 2 * hp + 1)]
        pair = jnp.concatenate(outs, axis=0)
        o_ref[0, :, hp * pair_w:(hp + 1) * pair_w] = pair.T.astype(BF16)


def _rel_bucket(dist):
    n = jnp.maximum(dist, 0)
    max_exact = REL_BUCKETS // 2
    nf = jnp.maximum(n, 1).astype(F32)
    large = max_exact + (jnp.log(nf / max_exact) / math.log(REL_MAX_DIST / max_exact)
                         * (REL_BUCKETS - max_exact)).astype(jnp.int32)
    large = jnp.minimum(large, REL_BUCKETS - 1)
    return jnp.where(n < max_exact, n, large)


def _moba_bias_tiles(rel_bias):
    blk = MOBA_BLOCK
    span = 2 * blk
    rel_bias = rel_bias * LOG2E
    vec = rel_bias[_rel_bucket(jnp.arange(span))].T.astype(F32)
    masked = jnp.full_like(vec[:, :blk], MASK_NEG)
    ring_own = jnp.concatenate([vec[:, :blk], masked], axis=1)
    ring_prev = jnp.concatenate([vec[:, blk:], vec[:, :blk]], axis=1)

    def toeplitz(ring):
        flat = jnp.tile(ring, (1, blk))[:, :blk * (span - 1)]
        return flat.reshape(-1, blk, span - 1)[:, :, :blk]

    tiles = jnp.stack([toeplitz(ring_own), toeplitz(ring_prev)], axis=1)
    far = rel_bias[_rel_bucket(jnp.array(span))]
    return tiles, far.astype(F32)


def _moba(q4, k4, vt4, bias_tiles, far_bias, bsz):
    nb_total, blk, a = q4.shape
    nb = nb_total // bsz
    return pl.pallas_call(
        _moba_kernel,
        grid=(bsz, nb),
        in_specs=[
            pl.BlockSpec(memory_space=pltpu.SMEM),
            pl.BlockSpec((1, blk, a), lambda b, i: (b * nb + i, 0, 0)),
            pl.BlockSpec((nb, blk, a), lambda b, i: (b, 0, 0)),
            pl.BlockSpec((nb, a, blk), lambda b, i: (b, 0, 0)),
            _full(bias_tiles.shape),
        ],
        out_specs=pl.BlockSpec((1, blk, a), lambda b, i: (b * nb + i, 0, 0)),
        out_shape=jax.ShapeDtypeStruct((nb_total, blk, a), BF16),
        scratch_shapes=[pltpu.VMEM((nb, a), F32), pltpu.VMEM((ATT_HEADS * nb, blk), F32)]
        + [pltpu.VMEM((1, blk), F32), pltpu.VMEM((HEAD_DIM + SUM_ROWS, blk), F32)] * ATT_HEADS,
        compiler_params=_cparams("arbitrary", "arbitrary"),
        name="moba_attention",
    )(far_bias, q4, k4, vt4, bias_tiles)


def _ssm_weights(a_re, a_im, log_dt, b_re, b_im, c_re, c_im, d_skip):
    g, p, cw, n = SSM_GROUPS, SSM_STATE, SSM_GROUP, SSM_CHUNK
    gs = LANES // cw
    nq = g // gs
    npair = n // 2
    pairs_per_set = gs // 2
    lam_re = jnp.minimum(a_re.astype(F32), -1e-4)
    lam_im = a_im.astype(F32)
    dt = jnp.exp(log_dt.astype(F32))[:, None]
    mag = jnp.exp(lam_re * dt)
    lb_re = mag * jnp.cos(lam_im * dt)
    lb_im = mag * jnp.sin(lam_im * dt)
    n_re = lb_re - 1.0
    n_im = lb_im
    den = lam_re * lam_re + lam_im * lam_im
    z_re = ((n_re * lam_re + n_im * lam_im) / den)[..., None]
    z_im = ((n_im * lam_re - n_re * lam_im) / den)[..., None]
    br, bi = b_re.astype(F32), b_im.astype(F32)
    bb_re = z_re * br - z_im * bi
    bb_im = z_re * bi + z_im * br
    pw_re, pw_im = [jnp.ones_like(lb_re)], [jnp.zeros_like(lb_im)]
    for _ in range(n):
        r, im = pw_re[-1], pw_im[-1]
        pw_re.append(r * lb_re - im * lb_im)
        pw_im.append(r * lb_im + im * lb_re)
    pw_re, pw_im = jnp.stack(pw_re), jnp.stack(pw_im)
    cr, ci = c_re.astype(F32), c_im.astype(F32)
    hi = lax.Precision.HIGHEST
    rev_re, rev_im = pw_re[n - 1::-1], pw_im[n - 1::-1]
    sb_re = jnp.einsum('sgp,gpc->sgcp', rev_re, bb_re) - jnp.einsum('sgp,gpc->sgcp', rev_im, bb_im)
    sb_im = jnp.einsum('sgp,gpc->sgcp', rev_re, bb_im) + jnp.einsum('sgp,gpc->sgcp', rev_im, bb_re)
    cl_re = jnp.einsum('gcp,tgp->tgpc', cr, pw_re[1:]) - jnp.einsum('gcp,tgp->tgpc', ci, pw_im[1:])
    cl_im = jnp.einsum('gcp,tgp->tgpc', cr, pw_im[1:]) + jnp.einsum('gcp,tgp->tgpc', ci, pw_re[1:])
    cp_re = jnp.einsum('gcp,tgp->tgcp', cr, pw_re[:n]) - jnp.einsum('gcp,tgp->tgcp', ci, pw_im[:n])
    cp_im = jnp.einsum('gcp,tgp->tgcp', cr, pw_im[:n]) + jnp.einsum('gcp,tgp->tgcp', ci, pw_re[:n])
    kern = (jnp.einsum('tgcp,gpd->gtcd', cp_re, bb_re, precision=hi)
            - jnp.einsum('tgcp,gpd->gtcd', cp_im, bb_im, precision=hi))

    kp = jnp.concatenate([jnp.zeros_like(kern[:, :1]), kern], axis=1)
    dl = jnp.arange(npair)[:, None, None]
    s0 = jnp.arange(2)[None, :, None]
    t0 = jnp.arange(2)[None, None, :]
    lag = 2 * dl + t0 - s0
    kl = kp[:, lag + 1].reshape(nq, gs, npair, 2, 2, cw, cw)
    base = kl.transpose(0, 2, 3, 1, 6, 4, 5)
    same_group = jnp.eye(gs, dtype=F32)
    w_toe = jnp.concatenate(
        [base[..., t0, :] * same_group[:, g1][None, None, None, :, None, None]
         for t0 in range(2) for g1 in range(gs)], axis=-1)
    w_toe = w_toe.reshape(nq, npair, MXU_DIM, MXU_DIM).astype(BF16)

    member = (jnp.arange(gs)[None, :, None]
              == 2 * jnp.arange(pairs_per_set)[:, None, None] + jnp.arange(2)[None, None, :]).astype(F32)

    def pair_tiles(x):
        cols = [x[ri][:, None] * member[:, :, gl][None, :, None, None, :, None, None]
                for ri in range(2) for gl in range(2)]
        return jnp.concatenate(cols, axis=-1).reshape(g // 2, npair, MXU_DIM, MXU_DIM)

    sbs = jnp.stack([sb_re, sb_im]).reshape(2, npair, 2, nq, gs, cw, p)
    w_in_state = pair_tiles(sbs.transpose(0, 3, 1, 2, 4, 5, 6)).astype(BF16)
    sos = jnp.stack([cl_re, -cl_im]).reshape(2, npair, 2, nq, gs, p, cw)
    w_state_out = jnp.swapaxes(pair_tiles(sos.transpose(0, 3, 1, 2, 4, 6, 5)), -1, -2).astype(BF16)
    decay = jnp.stack([pw_re[n].reshape(g // 2, 2 * p), pw_im[n].reshape(g // 2, 2 * p)])
    dvec = d_skip.astype(F32).reshape(1, g * cw)
    return w_toe, w_in_state, w_state_out, decay, dvec


def _ssm_kernel(u_ref, wt_ref, wi_ref, wo_ref, dec_ref, d_ref, y_ref, s_ref, xp_ref, x_ref, *, bsz):
    kt = u_ref.shape[0]
    n = u_ref.shape[1] // bsz
    rows = kt * bsz
    npair = n // 2
    nq = wt_ref.shape[0]
    npairs_g = wi_ref.shape[0]
    per_set = npairs_g // nq
    half = MXU_DIM // 2

    @pl.when(pl.program_id(0) == 0)
    def _():
        x_ref[...] = jnp.zeros_like(x_ref)

    def piece(s, q):
        return u_ref[:, s * bsz:(s + 1) * bsz, q * LANES:(q + 1) * LANES].reshape(rows, LANES)

    lhs = {(sp, q): jnp.concatenate([piece(2 * sp, q), piece(2 * sp + 1, q)], axis=-1)
           for sp in range(npair) for q in range(nq)}

    for gp in range(npairs_g):
        q = gp // per_set
        acc = jnp.dot(lhs[0, q], wi_ref[gp, 0], preferred_element_type=F32)
        for sp in range(1, npair):
            acc = acc + jnp.dot(lhs[sp, q], wi_ref[gp, sp], preferred_element_type=F32)
        s_ref[:, gp * MXU_DIM:(gp + 1) * MXU_DIM] = acc

    for gp in range(npairs_g):
        re_cols = slice(gp * MXU_DIM, gp * MXU_DIM + half)
        im_cols = slice(gp * MXU_DIM + half, (gp + 1) * MXU_DIM)
        ar = dec_ref[0, gp:gp + 1, :]
        ai = dec_ref[1, gp:gp + 1, :]
        xr = x_ref[:, re_cols]
        xi = x_ref[:, im_cols]
        for kc in range(kt):
            rs = slice(kc * bsz, (kc + 1) * bsz)
            xp_ref[rs, re_cols] = xr.astype(BF16)
            xp_ref[rs, im_cols] = xi.astype(BF16)
            sr = s_ref[rs, re_cols]
            si = s_ref[rs, im_cols]
            xr, xi = ar * xr - ai * xi + sr, ar * xi + ai * xr + si
        x_ref[:, re_cols] = xr
        x_ref[:, im_cols] = xi

    for q in range(nq):
        for tp in range(npair):
            acc = jnp.dot(lhs[0, q], wt_ref[q, tp], preferred_element_type=F32)
            for sp in range(1, tp + 1):
                acc = acc + jnp.dot(lhs[sp, q], wt_ref[q, tp - sp], preferred_element_type=F32)
            for gp in range(q * per_set, (q + 1) * per_set):
                acc = acc + jnp.dot(xp_ref[:, gp * MXU_DIM:(gp + 1) * MXU_DIM], wo_ref[gp, tp],
                                    preferred_element_type=F32)
            for t0 in range(2):
                s = 2 * tp + t0
                y = acc[:, t0 * half:(t0 + 1) * half] + d_ref[:, q * LANES:(q + 1) * LANES] * piece(s, q).astype(F32)
                y_ref[:, s * bsz:(s + 1) * bsz, q * LANES:(q + 1) * LANES] = (
                    jax.nn.gelu(y).astype(BF16).reshape(kt, bsz, LANES))


def _ssm(u_tm, weights, bsz, seq):
    n = SSM_CHUNK
    kt = SSM_CHUNKS_PER_STEP
    nchunk = seq // n
    rows = kt * bsz
    state_w = SSM_GROUPS * 2 * SSM_STATE
    u3 = u_tm.reshape(nchunk, n * bsz, SSM_WIDTH)
    blk = pl.BlockSpec((kt, n * bsz, SSM_WIDTH), lambda i: (i, 0, 0))

    once = pl.Buffered(1)
    y3 = pl.pallas_call(
        functools.partial(_ssm_kernel, bsz=bsz),
        grid=(nchunk // kt,),
        in_specs=[blk] + [_full(w.shape, pipeline_mode=once) for w in weights],
        out_specs=blk,
        out_shape=jax.ShapeDtypeStruct(u3.shape, BF16),
        scratch_shapes=[pltpu.VMEM((rows, state_w), F32), pltpu.VMEM((rows, state_w), BF16),
                        pltpu.VMEM((bsz, state_w), F32)],
        compiler_params=_cparams("arbitrary"),
        name="s5_chunked_scan",
    )(u3, *weights)
    return y3.reshape(seq, bsz * SSM_WIDTH)


def _postmix_kernel(att_ref, ys_ref, ga_ref, gs_ref, x_ref, gt_ref, sc_ref, sh_ref, gpost_ref, gpre_ref,
                    wglu_ref, wso_ref, wao_ref, wo_ref, wrh_ref, wrl_ref, br_ref,
                    x1_ref, h2_ref, idx_ref, gate_ref, rank_ref, cnt_ref, run_ref):
    tm = x_ref.shape[0]
    ne = wrh_ref.shape[0]

    @pl.when(pl.program_id(0) == 0)
    def _():
        run_ref[...] = jnp.zeros_like(run_ref)

    halves = [slice(r * (tm // 2), (r + 1) * (tm // 2)) for r in range(2)]
    dot = functools.partial(jnp.dot, preferred_element_type=F32)
    att = [dot(att_ref[r, :], wao_ref[...]) for r in halves]
    glu = [dot(ys_ref[r, :], wglu_ref[...]) for r in halves]
    ssm = []
    for g in glu:
        sg = (g[:, :SSM_WIDTH] * jax.nn.sigmoid(g[:, SSM_WIDTH:])).astype(BF16)
        ssm.append(dot(sg, wso_ref[...]))
    y = []
    for r, a, s in zip(halves, att, ssm):
        merged = (ga_ref[r, :].astype(F32) * a + gs_ref[r, :].astype(F32) * s).astype(BF16)
        y.append(dot(merged, wo_ref[...]))
    logits = []
    for r, yh in zip(halves, y):
        x1 = x_ref[r, :] + gt_ref[0] * _rms(yh, gpost_ref[...])
        x1_ref[r, :] = x1
        h2 = _rms(x1, gpre_ref[...]) * (1.0 + sc_ref[0]) + sh_ref[0]
        h2_ref[r, :] = _pack_bf16_pairs(h2)
        h_hi = h2.astype(BF16)
        h_lo = (h2 - h_hi.astype(F32)).astype(BF16)
        logits.append(lax.dot_general(wrh_ref[...], h_hi, _NT, preferred_element_type=F32)
                      + lax.dot_general(wrh_ref[...], h_lo, _NT, preferred_element_type=F32)
                      + lax.dot_general(wrl_ref[...], h_hi, _NT, preferred_element_type=F32))
    logits = jnp.concatenate(logits, axis=1) + br_ref[...]
    eidx = lax.broadcasted_iota(jnp.int32, (ne, tm), 0)
    lg = logits
    vals, idxs = [], []
    for _ in range(TOP_K):
        mx = jnp.max(lg, axis=0, keepdims=True)
        first = jnp.min(jnp.where(lg == mx, eidx, ne), axis=0, keepdims=True)
        vals.append(mx)
        idxs.append(first)
        lg = jnp.where(eidx == first, -jnp.inf, lg)
    exps = [jnp.exp(v - vals[0]) for v in vals]
    denom = exps[0] + exps[1] + exps[2] + exps[3]
    gate_ref[...] = jnp.concatenate([e / denom for e in exps], axis=0)
    idx_ref[...] = jnp.concatenate(idxs, axis=0)
    onehot = jnp.where(lg == -jnp.inf, 1.0, 0.0)
    t_row = lax.broadcasted_iota(jnp.int32, (tm, tm), 0)
    t_col = lax.broadcasted_iota(jnp.int32, (tm, tm), 1)
    earlier = jnp.where(t_row < t_col, 1.0, 0.0).astype(BF16)
    before = jnp.dot(onehot.astype(BF16), earlier, preferred_element_type=F32) + run_ref[:, 0:1]
    ranks = [jnp.sum(jnp.where(eidx == ix, before, 0.0), axis=0, keepdims=True) for ix in idxs]
    rank_ref[...] = jnp.concatenate(ranks, axis=0).astype(jnp.int32)
    run_ref[...] = run_ref[...] + jnp.sum(onehot, axis=1, keepdims=True)
    cnt_ref[...] = run_ref[...].astype(jnp.int32)


def _postmix(att2, ys_tm, ga, gs, x2, gt1, sc2, sh2, g_post, g_pre, w_glu, w_ssm_out, w_att_out, w_out,
             w_router, b_router, seq):
    t, d = x2.shape
    tm = TOKEN_TILE
    tiles_per_seq = seq // tm
    ne = w_router.shape[1]
    wr_t = w_router.T.astype(F32)
    wr_hi = wr_t.astype(BF16)
    wr_lo = (wr_t - wr_hi.astype(F32)).astype(BF16)
    mod_spec = pl.BlockSpec((1, 1, d), lambda i: (i // tiles_per_seq, 0, 0))
    row = lambda cols: pl.BlockSpec((tm, cols), lambda i: (i, 0))
    time_major = pl.BlockSpec((tm, SSM_WIDTH), lambda i: (i % tiles_per_seq, i // tiles_per_seq))
    col = pl.BlockSpec((TOP_K, tm), lambda i: (0, i))
    weights = [w_glu.astype(BF16), w_ssm_out.astype(BF16), w_att_out.astype(BF16), w_out.astype(BF16),
               wr_hi, wr_lo, b_router.astype(F32).reshape(ne, 1)]
    return pl.pallas_call(
        _postmix_kernel,
        grid=(t // tm,),
        in_specs=[row(ATT_WIDTH), time_major, row(d), row(d), row(d), mod_spec, mod_spec, mod_spec,
                  _full((1, d)), _full((1, d))]
        + [_full(w.shape, pipeline_mode=pl.Buffered(1)) for w in weights],
        out_specs=[row(d), row(d // 2), col, col, col, _full((ne, 128))],
        out_shape=[jax.ShapeDtypeStruct((t, d), F32), jax.ShapeDtypeStruct((t, d // 2), jnp.uint32),
                   jax.ShapeDtypeStruct((TOP_K, t), jnp.int32), jax.ShapeDtypeStruct((TOP_K, t), F32),
                   jax.ShapeDtypeStruct((TOP_K, t), jnp.int32), jax.ShapeDtypeStruct((ne, 128), jnp.int32)],
        scratch_shapes=[pltpu.VMEM((ne, 128), F32)],
        compiler_params=_cparams("arbitrary"),
        name="postmix_router",
    )(att2, ys_tm, ga, gs, x2, gt1, sc2, sh2, g_post.reshape(1, d), g_pre.reshape(1, d), *weights)


def _sc_worker_base(per_worker):
    return (lax.axis_index("s") * SC_CORES + lax.axis_index("c")) * per_worker


def _sc_mesh():
    return plsc.VectorSubcoreMesh(core_axis_name="c", subcore_axis_name="s",
                                  num_cores=SC_CORES, num_subcores=SC_SUBCORES)


def _sc_scatter_kernel(rows_hbm, idx_hbm, out_hbm, idx_v, rows_v, load_sems, store_sems, *, per_worker):
    chunk = rows_v.shape[1]
    n_tokens = rows_hbm.shape[0]
    n_chunks = per_worker // chunk
    base = _sc_worker_base(per_worker)

    def load(c, b):
        off = pl.multiple_of(base + c * chunk, chunk)
        for k in range(TOP_K):
            pltpu.sync_copy(idx_hbm.at[pl.ds(k * n_tokens + off, chunk)], idx_v.at[b, k])
        return pltpu.make_async_copy(rows_hbm.at[pl.ds(off, chunk)], rows_v.at[b], load_sems.at[b])

    def stores(b):
        return [pltpu.make_async_copy(rows_v.at[b], out_hbm.at[idx_v.at[b, k]], store_sems.at[b])
                for k in range(TOP_K)]

    load(0, 0).start()

    @pl.loop(0, n_chunks, step=2)
    def _(c0):
        for b in range(2):
            c = c0 + b

            @pl.when(c > 0)
            def _():
                for cp in stores(1 - b):
                    cp.wait()

            @pl.when(c + 1 < n_chunks)
            def _():
                load(c + 1, 1 - b).start()

            pltpu.make_async_copy(rows_hbm.at[pl.ds(0, chunk)], rows_v.at[b], load_sems.at[b]).wait()
            for cp in stores(b):
                cp.start()

    for cp in stores((n_chunks - 1) % 2):
        cp.wait()


def _sc_scatter_rows(rows, idx, n_out):
    t, d = rows.shape
    workers = SC_CORES * SC_SUBCORES
    per_worker = t // workers
    assert t % workers == 0 and per_worker % (2 * SC_GATHER_ROWS) == 0
    return pl.kernel(
        functools.partial(_sc_scatter_kernel, per_worker=per_worker),
        out_type=jax.ShapeDtypeStruct((n_out, d), rows.dtype),
        mesh=_sc_mesh(),
        scratch_types=[pltpu.VMEM((2, TOP_K, SC_GATHER_ROWS), jnp.int32),
                       pltpu.VMEM((2, SC_GATHER_ROWS, d), rows.dtype),
                       pltpu.SemaphoreType.DMA((2,)), pltpu.SemaphoreType.DMA((2,))],
        name="sc_row_scatter",
    )(rows, idx)


def _sc_gather_kernel(table_hbm, idx_hbm, out_hbm, idx_v, rows_v, sems, *, per_worker):
    chunk = rows_v.shape[1]
    n_chunks = per_worker // chunk
    base = _sc_worker_base(per_worker)

    def gather(c, b):
        off = pl.multiple_of(base + c * chunk, chunk)
        pltpu.sync_copy(idx_hbm.at[pl.ds(off, chunk)], idx_v.at[b])
        return pltpu.make_async_copy(table_hbm.at[idx_v.at[b]], rows_v.at[b], sems.at[b])

    gather(0, 0).start()

    @pl.loop(0, n_chunks, step=2)
    def _(c0):
        for b in range(2):
            c = c0 + b

            @pl.when(c + 1 < n_chunks)
            def _():
                gather(c + 1, 1 - b).start()

            pltpu.make_async_copy(table_hbm.at[idx_v.at[b]], rows_v.at[b], sems.at[b]).wait()
            off = pl.multiple_of(base + c * chunk, chunk)
            pltpu.sync_copy(rows_v.at[b], out_hbm.at[pl.ds(off, chunk)])


def _sc_gather_rows(table, idx):
    n = idx.shape[0]
    d = table.shape[1]
    workers = SC_CORES * SC_SUBCORES
    per_worker = n // workers
    assert n % workers == 0 and per_worker % (2 * SC_GATHER_ROWS) == 0
    return pl.kernel(
        functools.partial(_sc_gather_kernel, per_worker=per_worker),
        out_type=jax.ShapeDtypeStruct((n, d), table.dtype),
        mesh=_sc_mesh(),
        scratch_types=[pltpu.VMEM((2, SC_GATHER_ROWS), jnp.int32),
                       pltpu.VMEM((2, SC_GATHER_ROWS, d), table.dtype),
                       pltpu.SemaphoreType.DMA((2,))],
        name="sc_row_gather",
    )(table, idx)


def _experts_kernel(be_ref, nused_ref, valid_ref, next_ref, x_ref, b1_ref, b2_ref, w1_hbm, w2_hbm, y_ref,
                    w1s_ref, w2s_ref, w1b_ref, w2b_ref, sems, *, layer):
    i = pl.program_id(0)
    prev = be_ref[jnp.maximum(i - 1, 0)]
    fresh = (i < nused_ref[0]) & ((i == 0) | (be_ref[i] != prev))

    def fetch(e):
        return (pltpu.make_async_copy(w1_hbm.at[layer, e], w1s_ref, sems.at[0]),
                pltpu.make_async_copy(w2_hbm.at[layer, e], w2s_ref, sems.at[1]))

    @pl.when(i == 0)
    def _():
        for cp in fetch(be_ref[0]):
            cp.start()

    @pl.when(fresh)
    def _():
        for cp in fetch(be_ref[i]):
            cp.wait()
        w1b_ref[...] = w1s_ref[...].astype(BF16)
        w2b_ref[...] = w2s_ref[...].astype(BF16)

        @pl.when(next_ref[i] >= 0)
        def _():
            for cp in fetch(next_ref[i]):
                cp.start()

    used = i < nused_ref[0]
    valid = valid_ref[i]
    half = x_ref.shape[0] // 2

    def ffn(n_rows):
        row = lax.broadcasted_iota(jnp.int32, (n_rows, x_ref.shape[1]), 0)
        x_lo, x_hi = _unpack_bf16_pairs(jnp.where(row < valid, x_ref[0:n_rows, :], jnp.uint32(0)))
        x = jnp.concatenate([x_lo.astype(BF16), x_hi.astype(BF16)], axis=1)
        gu = jnp.dot(x, w1b_ref[...], preferred_element_type=F32) + b1_ref[...]
        g = jnp.minimum(gu[:, :D_FF], SWIGLU_LIMIT)
        up = jnp.clip(gu[:, D_FF:], -SWIGLU_LIMIT, SWIGLU_LIMIT)
        act = ((up + 1.0) * g * jax.nn.sigmoid(SWIGLU_ALPHA * g)).astype(BF16)
        y = jnp.dot(act, w2b_ref[...], preferred_element_type=F32) + b2_ref[...]
        y_ref[0:n_rows, :] = _pack_bf16_pairs(y)

    @pl.when(used & (valid > half))
    def _():
        ffn(2 * half)

    @pl.when(used & (valid <= half))
    def _():
        ffn(half)
        y_ref[half:, :] = jnp.zeros((half, y_ref.shape[1]), y_ref.dtype)

    @pl.when(jnp.logical_not(used))
    def _():
        y_ref[...] = jnp.zeros_like(y_ref)


def _experts(xb, block_e, n_used, valid, next_e, w1, b1, w2, b2, layer):
    p_rows, packed_w = xb.shape
    d = 2 * packed_w
    depth, ne = w1.shape[:2]
    rb = EXPERT_ROWS
    bmap = lambda i, be, nu, va, nx: (layer, be[i], 0, 0)
    rows = pl.BlockSpec((rb, packed_w), lambda i, be, nu, va, nx: (i, 0))
    grid_spec = pltpu.PrefetchScalarGridSpec(
        num_scalar_prefetch=4,
        grid=(p_rows // rb,),
        in_specs=[
            rows,
            pl.BlockSpec((None, None, 1, 2 * D_FF), bmap),
            pl.BlockSpec((None, None, 1, d), bmap),
            pl.BlockSpec(memory_space=pl.ANY),
            pl.BlockSpec(memory_space=pl.ANY),
        ],
        out_specs=rows,
        scratch_shapes=[pltpu.VMEM((d, 2 * D_FF), w1.dtype), pltpu.VMEM((D_FF, d), w2.dtype),
                        pltpu.VMEM((d, 2 * D_FF), BF16), pltpu.VMEM((D_FF, d), BF16),
                        pltpu.SemaphoreType.DMA((2,))],
    )
    return pl.pallas_call(
        functools.partial(_experts_kernel, layer=layer),
        grid_spec=grid_spec,
        out_shape=jax.ShapeDtypeStruct((p_rows, packed_w), jnp.uint32),
        compiler_params=_cparams("arbitrary"),
        name="expert_ffn",
    )(block_e, n_used, valid, next_e, xb, b1.reshape(depth, ne, 1, 2 * D_FF), b2.reshape(depth, ne, 1, d), w1, w2)


def _combine_kernel(gate_ref, x_ref, gt_ref, g_ref, y0_ref, y1_ref, y2_ref, y3_ref, o_ref):
    tm = x_ref.shape[0]
    gates = gate_ref[...]
    gates = jnp.concatenate([gates, jnp.zeros((LANES - TOP_K, tm), F32)], axis=0).T
    lo, hi = _unpack_bf16_pairs(y0_ref[...])
    y_lo, y_hi = gates[:, 0:1] * lo, gates[:, 0:1] * hi
    for k, y_ref in enumerate((y1_ref, y2_ref, y3_ref), start=1):
        lo, hi = _unpack_bf16_pairs(y_ref[...])
        y_lo, y_hi = y_lo + gates[:, k:k + 1] * lo, y_hi + gates[:, k:k + 1] * hi
    y = jnp.concatenate([y_lo, y_hi], axis=1)
    o_ref[...] = x_ref[...] + gt_ref[0] * _rms(y, g_ref[...])


def _combine(y4, gate_t, x2, gt2, g_post, seq):
    t, d = x2.shape
    tm = TOKEN_TILE
    tiles = t // tm
    tiles_per_seq = seq // tm
    row = pl.BlockSpec((tm, d), lambda i: (i, 0))
    slot = lambda k: pl.BlockSpec((tm, y4.shape[1]), lambda i: (k * tiles + i, 0))
    return pl.pallas_call(
        _combine_kernel,
        grid=(tiles,),
        in_specs=[pl.BlockSpec((TOP_K, tm), lambda i: (0, i)),
                  row, pl.BlockSpec((1, 1, d), lambda i: (i // tiles_per_seq, 0, 0)), _full((1, d))]
        + [slot(k) for k in range(TOP_K)],
        out_specs=row,
        out_shape=jax.ShapeDtypeStruct((t, d), F32),
        compiler_params=_cparams("arbitrary"),
        name="expert_combine",
    )(gate_t, x2, gt2, g_post.reshape(1, d), y4, y4, y4, y4)


def _route_plan(idx_t, rank_t, counts):
    rb = EXPERT_ROWS
    k, t = idx_t.shape
    padded = (counts + rb - 1) // rb * rb
    pad_ends = jnp.cumsum(padded)
    pad_starts = pad_ends - padded
    experts = jnp.arange(N_EXPERTS, dtype=jnp.int32)
    start_of = jnp.sum(jnp.where(idx_t[None] == experts[:, None, None], pad_starts[:, None, None], 0), axis=0)
    dest = (start_of + rank_t).astype(jnp.int32)
    n_blocks = (k * t) // rb + N_EXPERTS
    blk_start = jnp.arange(n_blocks, dtype=jnp.int32) * rb
    block_e = jnp.minimum(jnp.sum(pad_ends[None, :] <= blk_start[:, None], axis=1), N_EXPERTS - 1)
    onehot_e = block_e[:, None] == experts[None, :]
    cnt_b = jnp.sum(jnp.where(onehot_e, counts[None, :], 0), axis=1)
    start_b = jnp.sum(jnp.where(onehot_e, pad_starts[None, :], 0), axis=1)
    valid = jnp.clip(cnt_b - (blk_start - start_b), 0, rb).astype(jnp.int32)
    n_used = (pad_ends[-1] // rb).astype(jnp.int32).reshape(1)
    later_nonempty = (experts[None, :] > experts[:, None]) & (counts[None, :] > 0)
    next_nonempty = jnp.min(jnp.where(later_nonempty, experts[None, :], N_EXPERTS), axis=1)
    next_nonempty = jnp.where(next_nonempty == N_EXPERTS, -1, next_nonempty)
    next_e = jnp.sum(jnp.where(onehot_e, next_nonempty[None, :], 0), axis=1).astype(jnp.int32)
    return dest.reshape(-1), block_e.astype(jnp.int32), n_used, valid, next_e, n_blocks * rb


def kernel(x, c, rel_bias, w_ada, b_ada, g_pre_mix, g_post_mix, g_pre_ffn, g_post_ffn, w_in, ssm_a_re, ssm_a_im, ssm_log_dt, ssm_b_re, ssm_b_im, ssm_c_re, ssm_c_im, ssm_d, w_glu, w_ssm_out, w_att_out, w_out, w_router, b_router, w_exp_in, b_exp_in, w_exp_out, b_exp_out):
    bsz, seq, d = x.shape
    depth = w_ada.shape[0]
    t = bsz * seq
    assert d == D_MODEL and seq % TOKEN_TILE == 0 and TOKEN_TILE % MOBA_BLOCK == 0
    assert seq % (SSM_CHUNK * SSM_CHUNKS_PER_STEP) == 0

    mod = _ada_mod(c, w_ada, b_ada)
    bias_tiles, far_bias = _moba_bias_tiles(rel_bias.astype(F32))
    x2 = x.reshape(t, d)
    for l in range(depth):
        sh1, sc1, gt1, sh2, sc2, gt2 = [m.reshape(bsz, 1, d) for m in jnp.split(mod[l], N_MOD, axis=-1)]
        q4, k4, vt4, u_tm, ga, gs = _premix(x2, g_pre_mix[l], sc1, sh1, w_in[l], seq)
        att = _moba(q4, k4, vt4, bias_tiles, far_bias, bsz).reshape(t, ATT_WIDTH)
        ssm_w = _ssm_weights(ssm_a_re[l], ssm_a_im[l], ssm_log_dt[l], ssm_b_re[l], ssm_b_im[l],
                             ssm_c_re[l], ssm_c_im[l], ssm_d[l])
        ys_tm = _ssm(u_tm, ssm_w, bsz, seq)
        x1, h2, idx_t, gate_t, rank_t, cnt = _postmix(
            att, ys_tm, ga, gs, x2, gt1, sc2, sh2, g_post_mix[l], g_pre_ffn[l],
            w_glu[l], w_ssm_out[l], w_att_out[l], w_out[l], w_router[l], b_router[l], seq)
        dest_flat, block_e, n_used, valid, next_e, p_rows = _route_plan(idx_t, rank_t, cnt[:, 0])
        xb = _sc_scatter_rows(h2, dest_flat, p_rows)
        yb = _experts(xb, block_e, n_used, valid, next_e, w_exp_in, b_exp_in, w_exp_out, b_exp_out, l)
        y4 = _sc_gather_rows(yb, dest_flat)
        x2 = _combine(y4, gate_t, x1, gt2, g_post_ffn[l], seq)
    return x2.reshape(bsz, seq, d)
```

```python
import functools
import math

import jax
import jax.numpy as jnp
from jax import lax
from jax.experimental import pallas as pl
from jax.experimental.pallas import tpu as pltpu
from jax.experimental.pallas import tpu_sc as plsc

F32 = jnp.float32
BF16 = jnp.bfloat16

D_MODEL = 1024
ATT_HEADS = 8
HEAD_DIM = 64
ATT_WIDTH = ATT_HEADS * HEAD_DIM
MOBA_BLOCK = 256
MOBA_TOPK = 3
REL_BUCKETS = 32
REL_MAX_DIST = 128
SSM_WIDTH = D_MODEL // 2
SSM_GROUP = 16
SSM_GROUPS = SSM_WIDTH // SSM_GROUP
SSM_STATE = 64
N_EXPERTS = 32
TOP_K = 4
D_FF = D_MODEL
SWIGLU_ALPHA = 1.702
SWIGLU_LIMIT = 7.0
RMS_EPS = 1e-6
N_MOD = 6

SSM_CHUNK = 8
SSM_CHUNKS_PER_STEP = 16
LANES = 128
MXU_DIM = 256
TOKEN_TILE = 1024
SC_CORES = 2
SC_SUBCORES = 16
SC_GATHER_ROWS = 64
EXPERT_ROWS = 1024
MASK_NEG = -1e30
LOG2E = math.log2(math.e)
SUM_ROWS = 16
VMEM_LIMIT = 56 * 1024 * 1024

_NT = (((1,), (1,)), ((), ()))


def _cparams(*sem):
    return pltpu.CompilerParams(dimension_semantics=sem, vmem_limit_bytes=VMEM_LIMIT)


def _pack_bf16_pairs(x):
    n = x.shape[1] // 2
    bits = lax.bitcast_convert_type(x.astype(BF16).astype(F32), jnp.uint32)
    return lax.shift_right_logical(bits[:, :n], jnp.uint32(16)) | (bits[:, n:] & jnp.uint32(0xFFFF0000))


def _unpack_bf16_pairs(w):
    lo = lax.bitcast_convert_type(lax.shift_left(w, jnp.uint32(16)), F32)
    hi = lax.bitcast_convert_type(w & jnp.uint32(0xFFFF0000), F32)
    return lo, hi


def _rms(x, g):
    return x * lax.rsqrt(jnp.mean(x * x, axis=-1, keepdims=True) + RMS_EPS) * g


def _full(shape, **kw):
    n = len(shape)
    return pl.BlockSpec(shape, lambda *_: (0,) * n, **kw)


def _ada_kernel(c_ref, w_ref, b_ref, o_ref):
    c = c_ref[...]
    cond = c * jax.nn.sigmoid(c)
    o_ref[0] = jnp.dot(cond, w_ref[0], preferred_element_type=F32,
                       precision=lax.Precision.HIGHEST) + b_ref[0]


def _ada_mod(c, w_ada, b_ada):
    depth, d, nd = w_ada.shape
    bsz = c.shape[0]
    return pl.pallas_call(
        _ada_kernel,
        grid=(depth, nd // d),
        in_specs=[
            pl.BlockSpec((bsz, d), lambda l, j: (0, 0)),
            pl.BlockSpec((1, d, d), lambda l, j: (l, 0, j)),
            pl.BlockSpec((1, 1, d), lambda l, j: (l, 0, j)),
        ],
        out_specs=pl.BlockSpec((1, bsz, d), lambda l, j: (l, 0, j)),
        out_shape=jax.ShapeDtypeStruct((depth, bsz, nd), F32),
        compiler_params=_cparams("arbitrary", "arbitrary"),
        name="ada_mod",
    )(c, w_ada, b_ada.reshape(depth, 1, nd))


def _premix_kernel(x_ref, g_ref, sc_ref, sh_ref, wq_ref, wk_ref, wvt_ref, wu_ref, wga_ref, wgs_ref,
                   q_ref, k_ref, vt_ref, u_ref, ga_ref, gs_ref):
    x = x_ref[...]
    h = _rms(x, g_ref[...]) * (1.0 + sc_ref[0]) + sh_ref[0]
    hb = h.astype(BF16)
    nblk = q_ref.shape[0]
    q = (jnp.dot(hb, wq_ref[...], preferred_element_type=F32) * (HEAD_DIM ** -0.5 * LOG2E)).astype(BF16)
    k = jnp.dot(hb, wk_ref[...], preferred_element_type=F32).astype(BF16)
    vt = lax.dot_general(wvt_ref[...], hb, _NT, preferred_element_type=F32).astype(BF16)
    for r in range(nblk):
        q_ref[r] = q[r * MOBA_BLOCK:(r + 1) * MOBA_BLOCK]
        k_ref[r] = k[r * MOBA_BLOCK:(r + 1) * MOBA_BLOCK]
        vt_ref[r] = vt[:, r * MOBA_BLOCK:(r + 1) * MOBA_BLOCK]
    u_ref[...] = jnp.dot(hb, wu_ref[...], preferred_element_type=F32).astype(BF16)
    ga = jnp.dot(hb, wga_ref[...], preferred_element_type=F32)
    ga_ref[...] = jax.nn.sigmoid(ga).astype(BF16)
    gs = jnp.dot(hb, wgs_ref[...], preferred_element_type=F32)
    gs_ref[...] = jax.nn.sigmoid(gs).astype(BF16)


def _premix(x2, g, sc, sh, w_in, seq):
    t, d = x2.shape
    tm = TOKEN_TILE
    tiles_per_seq = seq // tm
    nblk = tm // MOBA_BLOCK
    a = ATT_WIDTH
    assert a == SSM_WIDTH and d == 2 * a and w_in.shape[1] == 4 * a + 2 * d
    wb = w_in.astype(BF16)
    wvt = wb[:, 2 * a:3 * a].T

    def cols(width, index):
        return pl.BlockSpec((d, width), lambda i: (0, index), pipeline_mode=pl.Buffered(1))

    mod_spec = pl.BlockSpec((1, 1, d), lambda i: (i // tiles_per_seq, 0, 0))
    blk3 = lambda rows, cols: pl.BlockSpec((nblk, rows, cols), lambda i: (i, 0, 0))
    row = lambda cols: pl.BlockSpec((tm, cols), lambda i: (i, 0))
    time_major = pl.BlockSpec((tm, SSM_WIDTH), lambda i: (i % tiles_per_seq, i // tiles_per_seq))
    once = functools.partial(_full, pipeline_mode=pl.Buffered(1))
    nb_total = t // MOBA_BLOCK
    return pl.pallas_call(
        _premix_kernel,
        grid=(t // tm,),
        in_specs=[row(d), _full((1, d)), mod_spec, mod_spec,
                  cols(a, 0), cols(a, 1), once((a, d)), cols(a, 3), cols(d, 2), cols(d, 3)],
        out_specs=[blk3(MOBA_BLOCK, a), blk3(MOBA_BLOCK, a), blk3(a, MOBA_BLOCK),
                   time_major, row(d), row(d)],
        out_shape=[jax.ShapeDtypeStruct((nb_total, MOBA_BLOCK, a), BF16),
                   jax.ShapeDtypeStruct((nb_total, MOBA_BLOCK, a), BF16),
                   jax.ShapeDtypeStruct((nb_total, a, MOBA_BLOCK), BF16),
                   jax.ShapeDtypeStruct((seq, (t // seq) * SSM_WIDTH), BF16),
                   jax.ShapeDtypeStruct((t, d), BF16),
                   jax.ShapeDtypeStruct((t, d), BF16)],
        compiler_params=_cparams("arbitrary"),
        name="premix_inproj",
    )(x2, g.reshape(1, d), sc, sh, wb, wb, wvt, wb, wb, wb)


def _moba_kernel(far_ref, q_ref, k_ref, vt_ref, bias_ref, o_ref, kmean_ref, neg_ref, *state_refs):
    m_refs = state_refs[0::2]
    acc_refs = state_refs[1::2]
    nb = k_ref.shape[0]
    blk = MOBA_BLOCK
    i = pl.program_id(1)

    @pl.when(i == 0)
    def _():
        for n in range(nb):
            kmean_ref[n:n + 1, :] = jnp.mean(k_ref[n].astype(F32), axis=0, keepdims=True)

    lane = lax.broadcasted_iota(jnp.int32, (blk, 2 * HEAD_DIM), 1)
    nidx = lax.broadcasted_iota(jnp.int32, (nb, blk), 0)
    valid = nidx < i
    pair_w = 2 * HEAD_DIM

    def pair_cols(head):
        return slice((head // 2) * pair_w, (head // 2 + 1) * pair_w)

    def head_rows(head):
        return slice(head * HEAD_DIM, (head + 1) * HEAD_DIM)

    def q_head(head):
        q2 = q_ref[0, :, pair_cols(head)]
        lo = (head % 2) * HEAD_DIM
        return jnp.where((lane >= lo) & (lane < lo + HEAD_DIM), q2, jnp.zeros_like(q2))

    for head in range(ATT_HEADS):
        km = kmean_ref[:, pair_cols(head)]
        km_hi = km.astype(BF16)
        km_lo = (km - km_hi.astype(F32)).astype(BF16)
        parts = lax.dot_general(jnp.concatenate([km_hi, km_lo], axis=0), q_head(head), _NT,
                                preferred_element_type=F32)
        gate = parts[0:nb] + parts[nb:2 * nb]
        g = jnp.where(valid, gate, -jnp.inf)
        sel = jnp.zeros((nb, blk), jnp.bool_)
        for _ in range(MOBA_TOPK):
            mx = jnp.max(g, axis=0, keepdims=True)
            first = jnp.min(jnp.where(g == mx, nidx, nb), axis=0, keepdims=True)
            pick = nidx == first
            sel = sel | pick
            g = jnp.where(pick, -jnp.inf, g)
        neg_ref[head * nb:(head + 1) * nb, :] = jnp.where(sel & valid, 0.0, MASK_NEG)

    def scores(head, js, adds):
        qh = q_head(head)
        parts = [lax.dot_general(k_ref[j, :, pair_cols(head)], qh, _NT, preferred_element_type=F32) + a
                 for j, a in zip(js, adds)]
        return parts[0] if len(parts) == 1 else jnp.concatenate(parts, axis=0)

    def weighted_values(head, js, p):
        vals = [vt_ref[j, head_rows(head), :] for j in js]
        vb = vals[0] if len(vals) == 1 else jnp.concatenate(vals, axis=1)
        vb = jnp.concatenate([vb, jnp.ones((SUM_ROWS, vb.shape[1]), BF16)], axis=0)
        return jnp.dot(vb, p.astype(BF16), preferred_element_type=F32)

    def start(head, js, st):
        m = jnp.max(st, axis=0, keepdims=True)
        m_refs[head][...] = m
        acc_refs[head][...] = weighted_values(head, js, jnp.exp2(st - m))

    def update(head, js, st):
        m = m_refs[head][...]
        m_new = jnp.maximum(m, jnp.max(st, axis=0, keepdims=True))
        alpha = jnp.exp2(m - m_new)
        m_refs[head][...] = m_new
        acc_refs[head][...] = alpha * acc_refs[head][...] + weighted_values(head, js, jnp.exp2(st - m_new))

    def sweep(js, adds_of, absorb):
        st = scores(0, js, adds_of(0))
        for head in range(ATT_HEADS):
            nxt = scores(head + 1, js, adds_of(head + 1)) if head + 1 < ATT_HEADS else None
            absorb(head, js, st)
            st = nxt

    def far_add(head, j):
        return neg_ref[pl.ds(head * nb + j, 1), :] + far_ref[head]

    def near_adds(h):
        return [bias_ref[h, 0], bias_ref[h, 1] + neg_ref[pl.ds(h * nb + i - 1, 1), :]]

    @pl.when(i == 0)
    def _():
        sweep([i], lambda h: [bias_ref[h, 0]], start)

    @pl.when((i == 1) | (i == 2))
    def _():
        sweep([i, i - 1], near_adds, start)

    @pl.when(i >= 3)
    def _():
        sweep([i, i - 1, i - 2, i - 3],
              lambda h: near_adds(h) + [far_add(h, i - 2), far_add(h, i - 3)], start)

    n_far = jnp.where(i >= 3, i - 3, jnp.maximum(i - 1, 0))

    def far_step(first, count):
        js = [first + c for c in range(count)]
        sweep(js, lambda h: [far_add(h, j) for j in js], update)

    def far_quad(jj, carry):
        far_step(4 * jj, 4)
        return carry

    lax.fori_loop(0, n_far // 4, far_quad, 0)
    rest = n_far % 4
    pl.when(rest >= 2)(lambda: far_step(n_far - rest, 2))
    pl.when(rest % 2 == 1)(lambda: far_step(n_far - 1, 1))

    for hp in range(ATT_HEADS // 2):
        outs = [acc_refs[h][0:HEAD_DIM, :] / acc_refs[h][HEAD_DIM:HEAD_DIM + 1, :] for h in (2 * hp, 2 * hp + 1)]
        pair = jnp.concatenate(outs, axis=0)
        o_ref[0, :, hp * pair_w:(hp + 1) * pair_w] = pair.T.astype(BF16)


def _rel_bucket(dist):
    n = jnp.maximum(dist, 0)
    max_exact = REL_BUCKETS // 2
    nf = jnp.maximum(n, 1).astype(F32)
    large = max_exact + (jnp.log(nf / max_exact) / math.log(REL_MAX_DIST / max_exact)
                         * (REL_BUCKETS - max_exact)).astype(jnp.int32)
    large = jnp.minimum(large, REL_BUCKETS - 1)
    return jnp.where(n < max_exact, n, large)


def _moba_bias_tiles(rel_bias):
    blk = MOBA_BLOCK
    span = 2 * blk
    rel_bias = rel_bias * LOG2E
    vec = rel_bias[_rel_bucket(jnp.arange(span))].T.astype(F32)
    masked = jnp.full_like(vec[:, :blk], MASK_NEG)
    ring_own = jnp.concatenate([vec[:, :blk], masked], axis=1)
    ring_prev = jnp.concatenate([vec[:, blk:], vec[:, :blk]], axis=1)

    def toeplitz(ring):
        flat = jnp.tile(ring, (1, blk))[:, :blk * (span - 1)]
        return flat.reshape(-1, blk, span - 1)[:, :, :blk]

    tiles = jnp.stack([toeplitz(ring_own), toeplitz(ring_prev)], axis=1)
    far = rel_bias[_rel_bucket(jnp.array(span))]
    return tiles, far.astype(F32)


def _moba(q4, k4, vt4, bias_tiles, far_bias, bsz):
    nb_total, blk, a = q4.shape
    nb = nb_total // bsz
    return pl.pallas_call(
        _moba_kernel,
        grid=(bsz, nb),
        in_specs=[
            pl.BlockSpec(memory_space=pltpu.SMEM),
            pl.BlockSpec((1, blk, a), lambda b, i: (b * nb + i, 0, 0)),
            pl.BlockSpec((nb, blk, a), lambda b, i: (b, 0, 0)),
            pl.BlockSpec((nb, a, blk), lambda b, i: (b, 0, 0)),
            _full(bias_tiles.shape),
        ],
        out_specs=pl.BlockSpec((1, blk, a), lambda b, i: (b * nb + i, 0, 0)),
        out_shape=jax.ShapeDtypeStruct((nb_total, blk, a), BF16),
        scratch_shapes=[pltpu.VMEM((nb, a), F32), pltpu.VMEM((ATT_HEADS * nb, blk), F32)]
        + [pltpu.VMEM((1, blk), F32), pltpu.VMEM((HEAD_DIM + SUM_ROWS, blk), F32)] * ATT_HEADS,
        compiler_params=_cparams("arbitrary", "arbitrary"),
        name="moba_attention",
    )(far_bias, q4, k4, vt4, bias_tiles)


def _ssm_weights(a_re, a_im, log_dt, b_re, b_im, c_re, c_im, d_skip):
    g, p, cw, n = SSM_GROUPS, SSM_STATE, SSM_GROUP, SSM_CHUNK
    gs = LANES // cw
    nq = g // gs
    npair = n // 2
    pairs_per_set = gs // 2
    lam_re = jnp.minimum(a_re.astype(F32), -1e-4)
    lam_im = a_im.astype(F32)
    dt = jnp.exp(log_dt.astype(F32))[:, None]
    mag = jnp.exp(lam_re * dt)
    lb_re = mag * jnp.cos(lam_im * dt)
    lb_im = mag * jnp.sin(lam_im * dt)
    n_re = lb_re - 1.0
    n_im = lb_im
    den = lam_re * lam_re + lam_im * lam_im
    z_re = ((n_re * lam_re + n_im * lam_im) / den)[..., None]
    z_im = ((n_im * lam_re - n_re * lam_im) / den)[..., None]
    br, bi = b_re.astype(F32), b_im.astype(F32)
    bb_re = z_re * br - z_im * bi
    bb_im = z_re * bi + z_im * br
    pw_re, pw_im = [jnp.ones_like(lb_re)], [jnp.zeros_like(lb_im)]
    for _ in range(n):
        r, im = pw_re[-1], pw_im[-1]
        pw_re.append(r * lb_re - im * lb_im)
        pw_im.append(r * lb_im + im * lb_re)
    pw_re, pw_im = jnp.stack(pw_re), jnp.stack(pw_im)
    cr, ci = c_re.astype(F32), c_im.astype(F32)
    hi = lax.Precision.HIGHEST
    rev_re, rev_im = pw_re[n - 1::-1], pw_im[n - 1::-1]
    sb_re = jnp.einsum('sgp,gpc->sgcp', rev_re, bb_re) - jnp.einsum('sgp,gpc->sgcp', rev_im, bb_im)
    sb_im = jnp.einsum('sgp,gpc->sgcp', rev_re, bb_im) + jnp.einsum('sgp,gpc->sgcp', rev_im, bb_re)
    cl_re = jnp.einsum('gcp,tgp->tgpc', cr, pw_re[1:]) - jnp.einsum('gcp,tgp->tgpc', ci, pw_im[1:])
    cl_im = jnp.einsum('gcp,tgp->tgpc', cr, pw_im[1:]) + jnp.einsum('gcp,tgp->tgpc', ci, pw_re[1:])
    cp_re = jnp.einsum('gcp,tgp->tgcp', cr, pw_re[:n]) - jnp.einsum('gcp,tgp->tgcp', ci, pw_im[:n])
    cp_im = jnp.einsum('gcp,tgp->tgcp', cr, pw_im[:n]) + jnp.einsum('gcp,tgp->tgcp', ci, pw_re[:n])
    kern = (jnp.einsum('tgcp,gpd->gtcd', cp_re, bb_re, precision=hi)
            - jnp.einsum('tgcp,gpd->gtcd', cp_im, bb_im, precision=hi))

    kp = jnp.concatenate([jnp.zeros_like(kern[:, :1]), kern], axis=1)
    dl = jnp.arange(npair)[:, None, None]
    s0 = jnp.arange(2)[None, :, None]
    t0 = jnp.arange(2)[None, None, :]
    lag = 2 * dl + t0 - s0
    kl = kp[:, lag + 1].reshape(nq, gs, npair, 2, 2, cw, cw)
    base = kl.transpose(0, 2, 3, 1, 6, 4, 5)
    same_group = jnp.eye(gs, dtype=F32)
    w_toe = jnp.concatenate(
        [base[..., t0, :] * same_group[:, g1][None, None, None, :, None, None]
         for t0 in range(2) for g1 in range(gs)], axis=-1)
    w_toe = w_toe.reshape(nq, npair, MXU_DIM, MXU_DIM).astype(BF16)

    member = (jnp.arange(gs)[None, :, None]
              == 2 * jnp.arange(pairs_per_set)[:, None, None] + jnp.arange(2)[None, None, :]).astype(F32)

    def pair_tiles(x):
        cols = [x[ri][:, None] * member[:, :, gl][None, :, None, None, :, None, None]
                for ri in range(2) for gl in range(2)]
        return jnp.concatenate(cols, axis=-1).reshape(g // 2, npair, MXU_DIM, MXU_DIM)

    sbs = jnp.stack([sb_re, sb_im]).reshape(2, npair, 2, nq, gs, cw, p)
    w_in_state = pair_tiles(sbs.transpose(0, 3, 1, 2, 4, 5, 6)).astype(BF16)
    sos = jnp.stack([cl_re, -cl_im]).reshape(2, npair, 2, nq, gs, p, cw)
    w_state_out = jnp.swapaxes(pair_tiles(sos.transpose(0, 3, 1, 2, 4, 6, 5)), -1, -2).astype(BF16)
    decay = jnp.stack([pw_re[n].reshape(g // 2, 2 * p), pw_im[n].reshape(g // 2, 2 * p)])
    dvec = d_skip.astype(F32).reshape(1, g * cw)
    return w_toe, w_in_state, w_state_out, decay, dvec


def _ssm_kernel(u_ref, wt_ref, wi_ref, wo_ref, dec_ref, d_ref, y_ref, s_ref, xp_ref, x_ref, *, bsz):
    kt = u_ref.shape[0]
    n = u_ref.shape[1] // bsz
    rows = kt * bsz
    npair = n // 2
    nq = wt_ref.shape[0]
    npairs_g = wi_ref.shape[0]
    per_set = npairs_g // nq
    half = MXU_DIM // 2

    @pl.when(pl.program_id(0) == 0)
    def _():
        x_ref[...] = jnp.zeros_like(x_ref)

    def piece(s, q):
        return u_ref[:, s * bsz:(s + 1) * bsz, q * LANES:(q + 1) * LANES].reshape(rows, LANES)

    lhs = {(sp, q): jnp.concatenate([piece(2 * sp, q), piece(2 * sp + 1, q)], axis=-1)
           for sp in range(npair) for q in range(nq)}

    for gp in range(npairs_g):
        q = gp // per_set
        acc = jnp.dot(lhs[0, q], wi_ref[gp, 0], preferred_element_type=F32)
        for sp in range(1, npair):
            acc = acc + jnp.dot(lhs[sp, q], wi_ref[gp, sp], preferred_element_type=F32)
        s_ref[:, gp * MXU_DIM:(gp + 1) * MXU_DIM] = acc

    for gp in range(npairs_g):
        re_cols = slice(gp * MXU_DIM, gp * MXU_DIM + half)
        im_cols = slice(gp * MXU_DIM + half, (gp + 1) * MXU_DIM)
        ar = dec_ref[0, gp:gp + 1, :]
        ai = dec_ref[1, gp:gp + 1, :]
        xr = x_ref[:, re_cols]
        xi = x_ref[:, im_cols]
        for kc in range(kt):
            rs = slice(kc * bsz, (kc + 1) * bsz)
            xp_ref[rs, re_cols] = xr.astype(BF16)
            xp_ref[rs, im_cols] = xi.astype(BF16)
            sr = s_ref[rs, re_cols]
            si = s_ref[rs, im_cols]
            xr, xi = ar * xr - ai * xi + sr, ar * xi + ai * xr + si
        x_ref[:, re_cols] = xr
        x_ref[:, im_cols] = xi

    for q in range(nq):
        for tp in range(npair):
            acc = jnp.dot(lhs[0, q], wt_ref[q, tp], preferred_element_type=F32)
            for sp in range(1, tp + 1):
                acc = acc + jnp.dot(lhs[sp, q], wt_ref[q, tp - sp], preferred_element_type=F32)
            for gp in range(q * per_set, (q + 1) * per_set):
                acc = acc + jnp.dot(xp_ref[:, gp * MXU_DIM:(gp + 1) * MXU_DIM], wo_ref[gp, tp],
                                    preferred_element_type=F32)
            for t0 in range(2):
                s = 2 * tp + t0
                y = acc[:, t0 * half:(t0 + 1) * half] + d_ref[:, q * LANES:(q + 1) * LANES] * piece(s, q).astype(F32)
                y_ref[:, s * bsz:(s + 1) * bsz, q * LANES:(q + 1) * LANES] = (
                    jax.nn.gelu(y).astype(BF16).reshape(kt, bsz, LANES))


def _ssm(u_tm, weights, bsz, seq):
    n = SSM_CHUNK
    kt = SSM_CHUNKS_PER_STEP
    nchunk = seq // n
    rows = kt * bsz
    state_w = SSM_GROUPS * 2 * SSM_STATE
    u3 = u_tm.reshape(nchunk, n * bsz, SSM_WIDTH)
    blk = pl.BlockSpec((kt, n * bsz, SSM_WIDTH), lambda i: (i, 0, 0))

    once = pl.Buffered(1)
    y3 = pl.pallas_call(
        functools.partial(_ssm_kernel, bsz=bsz),
        grid=(nchunk // kt,),
        in_specs=[blk] + [_full(w.shape, pipeline_mode=once) for w in weights],
        out_specs=blk,
        out_shape=jax.ShapeDtypeStruct(u3.shape, BF16),
        scratch_shapes=[pltpu.VMEM((rows, state_w), F32), pltpu.VMEM((rows, state_w), BF16),
                        pltpu.VMEM((bsz, state_w), F32)],
        compiler_params=_cparams("arbitrary"),
        name="s5_chunked_scan",
    )(u3, *weights)
    return y3.reshape(seq, bsz * SSM_WIDTH)


def _postmix_kernel(att_ref, ys_ref, ga_ref, gs_ref, x_ref, gt_ref, sc_ref, sh_ref, gpost_ref, gpre_ref,
                    wglu_ref, wso_ref, wao_ref, wo_ref, wrh_ref, wrl_ref, br_ref,
                    x1_ref, h2_ref, idx_ref, gate_ref, rank_ref, cnt_ref, run_ref):
    tm = x_ref.shape[0]
    ne = wrh_ref.shape[0]

    @pl.when(pl.program_id(0) == 0)
    def _():
        run_ref[...] = jnp.zeros_like(run_ref)

    halves = [slice(r * (tm // 2), (r + 1) * (tm // 2)) for r in range(2)]
    dot = functools.partial(jnp.dot, preferred_element_type=F32)
    att = [dot(att_ref[r, :], wao_ref[...]) for r in halves]
    glu = [dot(ys_ref[r, :], wglu_ref[...]) for r in halves]
    ssm = []
    for g in glu:
        sg = (g[:, :SSM_WIDTH] * jax.nn.sigmoid(g[:, SSM_WIDTH:])).astype(BF16)
        ssm.append(dot(sg, wso_ref[...]))
    y = []
    for r, a, s in zip(halves, att, ssm):
        merged = (ga_ref[r, :].astype(F32) * a + gs_ref[r, :].astype(F32) * s).astype(BF16)
        y.append(dot(merged, wo_ref[...]))
    logits = []
    for r, yh in zip(halves, y):
        x1 = x_ref[r, :] + gt_ref[0] * _rms(yh, gpost_ref[...])
        x1_ref[r, :] = x1
        h2 = _rms(x1, gpre_ref[...]) * (1.0 + sc_ref[0]) + sh_ref[0]
        h2_ref[r, :] = _pack_bf16_pairs(h2)
        h_hi = h2.astype(BF16)
        h_lo = (h2 - h_hi.astype(F32)).astype(BF16)
        logits.append(lax.dot_general(wrh_ref[...], h_hi, _NT, preferred_element_type=F32)
                      + lax.dot_general(wrh_ref[...], h_lo, _NT, preferred_element_type=F32)
                      + lax.dot_general(wrl_ref[...], h_hi, _NT, preferred_element_type=F32))
    logits = jnp.concatenate(logits, axis=1) + br_ref[...]
    eidx = lax.broadcasted_iota(jnp.int32, (ne, tm), 0)
    lg = logits
    vals, idxs = [], []
    for _ in range(TOP_K):
        mx = jnp.max(lg, axis=0, keepdims=True)
        first = jnp.min(jnp.where(lg == mx, eidx, ne), axis=0, keepdims=True)
        vals.append(mx)
        idxs.append(first)
        lg = jnp.where(eidx == first, -jnp.inf, lg)
    exps = [jnp.exp(v - vals[0]) for v in vals]
    denom = exps[0] + exps[1] + exps[2] + exps[3]
    gate_ref[...] = jnp.concatenate([e / denom for e in exps], axis=0)
    idx_ref[...] = jnp.concatenate(idxs, axis=0)
    onehot = jnp.where(lg == -jnp.inf, 1.0, 0.0)
    t_row = lax.broadcasted_iota(jnp.int32, (tm, tm), 0)
    t_col = lax.broadcasted_iota(jnp.int32, (tm, tm), 1)
    earlier = jnp.where(t_row < t_col, 1.0, 0.0).astype(BF16)
    before = jnp.dot(onehot.astype(BF16), earlier, preferred_element_type=F32) + run_ref[:, 0:1]
    ranks = [jnp.sum(jnp.where(eidx == ix, before, 0.0), axis=0, keepdims=True) for ix in idxs]
    rank_ref[...] = jnp.concatenate(ranks, axis=0).astype(jnp.int32)
    run_ref[...] = run_ref[...] + jnp.sum(onehot, axis=1, keepdims=True)
    cnt_ref[...] = run_ref[...].astype(jnp.int32)


def _postmix(att2, ys_tm, ga, gs, x2, gt1, sc2, sh2, g_post, g_pre, w_glu, w_ssm_out, w_att_out, w_out,
             w_router, b_router, seq):
    t, d = x2.shape
    tm = TOKEN_TILE
    tiles_per_seq = seq // tm
    ne = w_router.shape[1]
    wr_t = w_router.T.astype(F32)
    wr_hi = wr_t.astype(BF16)
    wr_lo = (wr_t - wr_hi.astype(F32)).astype(BF16)
    mod_spec = pl.BlockSpec((1, 1, d), lambda i: (i // tiles_per_seq, 0, 0))
    row = lambda cols: pl.BlockSpec((tm, cols), lambda i: (i, 0))
    time_major = pl.BlockSpec((tm, SSM_WIDTH), lambda i: (i % tiles_per_seq, i // tiles_per_seq))
    col = pl.BlockSpec((TOP_K, tm), lambda i: (0, i))
    weights = [w_glu.astype(BF16), w_ssm_out.astype(BF16), w_att_out.astype(BF16), w_out.astype(BF16),
               wr_hi, wr_lo, b_router.astype(F32).reshape(ne, 1)]
    return pl.pallas_call(
        _postmix_kernel,
        grid=(t // tm,),
        in_specs=[row(ATT_WIDTH), time_major, row(d), row(d), row(d), mod_spec, mod_spec, mod_spec,
                  _full((1, d)), _full((1, d))]
        + [_full(w.shape, pipeline_mode=pl.Buffered(1)) for w in weights],
        out_specs=[row(d), row(d // 2), col, col, col, _full((ne, 128))],
        out_shape=[jax.ShapeDtypeStruct((t, d), F32), jax.ShapeDtypeStruct((t, d // 2), jnp.uint32),
                   jax.ShapeDtypeStruct((TOP_K, t), jnp.int32), jax.ShapeDtypeStruct((TOP_K, t), F32),
                   jax.ShapeDtypeStruct((TOP_K, t), jnp.int32), jax.ShapeDtypeStruct((ne, 128), jnp.int32)],
        scratch_shapes=[pltpu.VMEM((ne, 128), F32)],
        compiler_params=_cparams("arbitrary"),
        name="postmix_router",
    )(att2, ys_tm, ga, gs, x2, gt1, sc2, sh2, g_post.reshape(1, d), g_pre.reshape(1, d), *weights)


def _sc_worker_base(per_worker):
    return (lax.axis_index("s") * SC_CORES + lax.axis_index("c")) * per_worker


def _sc_mesh():
    return plsc.VectorSubcoreMesh(core_axis_name="c", subcore_axis_name="s",
                                  num_cores=SC_CORES, num_subcores=SC_SUBCORES)


def _sc_scatter_kernel(rows_hbm, idx_hbm, out_hbm, idx_v, rows_v, load_sems, store_sems, *, per_worker):
    chunk = rows_v.shape[1]
    n_tokens = rows_hbm.shape[0]
    n_chunks = per_worker // chunk
    base = _sc_worker_base(per_worker)

    def load(c, b):
        off = pl.multiple_of(base + c * chunk, chunk)
        for k in range(TOP_K):
            pltpu.sync_copy(idx_hbm.at[pl.ds(k * n_tokens + off, chunk)], idx_v.at[b, k])
        return pltpu.make_async_copy(rows_hbm.at[pl.ds(off, chunk)], rows_v.at[b], load_sems.at[b])

    def stores(b):
        return [pltpu.make_async_copy(rows_v.at[b], out_hbm.at[idx_v.at[b, k]], store_sems.at[b])
                for k in range(TOP_K)]

    load(0, 0).start()

    @pl.loop(0, n_chunks, step=2)
    def _(c0):
        for b in range(2):
            c = c0 + b

            @pl.when(c > 0)
            def _():
                for cp in stores(1 - b):
                    cp.wait()

            @pl.when(c + 1 < n_chunks)
            def _():
                load(c + 1, 1 - b).start()

            pltpu.make_async_copy(rows_hbm.at[pl.ds(0, chunk)], rows_v.at[b], load_sems.at[b]).wait()
            for cp in stores(b):
                cp.start()

    for cp in stores((n_chunks - 1) % 2):
        cp.wait()


def _sc_scatter_rows(rows, idx, n_out):
    t, d = rows.shape
    workers = SC_CORES * SC_SUBCORES
    per_worker = t // workers
    assert t % workers == 0 and per_worker % (2 * SC_GATHER_ROWS) == 0
    return pl.kernel(
        functools.partial(_sc_scatter_kernel, per_worker=per_worker),
        out_type=jax.ShapeDtypeStruct((n_out, d), rows.dtype),
        mesh=_sc_mesh(),
        scratch_types=[pltpu.VMEM((2, TOP_K, SC_GATHER_ROWS), jnp.int32),
                       pltpu.VMEM((2, SC_GATHER_ROWS, d), rows.dtype),
                       pltpu.SemaphoreType.DMA((2,)), pltpu.SemaphoreType.DMA((2,))],
        name="sc_row_scatter",
    )(rows, idx)


def _sc_gather_kernel(table_hbm, idx_hbm, out_hbm, idx_v, rows_v, sems, *, per_worker):
    chunk = rows_v.shape[1]
    n_chunks = per_worker // chunk
    base = _sc_worker_base(per_worker)

    def gather(c, b):
        off = pl.multiple_of(base + c * chunk, chunk)
        pltpu.sync_copy(idx_hbm.at[pl.ds(off, chunk)], idx_v.at[b])
        return pltpu.make_async_copy(table_hbm.at[idx_v.at[b]], rows_v.at[b], sems.at[b])

    gather(0, 0).start()

    @pl.loop(0, n_chunks, step=2)
    def _(c0):
        for b in range(2):
            c = c0 + b

            @pl.when(c + 1 < n_chunks)
            def _():
                gather(c + 1, 1 - b).start()

            pltpu.make_async_copy(table_hbm.at[idx_v.at[b]], rows_v.at[b], sems.at[b]).wait()
            off = pl.multiple_of(base + c * chunk, chunk)
            pltpu.sync_copy(rows_v.at[b], out_hbm.at[pl.ds(off, chunk)])


def _sc_gather_rows(table, idx):
    n = idx.shape[0]
    d = table.shape[1]
    workers = SC_CORES * SC_SUBCORES
    per_worker = n // workers
    assert n % workers == 0 and per_worker % (2 * SC_GATHER_ROWS) == 0
    return pl.kernel(
        functools.partial(_sc_gather_kernel, per_worker=per_worker),
        out_type=jax.ShapeDtypeStruct((n, d), table.dtype),
        mesh=_sc_mesh(),
        scratch_types=[pltpu.VMEM((2, SC_GATHER_ROWS), jnp.int32),
                       pltpu.VMEM((2, SC_GATHER_ROWS, d), table.dtype),
                       pltpu.SemaphoreType.DMA((2,))],
        name="sc_row_gather",
    )(table, idx)


def _experts_kernel(be_ref, nused_ref, valid_ref, next_ref, x_ref, b1_ref, b2_ref, w1_hbm, w2_hbm, y_ref,
                    w1s_ref, w2s_ref, w1b_ref, w2b_ref, sems, *, layer):
    i = pl.program_id(0)
    prev = be_ref[jnp.maximum(i - 1, 0)]
    fresh = (i < nused_ref[0]) & ((i == 0) | (be_ref[i] != prev))

    def fetch(e):
        return (pltpu.make_async_copy(w1_hbm.at[layer, e], w1s_ref, sems.at[0]),
                pltpu.make_async_copy(w2_hbm.at[layer, e], w2s_ref, sems.at[1]))

    @pl.when(i == 0)
    def _():
        for cp in fetch(be_ref[0]):
            cp.start()

    @pl.when(fresh)
    def _():
        for cp in fetch(be_ref[i]):
            cp.wait()
        w1b_ref[...] = w1s_ref[...].astype(BF16)
        w2b_ref[...] = w2s_ref[...].astype(BF16)

        @pl.when(next_ref[i] >= 0)
        def _():
            for cp in fetch(next_ref[i]):
                cp.start()

    used = i < nused_ref[0]
    valid = valid_ref[i]
    half = x_ref.shape[0] // 2

    def ffn(n_rows):
        row = lax.broadcasted_iota(jnp.int32, (n_rows, x_ref.shape[1]), 0)
        x_lo, x_hi = _unpack_bf16_pairs(jnp.where(row < valid, x_ref[0:n_rows, :], jnp.uint32(0)))
        x = jnp.concatenate([x_lo.astype(BF16), x_hi.astype(BF16)], axis=1)
        gu = jnp.dot(x, w1b_ref[...], preferred_element_type=F32) + b1_ref[...]
        g = jnp.minimum(gu[:, :D_FF], SWIGLU_LIMIT)
        up = jnp.clip(gu[:, D_FF:], -SWIGLU_LIMIT, SWIGLU_LIMIT)
        act = ((up + 1.0) * g * jax.nn.sigmoid(SWIGLU_ALPHA * g)).astype(BF16)
        y = jnp.dot(act, w2b_ref[...], preferred_element_type=F32) + b2_ref[...]
        y_ref[0:n_rows, :] = _pack_bf16_pairs(y)

    @pl.when(used & (valid > half))
    def _():
        ffn(2 * half)

    def short(n_rows):
        ffn(n_rows)
        y_ref[n_rows:, :] = jnp.zeros((2 * half - n_rows, y_ref.shape[1]), y_ref.dtype)

    pl.when(used & (valid <= half) & (valid > half // 2))(lambda: short(half))
    pl.when(used & (valid <= half // 2))(lambda: short(half // 2))

    @pl.when(jnp.logical_not(used))
    def _():
        y_ref[...] = jnp.zeros_like(y_ref)


def _experts(xb, block_e, n_used, valid, next_e, w1, b1, w2, b2, layer):
    p_rows, packed_w = xb.shape
    d = 2 * packed_w
    depth, ne = w1.shape[:2]
    rb = EXPERT_ROWS
    bmap = lambda i, be, nu, va, nx: (layer, be[i], 0, 0)
    rows = pl.BlockSpec((rb, packed_w), lambda i, be, nu, va, nx: (i, 0))
    grid_spec = pltpu.PrefetchScalarGridSpec(
        num_scalar_prefetch=4,
        grid=(p_rows // rb,),
        in_specs=[
            rows,
            pl.BlockSpec((None, None, 1, 2 * D_FF), bmap),
            pl.BlockSpec((None, None, 1, d), bmap),
            pl.BlockSpec(memory_space=pl.ANY),
            pl.BlockSpec(memory_space=pl.ANY),
        ],
        out_specs=rows,
        scratch_shapes=[pltpu.VMEM((d, 2 * D_FF), w1.dtype), pltpu.VMEM((D_FF, d), w2.dtype),
                        pltpu.VMEM((d, 2 * D_FF), BF16), pltpu.VMEM((D_FF, d), BF16),
                        pltpu.SemaphoreType.DMA((2,))],
    )
    return pl.pallas_call(
        functools.partial(_experts_kernel, layer=layer),
        grid_spec=grid_spec,
        out_shape=jax.ShapeDtypeStruct((p_rows, packed_w), jnp.uint32),
        compiler_params=_cparams("arbitrary"),
        name="expert_ffn",
    )(block_e, n_used, valid, next_e, xb, b1.reshape(depth, ne, 1, 2 * D_FF), b2.reshape(depth, ne, 1, d), w1, w2)


def _combine_kernel(gate_ref, x_ref, gt_ref, g_ref, y0_ref, y1_ref, y2_ref, y3_ref, o_ref):
    tm = x_ref.shape[0]
    gates = gate_ref[...]
    gates = jnp.concatenate([gates, jnp.zeros((LANES - TOP_K, tm), F32)], axis=0).T
    lo, hi = _unpack_bf16_pairs(y0_ref[...])
    y_lo, y_hi = gates[:, 0:1] * lo, gates[:, 0:1] * hi
    for k, y_ref in enumerate((y1_ref, y2_ref, y3_ref), start=1):
        lo, hi = _unpack_bf16_pairs(y_ref[...])
        y_lo, y_hi = y_lo + gates[:, k:k + 1] * lo, y_hi + gates[:, k:k + 1] * hi
    y = jnp.concatenate([y_lo, y_hi], axis=1)
    o_ref[...] = x_ref[...] + gt_ref[0] * _rms(y, g_ref[...])


def _combine(y4, gate_t, x2, gt2, g_post, seq):
    t, d = x2.shape
    tm = TOKEN_TILE
    tiles = t // tm
    tiles_per_seq = seq // tm
    row = pl.BlockSpec((tm, d), lambda i: (i, 0))
    slot = lambda k: pl.BlockSpec((tm, y4.shape[1]), lambda i: (k * tiles + i, 0))
    return pl.pallas_call(
        _combine_kernel,
        grid=(tiles,),
        in_specs=[pl.BlockSpec((TOP_K, tm), lambda i: (0, i)),
                  row, pl.BlockSpec((1, 1, d), lambda i: (i // tiles_per_seq, 0, 0)), _full((1, d))]
        + [slot(k) for k in range(TOP_K)],
        out_specs=row,
        out_shape=jax.ShapeDtypeStruct((t, d), F32),
        compiler_params=_cparams("arbitrary"),
        name="expert_combine",
    )(gate_t, x2, gt2, g_post.reshape(1, d), y4, y4, y4, y4)


def _route_plan(idx_t, rank_t, counts):
    rb = EXPERT_ROWS
    k, t = idx_t.shape
    padded = (counts + rb - 1) // rb * rb
    pad_ends = jnp.cumsum(padded)
    pad_starts = pad_ends - padded
    experts = jnp.arange(N_EXPERTS, dtype=jnp.int32)
    start_of = jnp.sum(jnp.where(idx_t[None] == experts[:, None, None], pad_starts[:, None, None], 0), axis=0)
    dest = (start_of + rank_t).astype(jnp.int32)
    n_blocks = (k * t) // rb + N_EXPERTS
    blk_start = jnp.arange(n_blocks, dtype=jnp.int32) * rb
    block_e = jnp.minimum(jnp.sum(pad_ends[None, :] <= blk_start[:, None], axis=1), N_EXPERTS - 1)
    onehot_e = block_e[:, None] == experts[None, :]
    cnt_b = jnp.sum(jnp.where(onehot_e, counts[None, :], 0), axis=1)
    start_b = jnp.sum(jnp.where(onehot_e, pad_starts[None, :], 0), axis=1)
    valid = jnp.clip(cnt_b - (blk_start - start_b), 0, rb).astype(jnp.int32)
    n_used = (pad_ends[-1] // rb).astype(jnp.int32).reshape(1)
    later_nonempty = (experts[None, :] > experts[:, None]) & (counts[None, :] > 0)
    next_nonempty = jnp.min(jnp.where(later_nonempty, experts[None, :], N_EXPERTS), axis=1)
    next_nonempty = jnp.where(next_nonempty == N_EXPERTS, -1, next_nonempty)
    next_e = jnp.sum(jnp.where(onehot_e, next_nonempty[None, :], 0), axis=1).astype(jnp.int32)
    return dest.reshape(-1), block_e.astype(jnp.int32), n_used, valid, next_e, n_blocks * rb


def kernel(x, c, rel_bias, w_ada, b_ada, g_pre_mix, g_post_mix, g_pre_ffn, g_post_ffn, w_in, ssm_a_re, ssm_a_im, ssm_log_dt, ssm_b_re, ssm_b_im, ssm_c_re, ssm_c_im, ssm_d, w_glu, w_ssm_out, w_att_out, w_out, w_router, b_router, w_exp_in, b_exp_in, w_exp_out, b_exp_out):
    bsz, seq, d = x.shape
    depth = w_ada.shape[0]
    t = bsz * seq
    assert d == D_MODEL and seq % TOKEN_TILE == 0 and TOKEN_TILE % MOBA_BLOCK == 0
    assert seq % (SSM_CHUNK * SSM_CHUNKS_PER_STEP) == 0

    mod = _ada_mod(c, w_ada, b_ada)
    bias_tiles, far_bias = _moba_bias_tiles(rel_bias.astype(F32))
    x2 = x.reshape(t, d)
    for l in range(depth):
        sh1, sc1, gt1, sh2, sc2, gt2 = [m.reshape(bsz, 1, d) for m in jnp.split(mod[l], N_MOD, axis=-1)]
        q4, k4, vt4, u_tm, ga, gs = _premix(x2, g_pre_mix[l], sc1, sh1, w_in[l], seq)
        att = _moba(q4, k4, vt4, bias_tiles, far_bias, bsz).reshape(t, ATT_WIDTH)
        ssm_w = _ssm_weights(ssm_a_re[l], ssm_a_im[l], ssm_log_dt[l], ssm_b_re[l], ssm_b_im[l],
                             ssm_c_re[l], ssm_c_im[l], ssm_d[l])
        ys_tm = _ssm(u_tm, ssm_w, bsz, seq)
        x1, h2, idx_t, gate_t, rank_t, cnt = _postmix(
            att, ys_tm, ga, gs, x2, gt1, sc2, sh2, g_post_mix[l], g_pre_ffn[l],
            w_glu[l], w_ssm_out[l], w_att_out[l], w_out[l], w_router[l], b_router[l], seq)
        dest_flat, block_e, n_used, valid, next_e, p_rows = _route_plan(idx_t, rank_t, cnt[:, 0])
        xb = _sc_scatter_rows(h2, dest_flat, p_rows)
        yb = _experts(xb, block_e, n_used, valid, next_e, w_exp_in, b_exp_in, w_exp_out, b_exp_out, l)
        y4 = _sc_gather_rows(yb, dest_flat)
        x2 = _combine(y4, gate_t, x1, gt2, g_post_ffn[l], seq)
    return x2.reshape(bsz, seq, d)
```

```python
import functools
import math

import jax
import jax.numpy as jnp
from jax import lax
from jax.experimental import pallas as pl
from jax.experimental.pallas import tpu as pltpu
from jax.experimental.pallas import tpu_sc as plsc

F32 = jnp.float32
BF16 = jnp.bfloat16

D_MODEL = 1024
ATT_HEADS = 8
HEAD_DIM = 64
ATT_WIDTH = ATT_HEADS * HEAD_DIM
MOBA_BLOCK = 256
MOBA_TOPK = 3
REL_BUCKETS = 32
REL_MAX_DIST = 128
SSM_WIDTH = D_MODEL // 2
SSM_GROUP = 16
SSM_GROUPS = SSM_WIDTH // SSM_GROUP
SSM_STATE = 64
N_EXPERTS = 32
TOP_K = 4
D_FF = D_MODEL
SWIGLU_ALPHA = 1.702
SWIGLU_LIMIT = 7.0
RMS_EPS = 1e-6
N_MOD = 6

SSM_CHUNK = 8
SSM_CHUNKS_PER_STEP = 16
LANES = 128
MXU_DIM = 256
TOKEN_TILE = 1024
COMBINE_PARTS = 2
SC_CORES = 2
SC_SUBCORES = 16
SC_GATHER_ROWS = 64
EXPERT_ROWS = 1024
MASK_NEG = -1e30
LOG2E = math.log2(math.e)
SUM_ROWS = 16
VMEM_LIMIT = 56 * 1024 * 1024

_NT = (((1,), (1,)), ((), ()))


def _cparams(*sem):
    return pltpu.CompilerParams(dimension_semantics=sem, vmem_limit_bytes=VMEM_LIMIT)


def _pack_bf16_pairs(x):
    n = x.shape[1] // 2
    bits = lax.bitcast_convert_type(x.astype(BF16).astype(F32), jnp.uint32)
    return lax.shift_right_logical(bits[:, :n], jnp.uint32(16)) | (bits[:, n:] & jnp.uint32(0xFFFF0000))


def _unpack_bf16_pairs(w):
    lo = lax.bitcast_convert_type(lax.shift_left(w, jnp.uint32(16)), F32)
    hi = lax.bitcast_convert_type(w & jnp.uint32(0xFFFF0000), F32)
    return lo, hi


def _rms(x, g):
    return x * lax.rsqrt(jnp.mean(x * x, axis=-1, keepdims=True) + RMS_EPS) * g


def _full(shape, **kw):
    n = len(shape)
    return pl.BlockSpec(shape, lambda *_: (0,) * n, **kw)


def _ada_kernel(c_ref, w_ref, b_ref, o_ref):
    c = c_ref[...]
    cond = c * jax.nn.sigmoid(c)
    o_ref[0] = jnp.dot(cond, w_ref[0], preferred_element_type=F32,
                       precision=lax.Precision.HIGHEST) + b_ref[0]


def _ada_mod(c, w_ada, b_ada):
    depth, d, nd = w_ada.shape
    bsz = c.shape[0]
    return pl.pallas_call(
        _ada_kernel,
        grid=(depth, nd // d),
        in_specs=[
            pl.BlockSpec((bsz, d), lambda l, j: (0, 0)),
            pl.BlockSpec((1, d, d), lambda l, j: (l, 0, j)),
            pl.BlockSpec((1, 1, d), lambda l, j: (l, 0, j)),
        ],
        out_specs=pl.BlockSpec((1, bsz, d), lambda l, j: (l, 0, j)),
        out_shape=jax.ShapeDtypeStruct((depth, bsz, nd), F32),
        compiler_params=_cparams("arbitrary", "arbitrary"),
        name="ada_mod",
    )(c, w_ada, b_ada.reshape(depth, 1, nd))


def _premix_kernel(x_ref, g_ref, sc_ref, sh_ref, wq_ref, wk_ref, wvt_ref, wu_ref, wga_ref, wgs_ref,
                   q_ref, k_ref, vt_ref, u_ref, ga_ref, gs_ref):
    x = x_ref[...]
    h = _rms(x, g_ref[...]) * (1.0 + sc_ref[0]) + sh_ref[0]
    hb = h.astype(BF16)
    nblk = q_ref.shape[0]
    q = (jnp.dot(hb, wq_ref[...], preferred_element_type=F32) * (HEAD_DIM ** -0.5 * LOG2E)).astype(BF16)
    k = jnp.dot(hb, wk_ref[...], preferred_element_type=F32).astype(BF16)
    vt = lax.dot_general(wvt_ref[...], hb, _NT, preferred_element_type=F32).astype(BF16)
    for r in range(nblk):
        q_ref[r] = q[r * MOBA_BLOCK:(r + 1) * MOBA_BLOCK]
        k_ref[r] = k[r * MOBA_BLOCK:(r + 1) * MOBA_BLOCK]
        vt_ref[r] = vt[:, r * MOBA_BLOCK:(r + 1) * MOBA_BLOCK]
    u_ref[...] = jnp.dot(hb, wu_ref[...], preferred_element_type=F32).astype(BF16)
    ga = jnp.dot(hb, wga_ref[...], preferred_element_type=F32)
    ga_ref[...] = jax.nn.sigmoid(ga).astype(BF16)
    gs = jnp.dot(hb, wgs_ref[...], preferred_element_type=F32)
    gs_ref[...] = jax.nn.sigmoid(gs).astype(BF16)


def _premix(x2, g, sc, sh, w_in, seq):
    t, d = x2.shape
    tm = TOKEN_TILE
    tiles_per_seq = seq // tm
    nblk = tm // MOBA_BLOCK
    a = ATT_WIDTH
    assert a == SSM_WIDTH and d == 2 * a and w_in.shape[1] == 4 * a + 2 * d
    wb = w_in.astype(BF16)
    wvt = wb[:, 2 * a:3 * a].T

    def cols(width, index):
        return pl.BlockSpec((d, width), lambda i: (0, index), pipeline_mode=pl.Buffered(1))

    mod_spec = pl.BlockSpec((1, 1, d), lambda i: (i // tiles_per_seq, 0, 0))
    blk3 = lambda rows, cols: pl.BlockSpec((nblk, rows, cols), lambda i: (i, 0, 0))
    row = lambda cols: pl.BlockSpec((tm, cols), lambda i: (i, 0))
    time_major = pl.BlockSpec((tm, SSM_WIDTH), lambda i: (i % tiles_per_seq, i // tiles_per_seq))
    once = functools.partial(_full, pipeline_mode=pl.Buffered(1))
    nb_total = t // MOBA_BLOCK
    return pl.pallas_call(
        _premix_kernel,
        grid=(t // tm,),
        in_specs=[row(d), _full((1, d)), mod_spec, mod_spec,
                  cols(a, 0), cols(a, 1), once((a, d)), cols(a, 3), cols(d, 2), cols(d, 3)],
        out_specs=[blk3(MOBA_BLOCK, a), blk3(MOBA_BLOCK, a), blk3(a, MOBA_BLOCK),
                   time_major, row(d), row(d)],
        out_shape=[jax.ShapeDtypeStruct((nb_total, MOBA_BLOCK, a), BF16),
                   jax.ShapeDtypeStruct((nb_total, MOBA_BLOCK, a), BF16),
                   jax.ShapeDtypeStruct((nb_total, a, MOBA_BLOCK), BF16),
                   jax.ShapeDtypeStruct((seq, (t // seq) * SSM_WIDTH), BF16),
                   jax.ShapeDtypeStruct((t, d), BF16),
                   jax.ShapeDtypeStruct((t, d), BF16)],
        compiler_params=_cparams("arbitrary"),
        name="premix_inproj",
    )(x2, g.reshape(1, d), sc, sh, wb, wb, wvt, wb, wb, wb)


def _moba_kernel(far_ref, q_ref, k_ref, vt_ref, bias_ref, o_ref, kmean_ref, neg_ref, *state_refs):
    m_refs = state_refs[0::2]
    acc_refs = state_refs[1::2]
    nb = k_ref.shape[0]
    blk = MOBA_BLOCK
    i = pl.program_id(1)

    @pl.when(i == 0)
    def _():
        for n in range(nb):
            kmean_ref[n:n + 1, :] = jnp.mean(k_ref[n].astype(F32), axis=0, keepdims=True)

    lane = lax.broadcasted_iota(jnp.int32, (blk, 2 * HEAD_DIM), 1)
    nidx = lax.broadcasted_iota(jnp.int32, (nb, blk), 0)
    valid = nidx < i
    pair_w = 2 * HEAD_DIM

    def pair_cols(head):
        return slice((head // 2) * pair_w, (head // 2 + 1) * pair_w)

    def head_rows(head):
        return slice(head * HEAD_DIM, (head + 1) * HEAD_DIM)

    def q_head(head):
        q2 = q_ref[0, :, pair_cols(head)]
        lo = (head % 2) * HEAD_DIM
        return jnp.where((lane >= lo) & (lane < lo + HEAD_DIM), q2, jnp.zeros_like(q2))

    for head in range(ATT_HEADS):
        km = kmean_ref[:, pair_cols(head)]
        km_hi = km.astype(BF16)
        km_lo = (km - km_hi.astype(F32)).astype(BF16)
        parts = lax.dot_general(jnp.concatenate([km_hi, km_lo], axis=0), q_head(head), _NT,
                                preferred_element_type=F32)
        gate = parts[0:nb] + parts[nb:2 * nb]
        g = jnp.where(valid, gate, -jnp.inf)
        sel = jnp.zeros((nb, blk), jnp.bool_)
        for _ in range(MOBA_TOPK):
            mx = jnp.max(g, axis=0, keepdims=True)
            first = jnp.min(jnp.where(g == mx, nidx, nb), axis=0, keepdims=True)
            pick = nidx == first
            sel = sel | pick
            g = jnp.where(pick, -jnp.inf, g)
        neg_ref[head * nb:(head + 1) * nb, :] = jnp.where(sel & valid, 0.0, MASK_NEG)

    def scores(head, js, adds):
        qh = q_head(head)
        parts = [lax.dot_general(k_ref[j, :, pair_cols(head)], qh, _NT, preferred_element_type=F32) + a
                 for j, a in zip(js, adds)]
        return parts[0] if len(parts) == 1 else jnp.concatenate(parts, axis=0)

    def weighted_values(head, js, p):
        vals = [vt_ref[j, head_rows(head), :] for j in js]
        vb = vals[0] if len(vals) == 1 else jnp.concatenate(vals, axis=1)
        vb = jnp.concatenate([vb, jnp.ones((SUM_ROWS, vb.shape[1]), BF16)], axis=0)
        return jnp.dot(vb, p.astype(BF16), preferred_element_type=F32)

    def start(head, js, st):
        m = jnp.max(st, axis=0, keepdims=True)
        m_refs[head][...] = m
        acc_refs[head][...] = weighted_values(head, js, jnp.exp2(st - m))

    def update(head, js, st):
        m = m_refs[head][...]
        m_new = jnp.maximum(m, jnp.max(st, axis=0, keepdims=True))
        alpha = jnp.exp2(m - m_new)
        m_refs[head][...] = m_new
        acc_refs[head][...] = alpha * acc_refs[head][...] + weighted_values(head, js, jnp.exp2(st - m_new))

    def sweep(js, adds_of, absorb):
        st = scores(0, js, adds_of(0))
        for head in range(ATT_HEADS):
            nxt = scores(head + 1, js, adds_of(head + 1)) if head + 1 < ATT_HEADS else None
            absorb(head, js, st)
            st = nxt

    def far_add(head, j):
        return neg_ref[pl.ds(head * nb + j, 1), :] + far_ref[head]

    def near_adds(h):
        return [bias_ref[h, 0], bias_ref[h, 1] + neg_ref[pl.ds(h * nb + i - 1, 1), :]]

    @pl.when(i == 0)
    def _():
        sweep([i], lambda h: [bias_ref[h, 0]], start)

    @pl.when((i == 1) | (i == 2))
    def _():
        sweep([i, i - 1], near_adds, start)

    @pl.when(i >= 3)
    def _():
        sweep([i, i - 1, i - 2, i - 3],
              lambda h: near_adds(h) + [far_add(h, i - 2), far_add(h, i - 3)], start)

    n_far = jnp.where(i >= 3, i - 3, jnp.maximum(i - 1, 0))

    def far_step(first, count):
        js = [first + c for c in range(count)]
        sweep(js, lambda h: [far_add(h, j) for j in js], update)

    def far_quad(jj, carry):
        far_step(4 * jj, 4)
        return carry

    lax.fori_loop(0, n_far // 4, far_quad, 0)
    rest = n_far % 4
    pl.when(rest >= 2)(lambda: far_step(n_far - rest, 2))
    pl.when(rest % 2 == 1)(lambda: far_step(n_far - 1, 1))

    for hp in range(ATT_HEADS // 2):
        outs = [acc_refs[h][0:HEAD_DIM, :] / acc_refs[h][HEAD_DIM:HEAD_DIM + 1, :] for h in (2 * hp, 2 * hp + 1)]
        pair = jnp.concatenate(outs, axis=0)
        o_ref[0, :, hp * pair_w:(hp + 1) * pair_w] = pair.T.astype(BF16)


def _rel_bucket(dist):
    n = jnp.maximum(dist, 0)
    max_exact = REL_BUCKETS // 2
    nf = jnp.maximum(n, 1).astype(F32)
    large = max_exact + (jnp.log(nf / max_exact) / math.log(REL_MAX_DIST / max_exact)
                         * (REL_BUCKETS - max_exact)).astype(jnp.int32)
    large = jnp.minimum(large, REL_BUCKETS - 1)
    return jnp.where(n < max_exact, n, large)


def _moba_bias_tiles(rel_bias):
    blk = MOBA_BLOCK
    span = 2 * blk
    rel_bias = rel_bias * LOG2E
    vec = rel_bias[_rel_bucket(jnp.arange(span))].T.astype(F32)
    masked = jnp.full_like(vec[:, :blk], MASK_NEG)
    ring_own = jnp.concatenate([vec[:, :blk], masked], axis=1)
    ring_prev = jnp.concatenate([vec[:, blk:], vec[:, :blk]], axis=1)

    def toeplitz(ring):
        flat = jnp.tile(ring, (1, blk))[:, :blk * (span - 1)]
        return flat.reshape(-1, blk, span - 1)[:, :, :blk]

    tiles = jnp.stack([toeplitz(ring_own), toeplitz(ring_prev)], axis=1)
    far = rel_bias[_rel_bucket(jnp.array(span))]
    return tiles, far.astype(F32)


def _moba(q4, k4, vt4, bias_tiles, far_bias, bsz):
    nb_total, blk, a = q4.shape
    nb = nb_total // bsz
    return pl.pallas_call(
        _moba_kernel,
        grid=(bsz, nb),
        in_specs=[
            pl.BlockSpec(memory_space=pltpu.SMEM),
            pl.BlockSpec((1, blk, a), lambda b, i: (b * nb + i, 0, 0)),
            pl.BlockSpec((nb, blk, a), lambda b, i: (b, 0, 0)),
            pl.BlockSpec((nb, a, blk), lambda b, i: (b, 0, 0)),
            _full(bias_tiles.shape),
        ],
        out_specs=pl.BlockSpec((1, blk, a), lambda b, i: (b * nb + i, 0, 0)),
        out_shape=jax.ShapeDtypeStruct((nb_total, blk, a), BF16),
        scratch_shapes=[pltpu.VMEM((nb, a), F32), pltpu.VMEM((ATT_HEADS * nb, blk), F32)]
        + [pltpu.VMEM((1, blk), F32), pltpu.VMEM((HEAD_DIM + SUM_ROWS, blk), F32)] * ATT_HEADS,
        compiler_params=_cparams("arbitrary", "arbitrary"),
        name="moba_attention",
    )(far_bias, q4, k4, vt4, bias_tiles)


def _ssm_weights(a_re, a_im, log_dt, b_re, b_im, c_re, c_im, d_skip):
    g, p, cw, n = SSM_GROUPS, SSM_STATE, SSM_GROUP, SSM_CHUNK
    gs = LANES // cw
    nq = g // gs
    npair = n // 2
    pairs_per_set = gs // 2
    lam_re = jnp.minimum(a_re.astype(F32), -1e-4)
    lam_im = a_im.astype(F32)
    dt = jnp.exp(log_dt.astype(F32))[:, None]
    mag = jnp.exp(lam_re * dt)
    lb_re = mag * jnp.cos(lam_im * dt)
    lb_im = mag * jnp.sin(lam_im * dt)
    n_re = lb_re - 1.0
    n_im = lb_im
    den = lam_re * lam_re + lam_im * lam_im
    z_re = ((n_re * lam_re + n_im * lam_im) / den)[..., None]
    z_im = ((n_im * lam_re - n_re * lam_im) / den)[..., None]
    br, bi = b_re.astype(F32), b_im.astype(F32)
    bb_re = z_re * br - z_im * bi
    bb_im = z_re * bi + z_im * br
    pw_re, pw_im = [jnp.ones_like(lb_re)], [jnp.zeros_like(lb_im)]
    for _ in range(n):
        r, im = pw_re[-1], pw_im[-1]
        pw_re.append(r * lb_re - im * lb_im)
        pw_im.append(r * lb_im + im * lb_re)
    pw_re, pw_im = jnp.stack(pw_re), jnp.stack(pw_im)
    cr, ci = c_re.astype(F32), c_im.astype(F32)
    hi = lax.Precision.HIGHEST
    rev_re, rev_im = pw_re[n - 1::-1], pw_im[n - 1::-1]
    sb_re = jnp.einsum('sgp,gpc->sgcp', rev_re, bb_re) - jnp.einsum('sgp,gpc->sgcp', rev_im, bb_im)
    sb_im = jnp.einsum('sgp,gpc->sgcp', rev_re, bb_im) + jnp.einsum('sgp,gpc->sgcp', rev_im, bb_re)
    cl_re = jnp.einsum('gcp,tgp->tgpc', cr, pw_re[1:]) - jnp.einsum('gcp,tgp->tgpc', ci, pw_im[1:])
    cl_im = jnp.einsum('gcp,tgp->tgpc', cr, pw_im[1:]) + jnp.einsum('gcp,tgp->tgpc', ci, pw_re[1:])
    cp_re = jnp.einsum('gcp,tgp->tgcp', cr, pw_re[:n]) - jnp.einsum('gcp,tgp->tgcp', ci, pw_im[:n])
    cp_im = jnp.einsum('gcp,tgp->tgcp', cr, pw_im[:n]) + jnp.einsum('gcp,tgp->tgcp', ci, pw_re[:n])
    kern = (jnp.einsum('tgcp,gpd->gtcd', cp_re, bb_re, precision=hi)
            - jnp.einsum('tgcp,gpd->gtcd', cp_im, bb_im, precision=hi))

    kp = jnp.concatenate([jnp.zeros_like(kern[:, :1]), kern], axis=1)
    dl = jnp.arange(npair)[:, None, None]
    s0 = jnp.arange(2)[None, :, None]
    t0 = jnp.arange(2)[None, None, :]
    lag = 2 * dl + t0 - s0
    kl = kp[:, lag + 1].reshape(nq, gs, npair, 2, 2, cw, cw)
    base = kl.transpose(0, 2, 3, 1, 6, 4, 5)
    same_group = jnp.eye(gs, dtype=F32)
    w_toe = jnp.concatenate(
        [base[..., t0, :] * same_group[:, g1][None, None, None, :, None, None]
         for t0 in range(2) for g1 in range(gs)], axis=-1)
    w_toe = w_toe.reshape(nq, npair, MXU_DIM, MXU_DIM).astype(BF16)

    member = (jnp.arange(gs)[None, :, None]
              == 2 * jnp.arange(pairs_per_set)[:, None, None] + jnp.arange(2)[None, None, :]).astype(F32)

    def pair_tiles(x):
        cols = [x[ri][:, None] * member[:, :, gl][None, :, None, None, :, None, None]
                for ri in range(2) for gl in range(2)]
        return jnp.concatenate(cols, axis=-1).reshape(g // 2, npair, MXU_DIM, MXU_DIM)

    sbs = jnp.stack([sb_re, sb_im]).reshape(2, npair, 2, nq, gs, cw, p)
    w_in_state = pair_tiles(sbs.transpose(0, 3, 1, 2, 4, 5, 6)).astype(BF16)
    sos = jnp.stack([cl_re, -cl_im]).reshape(2, npair, 2, nq, gs, p, cw)
    w_state_out = jnp.swapaxes(pair_tiles(sos.transpose(0, 3, 1, 2, 4, 6, 5)), -1, -2).astype(BF16)
    decay = jnp.stack([pw_re[n].reshape(g // 2, 2 * p), pw_im[n].reshape(g // 2, 2 * p)])
    dvec = d_skip.astype(F32).reshape(1, g * cw)
    return w_toe, w_in_state, w_state_out, decay, dvec


def _ssm_kernel(u_ref, wt_ref, wi_ref, wo_ref, dec_ref, d_ref, y_ref, s_ref, xp_ref, x_ref, *, bsz):
    kt = u_ref.shape[0]
    n = u_ref.shape[1] // bsz
    rows = kt * bsz
    npair = n // 2
    nq = wt_ref.shape[0]
    npairs_g = wi_ref.shape[0]
    per_set = npairs_g // nq
    half = MXU_DIM // 2

    @pl.when(pl.program_id(0) == 0)
    def _():
        x_ref[...] = jnp.zeros_like(x_ref)

    def piece(s, q):
        return u_ref[:, s * bsz:(s + 1) * bsz, q * LANES:(q + 1) * LANES].reshape(rows, LANES)

    lhs = {(sp, q): jnp.concatenate([piece(2 * sp, q), piece(2 * sp + 1, q)], axis=-1)
           for sp in range(npair) for q in range(nq)}

    for gp in range(npairs_g):
        q = gp // per_set
        acc = jnp.dot(lhs[0, q], wi_ref[gp, 0], preferred_element_type=F32)
        for sp in range(1, npair):
            acc = acc + jnp.dot(lhs[sp, q], wi_ref[gp, sp], preferred_element_type=F32)
        s_ref[:, gp * MXU_DIM:(gp + 1) * MXU_DIM] = acc

    for gp in range(npairs_g):
        re_cols = slice(gp * MXU_DIM, gp * MXU_DIM + half)
        im_cols = slice(gp * MXU_DIM + half, (gp + 1) * MXU_DIM)
        ar = dec_ref[0, gp:gp + 1, :]
        ai = dec_ref[1, gp:gp + 1, :]
        xr = x_ref[:, re_cols]
        xi = x_ref[:, im_cols]
        for kc in range(kt):
            rs = slice(kc * bsz, (kc + 1) * bsz)
            xp_ref[rs, re_cols] = xr.astype(BF16)
            xp_ref[rs, im_cols] = xi.astype(BF16)
            sr = s_ref[rs, re_cols]
            si = s_ref[rs, im_cols]
            xr, xi = ar * xr - ai * xi + sr, ar * xi + ai * xr + si
        x_ref[:, re_cols] = xr
        x_ref[:, im_cols] = xi

    for q in range(nq):
        for tp in range(npair):
            acc = jnp.dot(lhs[0, q], wt_ref[q, tp], preferred_element_type=F32)
            for sp in range(1, tp + 1):
                acc = acc + jnp.dot(lhs[sp, q], wt_ref[q, tp - sp], preferred_element_type=F32)
            for gp in range(q * per_set, (q + 1) * per_set):
                acc = acc + jnp.dot(xp_ref[:, gp * MXU_DIM:(gp + 1) * MXU_DIM], wo_ref[gp, tp],
                                    preferred_element_type=F32)
            for t0 in range(2):
                s = 2 * tp + t0
                y = acc[:, t0 * half:(t0 + 1) * half] + d_ref[:, q * LANES:(q + 1) * LANES] * piece(s, q).astype(F32)
                y_ref[:, s * bsz:(s + 1) * bsz, q * LANES:(q + 1) * LANES] = (
                    jax.nn.gelu(y).astype(BF16).reshape(kt, bsz, LANES))


def _ssm(u_tm, weights, bsz, seq):
    n = SSM_CHUNK
    kt = SSM_CHUNKS_PER_STEP
    nchunk = seq // n
    rows = kt * bsz
    state_w = SSM_GROUPS * 2 * SSM_STATE
    u3 = u_tm.reshape(nchunk, n * bsz, SSM_WIDTH)
    blk = pl.BlockSpec((kt, n * bsz, SSM_WIDTH), lambda i: (i, 0, 0))

    once = pl.Buffered(1)
    y3 = pl.pallas_call(
        functools.partial(_ssm_kernel, bsz=bsz),
        grid=(nchunk // kt,),
        in_specs=[blk] + [_full(w.shape, pipeline_mode=once) for w in weights],
        out_specs=blk,
        out_shape=jax.ShapeDtypeStruct(u3.shape, BF16),
        scratch_shapes=[pltpu.VMEM((rows, state_w), F32), pltpu.VMEM((rows, state_w), BF16),
                        pltpu.VMEM((bsz, state_w), F32)],
        compiler_params=_cparams("arbitrary"),
        name="s5_chunked_scan",
    )(u3, *weights)
    return y3.reshape(seq, bsz * SSM_WIDTH)


def _postmix_kernel(att_ref, ys_ref, ga_ref, gs_ref, x_ref, gt_ref, sc_ref, sh_ref, gpost_ref, gpre_ref,
                    wglu_ref, wso_ref, wao_ref, wo_ref, wrh_ref, wrl_ref, br_ref,
                    x1_ref, h2_ref, idx_ref, gate_ref, rank_ref, cnt_ref, run_ref):
    tm = x_ref.shape[0]
    ne = wrh_ref.shape[0]

    @pl.when(pl.program_id(0) == 0)
    def _():
        run_ref[...] = jnp.zeros_like(run_ref)

    halves = [slice(r * (tm // 2), (r + 1) * (tm // 2)) for r in range(2)]
    dot = functools.partial(jnp.dot, preferred_element_type=F32)
    att = [dot(att_ref[r, :], wao_ref[...]) for r in halves]
    glu = [dot(ys_ref[r, :], wglu_ref[...]) for r in halves]
    ssm = []
    for g in glu:
        sg = (g[:, :SSM_WIDTH] * jax.nn.sigmoid(g[:, SSM_WIDTH:])).astype(BF16)
        ssm.append(dot(sg, wso_ref[...]))
    y = []
    for r, a, s in zip(halves, att, ssm):
        merged = (ga_ref[r, :].astype(F32) * a + gs_ref[r, :].astype(F32) * s).astype(BF16)
        y.append(dot(merged, wo_ref[...]))
    logits = []
    for r, yh in zip(halves, y):
        x1 = x_ref[r, :] + gt_ref[0] * _rms(yh, gpost_ref[...])
        x1_ref[r, :] = x1
        h2 = _rms(x1, gpre_ref[...]) * (1.0 + sc_ref[0]) + sh_ref[0]
        h2_ref[r, :] = _pack_bf16_pairs(h2)
        h_hi = h2.astype(BF16)
        h_lo = (h2 - h_hi.astype(F32)).astype(BF16)
        logits.append(lax.dot_general(wrh_ref[...], h_hi, _NT, preferred_element_type=F32)
                      + lax.dot_general(wrh_ref[...], h_lo, _NT, preferred_element_type=F32)
                      + lax.dot_general(wrl_ref[...], h_hi, _NT, preferred_element_type=F32))
    logits = jnp.concatenate(logits, axis=1) + br_ref[...]
    eidx = lax.broadcasted_iota(jnp.int32, (ne, tm), 0)
    lg = logits
    vals, idxs = [], []
    for _ in range(TOP_K):
        mx = jnp.max(lg, axis=0, keepdims=True)
        first = jnp.min(jnp.where(lg == mx, eidx, ne), axis=0, keepdims=True)
        vals.append(mx)
        idxs.append(first)
        lg = jnp.where(eidx == first, -jnp.inf, lg)
    exps = [jnp.exp(v - vals[0]) for v in vals]
    denom = exps[0] + exps[1] + exps[2] + exps[3]
    gate_ref[...] = jnp.concatenate([e / denom for e in exps], axis=0)
    idx_ref[...] = jnp.concatenate(idxs, axis=0)
    onehot = jnp.where(lg == -jnp.inf, 1.0, 0.0)
    t_row = lax.broadcasted_iota(jnp.int32, (tm, tm), 0)
    t_col = lax.broadcasted_iota(jnp.int32, (tm, tm), 1)
    earlier = jnp.where(t_row < t_col, 1.0, 0.0).astype(BF16)
    before = jnp.dot(onehot.astype(BF16), earlier, preferred_element_type=F32) + run_ref[:, 0:1]
    ranks = [jnp.sum(jnp.where(eidx == ix, before, 0.0), axis=0, keepdims=True) for ix in idxs]
    rank_ref[...] = jnp.concatenate(ranks, axis=0).astype(jnp.int32)
    run_ref[...] = run_ref[...] + jnp.sum(onehot, axis=1, keepdims=True)
    cnt_ref[...] = run_ref[...].astype(jnp.int32)


def _postmix(att2, ys_tm, ga, gs, x2, gt1, sc2, sh2, g_post, g_pre, w_glu, w_ssm_out, w_att_out, w_out,
             w_router, b_router, seq):
    t, d = x2.shape
    tm = TOKEN_TILE
    tiles_per_seq = seq // tm
    ne = w_router.shape[1]
    wr_t = w_router.T.astype(F32)
    wr_hi = wr_t.astype(BF16)
    wr_lo = (wr_t - wr_hi.astype(F32)).astype(BF16)
    mod_spec = pl.BlockSpec((1, 1, d), lambda i: (i // tiles_per_seq, 0, 0))
    row = lambda cols: pl.BlockSpec((tm, cols), lambda i: (i, 0))
    time_major = pl.BlockSpec((tm, SSM_WIDTH), lambda i: (i % tiles_per_seq, i // tiles_per_seq))
    col = pl.BlockSpec((TOP_K, tm), lambda i: (0, i))
    weights = [w_glu.astype(BF16), w_ssm_out.astype(BF16), w_att_out.astype(BF16), w_out.astype(BF16),
               wr_hi, wr_lo, b_router.astype(F32).reshape(ne, 1)]
    return pl.pallas_call(
        _postmix_kernel,
        grid=(t // tm,),
        in_specs=[row(ATT_WIDTH), time_major, row(d), row(d), row(d), mod_spec, mod_spec, mod_spec,
                  _full((1, d)), _full((1, d))]
        + [_full(w.shape, pipeline_mode=pl.Buffered(1)) for w in weights],
        out_specs=[row(d), row(d // 2), col, col, col, _full((ne, 128))],
        out_shape=[jax.ShapeDtypeStruct((t, d), F32), jax.ShapeDtypeStruct((t, d // 2), jnp.uint32),
                   jax.ShapeDtypeStruct((TOP_K, t), jnp.int32), jax.ShapeDtypeStruct((TOP_K, t), F32),
                   jax.ShapeDtypeStruct((TOP_K, t), jnp.int32), jax.ShapeDtypeStruct((ne, 128), jnp.int32)],
        scratch_shapes=[pltpu.VMEM((ne, 128), F32)],
        compiler_params=_cparams("arbitrary"),
        name="postmix_router",
    )(att2, ys_tm, ga, gs, x2, gt1, sc2, sh2, g_post.reshape(1, d), g_pre.reshape(1, d), *weights)


def _sc_worker_base(per_worker):
    return (lax.axis_index("s") * SC_CORES + lax.axis_index("c")) * per_worker


def _sc_mesh():
    return plsc.VectorSubcoreMesh(core_axis_name="c", subcore_axis_name="s",
                                  num_cores=SC_CORES, num_subcores=SC_SUBCORES)


def _sc_scatter_kernel(rows_hbm, idx_hbm, out_hbm, idx_v, rows_v, load_sems, store_sems, *, per_worker):
    chunk = rows_v.shape[1]
    n_tokens = rows_hbm.shape[0]
    n_chunks = per_worker // chunk
    base = _sc_worker_base(per_worker)

    def load(c, b):
        off = pl.multiple_of(base + c * chunk, chunk)
        for k in range(TOP_K):
            pltpu.sync_copy(idx_hbm.at[pl.ds(k * n_tokens + off, chunk)], idx_v.at[b, k])
        return pltpu.make_async_copy(rows_hbm.at[pl.ds(off, chunk)], rows_v.at[b], load_sems.at[b])

    def stores(b):
        return [pltpu.make_async_copy(rows_v.at[b], out_hbm.at[idx_v.at[b, k]], store_sems.at[b])
                for k in range(TOP_K)]

    load(0, 0).start()

    @pl.loop(0, n_chunks, step=2)
    def _(c0):
        for b in range(2):
            c = c0 + b

            @pl.when(c > 0)
            def _():
                for cp in stores(1 - b):
                    cp.wait()

            @pl.when(c + 1 < n_chunks)
            def _():
                load(c + 1, 1 - b).start()

            pltpu.make_async_copy(rows_hbm.at[pl.ds(0, chunk)], rows_v.at[b], load_sems.at[b]).wait()
            for cp in stores(b):
                cp.start()

    for cp in stores((n_chunks - 1) % 2):
        cp.wait()


def _sc_scatter_rows(rows, idx, n_out):
    t, d = rows.shape
    workers = SC_CORES * SC_SUBCORES
    per_worker = t // workers
    assert t % workers == 0 and per_worker % (2 * SC_GATHER_ROWS) == 0
    return pl.kernel(
        functools.partial(_sc_scatter_kernel, per_worker=per_worker),
        out_type=jax.ShapeDtypeStruct((n_out, d), rows.dtype),
        mesh=_sc_mesh(),
        scratch_types=[pltpu.VMEM((2, TOP_K, SC_GATHER_ROWS), jnp.int32),
                       pltpu.VMEM((2, SC_GATHER_ROWS, d), rows.dtype),
                       pltpu.SemaphoreType.DMA((2,)), pltpu.SemaphoreType.DMA((2,))],
        name="sc_row_scatter",
    )(rows, idx)


def _sc_gather_kernel(table_hbm, idx_hbm, out_hbm, idx_v, rows_v, sems, *, per_worker):
    chunk = rows_v.shape[1]
    n_chunks = per_worker // chunk
    base = _sc_worker_base(per_worker)

    def gather(c, b):
        off = pl.multiple_of(base + c * chunk, chunk)
        pltpu.sync_copy(idx_hbm.at[pl.ds(off, chunk)], idx_v.at[b])
        return pltpu.make_async_copy(table_hbm.at[idx_v.at[b]], rows_v.at[b], sems.at[b])

    gather(0, 0).start()

    @pl.loop(0, n_chunks, step=2)
    def _(c0):
        for b in range(2):
            c = c0 + b

            @pl.when(c + 1 < n_chunks)
            def _():
                gather(c + 1, 1 - b).start()

            pltpu.make_async_copy(table_hbm.at[idx_v.at[b]], rows_v.at[b], sems.at[b]).wait()
            off = pl.multiple_of(base + c * chunk, chunk)
            pltpu.sync_copy(rows_v.at[b], out_hbm.at[pl.ds(off, chunk)])


def _sc_gather_rows(table, idx):
    n = idx.shape[0]
    d = table.shape[1]
    workers = SC_CORES * SC_SUBCORES
    per_worker = n // workers
    assert n % workers == 0 and per_worker % (2 * SC_GATHER_ROWS) == 0
    return pl.kernel(
        functools.partial(_sc_gather_kernel, per_worker=per_worker),
        out_type=jax.ShapeDtypeStruct((n, d), table.dtype),
        mesh=_sc_mesh(),
        scratch_types=[pltpu.VMEM((2, SC_GATHER_ROWS), jnp.int32),
                       pltpu.VMEM((2, SC_GATHER_ROWS, d), table.dtype),
                       pltpu.SemaphoreType.DMA((2,))],
        name="sc_row_gather",
    )(table, idx)


def _experts_kernel(be_ref, nused_ref, valid_ref, next_ref, x_ref, b1_ref, b2_ref, w1_hbm, w2_hbm, y_ref,
                    w1s_ref, w2s_ref, w1b_ref, w2b_ref, sems, *, layer):
    i = pl.program_id(0)
    prev = be_ref[jnp.maximum(i - 1, 0)]
    fresh = (i < nused_ref[0]) & ((i == 0) | (be_ref[i] != prev))

    def fetch(e):
        return (pltpu.make_async_copy(w1_hbm.at[layer, e], w1s_ref, sems.at[0]),
                pltpu.make_async_copy(w2_hbm.at[layer, e], w2s_ref, sems.at[1]))

    @pl.when(i == 0)
    def _():
        for cp in fetch(be_ref[0]):
            cp.start()

    @pl.when(fresh)
    def _():
        for cp in fetch(be_ref[i]):
            cp.wait()
        w1b_ref[...] = w1s_ref[...].astype(BF16)
        w2b_ref[...] = w2s_ref[...].astype(BF16)

        @pl.when(next_ref[i] >= 0)
        def _():
            for cp in fetch(next_ref[i]):
                cp.start()

    used = i < nused_ref[0]
    valid = valid_ref[i]
    half = x_ref.shape[0] // 2

    def ffn(n_rows):
        row = lax.broadcasted_iota(jnp.int32, (n_rows, x_ref.shape[1]), 0)
        x_lo, x_hi = _unpack_bf16_pairs(jnp.where(row < valid, x_ref[0:n_rows, :], jnp.uint32(0)))
        x = jnp.concatenate([x_lo.astype(BF16), x_hi.astype(BF16)], axis=1)
        gu = jnp.dot(x, w1b_ref[...], preferred_element_type=F32) + b1_ref[...]
        g = jnp.minimum(gu[:, :D_FF], SWIGLU_LIMIT)
        up = jnp.clip(gu[:, D_FF:], -SWIGLU_LIMIT, SWIGLU_LIMIT)
        act = ((up + 1.0) * g * jax.nn.sigmoid(SWIGLU_ALPHA * g)).astype(BF16)
        y = jnp.dot(act, w2b_ref[...], preferred_element_type=F32) + b2_ref[...]
        y_ref[0:n_rows, :] = _pack_bf16_pairs(y)

    @pl.when(used & (valid > half))
    def _():
        ffn(2 * half)

    def short(n_rows):
        ffn(n_rows)
        y_ref[n_rows:, :] = jnp.zeros((2 * half - n_rows, y_ref.shape[1]), y_ref.dtype)

    pl.when(used & (valid <= half) & (valid > half // 2))(lambda: short(half))
    pl.when(used & (valid <= half // 2))(lambda: short(half // 2))

    @pl.when(jnp.logical_not(used))
    def _():
        y_ref[...] = jnp.zeros_like(y_ref)


def _experts(xb, block_e, n_used, valid, next_e, w1, b1, w2, b2, layer):
    p_rows, packed_w = xb.shape
    d = 2 * packed_w
    depth, ne = w1.shape[:2]
    rb = EXPERT_ROWS
    bmap = lambda i, be, nu, va, nx: (layer, be[i], 0, 0)
    rows = pl.BlockSpec((rb, packed_w), lambda i, be, nu, va, nx: (i, 0))
    grid_spec = pltpu.PrefetchScalarGridSpec(
        num_scalar_prefetch=4,
        grid=(p_rows // rb,),
        in_specs=[
            rows,
            pl.BlockSpec((None, None, 1, 2 * D_FF), bmap),
            pl.BlockSpec((None, None, 1, d), bmap),
            pl.BlockSpec(memory_space=pl.ANY),
            pl.BlockSpec(memory_space=pl.ANY),
        ],
        out_specs=rows,
        scratch_shapes=[pltpu.VMEM((d, 2 * D_FF), w1.dtype), pltpu.VMEM((D_FF, d), w2.dtype),
                        pltpu.VMEM((d, 2 * D_FF), BF16), pltpu.VMEM((D_FF, d), BF16),
                        pltpu.SemaphoreType.DMA((2,))],
    )
    return pl.pallas_call(
        functools.partial(_experts_kernel, layer=layer),
        grid_spec=grid_spec,
        out_shape=jax.ShapeDtypeStruct((p_rows, packed_w), jnp.uint32),
        compiler_params=_cparams("arbitrary"),
        name="expert_ffn",
    )(block_e, n_used, valid, next_e, xb, b1.reshape(depth, ne, 1, 2 * D_FF), b2.reshape(depth, ne, 1, d), w1, w2)


def _combine_kernel(gate_ref, x_ref, gt_ref, g_ref, y0_ref, y1_ref, y2_ref, y3_ref, *rest):
    o_ref = rest[-1]
    tm = x_ref.shape[0]
    gates = gate_ref[...]
    gates = jnp.concatenate([gates, jnp.zeros((LANES - TOP_K, tm), F32)], axis=0).T
    lo, hi = _unpack_bf16_pairs(y0_ref[...])
    y_lo, y_hi = gates[:, 0:1] * lo, gates[:, 0:1] * hi
    for k, y_ref in enumerate((y1_ref, y2_ref, y3_ref), start=1):
        lo, hi = _unpack_bf16_pairs(y_ref[...])
        y_lo, y_hi = y_lo + gates[:, k:k + 1] * lo, y_hi + gates[:, k:k + 1] * hi
    y = jnp.concatenate([y_lo, y_hi], axis=1)
    o_ref[...] = x_ref[...] + gt_ref[0] * _rms(y, g_ref[...])


def _combine(y4_parts, gate_t, x2, gt2, g_post, seq):
    t, d = x2.shape
    tm = TOKEN_TILE
    parts = len(y4_parts)
    tiles = t // tm // parts
    tiles_per_seq = seq // tm
    out = None
    for p, y4 in enumerate(y4_parts):
        first = p * tiles
        row = pl.BlockSpec((tm, d), lambda i, first=first: (first + i, 0))
        slot = lambda k: pl.BlockSpec((tm, y4.shape[1]), lambda i, k=k: (k * tiles + i, 0))
        in_specs = [pl.BlockSpec((TOP_K, tm), lambda i, first=first: (0, first + i)), row,
                    pl.BlockSpec((1, 1, d), lambda i, first=first: ((first + i) // tiles_per_seq, 0, 0)),
                    _full((1, d))] + [slot(k) for k in range(TOP_K)]
        args = [gate_t, x2, gt2, g_post.reshape(1, d), y4, y4, y4, y4]
        aliases = {}
        if out is not None:
            in_specs.append(pl.BlockSpec(memory_space=pl.ANY))
            aliases = {len(args): 0}
            args.append(out)
        out = pl.pallas_call(
            _combine_kernel,
            grid=(tiles,),
            in_specs=in_specs,
            out_specs=row,
            out_shape=jax.ShapeDtypeStruct((t, d), F32),
            input_output_aliases=aliases,
            compiler_params=_cparams("arbitrary"),
            name="expert_combine",
        )(*args)
    return out


def _route_plan(idx_t, rank_t, counts):
    rb = EXPERT_ROWS
    k, t = idx_t.shape
    padded = (counts + rb - 1) // rb * rb
    pad_ends = jnp.cumsum(padded)
    pad_starts = pad_ends - padded
    experts = jnp.arange(N_EXPERTS, dtype=jnp.int32)
    start_of = jnp.sum(jnp.where(idx_t[None] == experts[:, None, None], pad_starts[:, None, None], 0), axis=0)
    dest = (start_of + rank_t).astype(jnp.int32)
    n_blocks = (k * t) // rb + N_EXPERTS
    blk_start = jnp.arange(n_blocks, dtype=jnp.int32) * rb
    block_e = jnp.minimum(jnp.sum(pad_ends[None, :] <= blk_start[:, None], axis=1), N_EXPERTS - 1)
    onehot_e = block_e[:, None] == experts[None, :]
    cnt_b = jnp.sum(jnp.where(onehot_e, counts[None, :], 0), axis=1)
    start_b = jnp.sum(jnp.where(onehot_e, pad_starts[None, :], 0), axis=1)
    valid = jnp.clip(cnt_b - (blk_start - start_b), 0, rb).astype(jnp.int32)
    n_used = (pad_ends[-1] // rb).astype(jnp.int32).reshape(1)
    later_nonempty = (experts[None, :] > experts[:, None]) & (counts[None, :] > 0)
    next_nonempty = jnp.min(jnp.where(later_nonempty, experts[None, :], N_EXPERTS), axis=1)
    next_nonempty = jnp.where(next_nonempty == N_EXPERTS, -1, next_nonempty)
    next_e = jnp.sum(jnp.where(onehot_e, next_nonempty[None, :], 0), axis=1).astype(jnp.int32)
    return dest.reshape(-1), block_e.astype(jnp.int32), n_used, valid, next_e, n_blocks * rb


def kernel(x, c, rel_bias, w_ada, b_ada, g_pre_mix, g_post_mix, g_pre_ffn, g_post_ffn, w_in, ssm_a_re, ssm_a_im, ssm_log_dt, ssm_b_re, ssm_b_im, ssm_c_re, ssm_c_im, ssm_d, w_glu, w_ssm_out, w_att_out, w_out, w_router, b_router, w_exp_in, b_exp_in, w_exp_out, b_exp_out):
    bsz, seq, d = x.shape
    depth = w_ada.shape[0]
    t = bsz * seq
    assert d == D_MODEL and seq % TOKEN_TILE == 0 and TOKEN_TILE % MOBA_BLOCK == 0
    assert seq % (SSM_CHUNK * SSM_CHUNKS_PER_STEP) == 0

    mod = _ada_mod(c, w_ada, b_ada)
    bias_tiles, far_bias = _moba_bias_tiles(rel_bias.astype(F32))
    x2 = x.reshape(t, d)
    for l in range(depth):
        sh1, sc1, gt1, sh2, sc2, gt2 = [m.reshape(bsz, 1, d) for m in jnp.split(mod[l], N_MOD, axis=-1)]
        q4, k4, vt4, u_tm, ga, gs = _premix(x2, g_pre_mix[l], sc1, sh1, w_in[l], seq)
        att = _moba(q4, k4, vt4, bias_tiles, far_bias, bsz).reshape(t, ATT_WIDTH)
        ssm_w = _ssm_weights(ssm_a_re[l], ssm_a_im[l], ssm_log_dt[l], ssm_b_re[l], ssm_b_im[l],
                             ssm_c_re[l], ssm_c_im[l], ssm_d[l])
        ys_tm = _ssm(u_tm, ssm_w, bsz, seq)
        x1, h2, idx_t, gate_t, rank_t, cnt = _postmix(
            att, ys_tm, ga, gs, x2, gt1, sc2, sh2, g_post_mix[l], g_pre_ffn[l],
            w_glu[l], w_ssm_out[l], w_att_out[l], w_out[l], w_router[l], b_router[l], seq)
        dest_flat, block_e, n_used, valid, next_e, p_rows = _route_plan(idx_t, rank_t, cnt[:, 0])
        xb = _sc_scatter_rows(h2, dest_flat, p_rows)
        yb = _experts(xb, block_e, n_used, valid, next_e, w_exp_in, b_exp_in, w_exp_out, b_exp_out, l)
        part = t // COMBINE_PARTS
        dest_kt = dest_flat.reshape(TOP_K, t)
        y4_parts = [_sc_gather_rows(yb, dest_kt[:, p * part:(p + 1) * part].reshape(-1))
                    for p in range(COMBINE_PARTS)]
        x2 = _combine(y4_parts, gate_t, x1, gt2, g_post_ffn[l], seq)
    return x2.reshape(bsz, seq, d)
```

```python
import functools
import math

import jax
import jax.numpy as jnp
from jax import lax
from jax.experimental import pallas as pl
from jax.experimental.pallas import tpu as pltpu
from jax.experimental.pallas import tpu_sc as plsc

F32 = jnp.float32
BF16 = jnp.bfloat16

D_MODEL = 1024
ATT_HEADS = 8
HEAD_DIM = 64
ATT_WIDTH = ATT_HEADS * HEAD_DIM
MOBA_BLOCK = 256
MOBA_TOPK = 3
REL_BUCKETS = 32
REL_MAX_DIST = 128
SSM_WIDTH = D_MODEL // 2
SSM_GROUP = 16
SSM_GROUPS = SSM_WIDTH // SSM_GROUP
SSM_STATE = 64
N_EXPERTS = 32
TOP_K = 4
D_FF = D_MODEL
SWIGLU_ALPHA = 1.702
SWIGLU_LIMIT = 7.0
RMS_EPS = 1e-6
N_MOD = 6

SSM_CHUNK = 8
SSM_CHUNKS_PER_STEP = 16
LANES = 128
MXU_DIM = 256
TOKEN_TILE = 1024
MOE_PARTS = 2
SC_CORES = 2
SC_SUBCORES = 16
SC_GATHER_ROWS = 64
EXPERT_ROWS = 1024
MASK_NEG = -1e30
LOG2E = math.log2(math.e)
SUM_ROWS = 16
VMEM_LIMIT = 56 * 1024 * 1024

_NT = (((1,), (1,)), ((), ()))


def _cparams(*sem):
    return pltpu.CompilerParams(dimension_semantics=sem, vmem_limit_bytes=VMEM_LIMIT)


def _pack_bf16_pairs(x):
    n = x.shape[1] // 2
    bits = lax.bitcast_convert_type(x.astype(BF16).astype(F32), jnp.uint32)
    return lax.shift_right_logical(bits[:, :n], jnp.uint32(16)) | (bits[:, n:] & jnp.uint32(0xFFFF0000))


def _unpack_bf16_pairs(w):
    lo = lax.bitcast_convert_type(lax.shift_left(w, jnp.uint32(16)), F32)
    hi = lax.bitcast_convert_type(w & jnp.uint32(0xFFFF0000), F32)
    return lo, hi


def _rms(x, g):
    return x * lax.rsqrt(jnp.mean(x * x, axis=-1, keepdims=True) + RMS_EPS) * g


def _full(shape, **kw):
    n = len(shape)
    return pl.BlockSpec(shape, lambda *_: (0,) * n, **kw)


def _ada_kernel(c_ref, w_ref, b_ref, o_ref):
    c = c_ref[...]
    cond = c * jax.nn.sigmoid(c)
    o_ref[0] = jnp.dot(cond, w_ref[0], preferred_element_type=F32,
                       precision=lax.Precision.HIGHEST) + b_ref[0]


def _ada_mod(c, w_ada, b_ada):
    depth, d, nd = w_ada.shape
    bsz = c.shape[0]
    return pl.pallas_call(
        _ada_kernel,
        grid=(depth, nd // d),
        in_specs=[
            pl.BlockSpec((bsz, d), lambda l, j: (0, 0)),
            pl.BlockSpec((1, d, d), lambda l, j: (l, 0, j)),
            pl.BlockSpec((1, 1, d), lambda l, j: (l, 0, j)),
        ],
        out_specs=pl.BlockSpec((1, bsz, d), lambda l, j: (l, 0, j)),
        out_shape=jax.ShapeDtypeStruct((depth, bsz, nd), F32),
        compiler_params=_cparams("arbitrary", "arbitrary"),
        name="ada_mod",
    )(c, w_ada, b_ada.reshape(depth, 1, nd))


def _premix_kernel(x_ref, g_ref, sc_ref, sh_ref, wq_ref, wk_ref, wvt_ref, wu_ref, wga_ref, wgs_ref,
                   q_ref, k_ref, vt_ref, u_ref, ga_ref, gs_ref):
    x = x_ref[...]
    h = _rms(x, g_ref[...]) * (1.0 + sc_ref[0]) + sh_ref[0]
    hb = h.astype(BF16)
    nblk = q_ref.shape[0]
    q = (jnp.dot(hb, wq_ref[...], preferred_element_type=F32) * (HEAD_DIM ** -0.5 * LOG2E)).astype(BF16)
    k = jnp.dot(hb, wk_ref[...], preferred_element_type=F32).astype(BF16)
    vt = lax.dot_general(wvt_ref[...], hb, _NT, preferred_element_type=F32).astype(BF16)
    for r in range(nblk):
        q_ref[r] = q[r * MOBA_BLOCK:(r + 1) * MOBA_BLOCK]
        k_ref[r] = k[r * MOBA_BLOCK:(r + 1) * MOBA_BLOCK]
        vt_ref[r] = vt[:, r * MOBA_BLOCK:(r + 1) * MOBA_BLOCK]
    u_ref[...] = jnp.dot(hb, wu_ref[...], preferred_element_type=F32).astype(BF16)
    ga = jnp.dot(hb, wga_ref[...], preferred_element_type=F32)
    ga_ref[...] = jax.nn.sigmoid(ga).astype(BF16)
    gs = jnp.dot(hb, wgs_ref[...], preferred_element_type=F32)
    gs_ref[...] = jax.nn.sigmoid(gs).astype(BF16)


def _premix(x2, g, sc, sh, w_in, seq):
    t, d = x2.shape
    tm = TOKEN_TILE
    tiles_per_seq = seq // tm
    nblk = tm // MOBA_BLOCK
    a = ATT_WIDTH
    assert a == SSM_WIDTH and d == 2 * a and w_in.shape[1] == 4 * a + 2 * d
    wb = w_in.astype(BF16)
    wvt = wb[:, 2 * a:3 * a].T

    def cols(width, index):
        return pl.BlockSpec((d, width), lambda i: (0, index), pipeline_mode=pl.Buffered(1))

    mod_spec = pl.BlockSpec((1, 1, d), lambda i: (i // tiles_per_seq, 0, 0))
    blk3 = lambda rows, cols: pl.BlockSpec((nblk, rows, cols), lambda i: (i, 0, 0))
    row = lambda cols: pl.BlockSpec((tm, cols), lambda i: (i, 0))
    time_major = pl.BlockSpec((tm, SSM_WIDTH), lambda i: (i % tiles_per_seq, i // tiles_per_seq))
    once = functools.partial(_full, pipeline_mode=pl.Buffered(1))
    nb_total = t // MOBA_BLOCK
    return pl.pallas_call(
        _premix_kernel,
        grid=(t // tm,),
        in_specs=[row(d), _full((1, d)), mod_spec, mod_spec,
                  cols(a, 0), cols(a, 1), once((a, d)), cols(a, 3), cols(d, 2), cols(d, 3)],
        out_specs=[blk3(MOBA_BLOCK, a), blk3(MOBA_BLOCK, a), blk3(a, MOBA_BLOCK),
                   time_major, row(d), row(d)],
        out_shape=[jax.ShapeDtypeStruct((nb_total, MOBA_BLOCK, a), BF16),
                   jax.ShapeDtypeStruct((nb_total, MOBA_BLOCK, a), BF16),
                   jax.ShapeDtypeStruct((nb_total, a, MOBA_BLOCK), BF16),
                   jax.ShapeDtypeStruct((seq, (t // seq) * SSM_WIDTH), BF16),
                   jax.ShapeDtypeStruct((t, d), BF16),
                   jax.ShapeDtypeStruct((t, d), BF16)],
        compiler_params=_cparams("arbitrary"),
        name="premix_inproj",
    )(x2, g.reshape(1, d), sc, sh, wb, wb, wvt, wb, wb, wb)


def _moba_kernel(far_ref, q_ref, k_ref, vt_ref, bias_ref, o_ref, kmean_ref, neg_ref, *state_refs):
    m_refs = state_refs[0::2]
    acc_refs = state_refs[1::2]
    nb = k_ref.shape[0]
    blk = MOBA_BLOCK
    i = pl.program_id(1)

    @pl.when(i == 0)
    def _():
        for n in range(nb):
            kmean_ref[n:n + 1, :] = jnp.mean(k_ref[n].astype(F32), axis=0, keepdims=True)

    lane = lax.broadcasted_iota(jnp.int32, (blk, 2 * HEAD_DIM), 1)
    nidx = lax.broadcasted_iota(jnp.int32, (nb, blk), 0)
    valid = nidx < i
    pair_w = 2 * HEAD_DIM

    def pair_cols(head):
        return slice((head // 2) * pair_w, (head // 2 + 1) * pair_w)

    def head_rows(head):
        return slice(head * HEAD_DIM, (head + 1) * HEAD_DIM)

    def q_head(head):
        q2 = q_ref[0, :, pair_cols(head)]
        lo = (head % 2) * HEAD_DIM
        return jnp.where((lane >= lo) & (lane < lo + HEAD_DIM), q2, jnp.zeros_like(q2))

    for head in range(ATT_HEADS):
        km = kmean_ref[:, pair_cols(head)]
        km_hi = km.astype(BF16)
        km_lo = (km - km_hi.astype(F32)).astype(BF16)
        parts = lax.dot_general(jnp.concatenate([km_hi, km_lo], axis=0), q_head(head), _NT,
                                preferred_element_type=F32)
        gate = parts[0:nb] + parts[nb:2 * nb]
        g = jnp.where(valid, gate, -jnp.inf)
        sel = jnp.zeros((nb, blk), jnp.bool_)
        for _ in range(MOBA_TOPK):
            mx = jnp.max(g, axis=0, keepdims=True)
            first = jnp.min(jnp.where(g == mx, nidx, nb), axis=0, keepdims=True)
            pick = nidx == first
            sel = sel | pick
            g = jnp.where(pick, -jnp.inf, g)
        neg_ref[head * nb:(head + 1) * nb, :] = jnp.where(sel & valid, 0.0, MASK_NEG)

    def scores(head, js, adds):
        qh = q_head(head)
        parts = [lax.dot_general(k_ref[j, :, pair_cols(head)], qh, _NT, preferred_element_type=F32) + a
                 for j, a in zip(js, adds)]
        return parts[0] if len(parts) == 1 else jnp.concatenate(parts, axis=0)

    def weighted_values(head, js, p):
        vals = [vt_ref[j, head_rows(head), :] for j in js]
        vb = vals[0] if len(vals) == 1 else jnp.concatenate(vals, axis=1)
        vb = jnp.concatenate([vb, jnp.ones((SUM_ROWS, vb.shape[1]), BF16)], axis=0)
        return jnp.dot(vb, p.astype(BF16), preferred_element_type=F32)

    def start(head, js, st):
        m = jnp.max(st, axis=0, keepdims=True)
        m_refs[head][...] = m
        acc_refs[head][...] = weighted_values(head, js, jnp.exp2(st - m))

    def update(head, js, st):
        m = m_refs[head][...]
        m_new = jnp.maximum(m, jnp.max(st, axis=0, keepdims=True))
        alpha = jnp.exp2(m - m_new)
        m_refs[head][...] = m_new
        acc_refs[head][...] = alpha * acc_refs[head][...] + weighted_values(head, js, jnp.exp2(st - m_new))

    def sweep(js, adds_of, absorb):
        st = scores(0, js, adds_of(0))
        for head in range(ATT_HEADS):
            nxt = scores(head + 1, js, adds_of(head + 1)) if head + 1 < ATT_HEADS else None
            absorb(head, js, st)
            st = nxt

    def far_add(head, j):
        return neg_ref[pl.ds(head * nb + j, 1), :] + far_ref[head]

    def near_adds(h):
        return [bias_ref[h, 0], bias_ref[h, 1] + neg_ref[pl.ds(h * nb + i - 1, 1), :]]

    @pl.when(i == 0)
    def _():
        sweep([i], lambda h: [bias_ref[h, 0]], start)

    @pl.when((i == 1) | (i == 2))
    def _():
        sweep([i, i - 1], near_adds, start)

    @pl.when(i >= 3)
    def _():
        sweep([i, i - 1, i - 2, i - 3],
              lambda h: near_adds(h) + [far_add(h, i - 2), far_add(h, i - 3)], start)

    n_far = jnp.where(i >= 3, i - 3, jnp.maximum(i - 1, 0))

    def far_step(first, count):
        js = [first + c for c in range(count)]
        sweep(js, lambda h: [far_add(h, j) for j in js], update)

    def far_quad(jj, carry):
        far_step(4 * jj, 4)
        return carry

    lax.fori_loop(0, n_far // 4, far_quad, 0)
    rest = n_far % 4
    pl.when(rest >= 2)(lambda: far_step(n_far - rest, 2))
    pl.when(rest % 2 == 1)(lambda: far_step(n_far - 1, 1))

    for hp in range(ATT_HEADS // 2):
        outs = [acc_refs[h][0:HEAD_DIM, :] / acc_refs[h][HEAD_DIM:HEAD_DIM + 1, :] for h in (2 * hp, 2 * hp + 1)]
        pair = jnp.concatenate(outs, axis=0)
        o_ref[0, :, hp * pair_w:(hp + 1) * pair_w] = pair.T.astype(BF16)


def _rel_bucket(dist):
    n = jnp.maximum(dist, 0)
    max_exact = REL_BUCKETS // 2
    nf = jnp.maximum(n, 1).astype(F32)
    large = max_exact + (jnp.log(nf / max_exact) / math.log(REL_MAX_DIST / max_exact)
                         * (REL_BUCKETS - max_exact)).astype(jnp.int32)
    large = jnp.minimum(large, REL_BUCKETS - 1)
    return jnp.where(n < max_exact, n, large)


def _moba_bias_tiles(rel_bias):
    blk = MOBA_BLOCK
    span = 2 * blk
    rel_bias = rel_bias * LOG2E
    vec = rel_bias[_rel_bucket(jnp.arange(span))].T.astype(F32)
    masked = jnp.full_like(vec[:, :blk], MASK_NEG)
    ring_own = jnp.concatenate([vec[:, :blk], masked], axis=1)
    ring_prev = jnp.concatenate([vec[:, blk:], vec[:, :blk]], axis=1)

    def toeplitz(ring):
        flat = jnp.tile(ring, (1, blk))[:, :blk * (span - 1)]
        return flat.reshape(-1, blk, span - 1)[:, :, :blk]

    tiles = jnp.stack([toeplitz(ring_own), toeplitz(ring_prev)], axis=1)
    far = rel_bias[_rel_bucket(jnp.array(span))]
    return tiles, far.astype(F32)


def _moba(q4, k4, vt4, bias_tiles, far_bias, bsz):
    nb_total, blk, a = q4.shape
    nb = nb_total // bsz
    return pl.pallas_call(
        _moba_kernel,
        grid=(bsz, nb),
        in_specs=[
            pl.BlockSpec(memory_space=pltpu.SMEM),
            pl.BlockSpec((1, blk, a), lambda b, i: (b * nb + i, 0, 0)),
            pl.BlockSpec((nb, blk, a), lambda b, i: (b, 0, 0)),
            pl.BlockSpec((nb, a, blk), lambda b, i: (b, 0, 0)),
            _full(bias_tiles.shape),
        ],
        out_specs=pl.BlockSpec((1, blk, a), lambda b, i: (b * nb + i, 0, 0)),
        out_shape=jax.ShapeDtypeStruct((nb_total, blk, a), BF16),
        scratch_shapes=[pltpu.VMEM((nb, a), F32), pltpu.VMEM((ATT_HEADS * nb, blk), F32)]
        + [pltpu.VMEM((1, blk), F32), pltpu.VMEM((HEAD_DIM + SUM_ROWS, blk), F32)] * ATT_HEADS,
        compiler_params=_cparams("arbitrary", "arbitrary"),
        name="moba_attention",
    )(far_bias, q4, k4, vt4, bias_tiles)


def _ssm_weights(a_re, a_im, log_dt, b_re, b_im, c_re, c_im, d_skip):
    g, p, cw, n = SSM_GROUPS, SSM_STATE, SSM_GROUP, SSM_CHUNK
    gs = LANES // cw
    nq = g // gs
    npair = n // 2
    pairs_per_set = gs // 2
    lam_re = jnp.minimum(a_re.astype(F32), -1e-4)
    lam_im = a_im.astype(F32)
    dt = jnp.exp(log_dt.astype(F32))[:, None]
    mag = jnp.exp(lam_re * dt)
    lb_re = mag * jnp.cos(lam_im * dt)
    lb_im = mag * jnp.sin(lam_im * dt)
    n_re = lb_re - 1.0
    n_im = lb_im
    den = lam_re * lam_re + lam_im * lam_im
    z_re = ((n_re * lam_re + n_im * lam_im) / den)[..., None]
    z_im = ((n_im * lam_re - n_re * lam_im) / den)[..., None]
    br, bi = b_re.astype(F32), b_im.astype(F32)
    bb_re = z_re * br - z_im * bi
    bb_im = z_re * bi + z_im * br
    pw_re, pw_im = [jnp.ones_like(lb_re)], [jnp.zeros_like(lb_im)]
    for _ in range(n):
        r, im = pw_re[-1], pw_im[-1]
        pw_re.append(r * lb_re - im * lb_im)
        pw_im.append(r * lb_im + im * lb_re)
    pw_re, pw_im = jnp.stack(pw_re), jnp.stack(pw_im)
    cr, ci = c_re.astype(F32), c_im.astype(F32)
    hi = lax.Precision.HIGHEST
    rev_re, rev_im = pw_re[n - 1::-1], pw_im[n - 1::-1]
    sb_re = jnp.einsum('sgp,gpc->sgcp', rev_re, bb_re) - jnp.einsum('sgp,gpc->sgcp', rev_im, bb_im)
    sb_im = jnp.einsum('sgp,gpc->sgcp', rev_re, bb_im) + jnp.einsum('sgp,gpc->sgcp', rev_im, bb_re)
    cl_re = jnp.einsum('gcp,tgp->tgpc', cr, pw_re[1:]) - jnp.einsum('gcp,tgp->tgpc', ci, pw_im[1:])
    cl_im = jnp.einsum('gcp,tgp->tgpc', cr, pw_im[1:]) + jnp.einsum('gcp,tgp->tgpc', ci, pw_re[1:])
    cp_re = jnp.einsum('gcp,tgp->tgcp', cr, pw_re[:n]) - jnp.einsum('gcp,tgp->tgcp', ci, pw_im[:n])
    cp_im = jnp.einsum('gcp,tgp->tgcp', cr, pw_im[:n]) + jnp.einsum('gcp,tgp->tgcp', ci, pw_re[:n])
    kern = (jnp.einsum('tgcp,gpd->gtcd', cp_re, bb_re, precision=hi)
            - jnp.einsum('tgcp,gpd->gtcd', cp_im, bb_im, precision=hi))

    kp = jnp.concatenate([jnp.zeros_like(kern[:, :1]), kern], axis=1)
    dl = jnp.arange(npair)[:, None, None]
    s0 = jnp.arange(2)[None, :, None]
    t0 = jnp.arange(2)[None, None, :]
    lag = 2 * dl + t0 - s0
    kl = kp[:, lag + 1].reshape(nq, gs, npair, 2, 2, cw, cw)
    base = kl.transpose(0, 2, 3, 1, 6, 4, 5)
    same_group = jnp.eye(gs, dtype=F32)
    w_toe = jnp.concatenate(
        [base[..., t0, :] * same_group[:, g1][None, None, None, :, None, None]
         for t0 in range(2) for g1 in range(gs)], axis=-1)
    w_toe = w_toe.reshape(nq, npair, MXU_DIM, MXU_DIM).astype(BF16)

    member = (jnp.arange(gs)[None, :, None]
              == 2 * jnp.arange(pairs_per_set)[:, None, None] + jnp.arange(2)[None, None, :]).astype(F32)

    def pair_tiles(x):
        cols = [x[ri][:, None] * member[:, :, gl][None, :, None, None, :, None, None]
                for ri in range(2) for gl in range(2)]
        return jnp.concatenate(cols, axis=-1).reshape(g // 2, npair, MXU_DIM, MXU_DIM)

    sbs = jnp.stack([sb_re, sb_im]).reshape(2, npair, 2, nq, gs, cw, p)
    w_in_state = pair_tiles(sbs.transpose(0, 3, 1, 2, 4, 5, 6)).astype(BF16)
    sos = jnp.stack([cl_re, -cl_im]).reshape(2, npair, 2, nq, gs, p, cw)
    w_state_out = jnp.swapaxes(pair_tiles(sos.transpose(0, 3, 1, 2, 4, 6, 5)), -1, -2).astype(BF16)
    decay = jnp.stack([pw_re[n].reshape(g // 2, 2 * p), pw_im[n].reshape(g // 2, 2 * p)])
    dvec = d_skip.astype(F32).reshape(1, g * cw)
    return w_toe, w_in_state, w_state_out, decay, dvec


def _ssm_kernel(u_ref, wt_ref, wi_ref, wo_ref, dec_ref, d_ref, y_ref, s_ref, xp_ref, x_ref, *, bsz):
    kt = u_ref.shape[0]
    n = u_ref.shape[1] // bsz
    rows = kt * bsz
    npair = n // 2
    nq = wt_ref.shape[0]
    npairs_g = wi_ref.shape[0]
    per_set = npairs_g // nq
    half = MXU_DIM // 2

    @pl.when(pl.program_id(0) == 0)
    def _():
        x_ref[...] = jnp.zeros_like(x_ref)

    def piece(s, q):
        return u_ref[:, s * bsz:(s + 1) * bsz, q * LANES:(q + 1) * LANES].reshape(rows, LANES)

    lhs = {(sp, q): jnp.concatenate([piece(2 * sp, q), piece(2 * sp + 1, q)], axis=-1)
           for sp in range(npair) for q in range(nq)}

    for gp in range(npairs_g):
        q = gp // per_set
        acc = jnp.dot(lhs[0, q], wi_ref[gp, 0], preferred_element_type=F32)
        for sp in range(1, npair):
            acc = acc + jnp.dot(lhs[sp, q], wi_ref[gp, sp], preferred_element_type=F32)
        s_ref[:, gp * MXU_DIM:(gp + 1) * MXU_DIM] = acc

    for gp in range(npairs_g):
        re_cols = slice(gp * MXU_DIM, gp * MXU_DIM + half)
        im_cols = slice(gp * MXU_DIM + half, (gp + 1) * MXU_DIM)
        ar = dec_ref[0, gp:gp + 1, :]
        ai = dec_ref[1, gp:gp + 1, :]
        xr = x_ref[:, re_cols]
        xi = x_ref[:, im_cols]
        for kc in range(kt):
            rs = slice(kc * bsz, (kc + 1) * bsz)
            xp_ref[rs, re_cols] = xr.astype(BF16)
            xp_ref[rs, im_cols] = xi.astype(BF16)
            sr = s_ref[rs, re_cols]
            si = s_ref[rs, im_cols]
            xr, xi = ar * xr - ai * xi + sr, ar * xi + ai * xr + si
        x_ref[:, re_cols] = xr
        x_ref[:, im_cols] = xi

    for q in range(nq):
        for tp in range(npair):
            acc = jnp.dot(lhs[0, q], wt_ref[q, tp], preferred_element_type=F32)
            for sp in range(1, tp + 1):
                acc = acc + jnp.dot(lhs[sp, q], wt_ref[q, tp - sp], preferred_element_type=F32)
            for gp in range(q * per_set, (q + 1) * per_set):
                acc = acc + jnp.dot(xp_ref[:, gp * MXU_DIM:(gp + 1) * MXU_DIM], wo_ref[gp, tp],
                                    preferred_element_type=F32)
            for t0 in range(2):
                s = 2 * tp + t0
                y = acc[:, t0 * half:(t0 + 1) * half] + d_ref[:, q * LANES:(q + 1) * LANES] * piece(s, q).astype(F32)
                y_ref[:, s * bsz:(s + 1) * bsz, q * LANES:(q + 1) * LANES] = (
                    jax.nn.gelu(y).astype(BF16).reshape(kt, bsz, LANES))


def _ssm(u_tm, weights, bsz, seq):
    n = SSM_CHUNK
    kt = SSM_CHUNKS_PER_STEP
    nchunk = seq // n
    rows = kt * bsz
    state_w = SSM_GROUPS * 2 * SSM_STATE
    u3 = u_tm.reshape(nchunk, n * bsz, SSM_WIDTH)
    blk = pl.BlockSpec((kt, n * bsz, SSM_WIDTH), lambda i: (i, 0, 0))

    once = pl.Buffered(1)
    y3 = pl.pallas_call(
        functools.partial(_ssm_kernel, bsz=bsz),
        grid=(nchunk // kt,),
        in_specs=[blk] + [_full(w.shape, pipeline_mode=once) for w in weights],
        out_specs=blk,
        out_shape=jax.ShapeDtypeStruct(u3.shape, BF16),
        scratch_shapes=[pltpu.VMEM((rows, state_w), F32), pltpu.VMEM((rows, state_w), BF16),
                        pltpu.VMEM((bsz, state_w), F32)],
        compiler_params=_cparams("arbitrary"),
        name="s5_chunked_scan",
    )(u3, *weights)
    return y3.reshape(seq, bsz * SSM_WIDTH)


def _postmix_kernel(att_ref, ys_ref, ga_ref, gs_ref, x_ref, gt_ref, sc_ref, sh_ref, gpost_ref, gpre_ref,
                    wglu_ref, wso_ref, wao_ref, wo_ref, wrh_ref, wrl_ref, br_ref,
                    x1_ref, h2_ref, idx_ref, gate_ref, rank_ref, cnt_ref, run_ref):
    tm = x_ref.shape[0]
    ne = wrh_ref.shape[0]

    @pl.when(pl.program_id(0) == 0)
    def _():
        run_ref[...] = jnp.zeros_like(run_ref)

    halves = [slice(r * (tm // 2), (r + 1) * (tm // 2)) for r in range(2)]
    dot = functools.partial(jnp.dot, preferred_element_type=F32)
    att = [dot(att_ref[r, :], wao_ref[...]) for r in halves]
    glu = [dot(ys_ref[r, :], wglu_ref[...]) for r in halves]
    ssm = []
    for g in glu:
        sg = (g[:, :SSM_WIDTH] * jax.nn.sigmoid(g[:, SSM_WIDTH:])).astype(BF16)
        ssm.append(dot(sg, wso_ref[...]))
    y = []
    for r, a, s in zip(halves, att, ssm):
        merged = (ga_ref[r, :].astype(F32) * a + gs_ref[r, :].astype(F32) * s).astype(BF16)
        y.append(dot(merged, wo_ref[...]))
    logits = []
    for r, yh in zip(halves, y):
        x1 = x_ref[r, :] + gt_ref[0] * _rms(yh, gpost_ref[...])
        x1_ref[r, :] = x1
        h2 = _rms(x1, gpre_ref[...]) * (1.0 + sc_ref[0]) + sh_ref[0]
        h2_ref[r, :] = _pack_bf16_pairs(h2)
        h_hi = h2.astype(BF16)
        h_lo = (h2 - h_hi.astype(F32)).astype(BF16)
        logits.append(lax.dot_general(wrh_ref[...], h_hi, _NT, preferred_element_type=F32)
                      + lax.dot_general(wrh_ref[...], h_lo, _NT, preferred_element_type=F32)
                      + lax.dot_general(wrl_ref[...], h_hi, _NT, preferred_element_type=F32))
    logits = jnp.concatenate(logits, axis=1) + br_ref[...]
    eidx = lax.broadcasted_iota(jnp.int32, (ne, tm), 0)
    lg = logits
    vals, idxs = [], []
    for _ in range(TOP_K):
        mx = jnp.max(lg, axis=0, keepdims=True)
        first = jnp.min(jnp.where(lg == mx, eidx, ne), axis=0, keepdims=True)
        vals.append(mx)
        idxs.append(first)
        lg = jnp.where(eidx == first, -jnp.inf, lg)
    exps = [jnp.exp(v - vals[0]) for v in vals]
    denom = exps[0] + exps[1] + exps[2] + exps[3]
    gate_ref[...] = jnp.concatenate([e / denom for e in exps], axis=0)
    idx_ref[...] = jnp.concatenate(idxs, axis=0)
    onehot = jnp.where(lg == -jnp.inf, 1.0, 0.0)
    t_row = lax.broadcasted_iota(jnp.int32, (tm, tm), 0)
    t_col = lax.broadcasted_iota(jnp.int32, (tm, tm), 1)
    earlier = jnp.where(t_row < t_col, 1.0, 0.0).astype(BF16)
    before = jnp.dot(onehot.astype(BF16), earlier, preferred_element_type=F32) + run_ref[:, 0:1]
    ranks = [jnp.sum(jnp.where(eidx == ix, before, 0.0), axis=0, keepdims=True) for ix in idxs]
    rank_ref[...] = jnp.concatenate(ranks, axis=0).astype(jnp.int32)
    run_ref[...] = run_ref[...] + jnp.sum(onehot, axis=1, keepdims=True)
    cnt_ref[0] = run_ref[...].astype(jnp.int32)


def _postmix(att2, ys_tm, ga, gs, x2, gt1, sc2, sh2, g_post, g_pre, w_glu, w_ssm_out, w_att_out, w_out,
             w_router, b_router, seq):
    t, d = x2.shape
    tm = TOKEN_TILE
    tiles_per_seq = seq // tm
    ne = w_router.shape[1]
    wr_t = w_router.T.astype(F32)
    wr_hi = wr_t.astype(BF16)
    wr_lo = (wr_t - wr_hi.astype(F32)).astype(BF16)
    mod_spec = pl.BlockSpec((1, 1, d), lambda i: (i // tiles_per_seq, 0, 0))
    row = lambda cols: pl.BlockSpec((tm, cols), lambda i: (i, 0))
    time_major = pl.BlockSpec((tm, SSM_WIDTH), lambda i: (i % tiles_per_seq, i // tiles_per_seq))
    col = pl.BlockSpec((TOP_K, tm), lambda i: (0, i))
    weights = [w_glu.astype(BF16), w_ssm_out.astype(BF16), w_att_out.astype(BF16), w_out.astype(BF16),
               wr_hi, wr_lo, b_router.astype(F32).reshape(ne, 1)]
    return pl.pallas_call(
        _postmix_kernel,
        grid=(t // tm,),
        in_specs=[row(ATT_WIDTH), time_major, row(d), row(d), row(d), mod_spec, mod_spec, mod_spec,
                  _full((1, d)), _full((1, d))]
        + [_full(w.shape, pipeline_mode=pl.Buffered(1)) for w in weights],
        out_specs=[row(d), row(d // 2), col, col, col,
                   pl.BlockSpec((1, ne, 128), lambda i: (i // (t // tm // MOE_PARTS), 0, 0))],
        out_shape=[jax.ShapeDtypeStruct((t, d), F32), jax.ShapeDtypeStruct((t, d // 2), jnp.uint32),
                   jax.ShapeDtypeStruct((TOP_K, t), jnp.int32), jax.ShapeDtypeStruct((TOP_K, t), F32),
                   jax.ShapeDtypeStruct((TOP_K, t), jnp.int32),
                   jax.ShapeDtypeStruct((MOE_PARTS, ne, 128), jnp.int32)],
        scratch_shapes=[pltpu.VMEM((ne, 128), F32)],
        compiler_params=_cparams("arbitrary"),
        name="postmix_router",
    )(att2, ys_tm, ga, gs, x2, gt1, sc2, sh2, g_post.reshape(1, d), g_pre.reshape(1, d), *weights)


def _sc_worker_base(per_worker):
    return (lax.axis_index("s") * SC_CORES + lax.axis_index("c")) * per_worker


def _sc_mesh():
    return plsc.VectorSubcoreMesh(core_axis_name="c", subcore_axis_name="s",
                                  num_cores=SC_CORES, num_subcores=SC_SUBCORES)


def _sc_scatter_kernel(rows_hbm, idx_hbm, out_hbm, idx_v, rows_v, load_sems, store_sems, *,
                       per_worker, first_row):
    chunk = rows_v.shape[1]
    n_tokens = idx_hbm.shape[0] // TOP_K
    n_chunks = per_worker // chunk
    base = _sc_worker_base(per_worker)

    def load(c, b):
        off = pl.multiple_of(base + c * chunk, chunk)
        for k in range(TOP_K):
            pltpu.sync_copy(idx_hbm.at[pl.ds(k * n_tokens + off, chunk)], idx_v.at[b, k])
        return pltpu.make_async_copy(rows_hbm.at[pl.ds(first_row + off, chunk)], rows_v.at[b],
                                     load_sems.at[b])

    def stores(b):
        return [pltpu.make_async_copy(rows_v.at[b], out_hbm.at[idx_v.at[b, k]], store_sems.at[b])
                for k in range(TOP_K)]

    load(0, 0).start()

    @pl.loop(0, n_chunks, step=2)
    def _(c0):
        for b in range(2):
            c = c0 + b

            @pl.when(c > 0)
            def _():
                for cp in stores(1 - b):
                    cp.wait()

            @pl.when(c + 1 < n_chunks)
            def _():
                load(c + 1, 1 - b).start()

            pltpu.make_async_copy(rows_hbm.at[pl.ds(0, chunk)], rows_v.at[b], load_sems.at[b]).wait()
            for cp in stores(b):
                cp.start()

    for cp in stores((n_chunks - 1) % 2):
        cp.wait()


def _sc_scatter_rows(rows, idx, n_out, first_row):
    d = rows.shape[1]
    t = idx.shape[0] // TOP_K
    workers = SC_CORES * SC_SUBCORES
    per_worker = t // workers
    assert t % workers == 0 and per_worker % (2 * SC_GATHER_ROWS) == 0
    return pl.kernel(
        functools.partial(_sc_scatter_kernel, per_worker=per_worker, first_row=first_row),
        out_type=jax.ShapeDtypeStruct((n_out, d), rows.dtype),
        mesh=_sc_mesh(),
        scratch_types=[pltpu.VMEM((2, TOP_K, SC_GATHER_ROWS), jnp.int32),
                       pltpu.VMEM((2, SC_GATHER_ROWS, d), rows.dtype),
                       pltpu.SemaphoreType.DMA((2,)), pltpu.SemaphoreType.DMA((2,))],
        name="sc_row_scatter",
    )(rows, idx)


def _sc_gather_kernel(table_hbm, idx_hbm, out_hbm, idx_v, rows_v, sems, *, per_worker):
    chunk = rows_v.shape[1]
    n_chunks = per_worker // chunk
    base = _sc_worker_base(per_worker)

    def gather(c, b):
        off = pl.multiple_of(base + c * chunk, chunk)
        pltpu.sync_copy(idx_hbm.at[pl.ds(off, chunk)], idx_v.at[b])
        return pltpu.make_async_copy(table_hbm.at[idx_v.at[b]], rows_v.at[b], sems.at[b])

    gather(0, 0).start()

    @pl.loop(0, n_chunks, step=2)
    def _(c0):
        for b in range(2):
            c = c0 + b

            @pl.when(c + 1 < n_chunks)
            def _():
                gather(c + 1, 1 - b).start()

            pltpu.make_async_copy(table_hbm.at[idx_v.at[b]], rows_v.at[b], sems.at[b]).wait()
            off = pl.multiple_of(base + c * chunk, chunk)
            pltpu.sync_copy(rows_v.at[b], out_hbm.at[pl.ds(off, chunk)])


def _sc_gather_rows(table, idx):
    n = idx.shape[0]
    d = table.shape[1]
    workers = SC_CORES * SC_SUBCORES
    per_worker = n // workers
    assert n % workers == 0 and per_worker % (2 * SC_GATHER_ROWS) == 0
    return pl.kernel(
        functools.partial(_sc_gather_kernel, per_worker=per_worker),
        out_type=jax.ShapeDtypeStruct((n, d), table.dtype),
        mesh=_sc_mesh(),
        scratch_types=[pltpu.VMEM((2, SC_GATHER_ROWS), jnp.int32),
                       pltpu.VMEM((2, SC_GATHER_ROWS, d), table.dtype),
                       pltpu.SemaphoreType.DMA((2,))],
        name="sc_row_gather",
    )(table, idx)


def _experts_kernel(be_ref, nused_ref, valid_ref, next_ref, x_ref, b1_ref, b2_ref, w1_hbm, w2_hbm, y_ref,
                    w1s_ref, w2s_ref, w1b_ref, w2b_ref, sems, *, layer):
    i = pl.program_id(0)
    prev = be_ref[jnp.maximum(i - 1, 0)]
    fresh = (i < nused_ref[0]) & ((i == 0) | (be_ref[i] != prev))

    def fetch(e):
        return (pltpu.make_async_copy(w1_hbm.at[layer, e], w1s_ref, sems.at[0]),
                pltpu.make_async_copy(w2_hbm.at[layer, e], w2s_ref, sems.at[1]))

    @pl.when(i == 0)
    def _():
        for cp in fetch(be_ref[0]):
            cp.start()

    @pl.when(fresh)
    def _():
        for cp in fetch(be_ref[i]):
            cp.wait()
        w1b_ref[...] = w1s_ref[...].astype(BF16)
        w2b_ref[...] = w2s_ref[...].astype(BF16)

        @pl.when(next_ref[i] >= 0)
        def _():
            for cp in fetch(next_ref[i]):
                cp.start()

    used = i < nused_ref[0]
    valid = valid_ref[i]
    half = x_ref.shape[0] // 2

    def ffn(n_rows):
        row = lax.broadcasted_iota(jnp.int32, (n_rows, x_ref.shape[1]), 0)
        x_lo, x_hi = _unpack_bf16_pairs(jnp.where(row < valid, x_ref[0:n_rows, :], jnp.uint32(0)))
        x = jnp.concatenate([x_lo.astype(BF16), x_hi.astype(BF16)], axis=1)
        gu = jnp.dot(x, w1b_ref[...], preferred_element_type=F32) + b1_ref[...]
        g = jnp.minimum(gu[:, :D_FF], SWIGLU_LIMIT)
        up = jnp.clip(gu[:, D_FF:], -SWIGLU_LIMIT, SWIGLU_LIMIT)
        act = ((up + 1.0) * g * jax.nn.sigmoid(SWIGLU_ALPHA * g)).astype(BF16)
        y = jnp.dot(act, w2b_ref[...], preferred_element_type=F32) + b2_ref[...]
        y_ref[0:n_rows, :] = _pack_bf16_pairs(y)

    @pl.when(used & (valid > half))
    def _():
        ffn(2 * half)

    def short(n_rows):
        ffn(n_rows)
        y_ref[n_rows:, :] = jnp.zeros((2 * half - n_rows, y_ref.shape[1]), y_ref.dtype)

    pl.when(used & (valid <= half) & (valid > half // 2))(lambda: short(half))
    pl.when(used & (valid <= half // 2))(lambda: short(half // 2))

    @pl.when(jnp.logical_not(used))
    def _():
        y_ref[...] = jnp.zeros_like(y_ref)


def _experts(xb, block_e, n_used, valid, next_e, w1, b1, w2, b2, layer):
    p_rows, packed_w = xb.shape
    d = 2 * packed_w
    depth, ne = w1.shape[:2]
    rb = EXPERT_ROWS
    bmap = lambda i, be, nu, va, nx: (layer, be[i], 0, 0)
    rows = pl.BlockSpec((rb, packed_w), lambda i, be, nu, va, nx: (i, 0))
    grid_spec = pltpu.PrefetchScalarGridSpec(
        num_scalar_prefetch=4,
        grid=(p_rows // rb,),
        in_specs=[
            rows,
            pl.BlockSpec((None, None, 1, 2 * D_FF), bmap),
            pl.BlockSpec((None, None, 1, d), bmap),
            pl.BlockSpec(memory_space=pl.ANY),
            pl.BlockSpec(memory_space=pl.ANY),
        ],
        out_specs=rows,
        scratch_shapes=[pltpu.VMEM((d, 2 * D_FF), w1.dtype), pltpu.VMEM((D_FF, d), w2.dtype),
                        pltpu.VMEM((d, 2 * D_FF), BF16), pltpu.VMEM((D_FF, d), BF16),
                        pltpu.SemaphoreType.DMA((2,))],
    )
    return pl.pallas_call(
        functools.partial(_experts_kernel, layer=layer),
        grid_spec=grid_spec,
        out_shape=jax.ShapeDtypeStruct((p_rows, packed_w), jnp.uint32),
        compiler_params=_cparams("arbitrary"),
        name="expert_ffn",
    )(block_e, n_used, valid, next_e, xb, b1.reshape(depth, ne, 1, 2 * D_FF), b2.reshape(depth, ne, 1, d), w1, w2)


def _combine_kernel(gate_ref, x_ref, gt_ref, g_ref, y0_ref, y1_ref, y2_ref, y3_ref, *rest):
    o_ref = rest[-1]
    tm = x_ref.shape[0]
    gates = gate_ref[...]
    gates = jnp.concatenate([gates, jnp.zeros((LANES - TOP_K, tm), F32)], axis=0).T
    lo, hi = _unpack_bf16_pairs(y0_ref[...])
    y_lo, y_hi = gates[:, 0:1] * lo, gates[:, 0:1] * hi
    for k, y_ref in enumerate((y1_ref, y2_ref, y3_ref), start=1):
        lo, hi = _unpack_bf16_pairs(y_ref[...])
        y_lo, y_hi = y_lo + gates[:, k:k + 1] * lo, y_hi + gates[:, k:k + 1] * hi
    y = jnp.concatenate([y_lo, y_hi], axis=1)
    o_ref[...] = x_ref[...] + gt_ref[0] * _rms(y, g_ref[...])


def _combine(y4, part, out, gate_t, x2, gt2, g_post, seq):
    t, d = x2.shape
    tm = TOKEN_TILE
    tiles = y4.shape[0] // TOP_K // tm
    first = part * tiles
    tiles_per_seq = seq // tm
    row = pl.BlockSpec((tm, d), lambda i: (first + i, 0))
    slot = lambda k: pl.BlockSpec((tm, y4.shape[1]), lambda i: (k * tiles + i, 0))
    in_specs = [pl.BlockSpec((TOP_K, tm), lambda i: (0, first + i)), row,
                pl.BlockSpec((1, 1, d), lambda i: ((first + i) // tiles_per_seq, 0, 0)),
                _full((1, d))] + [slot(k) for k in range(TOP_K)]
    args = [gate_t, x2, gt2, g_post.reshape(1, d), y4, y4, y4, y4]
    aliases = {}
    if out is not None:
        in_specs.append(pl.BlockSpec(memory_space=pl.ANY))
        aliases = {len(args): 0}
        args.append(out)
    return pl.pallas_call(
        _combine_kernel,
        grid=(tiles,),
        in_specs=in_specs,
        out_specs=row,
        out_shape=jax.ShapeDtypeStruct((t, d), F32),
        input_output_aliases=aliases,
        compiler_params=_cparams("arbitrary"),
        name="expert_combine",
    )(*args)


def _route_plan(idx_t, rank_t, counts):
    rb = EXPERT_ROWS
    k, t = idx_t.shape
    padded = (counts + rb - 1) // rb * rb
    pad_ends = jnp.cumsum(padded)
    pad_starts = pad_ends - padded
    experts = jnp.arange(N_EXPERTS, dtype=jnp.int32)
    start_of = jnp.sum(jnp.where(idx_t[None] == experts[:, None, None], pad_starts[:, None, None], 0), axis=0)
    dest = (start_of + rank_t).astype(jnp.int32)
    n_blocks = (k * t) // rb + N_EXPERTS
    blk_start = jnp.arange(n_blocks, dtype=jnp.int32) * rb
    block_e = jnp.minimum(jnp.sum(pad_ends[None, :] <= blk_start[:, None], axis=1), N_EXPERTS - 1)
    onehot_e = block_e[:, None] == experts[None, :]
    cnt_b = jnp.sum(jnp.where(onehot_e, counts[None, :], 0), axis=1)
    start_b = jnp.sum(jnp.where(onehot_e, pad_starts[None, :], 0), axis=1)
    valid = jnp.clip(cnt_b - (blk_start - start_b), 0, rb).astype(jnp.int32)
    n_used = (pad_ends[-1] // rb).astype(jnp.int32).reshape(1)
    later_nonempty = (experts[None, :] > experts[:, None]) & (counts[None, :] > 0)
    next_nonempty = jnp.min(jnp.where(later_nonempty, experts[None, :], N_EXPERTS), axis=1)
    next_nonempty = jnp.where(next_nonempty == N_EXPERTS, -1, next_nonempty)
    next_e = jnp.sum(jnp.where(onehot_e, next_nonempty[None, :], 0), axis=1).astype(jnp.int32)
    return dest.reshape(-1), block_e.astype(jnp.int32), n_used, valid, next_e, n_blocks * rb


def kernel(x, c, rel_bias, w_ada, b_ada, g_pre_mix, g_post_mix, g_pre_ffn, g_post_ffn, w_in, ssm_a_re, ssm_a_im, ssm_log_dt, ssm_b_re, ssm_b_im, ssm_c_re, ssm_c_im, ssm_d, w_glu, w_ssm_out, w_att_out, w_out, w_router, b_router, w_exp_in, b_exp_in, w_exp_out, b_exp_out):
    bsz, seq, d = x.shape
    depth = w_ada.shape[0]
    t = bsz * seq
    assert d == D_MODEL and seq % TOKEN_TILE == 0 and TOKEN_TILE % MOBA_BLOCK == 0
    assert seq % (SSM_CHUNK * SSM_CHUNKS_PER_STEP) == 0

    mod = _ada_mod(c, w_ada, b_ada)
    bias_tiles, far_bias = _moba_bias_tiles(rel_bias.astype(F32))
    x2 = x.reshape(t, d)
    for l in range(depth):
        sh1, sc1, gt1, sh2, sc2, gt2 = [m.reshape(bsz, 1, d) for m in jnp.split(mod[l], N_MOD, axis=-1)]
        q4, k4, vt4, u_tm, ga, gs = _premix(x2, g_pre_mix[l], sc1, sh1, w_in[l], seq)
        att = _moba(q4, k4, vt4, bias_tiles, far_bias, bsz).reshape(t, ATT_WIDTH)
        ssm_w = _ssm_weights(ssm_a_re[l], ssm_a_im[l], ssm_log_dt[l], ssm_b_re[l], ssm_b_im[l],
                             ssm_c_re[l], ssm_c_im[l], ssm_d[l])
        ys_tm = _ssm(u_tm, ssm_w, bsz, seq)
        x1, h2, idx_t, gate_t, rank_t, cnt = _postmix(
            att, ys_tm, ga, gs, x2, gt1, sc2, sh2, g_post_mix[l], g_pre_ffn[l],
            w_glu[l], w_ssm_out[l], w_att_out[l], w_out[l], w_router[l], b_router[l], seq)
        n_part = t // MOE_PARTS
        seen = jnp.zeros((N_EXPERTS,), jnp.int32)
        experts = jnp.arange(N_EXPERTS, dtype=jnp.int32)
        plans = []
        for p in range(MOE_PARTS):
            idx_p = idx_t[:, p * n_part:(p + 1) * n_part]
            seen_of = jnp.sum(jnp.where(idx_p[None] == experts[:, None, None], seen[:, None, None], 0), axis=0)
            rank_p = rank_t[:, p * n_part:(p + 1) * n_part] - seen_of
            plans.append(_route_plan(idx_p, rank_p, cnt[p, :, 0] - seen))
            seen = cnt[p, :, 0]
        xbs = [_sc_scatter_rows(h2, plan[0], plan[5], p * n_part) for p, plan in enumerate(plans)]
        ybs = [_experts(xb, plan[1], plan[2], plan[3], plan[4], w_exp_in, b_exp_in, w_exp_out, b_exp_out, l)
               for xb, plan in zip(xbs, plans)]
        x2 = None
        for p, (yb, plan) in enumerate(zip(ybs, plans)):
            x2 = _combine(_sc_gather_rows(yb, plan[0]), p, x2, gate_t, x1, gt2, g_post_ffn[l], seq)
    return x2.reshape(bsz, seq, d)
```

```python
import functools
import math

import jax
import jax.numpy as jnp
from jax import lax
from jax.experimental import pallas as pl
from jax.experimental.pallas import tpu as pltpu
from jax.experimental.pallas import tpu_sc as plsc

F32 = jnp.float32
BF16 = jnp.bfloat16

D_MODEL = 1024
ATT_HEADS = 8
HEAD_DIM = 64
ATT_WIDTH = ATT_HEADS * HEAD_DIM
MOBA_BLOCK = 256
MOBA_TOPK = 3
REL_BUCKETS = 32
REL_MAX_DIST = 128
SSM_WIDTH = D_MODEL // 2
SSM_GROUP = 16
SSM_GROUPS = SSM_WIDTH // SSM_GROUP
SSM_STATE = 64
N_EXPERTS = 32
TOP_K = 4
D_FF = D_MODEL
SWIGLU_ALPHA = 1.702
SWIGLU_LIMIT = 7.0
RMS_EPS = 1e-6
N_MOD = 6

SSM_CHUNK = 8
SSM_CHUNKS_PER_STEP = 16
LANES = 128
MXU_DIM = 256
TOKEN_TILE = 1024
SC_CORES = 2
SC_SUBCORES = 16
SC_GATHER_ROWS = 64
EXPERT_ROWS = 1024
MASK_NEG = -1e30
LOG2E = math.log2(math.e)
SUM_ROWS = 16
VMEM_LIMIT = 56 * 1024 * 1024

_NT = (((1,), (1,)), ((), ()))


def _cparams(*sem):
    return pltpu.CompilerParams(dimension_semantics=sem, vmem_limit_bytes=VMEM_LIMIT)


def _pack_bf16_pairs(x):
    n = x.shape[1] // 2
    bits = lax.bitcast_convert_type(x.astype(BF16).astype(F32), jnp.uint32)
    return lax.shift_right_logical(bits[:, :n], jnp.uint32(16)) | (bits[:, n:] & jnp.uint32(0xFFFF0000))


def _unpack_bf16_pairs(w):
    lo = lax.bitcast_convert_type(lax.shift_left(w, jnp.uint32(16)), F32)
    hi = lax.bitcast_convert_type(w & jnp.uint32(0xFFFF0000), F32)
    return lo, hi


def _rms(x, g):
    return x * lax.rsqrt(jnp.mean(x * x, axis=-1, keepdims=True) + RMS_EPS) * g


def _full(shape, **kw):
    n = len(shape)
    return pl.BlockSpec(shape, lambda *_: (0,) * n, **kw)


def _ada_kernel(c_ref, w_ref, b_ref, o_ref):
    c = c_ref[...]
    cond = c * jax.nn.sigmoid(c)
    o_ref[0] = jnp.dot(cond, w_ref[0], preferred_element_type=F32,
                       precision=lax.Precision.HIGHEST) + b_ref[0]


def _ada_mod(c, w_ada, b_ada):
    depth, d, nd = w_ada.shape
    bsz = c.shape[0]
    return pl.pallas_call(
        _ada_kernel,
        grid=(depth, nd // d),
        in_specs=[
            pl.BlockSpec((bsz, d), lambda l, j: (0, 0)),
            pl.BlockSpec((1, d, d), lambda l, j: (l, 0, j)),
            pl.BlockSpec((1, 1, d), lambda l, j: (l, 0, j)),
        ],
        out_specs=pl.BlockSpec((1, bsz, d), lambda l, j: (l, 0, j)),
        out_shape=jax.ShapeDtypeStruct((depth, bsz, nd), F32),
        compiler_params=_cparams("arbitrary", "arbitrary"),
        name="ada_mod",
    )(c, w_ada, b_ada.reshape(depth, 1, nd))


def _premix_kernel(x_ref, g_ref, sc_ref, sh_ref, wq_ref, wk_ref, wvt_ref, wu_ref, wga_ref, wgs_ref,
                   q_ref, k_ref, vt_ref, u_ref, ga_ref, gs_ref):
    x = x_ref[...]
    h = _rms(x, g_ref[...]) * (1.0 + sc_ref[0]) + sh_ref[0]
    hb = h.astype(BF16)
    nblk = q_ref.shape[0]
    q = (jnp.dot(hb, wq_ref[...], preferred_element_type=F32) * (HEAD_DIM ** -0.5 * LOG2E)).astype(BF16)
    k = jnp.dot(hb, wk_ref[...], preferred_element_type=F32).astype(BF16)
    vt = lax.dot_general(wvt_ref[...], hb, _NT, preferred_element_type=F32).astype(BF16)
    for r in range(nblk):
        q_ref[r] = q[r * MOBA_BLOCK:(r + 1) * MOBA_BLOCK]
        k_ref[r] = k[r * MOBA_BLOCK:(r + 1) * MOBA_BLOCK]
        vt_ref[r] = vt[:, r * MOBA_BLOCK:(r + 1) * MOBA_BLOCK]
    u_ref[...] = jnp.dot(hb, wu_ref[...], preferred_element_type=F32)
    ga = jnp.dot(hb, wga_ref[...], preferred_element_type=F32)
    ga_ref[...] = jax.nn.sigmoid(ga).astype(BF16)
    gs = jnp.dot(hb, wgs_ref[...], preferred_element_type=F32)
    gs_ref[...] = jax.nn.sigmoid(gs).astype(BF16)


def _premix(x2, g, sc, sh, w_in, seq):
    t, d = x2.shape
    tm = TOKEN_TILE
    tiles_per_seq = seq // tm
    nblk = tm // MOBA_BLOCK
    a = ATT_WIDTH
    assert a == SSM_WIDTH and d == 2 * a and w_in.shape[1] == 4 * a + 2 * d
    wb = w_in.astype(BF16)
    wvt = wb[:, 2 * a:3 * a].T

    def cols(width, index):
        return pl.BlockSpec((d, width), lambda i: (0, index), pipeline_mode=pl.Buffered(1))

    mod_spec = pl.BlockSpec((1, 1, d), lambda i: (i // tiles_per_seq, 0, 0))
    blk3 = lambda rows, cols: pl.BlockSpec((nblk, rows, cols), lambda i: (i, 0, 0))
    row = lambda cols: pl.BlockSpec((tm, cols), lambda i: (i, 0))
    time_major = pl.BlockSpec((tm, SSM_WIDTH), lambda i: (i % tiles_per_seq, i // tiles_per_seq))
    once = functools.partial(_full, pipeline_mode=pl.Buffered(1))
    nb_total = t // MOBA_BLOCK
    return pl.pallas_call(
        _premix_kernel,
        grid=(t // tm,),
        in_specs=[row(d), _full((1, d)), mod_spec, mod_spec,
                  cols(a, 0), cols(a, 1), once((a, d)), cols(a, 3), cols(d, 2), cols(d, 3)],
        out_specs=[blk3(MOBA_BLOCK, a), blk3(MOBA_BLOCK, a), blk3(a, MOBA_BLOCK),
                   time_major, row(d), row(d)],
        out_shape=[jax.ShapeDtypeStruct((nb_total, MOBA_BLOCK, a), BF16),
                   jax.ShapeDtypeStruct((nb_total, MOBA_BLOCK, a), BF16),
                   jax.ShapeDtypeStruct((nb_total, a, MOBA_BLOCK), BF16),
                   jax.ShapeDtypeStruct((seq, (t // seq) * SSM_WIDTH), F32),
                   jax.ShapeDtypeStruct((t, d), BF16),
                   jax.ShapeDtypeStruct((t, d), BF16)],
        compiler_params=_cparams("arbitrary"),
        name="premix_inproj",
    )(x2, g.reshape(1, d), sc, sh, wb, wb, wvt, wb, wb, wb)


def _moba_kernel(far_ref, q_ref, k_ref, vt_ref, bias_ref, o_ref, kmean_ref, neg_ref, *state_refs):
    m_refs = state_refs[0::2]
    acc_refs = state_refs[1::2]
    nb = k_ref.shape[0]
    blk = MOBA_BLOCK
    i = pl.program_id(1)

    @pl.when(i == 0)
    def _():
        for n in range(nb):
            kmean_ref[n:n + 1, :] = jnp.mean(k_ref[n].astype(F32), axis=0, keepdims=True)

    lane = lax.broadcasted_iota(jnp.int32, (blk, 2 * HEAD_DIM), 1)
    nidx = lax.broadcasted_iota(jnp.int32, (nb, blk), 0)
    valid = nidx < i
    pair_w = 2 * HEAD_DIM

    def pair_cols(head):
        return slice((head // 2) * pair_w, (head // 2 + 1) * pair_w)

    def head_rows(head):
        return slice(head * HEAD_DIM, (head + 1) * HEAD_DIM)

    def q_head(head):
        q2 = q_ref[0, :, pair_cols(head)]
        lo = (head % 2) * HEAD_DIM
        return jnp.where((lane >= lo) & (lane < lo + HEAD_DIM), q2, jnp.zeros_like(q2))

    for head in range(ATT_HEADS):
        km = kmean_ref[:, pair_cols(head)]
        km_hi = km.astype(BF16)
        km_lo = (km - km_hi.astype(F32)).astype(BF16)
        parts = lax.dot_general(jnp.concatenate([km_hi, km_lo], axis=0), q_head(head), _NT,
                                preferred_element_type=F32)
        gate = parts[0:nb] + parts[nb:2 * nb]
        g = jnp.where(valid, gate, -jnp.inf)
        sel = jnp.zeros((nb, blk), jnp.bool_)
        for _ in range(MOBA_TOPK):
            mx = jnp.max(g, axis=0, keepdims=True)
            first = jnp.min(jnp.where(g == mx, nidx, nb), axis=0, keepdims=True)
            pick = nidx == first
            sel = sel | pick
            g = jnp.where(pick, -jnp.inf, g)
        neg_ref[head * nb:(head + 1) * nb, :] = jnp.where(sel & valid, 0.0, MASK_NEG)

    def scores(head, js, adds):
        qh = q_head(head)
        parts = [lax.dot_general(k_ref[j, :, pair_cols(head)], qh, _NT, preferred_element_type=F32) + a
                 for j, a in zip(js, adds)]
        return parts[0] if len(parts) == 1 else jnp.concatenate(parts, axis=0)

    def weighted_values(head, js, p):
        vals = [vt_ref[j, head_rows(head), :] for j in js]
        vb = vals[0] if len(vals) == 1 else jnp.concatenate(vals, axis=1)
        vb = jnp.concatenate([vb, jnp.ones((SUM_ROWS, vb.shape[1]), BF16)], axis=0)
        return jnp.dot(vb, p.astype(BF16), preferred_element_type=F32)

    def start(head, js, st):
        m = jnp.max(st, axis=0, keepdims=True)
        m_refs[head][...] = m
        acc_refs[head][...] = weighted_values(head, js, jnp.exp2(st - m))

    def update(head, js, st):
        m = m_refs[head][...]
        m_new = jnp.maximum(m, jnp.max(st, axis=0, keepdims=True))
        alpha = jnp.exp2(m - m_new)
        m_refs[head][...] = m_new
        acc_refs[head][...] = alpha * acc_refs[head][...] + weighted_values(head, js, jnp.exp2(st - m_new))

    def sweep(js, adds_of, absorb):
        st = scores(0, js, adds_of(0))
        for head in range(ATT_HEADS):
            nxt = scores(head + 1, js, adds_of(head + 1)) if head + 1 < ATT_HEADS else None
            absorb(head, js, st)
            st = nxt

    def far_add(head, j):
        return neg_ref[pl.ds(head * nb + j, 1), :] + far_ref[head]

    def near_adds(h):
        return [bias_ref[h, 0], bias_ref[h, 1] + neg_ref[pl.ds(h * nb + i - 1, 1), :]]

    @pl.when(i == 0)
    def _():
        sweep([i], lambda h: [bias_ref[h, 0]], start)

    @pl.when((i == 1) | (i == 2))
    def _():
        sweep([i, i - 1], near_adds, start)

    @pl.when(i >= 3)
    def _():
        sweep([i, i - 1, i - 2, i - 3],
              lambda h: near_adds(h) + [far_add(h, i - 2), far_add(h, i - 3)], start)

    n_far = jnp.where(i >= 3, i - 3, jnp.maximum(i - 1, 0))

    def far_step(first, count):
        js = [first + c for c in range(count)]
        sweep(js, lambda h: [far_add(h, j) for j in js], update)

    def far_quad(jj, carry):
        far_step(4 * jj, 4)
        return carry

    lax.fori_loop(0, n_far // 4, far_quad, 0)
    rest = n_far % 4
    pl.when(rest >= 2)(lambda: far_step(n_far - rest, 2))
    pl.when(rest % 2 == 1)(lambda: far_step(n_far - 1, 1))

    for hp in range(ATT_HEADS // 2):
        outs = [acc_refs[h][0:HEAD_DIM, :] / acc_refs[h][HEAD_DIM:HEAD_DIM + 1, :] for h in (2 * hp, 2 * hp + 1)]
        pair = jnp.concatenate(outs, axis=0)
        o_ref[0, :, hp * pair_w:(hp + 1) * pair_w] = pair.T.astype(BF16)


def _rel_bucket(dist):
    n = jnp.maximum(dist, 0)
    max_exact = REL_BUCKETS // 2
    nf = jnp.maximum(n, 1).astype(F32)
    large = max_exact + (jnp.log(nf / max_exact) / math.log(REL_MAX_DIST / max_exact)
                         * (REL_BUCKETS - max_exact)).astype(jnp.int32)
    large = jnp.minimum(large, REL_BUCKETS - 1)
    return jnp.where(n < max_exact, n, large)


def _moba_bias_tiles(rel_bias):
    blk = MOBA_BLOCK
    span = 2 * blk
    rel_bias = rel_bias * LOG2E
    vec = rel_bias[_rel_bucket(jnp.arange(span))].T.astype(F32)
    masked = jnp.full_like(vec[:, :blk], MASK_NEG)
    ring_own = jnp.concatenate([vec[:, :blk], masked], axis=1)
    ring_prev = jnp.concatenate([vec[:, blk:], vec[:, :blk]], axis=1)

    def toeplitz(ring):
        flat = jnp.tile(ring, (1, blk))[:, :blk * (span - 1)]
        return flat.reshape(-1, blk, span - 1)[:, :, :blk]

    tiles = jnp.stack([toeplitz(ring_own), toeplitz(ring_prev)], axis=1)
    far = rel_bias[_rel_bucket(jnp.array(span))]
    return tiles, far.astype(F32)


def _moba(q4, k4, vt4, bias_tiles, far_bias, bsz):
    nb_total, blk, a = q4.shape
    nb = nb_total // bsz
    return pl.pallas_call(
        _moba_kernel,
        grid=(bsz, nb),
        in_specs=[
            pl.BlockSpec(memory_space=pltpu.SMEM),
            pl.BlockSpec((1, blk, a), lambda b, i: (b * nb + i, 0, 0)),
            pl.BlockSpec((nb, blk, a), lambda b, i: (b, 0, 0)),
            pl.BlockSpec((nb, a, blk), lambda b, i: (b, 0, 0)),
            _full(bias_tiles.shape),
        ],
        out_specs=pl.BlockSpec((1, blk, a), lambda b, i: (b * nb + i, 0, 0)),
        out_shape=jax.ShapeDtypeStruct((nb_total, blk, a), BF16),
        scratch_shapes=[pltpu.VMEM((nb, a), F32), pltpu.VMEM((ATT_HEADS * nb, blk), F32)]
        + [pltpu.VMEM((1, blk), F32), pltpu.VMEM((HEAD_DIM + SUM_ROWS, blk), F32)] * ATT_HEADS,
        compiler_params=_cparams("arbitrary", "arbitrary"),
        name="moba_attention",
    )(far_bias, q4, k4, vt4, bias_tiles)


def _ssm_weights(a_re, a_im, log_dt, b_re, b_im, c_re, c_im, d_skip):
    g, p, cw, n = SSM_GROUPS, SSM_STATE, SSM_GROUP, SSM_CHUNK
    gs = LANES // cw
    nq = g // gs
    npair = n // 2
    pairs_per_set = gs // 2
    lam_re = jnp.minimum(a_re.astype(F32), -1e-4)
    lam_im = a_im.astype(F32)
    dt = jnp.exp(log_dt.astype(F32))[:, None]
    mag = jnp.exp(lam_re * dt)
    lb_re = mag * jnp.cos(lam_im * dt)
    lb_im = mag * jnp.sin(lam_im * dt)
    n_re = lb_re - 1.0
    n_im = lb_im
    den = lam_re * lam_re + lam_im * lam_im
    z_re = ((n_re * lam_re + n_im * lam_im) / den)[..., None]
    z_im = ((n_im * lam_re - n_re * lam_im) / den)[..., None]
    br, bi = b_re.astype(F32), b_im.astype(F32)
    bb_re = z_re * br - z_im * bi
    bb_im = z_re * bi + z_im * br
    pw_re, pw_im = [jnp.ones_like(lb_re)], [jnp.zeros_like(lb_im)]
    for _ in range(n):
        r, im = pw_re[-1], pw_im[-1]
        pw_re.append(r * lb_re - im * lb_im)
        pw_im.append(r * lb_im + im * lb_re)
    pw_re, pw_im = jnp.stack(pw_re), jnp.stack(pw_im)
    cr, ci = c_re.astype(F32), c_im.astype(F32)
    hi = lax.Precision.HIGHEST
    rev_re, rev_im = pw_re[n - 1::-1], pw_im[n - 1::-1]
    sb_re = jnp.einsum('sgp,gpc->sgcp', rev_re, bb_re) - jnp.einsum('sgp,gpc->sgcp', rev_im, bb_im)
    sb_im = jnp.einsum('sgp,gpc->sgcp', rev_re, bb_im) + jnp.einsum('sgp,gpc->sgcp', rev_im, bb_re)
    cl_re = jnp.einsum('gcp,tgp->tgpc', cr, pw_re[1:]) - jnp.einsum('gcp,tgp->tgpc', ci, pw_im[1:])
    cl_im = jnp.einsum('gcp,tgp->tgpc', cr, pw_im[1:]) + jnp.einsum('gcp,tgp->tgpc', ci, pw_re[1:])
    cp_re = jnp.einsum('gcp,tgp->tgcp', cr, pw_re[:n]) - jnp.einsum('gcp,tgp->tgcp', ci, pw_im[:n])
    cp_im = jnp.einsum('gcp,tgp->tgcp', cr, pw_im[:n]) + jnp.einsum('gcp,tgp->tgcp', ci, pw_re[:n])
    kern = (jnp.einsum('tgcp,gpd->gtcd', cp_re, bb_re, precision=hi)
            - jnp.einsum('tgcp,gpd->gtcd', cp_im, bb_im, precision=hi))

    kp = jnp.concatenate([jnp.zeros_like(kern[:, :1]), kern], axis=1)
    dl = jnp.arange(npair)[:, None, None]
    s0 = jnp.arange(2)[None, :, None]
    t0 = jnp.arange(2)[None, None, :]
    lag = 2 * dl + t0 - s0
    kl = kp[:, lag + 1].reshape(nq, gs, npair, 2, 2, cw, cw)
    base = kl.transpose(0, 2, 3, 1, 6, 4, 5)
    same_group = jnp.eye(gs, dtype=F32)
    w_toe = jnp.concatenate(
        [base[..., t0, :] * same_group[:, g1][None, None, None, :, None, None]
         for t0 in range(2) for g1 in range(gs)], axis=-1)
    w_toe = w_toe.reshape(nq, npair, MXU_DIM, MXU_DIM).astype(BF16)

    member = (jnp.arange(gs)[None, :, None]
              == 2 * jnp.arange(pairs_per_set)[:, None, None] + jnp.arange(2)[None, None, :]).astype(F32)

    def pair_tiles(x):
        cols = [x[ri][:, None] * member[:, :, gl][None, :, None, None, :, None, None]
                for ri in range(2) for gl in range(2)]
        return jnp.concatenate(cols, axis=-1).reshape(g // 2, npair, MXU_DIM, MXU_DIM)

    sbs = jnp.stack([sb_re, sb_im]).reshape(2, npair, 2, nq, gs, cw, p)
    w_in_state = pair_tiles(sbs.transpose(0, 3, 1, 2, 4, 5, 6)).astype(BF16)
    sos = jnp.stack([cl_re, -cl_im]).reshape(2, npair, 2, nq, gs, p, cw)
    w_state_out = jnp.swapaxes(pair_tiles(sos.transpose(0, 3, 1, 2, 4, 6, 5)), -1, -2).astype(BF16)
    decay = jnp.stack([pw_re[n].reshape(g // 2, 2 * p), pw_im[n].reshape(g // 2, 2 * p)])
    dvec = d_skip.astype(F32).reshape(1, g * cw)
    return w_toe, w_in_state, w_state_out, decay, dvec


def _ssm_kernel(u_ref, wt_ref, wi_ref, wo_ref, dec_ref, d_ref, y_ref, s_ref, xp_ref, x_ref, io_ref, *, bsz):
    n = SSM_CHUNK
    kt = u_ref.shape[0] // n
    width = u_ref.shape[1] // bsz
    rows = kt * bsz
    npair = n // 2
    nq = wt_ref.shape[0]
    npairs_g = wi_ref.shape[0]
    per_set = npairs_g // nq
    half = MXU_DIM // 2

    @pl.when(pl.program_id(0) == 0)
    def _():
        x_ref[...] = jnp.zeros_like(x_ref)

    nq_all = u_ref.shape[1] // LANES
    for j in range(nq_all):
        io_ref[j] = u_ref[:, j * LANES:(j + 1) * LANES]

    def block_of(b, q):
        return b * (width // LANES) + q

    def piece(s, q):
        return jnp.concatenate([io_ref[block_of(b, q), pl.ds(s, kt, stride=n), :] for b in range(bsz)], axis=0)

    lhs = {(sp, q): jnp.concatenate([piece(2 * sp, q), piece(2 * sp + 1, q)], axis=-1).astype(BF16)
           for sp in range(npair) for q in range(nq)}

    for gp in range(npairs_g):
        q = gp // per_set
        acc = jnp.dot(lhs[0, q], wi_ref[gp, 0], preferred_element_type=F32)
        for sp in range(1, npair):
            acc = acc + jnp.dot(lhs[sp, q], wi_ref[gp, sp], preferred_element_type=F32)
        s_ref[2 * gp] = acc[:, :half]
        s_ref[2 * gp + 1] = acc[:, half:]

    for gp in range(npairs_g):
        re_cols = slice(gp * MXU_DIM, gp * MXU_DIM + half)
        im_cols = slice(gp * MXU_DIM + half, (gp + 1) * MXU_DIM)
        ar = dec_ref[0, gp:gp + 1, :]
        ai = dec_ref[1, gp:gp + 1, :]
        xr = x_ref[:, re_cols]
        xi = x_ref[:, im_cols]
        for kc in range(kt):
            rs = pl.ds(kc, bsz, stride=kt)
            xp_ref[2 * gp, rs, :] = xr
            xp_ref[2 * gp + 1, rs, :] = xi
            sr = s_ref[2 * gp, rs, :]
            si = s_ref[2 * gp + 1, rs, :]
            xr, xi = ar * xr - ai * xi + sr, ar * xi + ai * xr + si
        x_ref[:, re_cols] = xr
        x_ref[:, im_cols] = xi

    for q in range(nq):
        for tp in range(npair):
            acc = jnp.dot(lhs[0, q], wt_ref[q, tp], preferred_element_type=F32)
            for sp in range(1, tp + 1):
                acc = acc + jnp.dot(lhs[sp, q], wt_ref[q, tp - sp], preferred_element_type=F32)
            for gp in range(q * per_set, (q + 1) * per_set):
                xp = jnp.concatenate([xp_ref[2 * gp], xp_ref[2 * gp + 1]], axis=1).astype(BF16)
                acc = acc + jnp.dot(xp, wo_ref[gp, tp], preferred_element_type=F32)
            for t0 in range(2):
                s = 2 * tp + t0
                y = acc[:, t0 * half:(t0 + 1) * half] + d_ref[:, q * LANES:(q + 1) * LANES] * piece(s, q)
                y = jax.nn.gelu(y)
                for b in range(bsz):
                    io_ref[block_of(b, q), pl.ds(s, kt, stride=n), :] = y[b * kt:(b + 1) * kt]

    for j in range(nq_all):
        y_ref[:, j * LANES:(j + 1) * LANES] = io_ref[j]


def _ssm(u_tm, weights, bsz, seq):
    n = SSM_CHUNK
    kt = SSM_CHUNKS_PER_STEP
    steps = kt * n
    rows = kt * bsz
    state_w = SSM_GROUPS * 2 * SSM_STATE
    blk = pl.BlockSpec((steps, bsz * SSM_WIDTH), lambda i: (i, 0))

    once = pl.Buffered(1)
    return pl.pallas_call(
        functools.partial(_ssm_kernel, bsz=bsz),
        grid=(seq // steps,),
        in_specs=[blk] + [_full(w.shape, pipeline_mode=once) for w in weights],
        out_specs=blk,
        out_shape=jax.ShapeDtypeStruct(u_tm.shape, F32),
        scratch_shapes=[pltpu.VMEM((state_w // LANES, rows, LANES), F32),
                        pltpu.VMEM((state_w // LANES, rows, LANES), F32),
                        pltpu.VMEM((bsz, state_w), F32),
                        pltpu.VMEM((bsz * SSM_WIDTH // LANES, steps, LANES), F32)],
        compiler_params=_cparams("arbitrary"),
        name="s5_chunked_scan",
    )(u_tm, *weights)


def _postmix_kernel(att_ref, ys_ref, ga_ref, gs_ref, x_ref, gt_ref, sc_ref, sh_ref, gpost_ref, gpre_ref,
                    wglu_ref, wso_ref, wao_ref, wo_ref, wrh_ref, wrl_ref, br_ref,
                    x1_ref, h2_ref, idx_ref, gate_ref, rank_ref, cnt_ref, run_ref):
    tm = x_ref.shape[0]
    ne = wrh_ref.shape[0]

    @pl.when(pl.program_id(0) == 0)
    def _():
        run_ref[...] = jnp.zeros_like(run_ref)

    halves = [slice(r * (tm // 2), (r + 1) * (tm // 2)) for r in range(2)]
    dot = functools.partial(jnp.dot, preferred_element_type=F32)
    att = [dot(att_ref[r, :], wao_ref[...]) for r in halves]
    glu = [dot(ys_ref[r, :].astype(BF16), wglu_ref[...]) for r in halves]
    ssm = []
    for g in glu:
        sg = (g[:, :SSM_WIDTH] * jax.nn.sigmoid(g[:, SSM_WIDTH:])).astype(BF16)
        ssm.append(dot(sg, wso_ref[...]))
    y = []
    for r, a, s in zip(halves, att, ssm):
        merged = (ga_ref[r, :].astype(F32) * a + gs_ref[r, :].astype(F32) * s).astype(BF16)
        y.append(dot(merged, wo_ref[...]))
    logits = []
    for r, yh in zip(halves, y):
        x1 = x_ref[r, :] + gt_ref[0] * _rms(yh, gpost_ref[...])
        x1_ref[r, :] = x1
        h2 = _rms(x1, gpre_ref[...]) * (1.0 + sc_ref[0]) + sh_ref[0]
        h2_ref[r, :] = _pack_bf16_pairs(h2)
        h_hi = h2.astype(BF16)
        h_lo = (h2 - h_hi.astype(F32)).astype(BF16)
        logits.append(lax.dot_general(wrh_ref[...], h_hi, _NT, preferred_element_type=F32)
                      + lax.dot_general(wrh_ref[...], h_lo, _NT, preferred_element_type=F32)
                      + lax.dot_general(wrl_ref[...], h_hi, _NT, preferred_element_type=F32))
    logits = jnp.concatenate(logits, axis=1) + br_ref[...]
    eidx = lax.broadcasted_iota(jnp.int32, (ne, tm), 0)
    lg = logits
    vals, idxs = [], []
    for _ in range(TOP_K):
        mx = jnp.max(lg, axis=0, keepdims=True)
        first = jnp.min(jnp.where(lg == mx, eidx, ne), axis=0, keepdims=True)
        vals.append(mx)
        idxs.append(first)
        lg = jnp.where(eidx == first, -jnp.inf, lg)
    exps = [jnp.exp(v - vals[0]) for v in vals]
    denom = exps[0] + exps[1] + exps[2] + exps[3]
    gate_ref[...] = jnp.concatenate([e / denom for e in exps], axis=0)
    idx_ref[...] = jnp.concatenate(idxs, axis=0)
    onehot = jnp.where(lg == -jnp.inf, 1.0, 0.0)
    t_row = lax.broadcasted_iota(jnp.int32, (tm, tm), 0)
    t_col = lax.broadcasted_iota(jnp.int32, (tm, tm), 1)
    earlier = jnp.where(t_row < t_col, 1.0, 0.0).astype(BF16)
    before = jnp.dot(onehot.astype(BF16), earlier, preferred_element_type=F32) + run_ref[:, 0:1]
    ranks = [jnp.sum(jnp.where(eidx == ix, before, 0.0), axis=0, keepdims=True) for ix in idxs]
    rank_ref[...] = jnp.concatenate(ranks, axis=0).astype(jnp.int32)
    run_ref[...] = run_ref[...] + jnp.sum(onehot, axis=1, keepdims=True)
    cnt_ref[...] = run_ref[...].astype(jnp.int32)


def _postmix(att2, ys_tm, ga, gs, x2, gt1, sc2, sh2, g_post, g_pre, w_glu, w_ssm_out, w_att_out, w_out,
             w_router, b_router, seq):
    t, d = x2.shape
    tm = TOKEN_TILE
    tiles_per_seq = seq // tm
    ne = w_router.shape[1]
    wr_t = w_router.T.astype(F32)
    wr_hi = wr_t.astype(BF16)
    wr_lo = (wr_t - wr_hi.astype(F32)).astype(BF16)
    mod_spec = pl.BlockSpec((1, 1, d), lambda i: (i // tiles_per_seq, 0, 0))
    row = lambda cols: pl.BlockSpec((tm, cols), lambda i: (i, 0))
    time_major = pl.BlockSpec((tm, SSM_WIDTH), lambda i: (i % tiles_per_seq, i // tiles_per_seq))
    col = pl.BlockSpec((TOP_K, tm), lambda i: (0, i))
    weights = [w_glu.astype(BF16), w_ssm_out.astype(BF16), w_att_out.astype(BF16), w_out.astype(BF16),
               wr_hi, wr_lo, b_router.astype(F32).reshape(ne, 1)]
    return pl.pallas_call(
        _postmix_kernel,
        grid=(t // tm,),
        in_specs=[row(ATT_WIDTH), time_major, row(d), row(d), row(d), mod_spec, mod_spec, mod_spec,
                  _full((1, d)), _full((1, d))]
        + [_full(w.shape, pipeline_mode=pl.Buffered(1)) for w in weights],
        out_specs=[row(d), row(d // 2), col, col, col, _full((ne, 128))],
        out_shape=[jax.ShapeDtypeStruct((t, d), F32), jax.ShapeDtypeStruct((t, d // 2), jnp.uint32),
                   jax.ShapeDtypeStruct((TOP_K, t), jnp.int32), jax.ShapeDtypeStruct((TOP_K, t), F32),
                   jax.ShapeDtypeStruct((TOP_K, t), jnp.int32), jax.ShapeDtypeStruct((ne, 128), jnp.int32)],
        scratch_shapes=[pltpu.VMEM((ne, 128), F32)],
        compiler_params=_cparams("arbitrary"),
        name="postmix_router",
    )(att2, ys_tm, ga, gs, x2, gt1, sc2, sh2, g_post.reshape(1, d), g_pre.reshape(1, d), *weights)


def _sc_worker_base(per_worker):
    return (lax.axis_index("s") * SC_CORES + lax.axis_index("c")) * per_worker


def _sc_mesh():
    return plsc.VectorSubcoreMesh(core_axis_name="c", subcore_axis_name="s",
                                  num_cores=SC_CORES, num_subcores=SC_SUBCORES)


def _sc_scatter_kernel(rows_hbm, idx_hbm, out_hbm, idx_v, rows_v, load_sems, store_sems, *, per_worker):
    chunk = rows_v.shape[1]
    n_tokens = rows_hbm.shape[0]
    n_chunks = per_worker // chunk
    base = _sc_worker_base(per_worker)

    def load(c, b):
        off = pl.multiple_of(base + c * chunk, chunk)
        for k in range(TOP_K):
            pltpu.sync_copy(idx_hbm.at[pl.ds(k * n_tokens + off, chunk)], idx_v.at[b, k])
        return pltpu.make_async_copy(rows_hbm.at[pl.ds(off, chunk)], rows_v.at[b], load_sems.at[b])

    def stores(b):
        return [pltpu.make_async_copy(rows_v.at[b], out_hbm.at[idx_v.at[b, k]], store_sems.at[b])
                for k in range(TOP_K)]

    load(0, 0).start()

    @pl.loop(0, n_chunks, step=2)
    def _(c0):
        for b in range(2):
            c = c0 + b

            @pl.when(c > 0)
            def _():
                for cp in stores(1 - b):
                    cp.wait()

            @pl.when(c + 1 < n_chunks)
            def _():
                load(c + 1, 1 - b).start()

            pltpu.make_async_copy(rows_hbm.at[pl.ds(0, chunk)], rows_v.at[b], load_sems.at[b]).wait()
            for cp in stores(b):
                cp.start()

    for cp in stores((n_chunks - 1) % 2):
        cp.wait()


def _sc_scatter_rows(rows, idx, n_out):
    t, d = rows.shape
    workers = SC_CORES * SC_SUBCORES
    per_worker = t // workers
    assert t % workers == 0 and per_worker % (2 * SC_GATHER_ROWS) == 0
    return pl.kernel(
        functools.partial(_sc_scatter_kernel, per_worker=per_worker),
        out_type=jax.ShapeDtypeStruct((n_out, d), rows.dtype),
        mesh=_sc_mesh(),
        scratch_types=[pltpu.VMEM((2, TOP_K, SC_GATHER_ROWS), jnp.int32),
                       pltpu.VMEM((2, SC_GATHER_ROWS, d), rows.dtype),
                       pltpu.SemaphoreType.DMA((2,)), pltpu.SemaphoreType.DMA((2,))],
        name="sc_row_scatter",
    )(rows, idx)


def _sc_gather_kernel(table_hbm, idx_hbm, out_hbm, idx_v, rows_v, sems, *, per_worker):
    chunk = rows_v.shape[1]
    n_chunks = per_worker // chunk
    base = _sc_worker_base(per_worker)

    def gather(c, b):
        off = pl.multiple_of(base + c * chunk, chunk)
        pltpu.sync_copy(idx_hbm.at[pl.ds(off, chunk)], idx_v.at[b])
        return pltpu.make_async_copy(table_hbm.at[idx_v.at[b]], rows_v.at[b], sems.at[b])

    gather(0, 0).start()

    @pl.loop(0, n_chunks, step=2)
    def _(c0):
        for b in range(2):
            c = c0 + b

            @pl.when(c + 1 < n_chunks)
            def _():
                gather(c + 1, 1 - b).start()

            pltpu.make_async_copy(table_hbm.at[idx_v.at[b]], rows_v.at[b], sems.at[b]).wait()
            off = pl.multiple_of(base + c * chunk, chunk)
            pltpu.sync_copy(rows_v.at[b], out_hbm.at[pl.ds(off, chunk)])


def _sc_gather_rows(table, idx):
    n = idx.shape[0]
    d = table.shape[1]
    workers = SC_CORES * SC_SUBCORES
    per_worker = n // workers
    assert n % workers == 0 and per_worker % (2 * SC_GATHER_ROWS) == 0
    return pl.kernel(
        functools.partial(_sc_gather_kernel, per_worker=per_worker),
        out_type=jax.ShapeDtypeStruct((n, d), table.dtype),
        mesh=_sc_mesh(),
        scratch_types=[pltpu.VMEM((2, SC_GATHER_ROWS), jnp.int32),
                       pltpu.VMEM((2, SC_GATHER_ROWS, d), table.dtype),
                       pltpu.SemaphoreType.DMA((2,))],
        name="sc_row_gather",
    )(table, idx)


def _experts_kernel(be_ref, nused_ref, valid_ref, next_ref, x_ref, b1_ref, b2_ref, w1_hbm, w2_hbm, y_ref,
                    w1s_ref, w2s_ref, w1b_ref, w2b_ref, sems, *, layer):
    i = pl.program_id(0)
    prev = be_ref[jnp.maximum(i - 1, 0)]
    fresh = (i < nused_ref[0]) & ((i == 0) | (be_ref[i] != prev))

    def fetch(e):
        return (pltpu.make_async_copy(w1_hbm.at[layer, e], w1s_ref, sems.at[0]),
                pltpu.make_async_copy(w2_hbm.at[layer, e], w2s_ref, sems.at[1]))

    @pl.when(i == 0)
    def _():
        for cp in fetch(be_ref[0]):
            cp.start()

    @pl.when(fresh)
    def _():
        for cp in fetch(be_ref[i]):
            cp.wait()
        w1b_ref[...] = w1s_ref[...].astype(BF16)
        w2b_ref[...] = w2s_ref[...].astype(BF16)

        @pl.when(next_ref[i] >= 0)
        def _():
            for cp in fetch(next_ref[i]):
                cp.start()

    used = i < nused_ref[0]
    valid = valid_ref[i]
    half = x_ref.shape[0] // 2

    def ffn(n_rows):
        row = lax.broadcasted_iota(jnp.int32, (n_rows, x_ref.shape[1]), 0)
        x_lo, x_hi = _unpack_bf16_pairs(jnp.where(row < valid, x_ref[0:n_rows, :], jnp.uint32(0)))
        x = jnp.concatenate([x_lo.astype(BF16), x_hi.astype(BF16)], axis=1)
        gu = jnp.dot(x, w1b_ref[...], preferred_element_type=F32) + b1_ref[...]
        g = jnp.minimum(gu[:, :D_FF], SWIGLU_LIMIT)
        up = jnp.clip(gu[:, D_FF:], -SWIGLU_LIMIT, SWIGLU_LIMIT)
        act = ((up + 1.0) * g * jax.nn.sigmoid(SWIGLU_ALPHA * g)).astype(BF16)
        y = jnp.dot(act, w2b_ref[...], preferred_element_type=F32) + b2_ref[...]
        y_ref[0:n_rows, :] = _pack_bf16_pairs(y)

    @pl.when(used & (valid > half))
    def _():
        ffn(2 * half)

    def short(n_rows):
        ffn(n_rows)
        y_ref[n_rows:, :] = jnp.zeros((2 * half - n_rows, y_ref.shape[1]), y_ref.dtype)

    pl.when(used & (valid <= half) & (valid > half // 2))(lambda: short(half))
    pl.when(used & (valid <= half // 2))(lambda: short(half // 2))

    @pl.when(jnp.logical_not(used))
    def _():
        y_ref[...] = jnp.zeros_like(y_ref)


def _experts(xb, block_e, n_used, valid, next_e, w1, b1, w2, b2, layer):
    p_rows, packed_w = xb.shape
    d = 2 * packed_w
    depth, ne = w1.shape[:2]
    rb = EXPERT_ROWS
    bmap = lambda i, be, nu, va, nx: (layer, be[i], 0, 0)
    rows = pl.BlockSpec((rb, packed_w), lambda i, be, nu, va, nx: (i, 0))
    grid_spec = pltpu.PrefetchScalarGridSpec(
        num_scalar_prefetch=4,
        grid=(p_rows // rb,),
        in_specs=[
            rows,
            pl.BlockSpec((None, None, 1, 2 * D_FF), bmap),
            pl.BlockSpec((None, None, 1, d), bmap),
            pl.BlockSpec(memory_space=pl.ANY),
            pl.BlockSpec(memory_space=pl.ANY),
        ],
        out_specs=rows,
        scratch_shapes=[pltpu.VMEM((d, 2 * D_FF), w1.dtype), pltpu.VMEM((D_FF, d), w2.dtype),
                        pltpu.VMEM((d, 2 * D_FF), BF16), pltpu.VMEM((D_FF, d), BF16),
                        pltpu.SemaphoreType.DMA((2,))],
    )
    return pl.pallas_call(
        functools.partial(_experts_kernel, layer=layer),
        grid_spec=grid_spec,
        out_shape=jax.ShapeDtypeStruct((p_rows, packed_w), jnp.uint32),
        compiler_params=_cparams("arbitrary"),
        name="expert_ffn",
    )(block_e, n_used, valid, next_e, xb, b1.reshape(depth, ne, 1, 2 * D_FF), b2.reshape(depth, ne, 1, d), w1, w2)


def _combine_kernel(gate_ref, x_ref, gt_ref, g_ref, y0_ref, y1_ref, y2_ref, y3_ref, o_ref):
    tm = x_ref.shape[0]
    gates = gate_ref[...]
    gates = jnp.concatenate([gates, jnp.zeros((LANES - TOP_K, tm), F32)], axis=0).T
    lo, hi = _unpack_bf16_pairs(y0_ref[...])
    y_lo, y_hi = gates[:, 0:1] * lo, gates[:, 0:1] * hi
    for k, y_ref in enumerate((y1_ref, y2_ref, y3_ref), start=1):
        lo, hi = _unpack_bf16_pairs(y_ref[...])
        y_lo, y_hi = y_lo + gates[:, k:k + 1] * lo, y_hi + gates[:, k:k + 1] * hi
    y = jnp.concatenate([y_lo, y_hi], axis=1)
    o_ref[...] = x_ref[...] + gt_ref[0] * _rms(y, g_ref[...])


def _combine(y4, gate_t, x2, gt2, g_post, seq):
    t, d = x2.shape
    tm = TOKEN_TILE
    tiles = t // tm
    tiles_per_seq = seq // tm
    row = pl.BlockSpec((tm, d), lambda i: (i, 0))
    slot = lambda k: pl.BlockSpec((tm, y4.shape[1]), lambda i: (k * tiles + i, 0))
    return pl.pallas_call(
        _combine_kernel,
        grid=(tiles,),
        in_specs=[pl.BlockSpec((TOP_K, tm), lambda i: (0, i)),
                  row, pl.BlockSpec((1, 1, d), lambda i: (i // tiles_per_seq, 0, 0)), _full((1, d))]
        + [slot(k) for k in range(TOP_K)],
        out_specs=row,
        out_shape=jax.ShapeDtypeStruct((t, d), F32),
        compiler_params=_cparams("arbitrary"),
        name="expert_combine",
    )(gate_t, x2, gt2, g_post.reshape(1, d), y4, y4, y4, y4)


def _route_plan(idx_t, rank_t, counts):
    rb = EXPERT_ROWS
    k, t = idx_t.shape
    padded = (counts + rb - 1) // rb * rb
    pad_ends = jnp.cumsum(padded)
    pad_starts = pad_ends - padded
    experts = jnp.arange(N_EXPERTS, dtype=jnp.int32)
    start_of = jnp.sum(jnp.where(idx_t[None] == experts[:, None, None], pad_starts[:, None, None], 0), axis=0)
    dest = (start_of + rank_t).astype(jnp.int32)
    n_blocks = (k * t) // rb + N_EXPERTS
    blk_start = jnp.arange(n_blocks, dtype=jnp.int32) * rb
    block_e = jnp.minimum(jnp.sum(pad_ends[None, :] <= blk_start[:, None], axis=1), N_EXPERTS - 1)
    onehot_e = block_e[:, None] == experts[None, :]
    cnt_b = jnp.sum(jnp.where(onehot_e, counts[None, :], 0), axis=1)
    start_b = jnp.sum(jnp.where(onehot_e, pad_starts[None, :], 0), axis=1)
    valid = jnp.clip(cnt_b - (blk_start - start_b), 0, rb).astype(jnp.int32)
    n_used = (pad_ends[-1] // rb).astype(jnp.int32).reshape(1)
    later_nonempty = (experts[None, :] > experts[:, None]) & (counts[None, :] > 0)
    next_nonempty = jnp.min(jnp.where(later_nonempty, experts[None, :], N_EXPERTS), axis=1)
    next_nonempty = jnp.where(next_nonempty == N_EXPERTS, -1, next_nonempty)
    next_e = jnp.sum(jnp.where(onehot_e, next_nonempty[None, :], 0), axis=1).astype(jnp.int32)
    return dest.reshape(-1), block_e.astype(jnp.int32), n_used, valid, next_e, n_blocks * rb


def kernel(x, c, rel_bias, w_ada, b_ada, g_pre_mix, g_post_mix, g_pre_ffn, g_post_ffn, w_in, ssm_a_re, ssm_a_im, ssm_log_dt, ssm_b_re, ssm_b_im, ssm_c_re, ssm_c_im, ssm_d, w_glu, w_ssm_out, w_att_out, w_out, w_router, b_router, w_exp_in, b_exp_in, w_exp_out, b_exp_out):
    bsz, seq, d = x.shape
    depth = w_ada.shape[0]
    t = bsz * seq
    assert d == D_MODEL and seq % TOKEN_TILE == 0 and TOKEN_TILE % MOBA_BLOCK == 0
    assert seq % (SSM_CHUNK * SSM_CHUNKS_PER_STEP) == 0

    mod = _ada_mod(c, w_ada, b_ada)
    bias_tiles, far_bias = _moba_bias_tiles(rel_bias.astype(F32))
    x2 = x.reshape(t, d)
    for l in range(depth):
        sh1, sc1, gt1, sh2, sc2, gt2 = [m.reshape(bsz, 1, d) for m in jnp.split(mod[l], N_MOD, axis=-1)]
        q4, k4, vt4, u_tm, ga, gs = _premix(x2, g_pre_mix[l], sc1, sh1, w_in[l], seq)
        att = _moba(q4, k4, vt4, bias_tiles, far_bias, bsz).reshape(t, ATT_WIDTH)
        ssm_w = _ssm_weights(ssm_a_re[l], ssm_a_im[l], ssm_log_dt[l], ssm_b_re[l], ssm_b_im[l],
                             ssm_c_re[l], ssm_c_im[l], ssm_d[l])
        ys_tm = _ssm(u_tm, ssm_w, bsz, seq)
        x1, h2, idx_t, gate_t, rank_t, cnt = _postmix(
            att, ys_tm, ga, gs, x2, gt1, sc2, sh2, g_post_mix[l], g_pre_ffn[l],
            w_glu[l], w_ssm_out[l], w_att_out[l], w_out[l], w_router[l], b_router[l], seq)
        dest_flat, block_e, n_used, valid, next_e, p_rows = _route_plan(idx_t, rank_t, cnt[:, 0])
        xb = _sc_scatter_rows(h2, dest_flat, p_rows)
        yb = _experts(xb, block_e, n_used, valid, next_e, w_exp_in, b_exp_in, w_exp_out, b_exp_out, l)
        y4 = _sc_gather_rows(yb, dest_flat)
        x2 = _combine(y4, gate_t, x1, gt2, g_post_ffn[l], seq)
    return x2.reshape(bsz, seq, d)
```

```python
import functools
import math

import jax
import jax.numpy as jnp
from jax import lax
from jax.experimental import pallas as pl
from jax.experimental.pallas import tpu as pltpu
from jax.experimental.pallas import tpu_sc as plsc

F32 = jnp.float32
BF16 = jnp.bfloat16

D_MODEL = 1024
ATT_HEADS = 8
HEAD_DIM = 64
ATT_WIDTH = ATT_HEADS * HEAD_DIM
MOBA_BLOCK = 256
MOBA_TOPK = 3
REL_BUCKETS = 32
REL_MAX_DIST = 128
SSM_WIDTH = D_MODEL // 2
SSM_GROUP = 16
SSM_GROUPS = SSM_WIDTH // SSM_GROUP
SSM_STATE = 64
N_EXPERTS = 32
TOP_K = 4
D_FF = D_MODEL
SWIGLU_ALPHA = 1.702
SWIGLU_LIMIT = 7.0
RMS_EPS = 1e-6
N_MOD = 6

SSM_CHUNK = 8
SSM_CHUNKS_PER_STEP = 16
LANES = 128
MXU_DIM = 256
TOKEN_TILE = 1024
SC_CORES = 2
SC_SUBCORES = 16
SC_GATHER_ROWS = 64
EXPERT_ROWS = 1024
MASK_NEG = -1e30
LOG2E = math.log2(math.e)
SUM_ROWS = 16
VMEM_LIMIT = 56 * 1024 * 1024

_NT = (((1,), (1,)), ((), ()))


def _cparams(*sem):
    return pltpu.CompilerParams(dimension_semantics=sem, vmem_limit_bytes=VMEM_LIMIT)


def _pack_bf16_pairs(x):
    n = x.shape[1] // 2
    bits = lax.bitcast_convert_type(x.astype(BF16).astype(F32), jnp.uint32)
    return lax.shift_right_logical(bits[:, :n], jnp.uint32(16)) | (bits[:, n:] & jnp.uint32(0xFFFF0000))


def _unpack_bf16_pairs(w):
    lo = lax.bitcast_convert_type(lax.shift_left(w, jnp.uint32(16)), F32)
    hi = lax.bitcast_convert_type(w & jnp.uint32(0xFFFF0000), F32)
    return lo, hi


def _rms(x, g):
    return x * lax.rsqrt(jnp.mean(x * x, axis=-1, keepdims=True) + RMS_EPS) * g


def _full(shape, **kw):
    n = len(shape)
    return pl.BlockSpec(shape, lambda *_: (0,) * n, **kw)


def _ada_kernel(c_ref, w_ref, b_ref, o_ref):
    c = c_ref[...]
    cond = c * jax.nn.sigmoid(c)
    o_ref[0] = jnp.dot(cond, w_ref[0], preferred_element_type=F32,
                       precision=lax.Precision.HIGHEST) + b_ref[0]


def _ada_mod(c, w_ada, b_ada):
    depth, d, nd = w_ada.shape
    bsz = c.shape[0]
    return pl.pallas_call(
        _ada_kernel,
        grid=(depth, nd // d),
        in_specs=[
            pl.BlockSpec((bsz, d), lambda l, j: (0, 0)),
            pl.BlockSpec((1, d, d), lambda l, j: (l, 0, j)),
            pl.BlockSpec((1, 1, d), lambda l, j: (l, 0, j)),
        ],
        out_specs=pl.BlockSpec((1, bsz, d), lambda l, j: (l, 0, j)),
        out_shape=jax.ShapeDtypeStruct((depth, bsz, nd), F32),
        compiler_params=_cparams("arbitrary", "arbitrary"),
        name="ada_mod",
    )(c, w_ada, b_ada.reshape(depth, 1, nd))


def _premix_kernel(x_ref, g_ref, sc_ref, sh_ref, wq_ref, wk_ref, wvt_ref, wu_ref, wga_ref, wgs_ref,
                   q_ref, k_ref, vt_ref, u_ref, ga_ref, gs_ref):
    x = x_ref[...]
    h = _rms(x, g_ref[...]) * (1.0 + sc_ref[0]) + sh_ref[0]
    hb = h.astype(BF16)
    nblk = q_ref.shape[0]
    q = (jnp.dot(hb, wq_ref[...], preferred_element_type=F32) * (HEAD_DIM ** -0.5 * LOG2E)).astype(BF16)
    k = jnp.dot(hb, wk_ref[...], preferred_element_type=F32).astype(BF16)
    vt = lax.dot_general(wvt_ref[...], hb, _NT, preferred_element_type=F32).astype(BF16)
    for r in range(nblk):
        q_ref[r] = q[r * MOBA_BLOCK:(r + 1) * MOBA_BLOCK]
        k_ref[r] = k[r * MOBA_BLOCK:(r + 1) * MOBA_BLOCK]
        vt_ref[r] = vt[:, r * MOBA_BLOCK:(r + 1) * MOBA_BLOCK]
    u = jnp.dot(hb, wu_ref[...], preferred_element_type=F32)
    for j in range(u_ref.shape[0]):
        u_ref[j] = u[:, j * LANES:(j + 1) * LANES]
    ga = jnp.dot(hb, wga_ref[...], preferred_element_type=F32)
    ga_ref[...] = jax.nn.sigmoid(ga).astype(BF16)
    gs = jnp.dot(hb, wgs_ref[...], preferred_element_type=F32)
    gs_ref[...] = jax.nn.sigmoid(gs).astype(BF16)


def _premix(x2, g, sc, sh, w_in, seq):
    t, d = x2.shape
    tm = TOKEN_TILE
    tiles_per_seq = seq // tm
    nblk = tm // MOBA_BLOCK
    a = ATT_WIDTH
    assert a == SSM_WIDTH and d == 2 * a and w_in.shape[1] == 4 * a + 2 * d
    wb = w_in.astype(BF16)
    wvt = wb[:, 2 * a:3 * a].T

    def cols(width, index):
        return pl.BlockSpec((d, width), lambda i: (0, index), pipeline_mode=pl.Buffered(1))

    mod_spec = pl.BlockSpec((1, 1, d), lambda i: (i // tiles_per_seq, 0, 0))
    blk3 = lambda rows, cols: pl.BlockSpec((nblk, rows, cols), lambda i: (i, 0, 0))
    row = lambda cols: pl.BlockSpec((tm, cols), lambda i: (i, 0))
    lane_blocks = SSM_WIDTH // LANES
    time_major = pl.BlockSpec((lane_blocks, tm, LANES), lambda i: (i // tiles_per_seq, i % tiles_per_seq, 0))
    once = functools.partial(_full, pipeline_mode=pl.Buffered(1))
    nb_total = t // MOBA_BLOCK
    return pl.pallas_call(
        _premix_kernel,
        grid=(t // tm,),
        in_specs=[row(d), _full((1, d)), mod_spec, mod_spec,
                  cols(a, 0), cols(a, 1), once((a, d)), cols(a, 3), cols(d, 2), cols(d, 3)],
        out_specs=[blk3(MOBA_BLOCK, a), blk3(MOBA_BLOCK, a), blk3(a, MOBA_BLOCK),
                   time_major, row(d), row(d)],
        out_shape=[jax.ShapeDtypeStruct((nb_total, MOBA_BLOCK, a), BF16),
                   jax.ShapeDtypeStruct((nb_total, MOBA_BLOCK, a), BF16),
                   jax.ShapeDtypeStruct((nb_total, a, MOBA_BLOCK), BF16),
                   jax.ShapeDtypeStruct(((t // seq) * lane_blocks, seq, LANES), F32),
                   jax.ShapeDtypeStruct((t, d), BF16),
                   jax.ShapeDtypeStruct((t, d), BF16)],
        compiler_params=_cparams("arbitrary"),
        name="premix_inproj",
    )(x2, g.reshape(1, d), sc, sh, wb, wb, wvt, wb, wb, wb)


def _moba_kernel(far_ref, q_ref, k_ref, vt_ref, bias_ref, o_ref, kmean_ref, neg_ref, *state_refs):
    m_refs = state_refs[0::2]
    acc_refs = state_refs[1::2]
    nb = k_ref.shape[0]
    blk = MOBA_BLOCK
    i = pl.program_id(1)

    @pl.when(i == 0)
    def _():
        for n in range(nb):
            kmean_ref[n:n + 1, :] = jnp.mean(k_ref[n].astype(F32), axis=0, keepdims=True)

    lane = lax.broadcasted_iota(jnp.int32, (blk, 2 * HEAD_DIM), 1)
    nidx = lax.broadcasted_iota(jnp.int32, (nb, blk), 0)
    valid = nidx < i
    pair_w = 2 * HEAD_DIM

    def pair_cols(head):
        return slice((head // 2) * pair_w, (head // 2 + 1) * pair_w)

    def head_rows(head):
        return slice(head * HEAD_DIM, (head + 1) * HEAD_DIM)

    def q_head(head):
        q2 = q_ref[0, :, pair_cols(head)]
        lo = (head % 2) * HEAD_DIM
        return jnp.where((lane >= lo) & (lane < lo + HEAD_DIM), q2, jnp.zeros_like(q2))

    for head in range(ATT_HEADS):
        km = kmean_ref[:, pair_cols(head)]
        km_hi = km.astype(BF16)
        km_lo = (km - km_hi.astype(F32)).astype(BF16)
        parts = lax.dot_general(jnp.concatenate([km_hi, km_lo], axis=0), q_head(head), _NT,
                                preferred_element_type=F32)
        gate = parts[0:nb] + parts[nb:2 * nb]
        g = jnp.where(valid, gate, -jnp.inf)
        sel = jnp.zeros((nb, blk), jnp.bool_)
        for _ in range(MOBA_TOPK):
            mx = jnp.max(g, axis=0, keepdims=True)
            first = jnp.min(jnp.where(g == mx, nidx, nb), axis=0, keepdims=True)
            pick = nidx == first
            sel = sel | pick
            g = jnp.where(pick, -jnp.inf, g)
        neg_ref[head * nb:(head + 1) * nb, :] = jnp.where(sel & valid, 0.0, MASK_NEG)

    def scores(head, js, adds):
        qh = q_head(head)
        parts = [lax.dot_general(k_ref[j, :, pair_cols(head)], qh, _NT, preferred_element_type=F32) + a
                 for j, a in zip(js, adds)]
        return parts[0] if len(parts) == 1 else jnp.concatenate(parts, axis=0)

    def weighted_values(head, js, p):
        vals = [vt_ref[j, head_rows(head), :] for j in js]
        vb = vals[0] if len(vals) == 1 else jnp.concatenate(vals, axis=1)
        vb = jnp.concatenate([vb, jnp.ones((SUM_ROWS, vb.shape[1]), BF16)], axis=0)
        return jnp.dot(vb, p.astype(BF16), preferred_element_type=F32)

    def start(head, js, st):
        m = jnp.max(st, axis=0, keepdims=True)
        m_refs[head][...] = m
        acc_refs[head][...] = weighted_values(head, js, jnp.exp2(st - m))

    def update(head, js, st):
        m = m_refs[head][...]
        m_new = jnp.maximum(m, jnp.max(st, axis=0, keepdims=True))
        alpha = jnp.exp2(m - m_new)
        m_refs[head][...] = m_new
        acc_refs[head][...] = alpha * acc_refs[head][...] + weighted_values(head, js, jnp.exp2(st - m_new))

    def sweep(js, adds_of, absorb):
        st = scores(0, js, adds_of(0))
        for head in range(ATT_HEADS):
            nxt = scores(head + 1, js, adds_of(head + 1)) if head + 1 < ATT_HEADS else None
            absorb(head, js, st)
            st = nxt

    def far_add(head, j):
        return neg_ref[pl.ds(head * nb + j, 1), :] + far_ref[head]

    def near_adds(h):
        return [bias_ref[h, 0], bias_ref[h, 1] + neg_ref[pl.ds(h * nb + i - 1, 1), :]]

    @pl.when(i == 0)
    def _():
        sweep([i], lambda h: [bias_ref[h, 0]], start)

    @pl.when((i == 1) | (i == 2))
    def _():
        sweep([i, i - 1], near_adds, start)

    @pl.when(i >= 3)
    def _():
        sweep([i, i - 1, i - 2, i - 3],
              lambda h: near_adds(h) + [far_add(h, i - 2), far_add(h, i - 3)], start)

    n_far = jnp.where(i >= 3, i - 3, jnp.maximum(i - 1, 0))

    def far_step(first, count):
        js = [first + c for c in range(count)]
        sweep(js, lambda h: [far_add(h, j) for j in js], update)

    def far_quad(jj, carry):
        far_step(4 * jj, 4)
        return carry

    lax.fori_loop(0, n_far // 4, far_quad, 0)
    rest = n_far % 4
    pl.when(rest >= 2)(lambda: far_step(n_far - rest, 2))
    pl.when(rest % 2 == 1)(lambda: far_step(n_far - 1, 1))

    for hp in range(ATT_HEADS // 2):
        outs = [acc_refs[h][0:HEAD_DIM, :] / acc_refs[h][HEAD_DIM:HEAD_DIM + 1, :] for h in (2 * hp, 2 * hp + 1)]
        pair = jnp.concatenate(outs, axis=0)
        o_ref[0, :, hp * pair_w:(hp + 1) * pair_w] = pair.T.astype(BF16)


def _rel_bucket(dist):
    n = jnp.maximum(dist, 0)
    max_exact = REL_BUCKETS // 2
    nf = jnp.maximum(n, 1).astype(F32)
    large = max_exact + (jnp.log(nf / max_exact) / math.log(REL_MAX_DIST / max_exact)
                         * (REL_BUCKETS - max_exact)).astype(jnp.int32)
    large = jnp.minimum(large, REL_BUCKETS - 1)
    return jnp.where(n < max_exact, n, large)


def _moba_bias_tiles(rel_bias):
    blk = MOBA_BLOCK
    span = 2 * blk
    rel_bias = rel_bias * LOG2E
    vec = rel_bias[_rel_bucket(jnp.arange(span))].T.astype(F32)
    masked = jnp.full_like(vec[:, :blk], MASK_NEG)
    ring_own = jnp.concatenate([vec[:, :blk], masked], axis=1)
    ring_prev = jnp.concatenate([vec[:, blk:], vec[:, :blk]], axis=1)

    def toeplitz(ring):
        flat = jnp.tile(ring, (1, blk))[:, :blk * (span - 1)]
        return flat.reshape(-1, blk, span - 1)[:, :, :blk]

    tiles = jnp.stack([toeplitz(ring_own), toeplitz(ring_prev)], axis=1)
    far = rel_bias[_rel_bucket(jnp.array(span))]
    return tiles, far.astype(F32)


def _moba(q4, k4, vt4, bias_tiles, far_bias, bsz):
    nb_total, blk, a = q4.shape
    nb = nb_total // bsz
    return pl.pallas_call(
        _moba_kernel,
        grid=(bsz, nb),
        in_specs=[
            pl.BlockSpec(memory_space=pltpu.SMEM),
            pl.BlockSpec((1, blk, a), lambda b, i: (b * nb + i, 0, 0)),
            pl.BlockSpec((nb, blk, a), lambda b, i: (b, 0, 0)),
            pl.BlockSpec((nb, a, blk), lambda b, i: (b, 0, 0)),
            _full(bias_tiles.shape),
        ],
        out_specs=pl.BlockSpec((1, blk, a), lambda b, i: (b * nb + i, 0, 0)),
        out_shape=jax.ShapeDtypeStruct((nb_total, blk, a), BF16),
        scratch_shapes=[pltpu.VMEM((nb, a), F32), pltpu.VMEM((ATT_HEADS * nb, blk), F32)]
        + [pltpu.VMEM((1, blk), F32), pltpu.VMEM((HEAD_DIM + SUM_ROWS, blk), F32)] * ATT_HEADS,
        compiler_params=_cparams("arbitrary", "arbitrary"),
        name="moba_attention",
    )(far_bias, q4, k4, vt4, bias_tiles)


def _ssm_weights(a_re, a_im, log_dt, b_re, b_im, c_re, c_im, d_skip):
    g, p, cw, n = SSM_GROUPS, SSM_STATE, SSM_GROUP, SSM_CHUNK
    gs = LANES // cw
    nq = g // gs
    npair = n // 2
    pairs_per_set = gs // 2
    lam_re = jnp.minimum(a_re.astype(F32), -1e-4)
    lam_im = a_im.astype(F32)
    dt = jnp.exp(log_dt.astype(F32))[:, None]
    mag = jnp.exp(lam_re * dt)
    lb_re = mag * jnp.cos(lam_im * dt)
    lb_im = mag * jnp.sin(lam_im * dt)
    n_re = lb_re - 1.0
    n_im = lb_im
    den = lam_re * lam_re + lam_im * lam_im
    z_re = ((n_re * lam_re + n_im * lam_im) / den)[..., None]
    z_im = ((n_im * lam_re - n_re * lam_im) / den)[..., None]
    br, bi = b_re.astype(F32), b_im.astype(F32)
    bb_re = z_re * br - z_im * bi
    bb_im = z_re * bi + z_im * br
    pw_re, pw_im = [jnp.ones_like(lb_re)], [jnp.zeros_like(lb_im)]
    for _ in range(n):
        r, im = pw_re[-1], pw_im[-1]
        pw_re.append(r * lb_re - im * lb_im)
        pw_im.append(r * lb_im + im * lb_re)
    pw_re, pw_im = jnp.stack(pw_re), jnp.stack(pw_im)
    cr, ci = c_re.astype(F32), c_im.astype(F32)
    hi = lax.Precision.HIGHEST
    rev_re, rev_im = pw_re[n - 1::-1], pw_im[n - 1::-1]
    sb_re = jnp.einsum('sgp,gpc->sgcp', rev_re, bb_re) - jnp.einsum('sgp,gpc->sgcp', rev_im, bb_im)
    sb_im = jnp.einsum('sgp,gpc->sgcp', rev_re, bb_im) + jnp.einsum('sgp,gpc->sgcp', rev_im, bb_re)
    cl_re = jnp.einsum('gcp,tgp->tgpc', cr, pw_re[1:]) - jnp.einsum('gcp,tgp->tgpc', ci, pw_im[1:])
    cl_im = jnp.einsum('gcp,tgp->tgpc', cr, pw_im[1:]) + jnp.einsum('gcp,tgp->tgpc', ci, pw_re[1:])
    cp_re = jnp.einsum('gcp,tgp->tgcp', cr, pw_re[:n]) - jnp.einsum('gcp,tgp->tgcp', ci, pw_im[:n])
    cp_im = jnp.einsum('gcp,tgp->tgcp', cr, pw_im[:n]) + jnp.einsum('gcp,tgp->tgcp', ci, pw_re[:n])
    kern = (jnp.einsum('tgcp,gpd->gtcd', cp_re, bb_re, precision=hi)
            - jnp.einsum('tgcp,gpd->gtcd', cp_im, bb_im, precision=hi))

    kp = jnp.concatenate([jnp.zeros_like(kern[:, :1]), kern], axis=1)
    dl = jnp.arange(npair)[:, None, None]
    s0 = jnp.arange(2)[None, :, None]
    t0 = jnp.arange(2)[None, None, :]
    lag = 2 * dl + t0 - s0
    kl = kp[:, lag + 1].reshape(nq, gs, npair, 2, 2, cw, cw)
    base = kl.transpose(0, 2, 3, 1, 6, 4, 5)
    same_group = jnp.eye(gs, dtype=F32)
    w_toe = jnp.concatenate(
        [base[..., t0, :] * same_group[:, g1][None, None, None, :, None, None]
         for t0 in range(2) for g1 in range(gs)], axis=-1)
    w_toe = w_toe.reshape(nq, npair, MXU_DIM, MXU_DIM).astype(BF16)

    member = (jnp.arange(gs)[None, :, None]
              == 2 * jnp.arange(pairs_per_set)[:, None, None] + jnp.arange(2)[None, None, :]).astype(F32)

    def pair_tiles(x):
        cols = [x[ri][:, None] * member[:, :, gl][None, :, None, None, :, None, None]
                for ri in range(2) for gl in range(2)]
        return jnp.concatenate(cols, axis=-1).reshape(g // 2, npair, MXU_DIM, MXU_DIM)

    sbs = jnp.stack([sb_re, sb_im]).reshape(2, npair, 2, nq, gs, cw, p)
    w_in_state = pair_tiles(sbs.transpose(0, 3, 1, 2, 4, 5, 6)).astype(BF16)
    sos = jnp.stack([cl_re, -cl_im]).reshape(2, npair, 2, nq, gs, p, cw)
    w_state_out = jnp.swapaxes(pair_tiles(sos.transpose(0, 3, 1, 2, 4, 6, 5)), -1, -2).astype(BF16)
    decay = jnp.stack([pw_re[n].reshape(g // 2, 2 * p), pw_im[n].reshape(g // 2, 2 * p)])
    dvec = d_skip.astype(F32).reshape(1, g * cw)
    return w_toe, w_in_state, w_state_out, decay, dvec


def _ssm_kernel(u_ref, wt_ref, wi_ref, wo_ref, dec_ref, d_ref, y_ref, s_ref, xp_ref, x_ref, *, bsz):
    n = SSM_CHUNK
    kt = u_ref.shape[1] // n
    width = u_ref.shape[0] // bsz * LANES
    rows = kt * bsz
    npair = n // 2
    nq = wt_ref.shape[0]
    npairs_g = wi_ref.shape[0]
    per_set = npairs_g // nq
    half = MXU_DIM // 2

    @pl.when(pl.program_id(0) == 0)
    def _():
        x_ref[...] = jnp.zeros_like(x_ref)

    def block_of(b, q):
        return b * (width // LANES) + q

    def piece(s, q):
        return jnp.concatenate([u_ref[block_of(b, q), pl.ds(s, kt, stride=n), :] for b in range(bsz)], axis=0)

    lhs = {(sp, q): jnp.concatenate([piece(2 * sp, q), piece(2 * sp + 1, q)], axis=-1).astype(BF16)
           for sp in range(npair) for q in range(nq)}

    for gp in range(npairs_g):
        q = gp // per_set
        acc = jnp.dot(lhs[0, q], wi_ref[gp, 0], preferred_element_type=F32)
        for sp in range(1, npair):
            acc = acc + jnp.dot(lhs[sp, q], wi_ref[gp, sp], preferred_element_type=F32)
        s_ref[2 * gp] = acc[:, :half]
        s_ref[2 * gp + 1] = acc[:, half:]

    for gp in range(npairs_g):
        re_cols = slice(gp * MXU_DIM, gp * MXU_DIM + half)
        im_cols = slice(gp * MXU_DIM + half, (gp + 1) * MXU_DIM)
        ar = dec_ref[0, gp:gp + 1, :]
        ai = dec_ref[1, gp:gp + 1, :]
        xr = x_ref[:, re_cols]
        xi = x_ref[:, im_cols]
        for kc in range(kt):
            rs = pl.ds(kc, bsz, stride=kt)
            xp_ref[2 * gp, rs, :] = xr
            xp_ref[2 * gp + 1, rs, :] = xi
            sr = s_ref[2 * gp, rs, :]
            si = s_ref[2 * gp + 1, rs, :]
            xr, xi = ar * xr - ai * xi + sr, ar * xi + ai * xr + si
        x_ref[:, re_cols] = xr
        x_ref[:, im_cols] = xi

    for q in range(nq):
        for tp in range(npair):
            acc = jnp.dot(lhs[0, q], wt_ref[q, tp], preferred_element_type=F32)
            for sp in range(1, tp + 1):
                acc = acc + jnp.dot(lhs[sp, q], wt_ref[q, tp - sp], preferred_element_type=F32)
            for gp in range(q * per_set, (q + 1) * per_set):
                xp = jnp.concatenate([xp_ref[2 * gp], xp_ref[2 * gp + 1]], axis=1).astype(BF16)
                acc = acc + jnp.dot(xp, wo_ref[gp, tp], preferred_element_type=F32)
            for t0 in range(2):
                s = 2 * tp + t0
                y = acc[:, t0 * half:(t0 + 1) * half] + d_ref[:, q * LANES:(q + 1) * LANES] * piece(s, q)
                y = jax.nn.gelu(y)
                for b in range(bsz):
                    y_ref[block_of(b, q), pl.ds(s, kt, stride=n), :] = y[b * kt:(b + 1) * kt]


def _ssm(u_tm, weights, bsz, seq):
    n = SSM_CHUNK
    kt = SSM_CHUNKS_PER_STEP
    steps = kt * n
    rows = kt * bsz
    state_w = SSM_GROUPS * 2 * SSM_STATE
    blk = pl.BlockSpec((u_tm.shape[0], steps, LANES), lambda i: (0, i, 0))

    once = pl.Buffered(1)
    return pl.pallas_call(
        functools.partial(_ssm_kernel, bsz=bsz),
        grid=(seq // steps,),
        in_specs=[blk] + [_full(w.shape, pipeline_mode=once) for w in weights],
        out_specs=blk,
        out_shape=jax.ShapeDtypeStruct(u_tm.shape, F32),
        scratch_shapes=[pltpu.VMEM((state_w // LANES, rows, LANES), F32),
                        pltpu.VMEM((state_w // LANES, rows, LANES), F32),
                        pltpu.VMEM((bsz, state_w), F32)],
        compiler_params=_cparams("arbitrary"),
        name="s5_chunked_scan",
    )(u_tm, *weights)


def _postmix_kernel(att_ref, ys_ref, ga_ref, gs_ref, x_ref, gt_ref, sc_ref, sh_ref, gpost_ref, gpre_ref,
                    wglu_ref, wso_ref, wao_ref, wo_ref, wrh_ref, wrl_ref, br_ref,
                    x1_ref, h2_ref, idx_ref, gate_ref, rank_ref, cnt_ref, run_ref):
    tm = x_ref.shape[0]
    ne = wrh_ref.shape[0]

    @pl.when(pl.program_id(0) == 0)
    def _():
        run_ref[...] = jnp.zeros_like(run_ref)

    halves = [slice(r * (tm // 2), (r + 1) * (tm // 2)) for r in range(2)]
    dot = functools.partial(jnp.dot, preferred_element_type=F32)
    att = [dot(att_ref[r, :], wao_ref[...]) for r in halves]
    glu = [dot(jnp.concatenate([ys_ref[j, r, :] for j in range(ys_ref.shape[0])], axis=1).astype(BF16),
               wglu_ref[...]) for r in halves]
    ssm = []
    for g in glu:
        sg = (g[:, :SSM_WIDTH] * jax.nn.sigmoid(g[:, SSM_WIDTH:])).astype(BF16)
        ssm.append(dot(sg, wso_ref[...]))
    y = []
    for r, a, s in zip(halves, att, ssm):
        merged = (ga_ref[r, :].astype(F32) * a + gs_ref[r, :].astype(F32) * s).astype(BF16)
        y.append(dot(merged, wo_ref[...]))
    logits = []
    for r, yh in zip(halves, y):
        x1 = x_ref[r, :] + gt_ref[0] * _rms(yh, gpost_ref[...])
        x1_ref[r, :] = x1
        h2 = _rms(x1, gpre_ref[...]) * (1.0 + sc_ref[0]) + sh_ref[0]
        h2_ref[r, :] = _pack_bf16_pairs(h2)
        h_hi = h2.astype(BF16)
        h_lo = (h2 - h_hi.astype(F32)).astype(BF16)
        logits.append(lax.dot_general(wrh_ref[...], h_hi, _NT, preferred_element_type=F32)
                      + lax.dot_general(wrh_ref[...], h_lo, _NT, preferred_element_type=F32)
                      + lax.dot_general(wrl_ref[...], h_hi, _NT, preferred_element_type=F32))
    logits = jnp.concatenate(logits, axis=1) + br_ref[...]
    eidx = lax.broadcasted_iota(jnp.int32, (ne, tm), 0)
    lg = logits
    vals, idxs = [], []
    for _ in range(TOP_K):
        mx = jnp.max(lg, axis=0, keepdims=True)
        first = jnp.min(jnp.where(lg == mx, eidx, ne), axis=0, keepdims=True)
        vals.append(mx)
        idxs.append(first)
        lg = jnp.where(eidx == first, -jnp.inf, lg)
    exps = [jnp.exp(v - vals[0]) for v in vals]
    denom = exps[0] + exps[1] + exps[2] + exps[3]
    gate_ref[...] = jnp.concatenate([e / denom for e in exps], axis=0)
    idx_ref[...] = jnp.concatenate(idxs, axis=0)
    onehot = jnp.where(lg == -jnp.inf, 1.0, 0.0)
    t_row = lax.broadcasted_iota(jnp.int32, (tm, tm), 0)
    t_col = lax.broadcasted_iota(jnp.int32, (tm, tm), 1)
    earlier = jnp.where(t_row < t_col, 1.0, 0.0).astype(BF16)
    before = jnp.dot(onehot.astype(BF16), earlier, preferred_element_type=F32) + run_ref[:, 0:1]
    ranks = [jnp.sum(jnp.where(eidx == ix, before, 0.0), axis=0, keepdims=True) for ix in idxs]
    rank_ref[...] = jnp.concatenate(ranks, axis=0).astype(jnp.int32)
    run_ref[...] = run_ref[...] + jnp.sum(onehot, axis=1, keepdims=True)
    cnt_ref[...] = run_ref[...].astype(jnp.int32)


def _postmix(att2, ys_tm, ga, gs, x2, gt1, sc2, sh2, g_post, g_pre, w_glu, w_ssm_out, w_att_out, w_out,
             w_router, b_router, seq):
    t, d = x2.shape
    tm = TOKEN_TILE
    tiles_per_seq = seq // tm
    ne = w_router.shape[1]
    wr_t = w_router.T.astype(F32)
    wr_hi = wr_t.astype(BF16)
    wr_lo = (wr_t - wr_hi.astype(F32)).astype(BF16)
    mod_spec = pl.BlockSpec((1, 1, d), lambda i: (i // tiles_per_seq, 0, 0))
    row = lambda cols: pl.BlockSpec((tm, cols), lambda i: (i, 0))
    time_major = pl.BlockSpec((SSM_WIDTH // LANES, tm, LANES),
                              lambda i: (i // tiles_per_seq, i % tiles_per_seq, 0))
    col =pl.BlockSpec((TOP_K, tm), lambda i: (0, i))
    weights = [w_glu.astype(BF16), w_ssm_out.astype(BF16), w_att_out.astype(BF16), w_out.astype(BF16),
               wr_hi, wr_lo, b_router.astype(F32).reshape(ne, 1)]
    return pl.pallas_call(
        _postmix_kernel,
        grid=(t // tm,),
        in_specs=[row(ATT_WIDTH), time_major, row(d), row(d), row(d), mod_spec, mod_spec, mod_spec,
                  _full((1, d)), _full((1, d))]
        + [_full(w.shape, pipeline_mode=pl.Buffered(1)) for w in weights],
        out_specs=[row(d), row(d // 2), col, col, col, _full((ne, 128))],
        out_shape=[jax.ShapeDtypeStruct((t, d), F32), jax.ShapeDtypeStruct((t, d // 2), jnp.uint32),
                   jax.ShapeDtypeStruct((TOP_K, t), jnp.int32), jax.ShapeDtypeStruct((TOP_K, t), F32),
                   jax.ShapeDtypeStruct((TOP_K, t), jnp.int32), jax.ShapeDtypeStruct((ne, 128), jnp.int32)],
        scratch_shapes=[pltpu.VMEM((ne, 128), F32)],
        compiler_params=_cparams("arbitrary"),
        name="postmix_router",
    )(att2, ys_tm, ga, gs, x2, gt1, sc2, sh2, g_post.reshape(1, d), g_pre.reshape(1, d), *weights)


def _sc_worker_base(per_worker):
    return (lax.axis_index("s") * SC_CORES + lax.axis_index("c")) * per_worker


def _sc_mesh():
    return plsc.VectorSubcoreMesh(core_axis_name="c", subcore_axis_name="s",
                                  num_cores=SC_CORES, num_subcores=SC_SUBCORES)


def _sc_scatter_kernel(rows_hbm, idx_hbm, out_hbm, idx_v, rows_v, load_sems, store_sems, *, per_worker):
    chunk = rows_v.shape[1]
    n_tokens = rows_hbm.shape[0]
    n_chunks = per_worker // chunk
    base = _sc_worker_base(per_worker)

    def load(c, b):
        off = pl.multiple_of(base + c * chunk, chunk)
        for k in range(TOP_K):
            pltpu.sync_copy(idx_hbm.at[pl.ds(k * n_tokens + off, chunk)], idx_v.at[b, k])
        return pltpu.make_async_copy(rows_hbm.at[pl.ds(off, chunk)], rows_v.at[b], load_sems.at[b])

    def stores(b):
        return [pltpu.make_async_copy(rows_v.at[b], out_hbm.at[idx_v.at[b, k]], store_sems.at[b])
                for k in range(TOP_K)]

    load(0, 0).start()

    @pl.loop(0, n_chunks, step=2)
    def _(c0):
        for b in range(2):
            c = c0 + b

            @pl.when(c > 0)
            def _():
                for cp in stores(1 - b):
                    cp.wait()

            @pl.when(c + 1 < n_chunks)
            def _():
                load(c + 1, 1 - b).start()

            pltpu.make_async_copy(rows_hbm.at[pl.ds(0, chunk)], rows_v.at[b], load_sems.at[b]).wait()
            for cp in stores(b):
                cp.start()

    for cp in stores((n_chunks - 1) % 2):
        cp.wait()


def _sc_scatter_rows(rows, idx, n_out):
    t, d = rows.shape
    workers = SC_CORES * SC_SUBCORES
    per_worker = t // workers
    assert t % workers == 0 and per_worker % (2 * SC_GATHER_ROWS) == 0
    return pl.kernel(
        functools.partial(_sc_scatter_kernel, per_worker=per_worker),
        out_type=jax.ShapeDtypeStruct((n_out, d), rows.dtype),
        mesh=_sc_mesh(),
        scratch_types=[pltpu.VMEM((2, TOP_K, SC_GATHER_ROWS), jnp.int32),
                       pltpu.VMEM((2, SC_GATHER_ROWS, d), rows.dtype),
                       pltpu.SemaphoreType.DMA((2,)), pltpu.SemaphoreType.DMA((2,))],
        name="sc_row_scatter",
    )(rows, idx)


def _sc_gather_kernel(table_hbm, idx_hbm, out_hbm, idx_v, rows_v, sems, *, per_worker):
    chunk = rows_v.shape[1]
    n_chunks = per_worker // chunk
    base = _sc_worker_base(per_worker)

    def gather(c, b):
        off = pl.multiple_of(base + c * chunk, chunk)
        pltpu.sync_copy(idx_hbm.at[pl.ds(off, chunk)], idx_v.at[b])
        return pltpu.make_async_copy(table_hbm.at[idx_v.at[b]], rows_v.at[b], sems.at[b])

    gather(0, 0).start()

    @pl.loop(0, n_chunks, step=2)
    def _(c0):
        for b in range(2):
            c = c0 + b

            @pl.when(c + 1 < n_chunks)
            def _():
                gather(c + 1, 1 - b).start()

            pltpu.make_async_copy(table_hbm.at[idx_v.at[b]], rows_v.at[b], sems.at[b]).wait()
            off = pl.multiple_of(base + c * chunk, chunk)
            pltpu.sync_copy(rows_v.at[b], out_hbm.at[pl.ds(off, chunk)])


def _sc_gather_rows(table, idx):
    n = idx.shape[0]
    d = table.shape[1]
    workers = SC_CORES * SC_SUBCORES
    per_worker = n // workers
    assert n % workers == 0 and per_worker % (2 * SC_GATHER_ROWS) == 0
    return pl.kernel(
        functools.partial(_sc_gather_kernel, per_worker=per_worker),
        out_type=jax.ShapeDtypeStruct((n, d), table.dtype),
        mesh=_sc_mesh(),
        scratch_types=[pltpu.VMEM((2, SC_GATHER_ROWS), jnp.int32),
                       pltpu.VMEM((2, SC_GATHER_ROWS, d), table.dtype),
                       pltpu.SemaphoreType.DMA((2,))],
        name="sc_row_gather",
    )(table, idx)


def _experts_kernel(be_ref, nused_ref, valid_ref, next_ref, x_ref, b1_ref, b2_ref, w1_hbm, w2_hbm, y_ref,
                    w1s_ref, w2s_ref, w1b_ref, w2b_ref, sems, *, layer):
    i = pl.program_id(0)
    prev = be_ref[jnp.maximum(i - 1, 0)]
    fresh = (i < nused_ref[0]) & ((i == 0) | (be_ref[i] != prev))

    def fetch(e):
        return (pltpu.make_async_copy(w1_hbm.at[layer, e], w1s_ref, sems.at[0]),
                pltpu.make_async_copy(w2_hbm.at[layer, e], w2s_ref, sems.at[1]))

    @pl.when(i == 0)
    def _():
        for cp in fetch(be_ref[0]):
            cp.start()

    @pl.when(fresh)
    def _():
        for cp in fetch(be_ref[i]):
            cp.wait()
        w1b_ref[...] = w1s_ref[...].astype(BF16)
        w2b_ref[...] = w2s_ref[...].astype(BF16)

        @pl.when(next_ref[i] >= 0)
        def _():
            for cp in fetch(next_ref[i]):
                cp.start()

    used = i < nused_ref[0]
    valid = valid_ref[i]
    half = x_ref.shape[0] // 2

    def ffn(n_rows):
        row = lax.broadcasted_iota(jnp.int32, (n_rows, x_ref.shape[1]), 0)
        x_lo, x_hi = _unpack_bf16_pairs(jnp.where(row < valid, x_ref[0:n_rows, :], jnp.uint32(0)))
        x = jnp.concatenate([x_lo.astype(BF16), x_hi.astype(BF16)], axis=1)
        gu = jnp.dot(x, w1b_ref[...], preferred_element_type=F32) + b1_ref[...]
        g = jnp.minimum(gu[:, :D_FF], SWIGLU_LIMIT)
        up = jnp.clip(gu[:, D_FF:], -SWIGLU_LIMIT, SWIGLU_LIMIT)
        act = ((up + 1.0) * g * jax.nn.sigmoid(SWIGLU_ALPHA * g)).astype(BF16)
        y = jnp.dot(act, w2b_ref[...], preferred_element_type=F32) + b2_ref[...]
        y_ref[0:n_rows, :] = _pack_bf16_pairs(y)

    @pl.when(used & (valid > half))
    def _():
        ffn(2 * half)

    def short(n_rows):
        ffn(n_rows)
        y_ref[n_rows:, :] = jnp.zeros((2 * half - n_rows, y_ref.shape[1]), y_ref.dtype)

    pl.when(used & (valid <= half) & (valid > half // 2))(lambda: short(half))
    pl.when(used & (valid <= half // 2))(lambda: short(half // 2))

    @pl.when(jnp.logical_not(used))
    def _():
        y_ref[...] = jnp.zeros_like(y_ref)


def _experts(xb, block_e, n_used, valid, next_e, w1, b1, w2, b2, layer):
    p_rows, packed_w = xb.shape
    d = 2 * packed_w
    depth, ne = w1.shape[:2]
    rb = EXPERT_ROWS
    bmap = lambda i, be, nu, va, nx: (layer, be[i], 0, 0)
    rows = pl.BlockSpec((rb, packed_w), lambda i, be, nu, va, nx: (i, 0))
    grid_spec = pltpu.PrefetchScalarGridSpec(
        num_scalar_prefetch=4,
        grid=(p_rows // rb,),
        in_specs=[
            rows,
            pl.BlockSpec((None, None, 1, 2 * D_FF), bmap),
            pl.BlockSpec((None, None, 1, d), bmap),
            pl.BlockSpec(memory_space=pl.ANY),
            pl.BlockSpec(memory_space=pl.ANY),
        ],
        out_specs=rows,
        scratch_shapes=[pltpu.VMEM((d, 2 * D_FF), w1.dtype), pltpu.VMEM((D_FF, d), w2.dtype),
                        pltpu.VMEM((d, 2 * D_FF), BF16), pltpu.VMEM((D_FF, d), BF16),
                        pltpu.SemaphoreType.DMA((2,))],
    )
    return pl.pallas_call(
        functools.partial(_experts_kernel, layer=layer),
        grid_spec=grid_spec,
        out_shape=jax.ShapeDtypeStruct((p_rows, packed_w), jnp.uint32),
        compiler_params=_cparams("arbitrary"),
        name="expert_ffn",
    )(block_e, n_used, valid, next_e, xb, b1.reshape(depth, ne, 1, 2 * D_FF), b2.reshape(depth, ne, 1, d), w1, w2)


def _combine_kernel(gate_ref, x_ref, gt_ref, g_ref, y0_ref, y1_ref, y2_ref, y3_ref, o_ref):
    tm = x_ref.shape[0]
    gates = gate_ref[...]
    gates = jnp.concatenate([gates, jnp.zeros((LANES - TOP_K, tm), F32)], axis=0).T
    lo, hi = _unpack_bf16_pairs(y0_ref[...])
    y_lo, y_hi = gates[:, 0:1] * lo, gates[:, 0:1] * hi
    for k, y_ref in enumerate((y1_ref, y2_ref, y3_ref), start=1):
        lo, hi = _unpack_bf16_pairs(y_ref[...])
        y_lo, y_hi = y_lo + gates[:, k:k + 1] * lo, y_hi + gates[:, k:k + 1] * hi
    y = jnp.concatenate([y_lo, y_hi], axis=1)
    o_ref[...] = x_ref[...] + gt_ref[0] * _rms(y, g_ref[...])


def _combine(y4, gate_t, x2, gt2, g_post, seq):
    t, d = x2.shape
    tm = TOKEN_TILE
    tiles = t // tm
    tiles_per_seq = seq // tm
    row = pl.BlockSpec((tm, d), lambda i: (i, 0))
    slot = lambda k: pl.BlockSpec((tm, y4.shape[1]), lambda i: (k * tiles + i, 0))
    return pl.pallas_call(
        _combine_kernel,
        grid=(tiles,),
        in_specs=[pl.BlockSpec((TOP_K, tm), lambda i: (0, i)),
                  row, pl.BlockSpec((1, 1, d), lambda i: (i // tiles_per_seq, 0, 0)), _full((1, d))]
        + [slot(k) for k in range(TOP_K)],
        out_specs=row,
        out_shape=jax.ShapeDtypeStruct((t, d), F32),
        compiler_params=_cparams("arbitrary"),
        name="expert_combine",
    )(gate_t, x2, gt2, g_post.reshape(1, d), y4, y4, y4, y4)


def _route_plan(idx_t, rank_t, counts):
    rb = EXPERT_ROWS
    k, t = idx_t.shape
    padded = (counts + rb - 1) // rb * rb
    pad_ends = jnp.cumsum(padded)
    pad_starts = pad_ends - padded
    experts = jnp.arange(N_EXPERTS, dtype=jnp.int32)
    start_of = jnp.sum(jnp.where(idx_t[None] == experts[:, None, None], pad_starts[:, None, None], 0), axis=0)
    dest = (start_of + rank_t).astype(jnp.int32)
    n_blocks = (k * t) // rb + N_EXPERTS
    blk_start = jnp.arange(n_blocks, dtype=jnp.int32) * rb
    block_e = jnp.minimum(jnp.sum(pad_ends[None, :] <= blk_start[:, None], axis=1), N_EXPERTS - 1)
    onehot_e = block_e[:, None] == experts[None, :]
    cnt_b = jnp.sum(jnp.where(onehot_e, counts[None, :], 0), axis=1)
    start_b = jnp.sum(jnp.where(onehot_e, pad_starts[None, :], 0), axis=1)
    valid = jnp.clip(cnt_b - (blk_start - start_b), 0, rb).astype(jnp.int32)
    n_used = (pad_ends[-1] // rb).astype(jnp.int32).reshape(1)
    later_nonempty = (experts[None, :] > experts[:, None]) & (counts[None, :] > 0)
    next_nonempty = jnp.min(jnp.where(later_nonempty, experts[None, :], N_EXPERTS), axis=1)
    next_nonempty = jnp.where(next_nonempty == N_EXPERTS, -1, next_nonempty)
    next_e = jnp.sum(jnp.where(onehot_e, next_nonempty[None, :], 0), axis=1).astype(jnp.int32)
    return dest.reshape(-1), block_e.astype(jnp.int32), n_used, valid, next_e, n_blocks * rb


def kernel(x, c, rel_bias, w_ada, b_ada, g_pre_mix, g_post_mix, g_pre_ffn, g_post_ffn, w_in, ssm_a_re, ssm_a_im, ssm_log_dt, ssm_b_re, ssm_b_im, ssm_c_re, ssm_c_im, ssm_d, w_glu, w_ssm_out, w_att_out, w_out, w_router, b_router, w_exp_in, b_exp_in, w_exp_out, b_exp_out):
    bsz, seq, d = x.shape
    depth = w_ada.shape[0]
    t = bsz * seq
    assert d == D_MODEL and seq % TOKEN_TILE == 0 and TOKEN_TILE % MOBA_BLOCK == 0
    assert seq % (SSM_CHUNK * SSM_CHUNKS_PER_STEP) == 0

    mod = _ada_mod(c, w_ada, b_ada)
    bias_tiles, far_bias = _moba_bias_tiles(rel_bias.astype(F32))
    x2 = x.reshape(t, d)
    for l in range(depth):
        sh1, sc1, gt1, sh2, sc2, gt2 = [m.reshape(bsz, 1, d) for m in jnp.split(mod[l], N_MOD, axis=-1)]
        q4, k4, vt4, u_tm, ga, gs = _premix(x2, g_pre_mix[l], sc1, sh1, w_in[l], seq)
        att = _moba(q4, k4, vt4, bias_tiles, far_bias, bsz).reshape(t, ATT_WIDTH)
        ssm_w = _ssm_weights(ssm_a_re[l], ssm_a_im[l], ssm_log_dt[l], ssm_b_re[l], ssm_b_im[l],
                             ssm_c_re[l], ssm_c_im[l], ssm_d[l])
        ys_tm = _ssm(u_tm, ssm_w, bsz, seq)
        x1, h2, idx_t, gate_t, rank_t, cnt = _postmix(
            att, ys_tm, ga, gs, x2, gt1, sc2, sh2, g_post_mix[l], g_pre_ffn[l],
            w_glu[l], w_ssm_out[l], w_att_out[l], w_out[l], w_router[l], b_router[l], seq)
        dest_flat, block_e, n_used, valid, next_e, p_rows = _route_plan(idx_t, rank_t, cnt[:, 0])
        xb = _sc_scatter_rows(h2, dest_flat, p_rows)
        yb = _experts(xb, block_e, n_used, valid, next_e, w_exp_in, b_exp_in, w_exp_out, b_exp_out, l)
        y4 = _sc_gather_rows(yb, dest_flat)
        x2 = _combine(y4, gate_t, x1, gt2, g_post_ffn[l], seq)
    return x2.reshape(bsz, seq, d)
```
